```python
import math
import functools
import jax
import jax.numpy as jnp
from jax import lax
import numpy as np

D_MODEL = 1024
BATCH = 8
SEQ = 4096
DEPTH = 1
DEC_BATCH = 32
DEC_SEQ = 8
PAST_LEN = 16384
PAGE_SIZE = 128

SSD_EXPAND = 2
D_INNER = SSD_EXPAND * D_MODEL
SSD_HEAD_DIM = 64
SSD_HEADS = D_INNER // SSD_HEAD_DIM
SSD_GROUPS = 4
D_STATE = 128
CONV_WIDTH = 4
CONV_DIM = D_INNER + 2 * SSD_GROUPS * D_STATE
SSD_CHUNK = 128
N_HEADS = 16
N_KV_HEADS = 4
HEAD_DIM = 64
IDX_HEADS = 8
IDX_DIM = 64
TOPK_MAX = 256
Q_BLOCK = 128
N_MEM = 256
MEM_HEADS = 4
MEM_HEAD_DIM = 256
MEM_WIDTH = MEM_HEADS * MEM_HEAD_DIM
N_EXPERTS = 32
TOP_K = 4
D_FF = D_MODEL
SWIGLU_LIMIT = 7.0
SWIGLU_ALPHA = 1.702
MOE_BLOCK = 128
N_BRANCH = 3
EPS = 1e-6

IN_WIDTHS = (D_INNER, CONV_DIM, SSD_HEADS, N_HEADS * HEAD_DIM, N_KV_HEADS * HEAD_DIM, N_KV_HEADS * HEAD_DIM,
             IDX_HEADS * IDX_DIM, IDX_DIM, IDX_HEADS, MEM_WIDTH, N_BRANCH * D_MODEL)
D_IN_PROJ = sum(IN_WIDTHS)

kernel_name = 'hybrid_ssd_dsa_moe_decoder_step'


def rmsnorm(x, g):
    xf = x.astype(jnp.float32)
    xf = xf * lax.rsqrt(jnp.mean(xf * xf, axis=-1, keepdims=True) + EPS)
    return (xf * g.astype(jnp.float32)).astype(x.dtype)


def split_in(z):
    outs = []
    off = 0
    for w in IN_WIDTHS:
        outs.append(z[..., off:off + w])
        off += w
    return outs


def causal_dwconv(xbc, conv_w, conv_b, prev):
    l = xbc.shape[1]
    xp = jnp.concatenate([prev, xbc], axis=1)
    out = conv_b
    for j in range(CONV_WIDTH):
        out = out + xp[:, j:j + l] * conv_w[j]
    return jax.nn.silu(out), xp[:, -(CONV_WIDTH - 1):]


def ssd_scan(xs, dt, A, Bm, Cm, init_state, chunk):
    b, l, h, p = xs.shape
    g, n = Bm.shape[2], Bm.shape[3]
    e = h // g
    c = l // chunk
    f32 = jnp.float32
    xdt = (xs.astype(f32) * dt[..., None]).reshape(b, c, chunk, g, e, p)
    a_cs = jnp.cumsum((dt * A).reshape(b, c, chunk, g, e), axis=2)
    Bc = Bm.astype(f32).reshape(b, c, chunk, g, n)
    Cc = Cm.astype(f32).reshape(b, c, chunk, g, n)
    causal = jnp.tril(jnp.ones((chunk, chunk), bool))[:, :, None, None]
    diff = a_cs[:, :, :, None] - a_cs[:, :, None, :]
    decay_in = jnp.exp(jnp.where(causal, diff, -jnp.inf))
    cb = jnp.einsum('bcign,bcjgn->bcijg', Cc, Bc)
    y_diag = jnp.einsum('bcijge,bcjgep->bcigep', cb[..., None] * decay_in, xdt)
    decay_to_end = jnp.exp(a_cs[:, :, -1:] - a_cs)
    chunk_states = jnp.einsum('bcjgn,bcjgep->bcgepn', Bc, xdt * decay_to_end[..., None])
    chunk_decay = jnp.exp(a_cs[:, :, -1])

    def carry(state, inp):
        st, dec = inp
        return dec[..., None, None] * state + st, state

    final, starts = lax.scan(carry, init_state.astype(f32).reshape(b, g, e, p, n),
                             (jnp.moveaxis(chunk_states, 1, 0), jnp.moveaxis(chunk_decay, 1, 0)))
    starts = jnp.moveaxis(starts, 0, 1)
    y_off = jnp.einsum('bcign,bcgepn->bcigep', Cc, starts) * jnp.exp(a_cs)[..., None]
    return (y_diag + y_off).reshape(b, l, h, p), final.reshape(b, h, p, n)


def ssd_branch(z, xbc, dt_raw, conv_prev, ssm_init, conv_w, conv_b, dt_bias, a_log, d_skip, g_ssd):
    b, l, _ = z.shape
    xc, conv_new = causal_dwconv(xbc, conv_w, conv_b, conv_prev)
    xs = xc[..., :D_INNER].reshape(b, l, SSD_HEADS, SSD_HEAD_DIM)
    Bm = xc[..., D_INNER:D_INNER + SSD_GROUPS * D_STATE].reshape(b, l, SSD_GROUPS, D_STATE)
    Cm = xc[..., D_INNER + SSD_GROUPS * D_STATE:].reshape(b, l, SSD_GROUPS, D_STATE)
    dt = jax.nn.softplus(dt_raw.astype(jnp.float32) + dt_bias.astype(jnp.float32))
    A = -jnp.exp(a_log.astype(jnp.float32))
    chunk = SSD_CHUNK if l % SSD_CHUNK == 0 else l
    y, ssm_new = ssd_scan(xs, dt, A, Bm, Cm, ssm_init, chunk)
    y = y + d_skip.astype(jnp.float32)[:, None] * xs.astype(jnp.float32)
    y = y.reshape(b, l, D_INNER) * jax.nn.silu(z.astype(jnp.float32))
    yg = y.reshape(b, l, SSD_GROUPS, D_INNER // SSD_GROUPS)
    yg = yg * lax.rsqrt(jnp.mean(yg * yg, axis=-1, keepdims=True) + EPS)
    y = yg.reshape(b, l, D_INNER) * g_ssd.astype(jnp.float32)
    return y.astype(z.dtype), conv_new, ssm_new.astype(ssm_init.dtype)


def indexer_topk(qi, wi, ki, qpos, kpos, n_sel):
    dots = jnp.einsum('bthd,bsd->bths', qi.astype(jnp.float32), ki.astype(jnp.float32)) * (IDX_DIM ** -0.5)
    score = jnp.einsum('bths,bth->bts', jax.nn.relu(dots), wi.astype(jnp.float32) * (IDX_HEADS ** -0.5))
    visible = kpos[None, :] <= qpos[:, None]
    score = jnp.where(visible[None], score, -jnp.inf)
    _, sel = lax.top_k(score, n_sel)
    valid = sel <= qpos[None, :, None]
    return sel, valid


def gather_rows(a, idx):
    return jax.vmap(lambda ab, ib: ab[ib])(a, idx)


def sparse_attend(q, ks, vs, valid):
    b, t = q.shape[:2]
    qg = q.reshape(b, t, N_KV_HEADS, N_HEADS // N_KV_HEADS, HEAD_DIM).astype(jnp.float32)
    s = jnp.einsum('btkgd,btnkd->btkgn', qg, ks.astype(jnp.float32)) * (HEAD_DIM ** -0.5)
    s = jnp.where(valid[:, :, None, None, :], s, -jnp.inf)
    pr = jax.nn.softmax(s, axis=-1)
    o = jnp.einsum('btkgn,btnkd->btkgd', pr, vs.astype(jnp.float32))
    return o.reshape(b, t, N_HEADS * HEAD_DIM).astype(q.dtype)


def dsa_prompt(q, k, v, qi, ki, wi):
    b, l = q.shape[:2]
    n_sel = min(TOPK_MAX, l // 4)
    kpos = jnp.arange(l)

    def one_block(i):
        start = i * Q_BLOCK
        qb = lax.dynamic_slice_in_dim(q, start, Q_BLOCK, axis=1)
        qib = lax.dynamic_slice_in_dim(qi, start, Q_BLOCK, axis=1)
        wib = lax.dynamic_slice_in_dim(wi, start, Q_BLOCK, axis=1)
        qpos = start + jnp.arange(Q_BLOCK)
        sel, valid = indexer_topk(qib, wib, ki, qpos, kpos, n_sel)
        return sparse_attend(qb, gather_rows(k, sel), gather_rows(v, sel), valid)

    out = lax.map(one_block, jnp.arange(l // Q_BLOCK))
    return jnp.moveaxis(out, 0, 1).reshape(b, l, N_HEADS * HEAD_DIM)


def dsa_sample(q, k, v, qi, ki, wi, *, cache_k, cache_v, cache_ki, page_table):
    b, t = q.shape[:2]
    n_pages = page_table.shape[1]
    past = n_pages * PAGE_SIZE
    total = past + t
    n_sel = min(TOPK_MAX, total // 4)
    ki_past = cache_ki[page_table].reshape(b, past, IDX_DIM)
    ki_all = jnp.concatenate([ki_past, ki.astype(ki_past.dtype)], axis=1)
    qpos = past + jnp.arange(t)
    sel, valid = indexer_topk(qi, wi, ki_all, qpos, jnp.arange(total), n_sel)
    is_past = (sel < past)[..., None, None]
    page = jnp.minimum(sel // PAGE_SIZE, n_pages - 1)
    phys = jnp.take_along_axis(page_table, page.reshape(b, -1), axis=1).reshape(sel.shape)
    flat = phys * PAGE_SIZE + sel % PAGE_SIZE
    new_idx = jnp.clip(sel - past, 0, t - 1)
    k_pool = cache_k.reshape(-1, N_KV_HEADS, HEAD_DIM)
    v_pool = cache_v.reshape(-1, N_KV_HEADS, HEAD_DIM)
    ks = jnp.where(is_past, k_pool[flat], gather_rows(k, new_idx).astype(k_pool.dtype))
    vs = jnp.where(is_past, v_pool[flat], gather_rows(v, new_idx).astype(v_pool.dtype))
    return sparse_attend(q, ks, vs, valid)


def memory_kv(mem, g_mem, w_mem_kv):
    b, m, _ = mem.shape
    kv = rmsnorm(mem, g_mem) @ w_mem_kv
    mk = kv[..., :MEM_WIDTH].reshape(b, m, MEM_HEADS, MEM_HEAD_DIM)
    mv = kv[..., MEM_WIDTH:].reshape(b, m, MEM_HEADS, MEM_HEAD_DIM)
    return mk, mv


def mem_attend(q, mk, mv):
    b, l = q.shape[:2]
    s = jnp.einsum('blhd,bmhd->bhlm', q.astype(jnp.float32), mk.astype(jnp.float32)) * (MEM_HEAD_DIM ** -0.5)
    pr = jax.nn.softmax(s, axis=-1)
    o = jnp.einsum('bhlm,bmhd->blhd', pr, mv.astype(jnp.float32))
    return o.reshape(b, l, MEM_WIDTH).astype(q.dtype)


def clamped_swiglu(u):
    u = u.astype(jnp.float32)
    glu = jnp.minimum(u[..., ::2], SWIGLU_LIMIT)
    lin = jnp.clip(u[..., 1::2], -SWIGLU_LIMIT, SWIGLU_LIMIT)
    return glu * jax.nn.sigmoid(SWIGLU_ALPHA * glu) * (lin + 1.0)


def moe_ffn(h, w_router, b_router, w1, b1, w2, b2):
    shp = h.shape
    hf = h.reshape(-1, D_MODEL)
    n = hf.shape[0]
    logits = hf.astype(jnp.float32) @ w_router.astype(jnp.float32) + b_router.astype(jnp.float32)
    top_v, top_e = lax.top_k(logits, TOP_K)
    gates = jax.nn.softmax(top_v, axis=-1)
    n_rows = n * TOP_K
    flat_e = top_e.reshape(-1)
    order = jnp.argsort(flat_e)
    sorted_e = flat_e[order]
    counts = jnp.bincount(flat_e, length=N_EXPERTS)
    padded = (counts + MOE_BLOCK - 1) // MOE_BLOCK * MOE_BLOCK
    pad_end = jnp.cumsum(padded)
    pad_start = pad_end - padded
    start = jnp.cumsum(counts) - counts
    dest_sorted = pad_start[sorted_e] + jnp.arange(n_rows) - start[sorted_e]
    n_blocks = -(-(n_rows + N_EXPERTS * (MOE_BLOCK - 1)) // MOE_BLOCK)
    row_tok = jnp.zeros((n_blocks * MOE_BLOCK,), jnp.int32).at[dest_sorted].set((order // TOP_K).astype(jnp.int32))
    block_start = jnp.arange(n_blocks) * MOE_BLOCK
    block_e = jnp.minimum(jnp.sum(pad_end[None, :] <= block_start[:, None], axis=1), N_EXPERTS - 1)

    def expert_block(args):
        tok, e = args
        xb = hf[tok]
        u = xb @ w1[e] + b1[e]
        return clamped_swiglu(u).astype(xb.dtype) @ w2[e] + b2[e]

    out = lax.map(expert_block, (row_tok.reshape(n_blocks, MOE_BLOCK), block_e)).reshape(-1, D_MODEL)
    dest = jnp.zeros((n_rows,), jnp.int32).at[order].set(dest_sorted.astype(jnp.int32)).reshape(n, TOP_K)
    y = jnp.einsum('nk,nkd->nd', gates, out[dest].astype(jnp.float32))
    return y.astype(h.dtype).reshape(shp)


def hybrid_layer(x, p, attn_fn, conv_prev, ssm_init, mem_k, mem_v):
    b, l, _ = x.shape
    h = rmsnorm(x, p['g_norm1'])
    z, xbc, dt_raw, q, k, v, qi, ki, wi, qm, gate_raw = split_in(h @ p['w_in'])
    k = k.reshape(b, l, N_KV_HEADS, HEAD_DIM)
    v = v.reshape(b, l, N_KV_HEADS, HEAD_DIM)
    o_ssd, conv_new, ssm_new = ssd_branch(z, xbc, dt_raw, conv_prev, ssm_init, p['conv_w'], p['conv_b'],
                                          p['dt_bias'], p['a_log'], p['d_skip'], p['g_ssd_norm'])
    o_attn = attn_fn(q.reshape(b, l, N_HEADS, HEAD_DIM), k, v, qi.reshape(b, l, IDX_HEADS, IDX_DIM), ki, wi)
    o_mem = mem_attend(qm.reshape(b, l, MEM_HEADS, MEM_HEAD_DIM), mem_k, mem_v)
    g = jax.nn.sigmoid((gate_raw + p['b_gate']).astype(jnp.float32)).reshape(b, l, N_BRANCH, D_MODEL)
    merged = (g[:, :, 0] * (o_ssd @ p['w_ssd_out']).astype(jnp.float32)
              + g[:, :, 1] * (o_attn @ p['w_attn_out']).astype(jnp.float32)
              + g[:, :, 2] * (o_mem @ p['w_mem_out']).astype(jnp.float32))
    x = x + merged.astype(x.dtype) @ p['w_out']
    x = x + moe_ffn(rmsnorm(x, p['g_norm2']), p['w_router'], p['b_router'], p['w_exp1'], p['b_exp1'],
                    p['w_exp2'], p['b_exp2'])
    return x, k, v, ki, conv_new, ssm_new


def setup_inputs(seed: int = 0) -> dict:
    key = jax.random.key(seed)
    keys = jax.random.split(key, 48)
    counter = iter(range(48))

    def nrm(shape, scale):
        return jax.random.normal(keys[next(counter)], shape, jnp.float32) * scale

    def gain(shape):
        return 1.0 + nrm(shape, 0.01)

    n_pages = PAST_LEN // PAGE_SIZE
    n_pool = (DEC_BATCH * n_pages * 5 + 3) // 4
    page_table = jax.random.permutation(keys[next(counter)], n_pool)[:DEC_BATCH * n_pages]
    page_table = page_table.reshape(DEC_BATCH, n_pages).astype(jnp.int32)
    u = jax.random.uniform(keys[next(counter)], (DEPTH, SSD_HEADS), jnp.float32)
    dt0 = jnp.exp(u * (math.log(0.1) - math.log(0.001)) + math.log(0.001))
    dt_bias = dt0 + jnp.log(-jnp.expm1(-dt0))
    a_log = jnp.log(jax.random.uniform(keys[next(counter)], (DEPTH, SSD_HEADS), jnp.float32, 1.0, 16.0))
    return {
        'x_prompt': nrm((BATCH, SEQ, D_MODEL), 1.0),
        'x_sample': nrm((DEC_BATCH, DEC_SEQ, D_MODEL), 1.0),
        'cache_k': nrm((DEPTH, n_pool, PAGE_SIZE, N_KV_HEADS, HEAD_DIM), 1.0),
        'cache_v': nrm((DEPTH, n_pool, PAGE_SIZE, N_KV_HEADS, HEAD_DIM), 1.0),
        'cache_idx_k': nrm((DEPTH, n_pool, PAGE_SIZE, IDX_DIM), 1.0),
        'state_conv': nrm((DEPTH, DEC_BATCH, CONV_WIDTH - 1, CONV_DIM), 1.0),
        'state_ssm': nrm((DEPTH, DEC_BATCH, SSD_HEADS, SSD_HEAD_DIM, D_STATE), 0.5),
        'cache_mem_k': nrm((DEPTH, DEC_BATCH, N_MEM, MEM_HEADS, MEM_HEAD_DIM), 1.0),
        'cache_mem_v': nrm((DEPTH, DEC_BATCH, N_MEM, MEM_HEADS, MEM_HEAD_DIM), 1.0),
        'page_table': page_table,
        'mem_prompt': nrm((BATCH, N_MEM, D_MODEL), 1.0),
        'g_norm1': gain((DEPTH, D_MODEL)),
        'w_in': nrm((DEPTH, D_MODEL, D_IN_PROJ), D_MODEL ** -0.5),
        'b_gate': nrm((DEPTH, N_BRANCH * D_MODEL), 0.01),
        'conv_w': nrm((DEPTH, CONV_WIDTH, CONV_DIM), 0.5),
        'conv_b': nrm((DEPTH, CONV_DIM), 0.01),
        'dt_bias': dt_bias,
        'a_log': a_log,
        'd_skip': 1.0 + nrm((DEPTH, SSD_HEADS), 0.1),
        'g_ssd_norm': gain((DEPTH, D_INNER)),
        'g_mem': gain((DEPTH, D_MODEL)),
        'w_mem_kv': nrm((DEPTH, D_MODEL, 2 * MEM_WIDTH), D_MODEL ** -0.5),
        'w_ssd_out': nrm((DEPTH, D_INNER, D_MODEL), D_INNER ** -0.5),
        'w_attn_out': nrm((DEPTH, N_HEADS * HEAD_DIM, D_MODEL), (N_HEADS * HEAD_DIM) ** -0.5),
        'w_mem_out': nrm((DEPTH, MEM_WIDTH, D_MODEL), MEM_WIDTH ** -0.5),
        'w_out': nrm((DEPTH, D_MODEL, D_MODEL), D_MODEL ** -0.5),
        'g_norm2': gain((DEPTH, D_MODEL)),
        'w_router': nrm((DEPTH, D_MODEL, N_EXPERTS), D_MODEL ** -0.5),
        'b_router': nrm((DEPTH, N_EXPERTS), 0.01),
        'w_exp1': nrm((DEPTH, N_EXPERTS, D_MODEL, 2 * D_FF), D_MODEL ** -0.5),
        'b_exp1': nrm((DEPTH, N_EXPERTS, 2 * D_FF), 0.01),
        'w_exp2': nrm((DEPTH, N_EXPERTS, D_FF, D_MODEL), D_FF ** -0.5),
        'b_exp2': nrm((DEPTH, N_EXPERTS, D_MODEL), 0.01),
        'g_final': gain((D_MODEL,)),
    }


def reference(x_prompt, x_sample, cache_k, cache_v, cache_idx_k, state_conv, state_ssm, cache_mem_k, cache_mem_v,
              page_table, mem_prompt, g_norm1, w_in, b_gate, conv_w, conv_b, dt_bias, a_log, d_skip, g_ssd_norm,
              g_mem, w_mem_kv, w_ssd_out, w_attn_out, w_mem_out, w_out, g_norm2, w_router, b_router, w_exp1,
              b_exp1, w_exp2, b_exp2, g_final):
    xp, xs = x_prompt, x_sample
    kp_l, vp_l, kip_l, cp_l, sp_l, mkp_l, mvp_l = [], [], [], [], [], [], []
    ks_l, vs_l, kis_l, cs_l, ss_l = [], [], [], [], []
    for l in range(DEPTH):
        p = dict(g_norm1=g_norm1[l], w_in=w_in[l], b_gate=b_gate[l], conv_w=conv_w[l], conv_b=conv_b[l],
                 dt_bias=dt_bias[l], a_log=a_log[l], d_skip=d_skip[l], g_ssd_norm=g_ssd_norm[l],
                 w_ssd_out=w_ssd_out[l], w_attn_out=w_attn_out[l], w_mem_out=w_mem_out[l], w_out=w_out[l],
                 g_norm2=g_norm2[l], w_router=w_router[l], b_router=b_router[l], w_exp1=w_exp1[l],
                 b_exp1=b_exp1[l], w_exp2=w_exp2[l], b_exp2=b_exp2[l])
        mk_p, mv_p = memory_kv(mem_prompt, g_mem[l], w_mem_kv[l])
        conv0 = jnp.zeros((xp.shape[0], CONV_WIDTH - 1, CONV_DIM), xp.dtype)
        ssm0 = jnp.zeros((xp.shape[0], SSD_HEADS, SSD_HEAD_DIM, D_STATE), xp.dtype)
        xp, kp, vp, kip, cp, sp = hybrid_layer(xp, p, dsa_prompt, conv0, ssm0, mk_p, mv_p)
        attn_s = functools.partial(dsa_sample, cache_k=cache_k[l], cache_v=cache_v[l], cache_ki=cache_idx_k[l],
                                   page_table=page_table)
        xs, ks, vs, kis, cs, ss = hybrid_layer(xs, p, attn_s, state_conv[l], state_ssm[l], cache_mem_k[l],
                                               cache_mem_v[l])
        kp_l.append(kp); vp_l.append(vp); kip_l.append(kip); cp_l.append(cp); sp_l.append(sp)
        mkp_l.append(mk_p); mvp_l.append(mv_p)
        ks_l.append(ks); vs_l.append(vs); kis_l.append(kis); cs_l.append(cs); ss_l.append(ss)
    y_prompt = rmsnorm(xp, g_final)
    y_sample = rmsnorm(xs, g_final)
    k_prompt = jnp.stack(kp_l)
    v_prompt = jnp.stack(vp_l)
    idx_k_prompt = jnp.stack(kip_l)
    conv_prompt = jnp.stack(cp_l)
    ssm_prompt = jnp.stack(sp_l)
    mem_k_prompt = jnp.stack(mkp_l)
    mem_v_prompt = jnp.stack(mvp_l)
    k_sample = jnp.stack(ks_l)
    v_sample = jnp.stack(vs_l)
    idx_k_sample = jnp.stack(kis_l)
    conv_sample = jnp.stack(cs_l)
    ssm_sample = jnp.stack(ss_l)
    return (y_prompt, y_sample, k_prompt, v_prompt, idx_k_prompt, conv_prompt, ssm_prompt, mem_k_prompt,
            mem_v_prompt, k_sample, v_sample, idx_k_sample, conv_sample, ssm_sample)
```

```python
import functools

import numpy as np
import jax
import jax.numpy as jnp
from jax import lax
from jax.experimental import pallas as pl
from jax.experimental.pallas import tpu as pltpu

F32 = jnp.float32
BF16 = jnp.bfloat16
I32 = jnp.int32
HIGHEST = lax.Precision.HIGHEST

D_MODEL = 1024
D_INNER = 2048
SSD_HEAD_DIM = 64
SSD_HEADS = 32
SSD_GROUPS = 4
D_STATE = 128
CONV_WIDTH = 4
CONV_DIM = D_INNER + 2 * SSD_GROUPS * D_STATE
SSD_CHUNK = 128
N_HEADS = 16
N_KV_HEADS = 4
HEAD_DIM = 64
IDX_HEADS = 8
IDX_DIM = 64
TOPK_MAX = 256
N_MEM = 256
MEM_HEADS = 4
MEM_HEAD_DIM = 256
MEM_WIDTH = MEM_HEADS * MEM_HEAD_DIM
N_EXPERTS = 32
TOP_K = 4
D_FF = D_MODEL
SWIGLU_LIMIT = 7.0
SWIGLU_ALPHA = 1.702
N_BRANCH = 3
EPS = 1e-6
PAGE_SIZE = 128
IN_WIDTHS = (D_INNER, CONV_DIM, SSD_HEADS, N_HEADS * HEAD_DIM, N_KV_HEADS * HEAD_DIM, N_KV_HEADS * HEAD_DIM,
             IDX_HEADS * IDX_DIM, IDX_DIM, IDX_HEADS, MEM_WIDTH, N_BRANCH * D_MODEL)

LANES = 128
SUBLANES = 8
VMEM_LIMIT = 56 * 1024 * 1024

OFF_XBC = 0
OFF_GATE = OFF_XBC + CONV_DIM
OFF_Z = OFF_GATE + N_BRANCH * D_MODEL
OFF_Q = OFF_Z + D_INNER
OFF_QM = OFF_Q + N_HEADS * HEAD_DIM
OFF_QI = OFF_QM + MEM_WIDTH
OFF_K = OFF_QI + IDX_HEADS * IDX_DIM
OFF_V = OFF_K + N_KV_HEADS * HEAD_DIM
OFF_SM = OFF_V + N_KV_HEADS * HEAD_DIM
SM_KI = 0
SM_DT = SM_KI + IDX_DIM
SM_WI = SM_DT + SSD_HEADS
IN_PROJ_TN = 1280
W_ALL = OFF_SM + 2 * LANES

NEG = -1e30
INT_MIN = np.int32(-2 ** 31)
INT_MAX = np.int32(2 ** 31 - 1)

MOE_BM = 512
MOE_T = 256


def _cparams(*sem):
    return pltpu.CompilerParams(dimension_semantics=sem, vmem_limit_bytes=VMEM_LIMIT)


def _nt_dot(a, b):
    return lax.dot_general(a, b, (((1,), (1,)), ((), ())), preferred_element_type=F32)


def _float_key(x):
    bits = lax.bitcast_convert_type(x, I32)
    return jnp.where(bits < 0, bits ^ INT_MAX, bits)


def _norm_matmul_body(x_ref, g_ref, w_ref, o_ref, h_ref):
    @pl.when(pl.program_id(1) == 0)
    def _():
        x = x_ref[...]
        h = x * lax.rsqrt(jnp.mean(x * x, axis=-1, keepdims=True) + EPS)
        h_ref[...] = (h * g_ref[...]).astype(BF16)

    o_ref[...] = jnp.dot(h_ref[...], w_ref[...], preferred_element_type=F32)


def norm_matmul(x, g, w, tm, tn):
    n, d = x.shape
    wn = w.shape[1]
    return pl.pallas_call(
        _norm_matmul_body,
        grid=(n // tm, wn // tn),
        in_specs=[pl.BlockSpec((tm, d), lambda i, j: (i, 0)),
                  pl.BlockSpec((1, d), lambda i, j: (0, 0)),
                  pl.BlockSpec((d, tn), lambda i, j: (0, j))],
        out_specs=pl.BlockSpec((tm, tn), lambda i, j: (i, j)),
        out_shape=jax.ShapeDtypeStruct((n, wn), F32),
        scratch_shapes=[pltpu.VMEM((tm, d), BF16)],
        compiler_params=_cparams("parallel", "arbitrary"),
        name="norm_matmul",
    )(x, g.reshape(1, d), w)


def _softplus(x):
    return jnp.maximum(x, 0.0) + jnp.log1p(jnp.exp(-jnp.abs(x)))


def _silu(x):
    return x * jax.nn.sigmoid(x)


def _expand_heads(v, q):
    lane = lax.broadcasted_iota(I32, (q, LANES), 1)
    cols = []
    for t in range(SSD_HEADS // 2):
        c0 = jnp.broadcast_to(v[:, SM_DT + 2 * t:SM_DT + 2 * t + 1], (q, LANES))
        c1 = jnp.broadcast_to(v[:, SM_DT + 2 * t + 1:SM_DT + 2 * t + 2], (q, LANES))
        cols.append(jnp.where(lane < SSD_HEAD_DIM, c0, c1))
    return jnp.concatenate(cols, axis=1)


def _ssd_body(xbc_ref, z_ref, sm_ref, convp_ref, init_ref, cw_ref, cb_ref, dtb_ref, aneg_ref, dsk_ref, gs_ref,
              y_ref, st_ref, xpad_ref, state_ref, *, rows_in, q):
    c = pl.program_id(1)
    pad = SUBLANES

    @pl.when(c == 0)
    def _():
        xpad_ref[0:pad, :] = convp_ref[0]
        state_ref[...] = init_ref[0]

    xpad_ref[pad:pad + rows_in, :] = xbc_ref[...]
    if rows_in < q:
        xpad_ref[pad + rows_in:pad + q, :] = jnp.zeros((q - rows_in, CONV_DIM), F32)

    acc = cb_ref[...]
    for j in range(CONV_WIDTH):
        lo = pad - (CONV_WIDTH - 1) + j
        acc = acc + xpad_ref[lo:lo + q, :] * cw_ref[j:j + 1, :]
    xc = _silu(acc)
    xpad_ref[0:pad, :] = xpad_ref[q:q + pad, :]

    xs = xc[:, :D_INNER]
    gn = SSD_GROUPS * D_STATE
    bm = xc[:, D_INNER:D_INNER + gn].astype(BF16)
    cm = xc[:, D_INNER + gn:].astype(BF16)

    sm = sm_ref[...]
    zz = z_ref[...]
    if rows_in < q:
        sm = jnp.concatenate([sm, jnp.zeros((q - rows_in, LANES), F32)], axis=0)
        zz = jnp.concatenate([zz, jnp.zeros((q - rows_in, D_INNER), F32)], axis=0)
    row = lax.broadcasted_iota(I32, (q, LANES), 0)
    dt = _softplus(sm + dtb_ref[...])
    if rows_in < q:
        dt = jnp.where(row < rows_in, dt, 0.0)
    a = dt * aneg_ref[...]
    tri = (lax.broadcasted_iota(I32, (q, q), 0) >= lax.broadcasted_iota(I32, (q, q), 1)).astype(F32)
    a_cs = jnp.dot(tri, a, precision=HIGHEST, preferred_element_type=F32)
    a_t = a_cs.T
    a_last = a_cs[q - 1:q, :]
    dte = jnp.exp(a_last - a_cs)
    e_in = jnp.exp(a_cs)

    xdt = xs * _expand_heads(dt, q)
    xdt_bf = xdt.astype(BF16)
    xw_bf = (xdt * _expand_heads(dte, q)).astype(BF16)
    ein_x = _expand_heads(e_in, q)

    causal = lax.broadcasted_iota(I32, (q, q), 0) >= lax.broadcasted_iota(I32, (q, q), 1)
    lane = lax.broadcasted_iota(I32, (q, LANES), 1)
    hpg = SSD_HEADS // SSD_GROUPS
    gw = hpg * SSD_HEAD_DIM
    y_parts = []
    for g in range(SSD_GROUPS):
        bg = bm[:, g * D_STATE:(g + 1) * D_STATE]
        cg = cm[:, g * D_STATE:(g + 1) * D_STATE]
        cb = _nt_dot(cg, bg)
        m_h = []
        for e in range(hpg):
            h = g * hpg + e
            col = a_cs[:, SM_DT + h:SM_DT + h + 1]
            rw = a_t[SM_DT + h:SM_DT + h + 1, :]
            decay = jnp.exp(jnp.where(causal, col - rw, -jnp.inf))
            m_h.append((cb * decay).astype(BF16))
        yd = []
        for t in range(hpg // 2):
            pair = g * (hpg // 2) + t
            slab = xdt_bf[:, pair * LANES:(pair + 1) * LANES]
            ya = jnp.dot(m_h[2 * t], slab, preferred_element_type=F32)
            yb = jnp.dot(m_h[2 * t + 1], slab, preferred_element_type=F32)
            yd.append(jnp.where(lane < SSD_HEAD_DIM, ya, yb))
        s_old = state_ref[g * gw:(g + 1) * gw, :]
        y_off = _nt_dot(cg, s_old.astype(BF16)) * ein_x[:, g * gw:(g + 1) * gw]
        y_parts.append(jnp.concatenate(yd, axis=1) + y_off)
        new = lax.dot_general(xw_bf[:, g * gw:(g + 1) * gw], bg, (((0,), (0,)), ((), ())),
                              preferred_element_type=F32)
        for e in range(hpg):
            h = g * hpg + e
            dec = jnp.exp(a_t[SM_DT + h:SM_DT + h + 1, q - 1:q])
            lo = e * SSD_HEAD_DIM
            state_ref[h * SSD_HEAD_DIM:(h + 1) * SSD_HEAD_DIM, :] = (
                s_old[lo:lo + SSD_HEAD_DIM, :] * dec + new[lo:lo + SSD_HEAD_DIM, :])

    y = jnp.concatenate(y_parts, axis=1) + dsk_ref[...] * xs
    y = y * _silu(zz)
    outs = []
    for g in range(SSD_GROUPS):
        yg = y[:, g * gw:(g + 1) * gw]
        outs.append(yg * lax.rsqrt(jnp.mean(yg * yg, axis=-1, keepdims=True) + EPS))
    y = jnp.concatenate(outs, axis=1) * gs_ref[...]
    y_ref[...] = y[:rows_in].astype(BF16)

    @pl.when(c == pl.num_programs(1) - 1)
    def _():
        st_ref[0] = state_ref[...]


def ssd(zall, batch, seq, conv_prev8, ssm_init, conv_w, conv_b, dtb_row, aneg_row, dsk_row, gs_row):
    q = SSD_CHUNK
    rows_in = min(seq, q)
    nch = seq // rows_in
    row_map = lambda b, c: (b * nch + c)
    const2 = lambda b, c: (0, 0)
    body = functools.partial(_ssd_body, rows_in=rows_in, q=q)
    return pl.pallas_call(
        body,
        grid=(batch, nch),
        in_specs=[pl.BlockSpec((rows_in, CONV_DIM), lambda b, c: (row_map(b, c), OFF_XBC // CONV_DIM)),
                  pl.BlockSpec((rows_in, D_INNER), lambda b, c: (row_map(b, c), OFF_Z // D_INNER)),
                  pl.BlockSpec((rows_in, LANES), lambda b, c: (row_map(b, c), OFF_SM // LANES)),
                  pl.BlockSpec((1, SUBLANES, CONV_DIM), lambda b, c: (b, 0, 0)),
                  pl.BlockSpec((1, D_INNER, D_STATE), lambda b, c: (b, 0, 0)),
                  pl.BlockSpec((CONV_WIDTH, CONV_DIM), const2),
                  pl.BlockSpec((1, CONV_DIM), const2),
                  pl.BlockSpec((1, LANES), const2),
                  pl.BlockSpec((1, LANES), const2),
                  pl.BlockSpec((1, D_INNER), const2),
                  pl.BlockSpec((1, D_INNER), const2)],
        out_specs=[pl.BlockSpec((rows_in, D_INNER), lambda b, c: (row_map(b, c), 0)),
                   pl.BlockSpec((1, D_INNER, D_STATE), lambda b, c: (b, 0, 0))],
        out_shape=[jax.ShapeDtypeStruct((batch * seq, D_INNER), BF16),
                   jax.ShapeDtypeStruct((batch, D_INNER, D_STATE), F32)],
        scratch_shapes=[pltpu.VMEM((q + 2 * SUBLANES, CONV_DIM), F32),
                        pltpu.VMEM((D_INNER, D_STATE), F32)],
        compiler_params=_cparams("parallel", "arbitrary"),
        name="ssd",
    )(zall, zall, zall, conv_prev8, ssm_init, conv_w, conv_b, dtb_row, aneg_row, dsk_row, gs_row)


def _mem_attn_body(q_ref, k_ref, v_ref, o_ref):
    for h in range(MEM_HEADS):
        sl = slice(h * MEM_HEAD_DIM, (h + 1) * MEM_HEAD_DIM)
        s = _nt_dot(q_ref[:, sl].astype(BF16), k_ref[:, sl].astype(BF16)) * (MEM_HEAD_DIM ** -0.5)
        m = jnp.max(s, axis=-1, keepdims=True)
        p = jnp.exp(s - m)
        p = p / jnp.sum(p, axis=-1, keepdims=True)
        o = jnp.dot(p.astype(BF16), v_ref[:, sl].astype(BF16), preferred_element_type=F32)
        o_ref[:, sl] = o.astype(BF16)


def mem_attn(zall, batch, seq, k_arr, k_col, v_arr, v_col, tm):
    nt = seq // tm
    return pl.pallas_call(
        _mem_attn_body,
        grid=(batch, nt),
        in_specs=[pl.BlockSpec((tm, MEM_WIDTH), lambda b, i: (b * nt + i, OFF_QM // MEM_WIDTH)),
                  pl.BlockSpec((N_MEM, MEM_WIDTH), lambda b, i: (b, k_col)),
                  pl.BlockSpec((N_MEM, MEM_WIDTH), lambda b, i: (b, v_col))],
        out_specs=pl.BlockSpec((tm, MEM_WIDTH), lambda b, i: (b * nt + i, 0)),
        out_shape=jax.ShapeDtypeStruct((batch * seq, MEM_WIDTH), BF16),
        compiler_params=_cparams("parallel", "arbitrary"),
        name="mem_attn",
    )(zall, k_arr, v_arr)


def _kth_largest_key(count_ge, rows, n_sel):
    def bit_body(t, ans):
        cand = ans | jnp.left_shift(jnp.int32(1), 31 - t)
        cnt = count_ge(cand ^ INT_MIN)
        return jnp.where(cnt >= n_sel, cand, ans)

    ans = lax.fori_loop(0, 32, bit_body, jnp.zeros((rows, 1), I32))
    return ans ^ INT_MIN


def _tie_cut(count_eq_below, need, rows, nbits):
    def bit_body(t, lo):
        cand = lo | jnp.left_shift(jnp.int32(1), nbits - 1 - t)
        cnt = count_eq_below(cand)
        return jnp.where(cnt < need, cand, lo)

    return lax.fori_loop(0, nbits, bit_body, jnp.zeros((rows, 1), I32))


def _select_bias(key, kpos, thr, cut, visible):
    sel = (key > thr) | ((key == thr) & (kpos <= cut))
    return jnp.where(sel & visible, 0.0, NEG)


def _dsa_prompt_body(q_ref, qi_ref, smq_ref, k_ref, v_ref, smk_ref, o_ref,
                     kh_ref, vh_ref, kis_ref, qs_ref, keys_ref, bias_ref, *, tq, kc, seq, n_sel):
    i = pl.program_id(1)
    rchunk = 512 if seq % 512 == 0 else seq

    @pl.when(i == 0)
    def _():
        def cast_rows(r, carry):
            rs = pl.ds(pl.multiple_of(r * rchunk, rchunk), rchunk)
            kk = k_ref[rs, :]
            vv = v_ref[rs, :]
            for h in range(N_KV_HEADS):
                kh_ref[h, rs, :] = kk[:, h * HEAD_DIM:(h + 1) * HEAD_DIM].astype(BF16)
                vh_ref[h, rs, :] = vv[:, h * HEAD_DIM:(h + 1) * HEAD_DIM].astype(BF16)
            kis_ref[rs, :] = smk_ref[rs, SM_KI:SM_KI + IDX_DIM].astype(BF16)
            return carry
        lax.fori_loop(0, seq // rchunk, cast_rows, 0)

    nkc = (i * tq + tq - 1) // kc + 1
    qpos = i * tq + lax.broadcasted_iota(I32, (tq, kc), 0)
    lane = lax.broadcasted_iota(I32, (tq, kc), 1)

    for h in range(IDX_HEADS):
        qs_ref[h * tq:(h + 1) * tq, :] = qi_ref[:, h * IDX_DIM:(h + 1) * IDX_DIM].astype(BF16)
    w = smq_ref[:, SM_WI:SM_WI + IDX_HEADS] * (IDX_HEADS ** -0.5)

    def score_chunk(c, carry):
        ks = pl.ds(pl.multiple_of(c * kc, kc), kc)
        d = _nt_dot(qs_ref[...], kis_ref[ks, :])
        sc = jnp.zeros((tq, kc), F32)
        for h in range(IDX_HEADS):
            sc = sc + jnp.maximum(d[h * tq:(h + 1) * tq, :] * (IDX_DIM ** -0.5), 0.0) * w[:, h:h + 1]
        key = _float_key(sc + 0.0)
        keys_ref[c] = jnp.where(c * kc + lane <= qpos, key, INT_MIN)
        return carry
    lax.fori_loop(0, nkc, score_chunk, 0)

    def count(pred):
        def body(c, acc):
            return acc + jnp.where(pred(keys_ref[c], c * kc + lane), 1.0, 0.0)
        acc = lax.fori_loop(0, nkc, body, jnp.zeros((tq, kc), F32))
        return jnp.sum(acc, axis=1, keepdims=True)

    thr = _kth_largest_key(lambda cand: count(lambda key, kpos: key >= cand), tq, n_sel)
    n_gt = count(lambda key, kpos: key > thr)
    n_eq = count(lambda key, kpos: key == thr)
    need = n_sel - n_gt
    excess = jnp.max(jnp.where((n_eq > need) & (thr != INT_MIN), 1.0, 0.0))
    nbits = max(1, int(seq - 1).bit_length())
    cut = lax.cond(
        excess > 0.0,
        lambda: _tie_cut(lambda cand: count(lambda key, kpos: (key == thr) & (kpos < cand)), need, tq, nbits),
        lambda: jnp.full((tq, 1), INT_MAX, I32))

    def bias_chunk(c, carry):
        kpos = c * kc + lane
        bias_ref[c] = _select_bias(keys_ref[c], kpos, thr, cut, kpos <= qpos)
        return carry
    lax.fori_loop(0, nkc, bias_chunk, 0)

    grp = N_HEADS // N_KV_HEADS
    for kh in range(N_KV_HEADS):
        qsk = jnp.concatenate(
            [q_ref[:, (kh * grp + g) * HEAD_DIM:(kh * grp + g + 1) * HEAD_DIM] for g in range(grp)], axis=0
        ).astype(BF16)

        def att_chunk(c, carry, kh=kh, qsk=qsk):
            m, l, acc = carry
            ks = pl.ds(pl.multiple_of(c * kc, kc), kc)
            b = bias_ref[c]
            s = _nt_dot(qsk, kh_ref[kh, ks, :]) * (HEAD_DIM ** -0.5) + jnp.concatenate([b] * grp, axis=0)
            m_new = jnp.maximum(m, jnp.max(s, axis=1, keepdims=True))
            p = jnp.exp(s - m_new)
            alpha = jnp.exp(m - m_new)
            l = alpha * l + jnp.sum(p, axis=1, keepdims=True)
            acc = alpha * acc + jnp.dot(p.astype(BF16), vh_ref[kh, ks, :], preferred_element_type=F32)
            return m_new, l, acc

        m, l, acc = lax.fori_loop(
            0, nkc, att_chunk,
            (jnp.full((grp * tq, 1), NEG, F32), jnp.zeros((grp * tq, 1), F32), jnp.zeros((grp * tq, HEAD_DIM), F32)))
        o = acc / l
        o_ref[:, kh * grp * HEAD_DIM:(kh + 1) * grp * HEAD_DIM] = jnp.concatenate(
            [o[g * tq:(g + 1) * tq, :] for g in range(grp)], axis=1).astype(BF16)


def dsa_prompt(zall, batch, seq):
    tq = 128
    kc = 256
    nq = seq // tq
    n_sel = min(TOPK_MAX, seq // 4)
    kvw = N_KV_HEADS * HEAD_DIM
    body = functools.partial(_dsa_prompt_body, tq=tq, kc=kc, seq=seq, n_sel=n_sel)
    return pl.pallas_call(
        body,
        grid=(batch, nq),
        in_specs=[pl.BlockSpec((tq, N_HEADS * HEAD_DIM), lambda b, i: (b * nq + i, OFF_Q // (N_HEADS * HEAD_DIM))),
                  pl.BlockSpec((tq, IDX_HEADS * IDX_DIM), lambda b, i: (b * nq + i, OFF_QI // (IDX_HEADS * IDX_DIM))),
                  pl.BlockSpec((tq, LANES), lambda b, i: (b * nq + i, OFF_SM // LANES)),
                  pl.BlockSpec((seq, kvw), lambda b, i: (b, OFF_K // kvw)),
                  pl.BlockSpec((seq, kvw), lambda b, i: (b, OFF_V // kvw)),
                  pl.BlockSpec((seq, LANES), lambda b, i: (b, OFF_SM // LANES))],
        out_specs=pl.BlockSpec((tq, N_HEADS * HEAD_DIM), lambda b, i: (b * nq + i, 0)),
        out_shape=jax.ShapeDtypeStruct((batch * seq, N_HEADS * HEAD_DIM), BF16),
        scratch_shapes=[pltpu.VMEM((N_KV_HEADS, seq, HEAD_DIM), BF16),
                        pltpu.VMEM((N_KV_HEADS, seq, HEAD_DIM), BF16),
                        pltpu.VMEM((seq, IDX_DIM), BF16),
                        pltpu.VMEM((IDX_HEADS * tq, IDX_DIM), BF16),
                        pltpu.VMEM((seq // kc, tq, kc), I32),
                        pltpu.VMEM((seq // kc, tq, kc), F32)],
        compiler_params=_cparams("parallel", "arbitrary"),
        name="dsa_prompt",
    )(zall, zall, zall, zall, zall, zall)


def _dsa_s_score_body(pt_ref, qs_ref, w_ref, *refs, pg):
    ki_refs, o_ref = refs[:pg], refs[pg]
    qs = qs_ref[0]
    wcol = w_ref[0] * (IDX_HEADS ** -0.5)
    t = qs.shape[0] // IDX_HEADS
    for p in range(pg):
        d = _nt_dot(qs, ki_refs[p][0].astype(BF16))
        r = jnp.maximum(d * (IDX_DIM ** -0.5), 0.0) * wcol
        sc = r[0:t, :]
        for h in range(1, IDX_HEADS):
            sc = sc + r[h * t:(h + 1) * t, :]
        o_ref[0, :, p * PAGE_SIZE:(p + 1) * PAGE_SIZE] = sc + 0.0


def _dsa_s_select_body(sc_ref, qs_ref, w_ref, smn_ref, o_ref, *, t, past, n_sel):
    qs = qs_ref[0]
    wcol = w_ref[0] * (IDX_HEADS ** -0.5)
    ki_new = smn_ref[:, SM_KI:SM_KI + IDX_DIM].astype(BF16)
    ki_new = jnp.concatenate([ki_new, jnp.zeros((LANES - t, IDX_DIM), BF16)], axis=0)
    d = _nt_dot(qs, ki_new)
    r = jnp.maximum(d * (IDX_DIM ** -0.5), 0.0) * wcol
    sc_new = r[0:t, :]
    for h in range(1, IDX_HEADS):
        sc_new = sc_new + r[h * t:(h + 1) * t, :]
    lane_n = lax.broadcasted_iota(I32, (t, LANES), 1)
    vis_n = lane_n <= lax.broadcasted_iota(I32, (t, LANES), 0)
    key_n = jnp.where(vis_n, _float_key(sc_new + 0.0), INT_MIN)
    key_p = _float_key(sc_ref[0])
    pos_p = lax.broadcasted_iota(I32, (t, past), 1)
    pos_n = past + lane_n

    def count(pred):
        return (jnp.sum(jnp.where(pred(key_p, pos_p), 1.0, 0.0), axis=1, keepdims=True)
                + jnp.sum(jnp.where(pred(key_n, pos_n), 1.0, 0.0), axis=1, keepdims=True))

    thr = _kth_largest_key(lambda cand: count(lambda key, kpos: key >= cand), t, n_sel)
    need = n_sel - count(lambda key, kpos: key > thr)
    nbits = max(1, int(past + t - 1).bit_length())
    cut = _tie_cut(lambda cand: count(lambda key, kpos: (key == thr) & (kpos < cand)), need, t, nbits)
    o_ref[0, :, 0:past] = _select_bias(key_p, pos_p, thr, cut, pos_p >= 0)
    o_ref[0, :, past:past + LANES] = _select_bias(key_n, pos_n, thr, cut, vis_n)


def _dsa_s_attn_body(pt_ref, qbd_ref, bias_ref, biasn_ref, kn_ref, vn_ref, *refs, pg, t):
    k_refs, v_refs = refs[:pg], refs[pg:2 * pg]
    o_ref, m_ref, l_ref, acc_ref = refs[2 * pg:]
    j = pl.program_id(1)
    rows = qbd_ref.shape[1]
    rep = rows // t

    @pl.when(j == 0)
    def _():
        m_ref[...] = jnp.full(m_ref.shape, NEG, F32)
        l_ref[...] = jnp.zeros(l_ref.shape, F32)
        acc_ref[...] = jnp.zeros(acc_ref.shape, F32)

    qbd = qbd_ref[0]

    def update(kcat, vcat, bias):
        s = _nt_dot(qbd, kcat) * (HEAD_DIM ** -0.5) + jnp.concatenate([bias] * rep, axis=0)
        m = m_ref[...]
        m_new = jnp.maximum(m, jnp.max(s, axis=1, keepdims=True))
        p = jnp.exp(s - m_new)
        alpha = jnp.exp(m - m_new)
        l_ref[...] = alpha * l_ref[...] + jnp.sum(p, axis=1, keepdims=True)
        acc_ref[...] = alpha * acc_ref[...] + jnp.dot(p.astype(BF16), vcat, preferred_element_type=F32)
        m_ref[...] = m_new

    kcat = jnp.concatenate([r[0] for r in k_refs], axis=0).astype(BF16)
    vcat = jnp.concatenate([r[0] for r in v_refs], axis=0).astype(BF16)
    update(kcat, vcat, bias_ref[0])

    @pl.when(j == pl.num_programs(1) - 1)
    def _():
        kvw = N_KV_HEADS * HEAD_DIM
        zpad = jnp.zeros((LANES - t, kvw), BF16)
        update(jnp.concatenate([kn_ref[...].astype(BF16), zpad], axis=0),
               jnp.concatenate([vn_ref[...].astype(BF16), zpad], axis=0), biasn_ref[0])
        o_ref[0] = acc_ref[...] / l_ref[...]


def dsa_sample(zs, batch, t, cache_k, cache_v, cache_ki, page_table):
    n_pages = page_table.shape[1]
    past = n_pages * PAGE_SIZE
    n_sel = min(TOPK_MAX, (past + t) // 4)
    pg = 8
    nj = n_pages // pg
    n_pool = cache_k.shape[0]
    kvw = N_KV_HEADS * HEAD_DIM
    grp = N_HEADS // N_KV_HEADS

    qi = zs[:, OFF_QI:OFF_QI + IDX_HEADS * IDX_DIM].reshape(batch, t, IDX_HEADS, IDX_DIM)
    qs = jnp.transpose(qi, (0, 2, 1, 3)).reshape(batch, IDX_HEADS * t, IDX_DIM).astype(BF16)
    wi = zs[:, OFF_SM + SM_WI:OFF_SM + SM_WI + IDX_HEADS].reshape(batch, t, IDX_HEADS)
    wcol = jnp.transpose(wi, (0, 2, 1)).reshape(batch, IDX_HEADS * t, 1)

    def page_spec(shape, p):
        return pl.BlockSpec(shape, lambda b, j, pt: (pt[b, j * pg + p],) + (0,) * (len(shape) - 1))

    scores = pl.pallas_call(
        functools.partial(_dsa_s_score_body, pg=pg),
        grid_spec=pltpu.PrefetchScalarGridSpec(
            num_scalar_prefetch=1,
            grid=(batch, nj),
            in_specs=[pl.BlockSpec((1, IDX_HEADS * t, IDX_DIM), lambda b, j, pt: (b, 0, 0)),
                      pl.BlockSpec((1, IDX_HEADS * t, 1), lambda b, j, pt: (b, 0, 0))]
                     + [page_spec((1, PAGE_SIZE, IDX_DIM), p) for p in range(pg)],
            out_specs=pl.BlockSpec((1, t, pg * PAGE_SIZE), lambda b, j, pt: (b, 0, j))),
        out_shape=jax.ShapeDtypeStruct((batch, t, past), F32),
        compiler_params=_cparams("parallel", "arbitrary"),
        name="dsa_sample_scores",
    )(page_table, qs, wcol, *([cache_ki] * pg))

    bias = pl.pallas_call(
        functools.partial(_dsa_s_select_body, t=t, past=past, n_sel=n_sel),
        grid=(batch,),
        in_specs=[pl.BlockSpec((1, t, past), lambda b: (b, 0, 0)),
                  pl.BlockSpec((1, IDX_HEADS * t, IDX_DIM), lambda b: (b, 0, 0)),
                  pl.BlockSpec((1, IDX_HEADS * t, 1), lambda b: (b, 0, 0)),
                  pl.BlockSpec((t, LANES), lambda b: (b, OFF_SM // LANES))],
        out_specs=pl.BlockSpec((1, t, past + LANES), lambda b: (b, 0, 0)),
        out_shape=jax.ShapeDtypeStruct((batch, t, past + LANES), F32),
        compiler_params=_cparams("parallel"),
        name="dsa_sample_select",
    )(scores, qs, wcol, zs)

    q = zs[:, OFF_Q:OFF_Q + N_HEADS * HEAD_DIM].reshape(batch, t, N_KV_HEADS, grp, HEAD_DIM)
    q = jnp.transpose(q, (0, 2, 3, 1, 4))
    eye = jnp.eye(N_KV_HEADS, dtype=F32)
    qbd = (q[:, :, :, :, None, :] * eye[None, :, None, None, :, None]).reshape(batch, N_HEADS * t, kvw).astype(BF16)

    ck = cache_k.reshape(n_pool, PAGE_SIZE, kvw)
    cv = cache_v.reshape(n_pool, PAGE_SIZE, kvw)
    rows = N_HEADS * t
    out = pl.pallas_call(
        functools.partial(_dsa_s_attn_body, pg=pg, t=t),
        grid_spec=pltpu.PrefetchScalarGridSpec(
            num_scalar_prefetch=1,
            grid=(batch, nj),
            in_specs=[pl.BlockSpec((1, rows, kvw), lambda b, j, pt: (b, 0, 0)),
                      pl.BlockSpec((1, t, pg * PAGE_SIZE), lambda b, j, pt: (b, 0, j)),
                      pl.BlockSpec((1, t, LANES), lambda b, j, pt: (b, 0, past // LANES)),
                      pl.BlockSpec((t, kvw), lambda b, j, pt: (b, OFF_K // kvw)),
                      pl.BlockSpec((t, kvw), lambda b, j, pt: (b, OFF_V // kvw))]
                     + [page_spec((1, PAGE_SIZE, kvw), p) for p in range(pg)]
                     + [page_spec((1, PAGE_SIZE, kvw), p) for p in range(pg)],
            out_specs=pl.BlockSpec((1, rows, kvw), lambda b, j, pt: (b, 0, 0)),
            scratch_shapes=[pltpu.VMEM((rows, 1), F32), pltpu.VMEM((rows, 1), F32), pltpu.VMEM((rows, kvw), F32)]),
        out_shape=jax.ShapeDtypeStruct((batch, rows, kvw), F32),
        compiler_params=_cparams("parallel", "arbitrary"),
        name="dsa_sample_attn",
    )(page_table, qbd, bias, bias, zs, zs, *([ck] * pg), *([cv] * pg))

    o = out.reshape(batch, N_KV_HEADS, grp, t, N_KV_HEADS, HEAD_DIM)
    o = jnp.stack([o[:, kh, :, :, kh, :] for kh in range(N_KV_HEADS)], axis=1)
    return jnp.transpose(o, (0, 3, 1, 2, 4)).reshape(batch * t, N_HEADS * HEAD_DIM).astype(BF16)


def _merge_body(x_ref, gate_ref, ys_ref, oa_ref, om_ref, bg_ref, ws_ref, wa_ref, wm_ref, wo_ref, g2_ref, wr_ref, br_ref,
                x1_ref, h2_ref, te_ref, gw_ref):
    gates = jax.nn.sigmoid(gate_ref[...] + bg_ref[...])
    merged = (gates[:, 0:D_MODEL] * jnp.dot(ys_ref[...], ws_ref[...], preferred_element_type=F32)
              + gates[:, D_MODEL:2 * D_MODEL] * jnp.dot(oa_ref[...], wa_ref[...], preferred_element_type=F32)
              + gates[:, 2 * D_MODEL:] * jnp.dot(om_ref[...], wm_ref[...], preferred_element_type=F32))
    x1 = x_ref[...] + jnp.dot(merged.astype(BF16), wo_ref[...], preferred_element_type=F32)
    x1_ref[...] = x1
    h2 = x1 * lax.rsqrt(jnp.mean(x1 * x1, axis=-1, keepdims=True) + EPS)
    h2 = h2 * g2_ref[...]
    h2_ref[...] = h2
    logits = jnp.dot(h2, wr_ref[...], precision=HIGHEST, preferred_element_type=F32) + br_ref[...]
    lane = lax.broadcasted_iota(I32, logits.shape, 1)
    te = jnp.zeros(logits.shape, I32)
    tv = []
    for k in range(TOP_K):
        m = jnp.max(logits, axis=1, keepdims=True)
        idx = jnp.min(jnp.where(logits == m, lane, LANES), axis=1, keepdims=True)
        te = jnp.where(lane == k, idx, te)
        tv.append(m)
        logits = jnp.where(lane == idx, -jnp.inf, logits)
    ex = [jnp.exp(v - tv[0]) for v in tv]
    den = ex[0] + ex[1] + ex[2] + ex[3]
    gw = jnp.zeros(logits.shape, F32)
    for k in range(TOP_K):
        gw = jnp.where(lane == k, ex[k] / den, gw)
    te_ref[...] = te
    gw_ref[...] = gw


def merge(x, zall, ys, oa, om, bg, ws, wa, wm, wo, g2, wr, br, tm):
    n = x.shape[0]
    gw3 = N_BRANCH * D_MODEL
    row = lambda i: (i, 0)
    const = lambda i: (0, 0)
    return pl.pallas_call(
        _merge_body,
        grid=(n // tm,),
        in_specs=[pl.BlockSpec((tm, D_MODEL), row),
                  pl.BlockSpec((tm, gw3), lambda i: (i, OFF_GATE // gw3)),
                  pl.BlockSpec((tm, D_INNER), row),
                  pl.BlockSpec((tm, N_HEADS * HEAD_DIM), row),
                  pl.BlockSpec((tm, MEM_WIDTH), row),
                  pl.BlockSpec((1, gw3), const),
                  pl.BlockSpec((D_INNER, D_MODEL), const),
                  pl.BlockSpec((N_HEADS * HEAD_DIM, D_MODEL), const),
                  pl.BlockSpec((MEM_WIDTH, D_MODEL), const),
                  pl.BlockSpec((D_MODEL, D_MODEL), const),
                  pl.BlockSpec((1, D_MODEL), const),
                  pl.BlockSpec((D_MODEL, LANES), const),
                  pl.BlockSpec((1, LANES), const)],
        out_specs=[pl.BlockSpec((tm, D_MODEL), row), pl.BlockSpec((tm, D_MODEL), row),
                   pl.BlockSpec((tm, LANES), row), pl.BlockSpec((tm, LANES), row)],
        out_shape=[jax.ShapeDtypeStruct((n, D_MODEL), F32), jax.ShapeDtypeStruct((n, D_MODEL), F32),
                   jax.ShapeDtypeStruct((n, LANES), I32), jax.ShapeDtypeStruct((n, LANES), F32)],
        compiler_params=_cparams("parallel"),
        name="merge",
    )(x, zall, ys, oa, om, bg, ws, wa, wm, wo, g2, wr, br)


def _moe_pos_body(te_ref, pos_ref, cnt_ref, carry_ref):
    i = pl.program_id(0)
    tt = te_ref.shape[0]

    @pl.when(i == 0)
    def _():
        carry_ref[...] = jnp.zeros(carry_ref.shape, F32)

    te = te_ref[...]
    lane = lax.broadcasted_iota(I32, (tt, LANES), 1)
    onehot = [lane == te[:, k:k + 1] for k in range(TOP_K)]
    msum = jnp.zeros((tt, LANES), F32)
    for k in range(TOP_K):
        msum = msum + jnp.where(onehot[k], 1.0, 0.0)
    strict = (lax.broadcasted_iota(I32, (tt, tt), 0) > lax.broadcasted_iota(I32, (tt, tt), 1))
    prefix = jnp.dot(jnp.where(strict, 1.0, 0.0).astype(BF16), msum.astype(BF16), preferred_element_type=F32)
    prefix = prefix + carry_ref[0:1, :]
    pos = jnp.zeros((tt, LANES), F32)
    for k in range(TOP_K):
        pk = jnp.sum(jnp.where(onehot[k], prefix, 0.0), axis=1, keepdims=True)
        pos = jnp.where(lane == k, pk, pos)
    pos_ref[...] = pos
    carry_ref[...] = carry_ref[...] + jnp.sum(msum, axis=0, keepdims=True)
    cnt_ref[...] = carry_ref[...]


def _moe_dest_body(te_ref, pos_ref, cnt_ref, dest_ref, be_ref, nu_ref, *, bm):
    tt = te_ref.shape[0]
    cnt = cnt_ref[...]
    padded = jnp.floor((cnt + (bm - 1)) * (1.0 / bm)) * bm
    upper = (lax.broadcasted_iota(I32, (LANES, LANES), 0) < lax.broadcasted_iota(I32, (LANES, LANES), 1))
    pad_start = jnp.dot(padded, jnp.where(upper, 1.0, 0.0), precision=HIGHEST, preferred_element_type=F32)
    pad_end = pad_start + padded
    te = te_ref[...]
    pos = pos_ref[...]
    lane = lax.broadcasted_iota(I32, (tt, LANES), 1)
    dest = jnp.zeros((tt, LANES), F32)
    for k in range(TOP_K):
        ps = jnp.sum(jnp.where(lane == te[:, k:k + 1], pad_start[0:1, :], 0.0), axis=1, keepdims=True)
        dest = jnp.where(lane == k, ps + pos[:, k:k + 1], dest)
    dest_ref[...] = dest.astype(I32)
    nb = be_ref.shape[0]
    bstart = (lax.broadcasted_iota(I32, (nb, LANES), 0) * bm).astype(F32)
    lane_b = lax.broadcasted_iota(I32, (nb, LANES), 1)
    done = jnp.where((pad_end[0:1, :] <= bstart) & (lane_b < N_EXPERTS), 1.0, 0.0)
    be = jnp.minimum(jnp.sum(done, axis=1, keepdims=True), N_EXPERTS - 1.0)
    be_ref[...] = jnp.broadcast_to(be, (nb, LANES)).astype(I32)
    total = jnp.sum(padded[0:1, :], axis=1, keepdims=True)
    nu_ref[...] = jnp.broadcast_to(total * (1.0 / bm), nu_ref.shape).astype(I32)


def _moe_dispatch_body(dest_ref, h_ref, xs_in_ref, xs_ref, sem):
    del xs_in_ref
    tt = h_ref.shape[0]

    def copy(r, k):
        d = dest_ref[r * TOP_K + k]
        return pltpu.make_async_copy(h_ref.at[pl.ds(r, 1), :], xs_ref.at[pl.ds(d, 1), :], sem)

    def issue(r, carry):
        for k in range(TOP_K):
            copy(r, k).start()
        return carry
    lax.fori_loop(0, tt, issue, 0)

    def drain(r, carry):
        for k in range(TOP_K):
            copy(r, k).wait()
        return carry
    lax.fori_loop(0, tt, drain, 0)


def _moe_expert_body(be_ref, nu_ref, xs_ref, w1_ref, b1_ref, w2_ref, b2_ref, o_ref):
    i = pl.program_id(0)

    @pl.when(i < nu_ref[0])
    def _():
        u = jnp.dot(xs_ref[...].astype(BF16), w1_ref[0], preferred_element_type=F32) + b1_ref[0]
        glu = jnp.minimum(u[:, :D_FF], SWIGLU_LIMIT)
        lin = jnp.clip(u[:, D_FF:], -SWIGLU_LIMIT, SWIGLU_LIMIT)
        act = glu * jax.nn.sigmoid(SWIGLU_ALPHA * glu) * (lin + 1.0)
        o_ref[...] = jnp.dot(act.astype(BF16), w2_ref[0], preferred_element_type=F32) + b2_ref[0]

    @pl.when(i >= nu_ref[0])
    def _():
        o_ref[...] = jnp.zeros(o_ref.shape, F32)


def _moe_combine_body(dest_ref, gw_ref, x1_ref, gf_ref, os_ref, y_ref, buf_ref, sem):
    tt = x1_ref.shape[0]

    def copy(r, k):
        d = dest_ref[r * TOP_K + k]
        return pltpu.make_async_copy(os_ref.at[pl.ds(d, 1), :], buf_ref.at[k, pl.ds(r, 1), :], sem)

    def issue(r, carry):
        for k in range(TOP_K):
            copy(r, k).start()
        return carry
    lax.fori_loop(0, tt, issue, 0)

    def drain(r, carry):
        for k in range(TOP_K):
            copy(r, k).wait()
        return carry
    lax.fori_loop(0, tt, drain, 0)

    gw = gw_ref[...]
    y = gw[:, 0:1] * buf_ref[0]
    for k in range(1, TOP_K):
        y = y + gw[:, k:k + 1] * buf_ref[k]
    x2 = x1_ref[...] + y
    out = x2 * lax.rsqrt(jnp.mean(x2 * x2, axis=-1, keepdims=True) + EPS)
    y_ref[...] = out * gf_ref[...]


def moe_and_final_norm(x1, h2, te, gw, w1p, b1p, w2, b2, g_final):
    n = x1.shape[0]
    tt = MOE_T
    bm = MOE_BM
    nb = -(-(n * TOP_K + N_EXPERTS * (bm - 1)) // bm)
    nbp = -(-nb // SUBLANES) * SUBLANES
    row = lambda i: (i, 0)
    const = lambda i: (0, 0)

    pos, cnt = pl.pallas_call(
        _moe_pos_body,
        grid=(n // tt,),
        in_specs=[pl.BlockSpec((tt, LANES), row)],
        out_specs=[pl.BlockSpec((tt, LANES), row), pl.BlockSpec((SUBLANES, LANES), const)],
        out_shape=[jax.ShapeDtypeStruct((n, LANES), F32), jax.ShapeDtypeStruct((SUBLANES, LANES), F32)],
        scratch_shapes=[pltpu.VMEM((SUBLANES, LANES), F32)],
        compiler_params=_cparams("arbitrary"),
        name="moe_positions",
    )(te)

    dest, be, nu = pl.pallas_call(
        functools.partial(_moe_dest_body, bm=bm),
        grid=(n // tt,),
        in_specs=[pl.BlockSpec((tt, LANES), row), pl.BlockSpec((tt, LANES), row),
                  pl.BlockSpec((SUBLANES, LANES), const)],
        out_specs=[pl.BlockSpec((tt, LANES), row), pl.BlockSpec((nbp, LANES), const),
                   pl.BlockSpec((SUBLANES, LANES), const)],
        out_shape=[jax.ShapeDtypeStruct((n, LANES), I32), jax.ShapeDtypeStruct((nbp, LANES), I32),
                   jax.ShapeDtypeStruct((SUBLANES, LANES), I32)],
        compiler_params=_cparams("arbitrary"),
        name="moe_destinations",
    )(te, pos, cnt)
    dest_flat = dest[:, :TOP_K].reshape(n * TOP_K)
    block_e = be[:nb, 0]
    n_used = nu[0, 0:1]

    xs = pl.pallas_call(
        _moe_dispatch_body,
        grid=(n // tt,),
        in_specs=[pl.BlockSpec((tt * TOP_K,), lambda i: (i,), memory_space=pltpu.SMEM),
                  pl.BlockSpec((tt, D_MODEL), row),
                  pl.BlockSpec(memory_space=pl.ANY)],
        out_specs=pl.BlockSpec(memory_space=pl.ANY),
        out_shape=jax.ShapeDtypeStruct((nb * bm, D_MODEL), F32),
        scratch_shapes=[pltpu.SemaphoreType.DMA(())],
        input_output_aliases={2: 0},
        compiler_params=_cparams("arbitrary"),
        name="moe_dispatch",
    )(dest_flat, h2, jnp.zeros((nb * bm, D_MODEL), F32))

    out_sorted = pl.pallas_call(
        _moe_expert_body,
        grid_spec=pltpu.PrefetchScalarGridSpec(
            num_scalar_prefetch=2,
            grid=(nb,),
            in_specs=[pl.BlockSpec((bm, D_MODEL), lambda i, be_, nu_: (i, 0)),
                      pl.BlockSpec((1, D_MODEL, 2 * D_FF), lambda i, be_, nu_: (be_[i], 0, 0)),
                      pl.BlockSpec((1, 1, 2 * D_FF), lambda i, be_, nu_: (be_[i], 0, 0)),
                      pl.BlockSpec((1, D_FF, D_MODEL), lambda i, be_, nu_: (be_[i], 0, 0)),
                      pl.BlockSpec((1, 1, D_MODEL), lambda i, be_, nu_: (be_[i], 0, 0))],
            out_specs=pl.BlockSpec((bm, D_MODEL), lambda i, be_, nu_: (i, 0))),
        out_shape=jax.ShapeDtypeStruct((nb * bm, D_MODEL), F32),
        compiler_params=_cparams("arbitrary"),
        name="moe_experts",
    )(block_e, n_used, xs, w1p, b1p, w2, b2)

    return pl.pallas_call(
        _moe_combine_body,
        grid=(n // tt,),
        in_specs=[pl.BlockSpec((tt * TOP_K,), lambda i: (i,), memory_space=pltpu.SMEM),
                  pl.BlockSpec((tt, LANES), row),
                  pl.BlockSpec((tt, D_MODEL), row),
                  pl.BlockSpec((1, D_MODEL), const),
                  pl.BlockSpec(memory_space=pl.ANY)],
        out_specs=pl.BlockSpec((tt, D_MODEL), row),
        out_shape=jax.ShapeDtypeStruct((n, D_MODEL), F32),
        scratch_shapes=[pltpu.VMEM((TOP_K, tt, D_MODEL), F32), pltpu.SemaphoreType.DMA(())],
        compiler_params=_cparams("arbitrary"),
        name="moe_combine",
    )(dest_flat, gw, x1, g_final.reshape(1, D_MODEL), out_sorted)


def _split_cols(w):
    outs, off = [], 0
    for wd in IN_WIDTHS:
        outs.append(w[:, off:off + wd])
        off += wd
    return outs


def _lane_row(v, off):
    return jnp.zeros((1, LANES), F32).at[0, off:off + v.shape[0]].set(v.astype(F32))


def kernel(x_prompt, x_sample, cache_k, cache_v, cache_idx_k, state_conv, state_ssm, cache_mem_k, cache_mem_v,
           page_table, mem_prompt, g_norm1, w_in, b_gate, conv_w, conv_b, dt_bias, a_log, d_skip, g_ssd_norm,
           g_mem, w_mem_kv, w_ssd_out, w_attn_out, w_mem_out, w_out, g_norm2, w_router, b_router, w_exp1,
           b_exp1, w_exp2, b_exp2, g_final):
    assert w_in.shape[0] == 1, "single-layer trunk"
    bp, lp, _ = x_prompt.shape
    bs, ls, _ = x_sample.shape
    np_, ns = bp * lp, bs * ls

    wz, wxbc, wdt, wq, wk, wv, wqi, wki, wwi, wqm, wgate = _split_cols(w_in[0])
    w_all = jnp.concatenate(
        [wxbc, wgate, wz, wq, wqm, wqi, wk, wv, wki, wdt, wwi,
         jnp.zeros((D_MODEL, W_ALL - OFF_SM - SM_WI - IDX_HEADS), F32)], axis=1).astype(BF16)
    dtb_row = _lane_row(dt_bias[0], SM_DT)
    aneg_row = _lane_row(-jnp.exp(a_log[0].astype(F32)), SM_DT)
    dsk_row = jnp.repeat(d_skip[0].astype(F32), SSD_HEAD_DIM).reshape(1, D_INNER)
    gs_row = g_ssd_norm[0].reshape(1, D_INNER)
    cb_row = conv_b[0].reshape(1, CONV_DIM)
    wr_pad = jnp.zeros((D_MODEL, LANES), F32).at[:, :N_EXPERTS].set(w_router[0])
    br_pad = jnp.full((1, LANES), NEG, F32).at[0, :N_EXPERTS].set(b_router[0])
    w1p = jnp.concatenate([w_exp1[0][..., 0::2], w_exp1[0][..., 1::2]], axis=-1).astype(BF16)
    b1p = jnp.concatenate([b_exp1[0][..., 0::2], b_exp1[0][..., 1::2]], axis=-1).reshape(N_EXPERTS, 1, 2 * D_FF)
    w2 = w_exp2[0].astype(BF16)
    b2 = b_exp2[0].reshape(N_EXPERTS, 1, D_MODEL)

    xp = x_prompt.reshape(np_, D_MODEL)
    xs = x_sample.reshape(ns, D_MODEL)
    zp = norm_matmul(xp, g_norm1[0], w_all, 1024, IN_PROJ_TN)
    zs = norm_matmul(xs, g_norm1[0], w_all, ns, IN_PROJ_TN)

    kv_p = norm_matmul(mem_prompt.reshape(bp * N_MEM, D_MODEL), g_mem[0], w_mem_kv[0].astype(BF16),
                       min(1024, bp * N_MEM), MEM_WIDTH)
    om_p = mem_attn(zp, bp, lp, kv_p, 0, kv_p, 1, 512)
    om_s = mem_attn(zs, bs, ls, cache_mem_k[0].reshape(bs * N_MEM, MEM_WIDTH), 0,
                    cache_mem_v[0].reshape(bs * N_MEM, MEM_WIDTH), 0, ls)

    conv_prev_p = jnp.zeros((bp, SUBLANES, CONV_DIM), F32)
    conv_prev_s = jnp.concatenate(
        [jnp.zeros((bs, SUBLANES - (CONV_WIDTH - 1), CONV_DIM), F32), state_conv[0]], axis=1)
    ssm0_p = jnp.zeros((bp, D_INNER, D_STATE), F32)
    ssm0_s = state_ssm[0].reshape(bs, D_INNER, D_STATE)
    ys_p, ssm_p = ssd(zp, bp, lp, conv_prev_p, ssm0_p, conv_w[0], cb_row, dtb_row, aneg_row, dsk_row, gs_row)
    ys_s, ssm_s = ssd(zs, bs, ls, conv_prev_s, ssm0_s, conv_w[0], cb_row, dtb_row, aneg_row, dsk_row, gs_row)

    oa_p = dsa_prompt(zp, bp, lp)
    oa_s = dsa_sample(zs, bs, ls, cache_k[0], cache_v[0], cache_idx_k[0], page_table)

    mw = (b_gate[0].reshape(1, -1), w_ssd_out[0].astype(BF16), w_attn_out[0].astype(BF16),
          w_mem_out[0].astype(BF16), w_out[0].astype(BF16), g_norm2[0].reshape(1, D_MODEL), wr_pad, br_pad)
    x1_p, h2_p, te_p, gw_p = merge(xp, zp, ys_p, oa_p, om_p, *mw, 512)
    x1_s, h2_s, te_s, gw_s = merge(xs, zs, ys_s, oa_s, om_s, *mw, ns)

    cat = lambda a, b: jnp.concatenate([a, b], axis=0)
    y_all = moe_and_final_norm(cat(x1_p, x1_s), cat(h2_p, h2_s), cat(te_p, te_s), cat(gw_p, gw_s),
                               w1p, b1p, w2, b2, g_final)
    y_prompt = y_all[:np_].reshape(bp, lp, D_MODEL)
    y_sample = y_all[np_:].reshape(bs, ls, D_MODEL)

    def kvi(z, b, l):
        k = z[:, OFF_K:OFF_K + N_KV_HEADS * HEAD_DIM].reshape(1, b, l, N_KV_HEADS, HEAD_DIM)
        v = z[:, OFF_V:OFF_V + N_KV_HEADS * HEAD_DIM].reshape(1, b, l, N_KV_HEADS, HEAD_DIM)
        ki = z[:, OFF_SM + SM_KI:OFF_SM + SM_KI + IDX_DIM].reshape(1, b, l, IDX_DIM)
        conv = z[:, OFF_XBC:OFF_XBC + CONV_DIM].reshape(b, l, CONV_DIM)[:, l - (CONV_WIDTH - 1):, :][None]
        return k, v, ki, conv

    k_p, v_p, ki_p, conv_p = kvi(zp, bp, lp)
    k_s, v_s, ki_s, conv_s = kvi(zs, bs, ls)
    mk_p = kv_p[:, :MEM_WIDTH].reshape(1, bp, N_MEM, MEM_HEADS, MEM_HEAD_DIM)
    mv_p = kv_p[:, MEM_WIDTH:].reshape(1, bp, N_MEM, MEM_HEADS, MEM_HEAD_DIM)
    ssm_shape = (1, -1, SSD_HEADS, SSD_HEAD_DIM, D_STATE)
    return (y_prompt, y_sample, k_p, v_p, ki_p, conv_p, ssm_p.reshape(ssm_shape), mk_p, mv_p,
            k_s, v_s, ki_s, conv_s, ssm_s.reshape(ssm_shape))
```

```python
import functools

import numpy as np
import jax
import jax.numpy as jnp
from jax import lax
from jax.experimental import pallas as pl
from jax.experimental.pallas import tpu as pltpu

F32 = jnp.float32
BF16 = jnp.bfloat16
I32 = jnp.int32
HIGHEST = lax.Precision.HIGHEST

D_MODEL = 1024
D_INNER = 2048
SSD_HEAD_DIM = 64
SSD_HEADS = 32
SSD_GROUPS = 4
D_STATE = 128
CONV_WIDTH = 4
CONV_DIM = D_INNER + 2 * SSD_GROUPS * D_STATE
SSD_CHUNK = 128
N_HEADS = 16
N_KV_HEADS = 4
HEAD_DIM = 64
IDX_HEADS = 8
IDX_DIM = 64
TOPK_MAX = 256
N_MEM = 256
MEM_HEADS = 4
MEM_HEAD_DIM = 256
MEM_WIDTH = MEM_HEADS * MEM_HEAD_DIM
N_EXPERTS = 32
TOP_K = 4
D_FF = D_MODEL
SWIGLU_LIMIT = 7.0
SWIGLU_ALPHA = 1.702
N_BRANCH = 3
EPS = 1e-6
PAGE_SIZE = 128
IN_WIDTHS = (D_INNER, CONV_DIM, SSD_HEADS, N_HEADS * HEAD_DIM, N_KV_HEADS * HEAD_DIM, N_KV_HEADS * HEAD_DIM,
             IDX_HEADS * IDX_DIM, IDX_DIM, IDX_HEADS, MEM_WIDTH, N_BRANCH * D_MODEL)

LANES = 128
SUBLANES = 8
VMEM_LIMIT = 56 * 1024 * 1024

OFF_XBC = 0
OFF_GATE = OFF_XBC + CONV_DIM
OFF_Z = OFF_GATE + N_BRANCH * D_MODEL
OFF_Q = OFF_Z + D_INNER
OFF_QM = OFF_Q + N_HEADS * HEAD_DIM
OFF_QI = OFF_QM + MEM_WIDTH
OFF_K = OFF_QI + IDX_HEADS * IDX_DIM
OFF_V = OFF_K + N_KV_HEADS * HEAD_DIM
OFF_SM = OFF_V + N_KV_HEADS * HEAD_DIM
SM_KI = 0
SM_DT = SM_KI + IDX_DIM
SM_WI = SM_DT + SSD_HEADS
IN_PROJ_TN = 1280
W_ALL = OFF_SM + 2 * LANES

NEG = -1e30
INT_MIN = np.int32(-2 ** 31)
INT_MAX = np.int32(2 ** 31 - 1)

MOE_BM = 512
MOE_T = 256


def _cparams(*sem):
    return pltpu.CompilerParams(dimension_semantics=sem, vmem_limit_bytes=VMEM_LIMIT)


def _nt_dot(a, b):
    return lax.dot_general(a, b, (((1,), (1,)), ((), ())), preferred_element_type=F32)


def _float_key(x):
    bits = lax.bitcast_convert_type(x, I32)
    return jnp.where(bits < 0, bits ^ INT_MAX, bits)


def _norm_matmul_body(x_ref, g_ref, w_ref, o_ref, h_ref):
    @pl.when(pl.program_id(1) == 0)
    def _():
        x = x_ref[...]
        h = x * lax.rsqrt(jnp.mean(x * x, axis=-1, keepdims=True) + EPS)
        h_ref[...] = (h * g_ref[...]).astype(BF16)

    o_ref[...] = jnp.dot(h_ref[...], w_ref[...], preferred_element_type=F32)


def norm_matmul(x, g, w, tm, tn):
    n, d = x.shape
    wn = w.shape[1]
    return pl.pallas_call(
        _norm_matmul_body,
        grid=(n // tm, wn // tn),
        in_specs=[pl.BlockSpec((tm, d), lambda i, j: (i, 0)),
                  pl.BlockSpec((1, d), lambda i, j: (0, 0)),
                  pl.BlockSpec((d, tn), lambda i, j: (0, j))],
        out_specs=pl.BlockSpec((tm, tn), lambda i, j: (i, j)),
        out_shape=jax.ShapeDtypeStruct((n, wn), F32),
        scratch_shapes=[pltpu.VMEM((tm, d), BF16)],
        compiler_params=_cparams("parallel", "arbitrary"),
        name="norm_matmul",
    )(x, g.reshape(1, d), w)


def _softplus(x):
    return jnp.maximum(x, 0.0) + jnp.log1p(jnp.exp(-jnp.abs(x)))


def _silu(x):
    return x * jax.nn.sigmoid(x)


def _expand_heads(v, q):
    lane = lax.broadcasted_iota(I32, (q, LANES), 1)
    cols = []
    for t in range(SSD_HEADS // 2):
        c0 = jnp.broadcast_to(v[:, SM_DT + 2 * t:SM_DT + 2 * t + 1], (q, LANES))
        c1 = jnp.broadcast_to(v[:, SM_DT + 2 * t + 1:SM_DT + 2 * t + 2], (q, LANES))
        cols.append(jnp.where(lane < SSD_HEAD_DIM, c0, c1))
    return jnp.concatenate(cols, axis=1)


def _ssd_body(xbc_ref, z_ref, sm_ref, convp_ref, init_ref, cw_ref, cb_ref, dtb_ref, aneg_ref, dsk_ref, gs_ref,
              y_ref, st_ref, xpad_ref, state_ref, *, rows_in, q):
    c = pl.program_id(1)
    pad = SUBLANES

    @pl.when(c == 0)
    def _():
        xpad_ref[0:pad, :] = convp_ref[0]
        state_ref[...] = init_ref[0]

    xpad_ref[pad:pad + rows_in, :] = xbc_ref[...]
    if rows_in < q:
        xpad_ref[pad + rows_in:pad + q, :] = jnp.zeros((q - rows_in, CONV_DIM), F32)

    acc = cb_ref[...]
    for j in range(CONV_WIDTH):
        lo = pad - (CONV_WIDTH - 1) + j
        acc = acc + xpad_ref[lo:lo + q, :] * cw_ref[j:j + 1, :]
    xc = _silu(acc)
    xpad_ref[0:pad, :] = xpad_ref[q:q + pad, :]

    xs = xc[:, :D_INNER]
    gn = SSD_GROUPS * D_STATE
    bm = xc[:, D_INNER:D_INNER + gn].astype(BF16)
    cm = xc[:, D_INNER + gn:].astype(BF16)

    sm = sm_ref[...]
    zz = z_ref[...]
    if rows_in < q:
        sm = jnp.concatenate([sm, jnp.zeros((q - rows_in, LANES), F32)], axis=0)
        zz = jnp.concatenate([zz, jnp.zeros((q - rows_in, D_INNER), F32)], axis=0)
    row = lax.broadcasted_iota(I32, (q, LANES), 0)
    dt = _softplus(sm + dtb_ref[...])
    if rows_in < q:
        dt = jnp.where(row < rows_in, dt, 0.0)
    a = dt * aneg_ref[...]
    tri = (lax.broadcasted_iota(I32, (q, q), 0) >= lax.broadcasted_iota(I32, (q, q), 1)).astype(F32)
    a_cs = jnp.dot(tri, a, precision=HIGHEST, preferred_element_type=F32)
    a_t = a_cs.T
    a_last = a_cs[q - 1:q, :]
    dte = jnp.exp(a_last - a_cs)
    e_in = jnp.exp(a_cs)

    xdt = xs * _expand_heads(dt, q)
    xdt_bf = xdt.astype(BF16)
    xw_bf = (xdt * _expand_heads(dte, q)).astype(BF16)
    ein_x = _expand_heads(e_in, q)

    causal = lax.broadcasted_iota(I32, (q, q), 0) >= lax.broadcasted_iota(I32, (q, q), 1)
    lane = lax.broadcasted_iota(I32, (q, LANES), 1)
    hpg = SSD_HEADS // SSD_GROUPS
    gw = hpg * SSD_HEAD_DIM
    y_parts = []
    for g in range(SSD_GROUPS):
        bg = bm[:, g * D_STATE:(g + 1) * D_STATE]
        cg = cm[:, g * D_STATE:(g + 1) * D_STATE]
        cb = _nt_dot(cg, bg)
        m_h = []
        for e in range(hpg):
            h = g * hpg + e
            col = a_cs[:, SM_DT + h:SM_DT + h + 1]
            rw = a_t[SM_DT + h:SM_DT + h + 1, :]
            decay = jnp.exp(jnp.where(causal, col - rw, -jnp.inf))
            m_h.append((cb * decay).astype(BF16))
        yd = []
        for t in range(hpg // 2):
            pair = g * (hpg // 2) + t
            slab = xdt_bf[:, pair * LANES:(pair + 1) * LANES]
            ya = jnp.dot(m_h[2 * t], slab, preferred_element_type=F32)
            yb = jnp.dot(m_h[2 * t + 1], slab, preferred_element_type=F32)
            yd.append(jnp.where(lane < SSD_HEAD_DIM, ya, yb))
        s_old = state_ref[g * gw:(g + 1) * gw, :]
        y_off = _nt_dot(cg, s_old.astype(BF16)) * ein_x[:, g * gw:(g + 1) * gw]
        y_parts.append(jnp.concatenate(yd, axis=1) + y_off)
        new = lax.dot_general(xw_bf[:, g * gw:(g + 1) * gw], bg, (((0,), (0,)), ((), ())),
                              preferred_element_type=F32)
        for e in range(hpg):
            h = g * hpg + e
            dec = jnp.exp(a_t[SM_DT + h:SM_DT + h + 1, q - 1:q])
            lo = e * SSD_HEAD_DIM
            state_ref[h * SSD_HEAD_DIM:(h + 1) * SSD_HEAD_DIM, :] = (
                s_old[lo:lo + SSD_HEAD_DIM, :] * dec + new[lo:lo + SSD_HEAD_DIM, :])

    y = jnp.concatenate(y_parts, axis=1) + dsk_ref[...] * xs
    y = y * _silu(zz)
    outs = []
    for g in range(SSD_GROUPS):
        yg = y[:, g * gw:(g + 1) * gw]
        outs.append(yg * lax.rsqrt(jnp.mean(yg * yg, axis=-1, keepdims=True) + EPS))
    y = jnp.concatenate(outs, axis=1) * gs_ref[...]
    y_ref[...] = y[:rows_in].astype(BF16)

    @pl.when(c == pl.num_programs(1) - 1)
    def _():
        st_ref[0] = state_ref[...]


def ssd(zall, batch, seq, conv_prev8, ssm_init, conv_w, conv_b, dtb_row, aneg_row, dsk_row, gs_row):
    q = SSD_CHUNK
    rows_in = min(seq, q)
    nch = seq // rows_in
    row_map = lambda b, c: (b * nch + c)
    const2 = lambda b, c: (0, 0)
    body = functools.partial(_ssd_body, rows_in=rows_in, q=q)
    return pl.pallas_call(
        body,
        grid=(batch, nch),
        in_specs=[pl.BlockSpec((rows_in, CONV_DIM), lambda b, c: (row_map(b, c), OFF_XBC // CONV_DIM)),
                  pl.BlockSpec((rows_in, D_INNER), lambda b, c: (row_map(b, c), OFF_Z // D_INNER)),
                  pl.BlockSpec((rows_in, LANES), lambda b, c: (row_map(b, c), OFF_SM // LANES)),
                  pl.BlockSpec((1, SUBLANES, CONV_DIM), lambda b, c: (b, 0, 0)),
                  pl.BlockSpec((1, D_INNER, D_STATE), lambda b, c: (b, 0, 0)),
                  pl.BlockSpec((CONV_WIDTH, CONV_DIM), const2),
                  pl.BlockSpec((1, CONV_DIM), const2),
                  pl.BlockSpec((1, LANES), const2),
                  pl.BlockSpec((1, LANES), const2),
                  pl.BlockSpec((1, D_INNER), const2),
                  pl.BlockSpec((1, D_INNER), const2)],
        out_specs=[pl.BlockSpec((rows_in, D_INNER), lambda b, c: (row_map(b, c), 0)),
                   pl.BlockSpec((1, D_INNER, D_STATE), lambda b, c: (b, 0, 0))],
        out_shape=[jax.ShapeDtypeStruct((batch * seq, D_INNER), BF16),
                   jax.ShapeDtypeStruct((batch, D_INNER, D_STATE), F32)],
        scratch_shapes=[pltpu.VMEM((q + 2 * SUBLANES, CONV_DIM), F32),
                        pltpu.VMEM((D_INNER, D_STATE), F32)],
        compiler_params=_cparams("parallel", "arbitrary"),
        name="ssd",
    )(zall, zall, zall, conv_prev8, ssm_init, conv_w, conv_b, dtb_row, aneg_row, dsk_row, gs_row)


def _mem_attn_body(q_ref, k_ref, v_ref, o_ref):
    for h in range(MEM_HEADS):
        sl = slice(h * MEM_HEAD_DIM, (h + 1) * MEM_HEAD_DIM)
        s = _nt_dot(q_ref[:, sl].astype(BF16), k_ref[:, sl].astype(BF16)) * (MEM_HEAD_DIM ** -0.5)
        m = jnp.max(s, axis=-1, keepdims=True)
        p = jnp.exp(s - m)
        p = p / jnp.sum(p, axis=-1, keepdims=True)
        o = jnp.dot(p.astype(BF16), v_ref[:, sl].astype(BF16), preferred_element_type=F32)
        o_ref[:, sl] = o.astype(BF16)


def mem_attn(zall, batch, seq, k_arr, k_col, v_arr, v_col, tm):
    nt = seq // tm
    return pl.pallas_call(
        _mem_attn_body,
        grid=(batch, nt),
        in_specs=[pl.BlockSpec((tm, MEM_WIDTH), lambda b, i: (b * nt + i, OFF_QM // MEM_WIDTH)),
                  pl.BlockSpec((N_MEM, MEM_WIDTH), lambda b, i: (b, k_col)),
                  pl.BlockSpec((N_MEM, MEM_WIDTH), lambda b, i: (b, v_col))],
        out_specs=pl.BlockSpec((tm, MEM_WIDTH), lambda b, i: (b * nt + i, 0)),
        out_shape=jax.ShapeDtypeStruct((batch * seq, MEM_WIDTH), BF16),
        compiler_params=_cparams("parallel", "arbitrary"),
        name="mem_attn",
    )(zall, k_arr, v_arr)


def _kth_largest_key(count_ge, shape, n_sel):
    def bit_body(t, ans):
        cand = ans | jnp.left_shift(jnp.int32(1), 31 - t)
        cnt = count_ge(cand ^ INT_MIN)
        return jnp.where(cnt >= n_sel, cand, ans)

    ans = lax.fori_loop(0, 32, bit_body, jnp.zeros(shape, I32))
    return ans ^ INT_MIN


def _tie_cut(count_eq_below, need, shape, nbits):
    def bit_body(t, lo):
        cand = lo | jnp.left_shift(jnp.int32(1), nbits - 1 - t)
        cnt = count_eq_below(cand)
        return jnp.where(cnt < need, cand, lo)

    return lax.fori_loop(0, nbits, bit_body, jnp.zeros(shape, I32))


def _select_bias(key, kpos, thr, cut, visible):
    sel = (key > thr) | ((key == thr) & (kpos <= cut))
    return jnp.where(sel & visible, 0.0, NEG)


def _dsa_prompt_body(q_ref, qi_ref, smq_ref, k_ref, v_ref, smk_ref, o_ref,
                     kh_ref, vt_ref, kis_ref, qt2_ref, qit_ref, keys_ref, bias_ref, ot_ref, m_ref, l_ref,
                     *, tq, kc, seq, n_sel):
    i = pl.program_id(1)

    @pl.when(i == 0)
    def _():
        def cast_rows(r, carry):
            rs = pl.ds(pl.multiple_of(r * kc, kc), kc)
            kk = k_ref[rs, :]
            for h in range(N_KV_HEADS):
                kh_ref[h, rs, :] = kk[:, h * HEAD_DIM:(h + 1) * HEAD_DIM].astype(BF16)
            vt_ref[r] = v_ref[rs, :].T.astype(BF16)
            kis_ref[rs, :] = smk_ref[rs, SM_KI:SM_KI + IDX_DIM].astype(BF16)
            return carry
        lax.fori_loop(0, seq // kc, cast_rows, 0)

    nkc = (i * tq + tq - 1) // kc + 1
    qpos = i * tq + lax.broadcasted_iota(I32, (kc, tq), 1)
    krow = lax.broadcasted_iota(I32, (kc, tq), 0)

    qt = (q_ref[...] * (HEAD_DIM ** -0.5)).T.astype(BF16)
    for h in range(N_HEADS):
        qt2_ref[:, h * tq:(h + 1) * tq] = qt[h * HEAD_DIM:(h + 1) * HEAD_DIM, :]
    qit_ref[...] = (qi_ref[...] * (IDX_DIM ** -0.5)).T.astype(BF16)
    wt = smq_ref[...].T[SM_WI:SM_WI + IDX_HEADS, :] * (IDX_HEADS ** -0.5)

    def score_chunk(c, carry):
        ks = pl.ds(pl.multiple_of(c * kc, kc), kc)
        kic = kis_ref[ks, :]
        sc = jnp.zeros((kc, tq), F32)
        for h in range(IDX_HEADS):
            d = jnp.dot(kic, qit_ref[h * IDX_DIM:(h + 1) * IDX_DIM, :], preferred_element_type=F32)
            sc = sc + jnp.maximum(d, 0.0) * wt[h:h + 1, :]
        key = _float_key(sc + 0.0)
        keys_ref[c] = jnp.where(c * kc + krow <= qpos, key, INT_MIN)
        return carry
    lax.fori_loop(0, nkc, score_chunk, 0)

    def count(pred):
        def body(c, acc):
            return acc + jnp.where(pred(keys_ref[c], c * kc + krow), 1.0, 0.0)
        acc = lax.fori_loop(0, nkc, body, jnp.zeros((kc, tq), F32))
        return jnp.sum(acc, axis=0, keepdims=True)

    vec = (1, tq)
    thr = _kth_largest_key(lambda cand: count(lambda key, kpos: key >= cand), vec, n_sel)
    n_gt = count(lambda key, kpos: key > thr)
    n_eq = count(lambda key, kpos: key == thr)
    need = n_sel - n_gt
    excess = jnp.max(jnp.where((n_eq > need) & (thr != INT_MIN), 1.0, 0.0))
    nbits = max(1, int(seq - 1).bit_length())
    cut = lax.cond(
        excess > 0.0,
        lambda: _tie_cut(lambda cand: count(lambda key, kpos: (key == thr) & (kpos < cand)), need, vec, nbits),
        lambda: jnp.full(vec, INT_MAX, I32))

    def bias_chunk(c, carry):
        kpos = c * kc + krow
        bias_ref[c] = _select_bias(keys_ref[c], kpos, thr, cut, kpos <= qpos)
        return carry
    lax.fori_loop(0, nkc, bias_chunk, 0)

    grp = N_HEADS // N_KV_HEADS
    m_ref[...] = jnp.full(m_ref.shape, NEG, F32)
    l_ref[...] = jnp.zeros(l_ref.shape, F32)
    ot_ref[...] = jnp.zeros(ot_ref.shape, F32)

    def att_chunk(c, carry):
        ks = pl.ds(pl.multiple_of(c * kc, kc), kc)
        bias = bias_ref[c]
        for kh in range(N_KV_HEADS):
            s4 = jnp.dot(kh_ref[kh, ks, :], qt2_ref[:, kh * grp * tq:(kh + 1) * grp * tq],
                         preferred_element_type=F32)
            vtc = vt_ref[c, kh * HEAD_DIM:(kh + 1) * HEAD_DIM, :]
            for g in range(grp):
                h = kh * grp + g
                hs = slice(h * HEAD_DIM, (h + 1) * HEAD_DIM)
                s = s4[:, g * tq:(g + 1) * tq] + bias
                m = m_ref[h]
                m_new = jnp.maximum(m, jnp.max(s, axis=0, keepdims=True))
                p = jnp.exp(s - m_new)
                alpha = jnp.exp(m - m_new)
                l_ref[h] = alpha * l_ref[h] + jnp.sum(p, axis=0, keepdims=True)
                ot_ref[hs, :] = alpha * ot_ref[hs, :] + jnp.dot(vtc, p.astype(BF16), preferred_element_type=F32)
                m_ref[h] = m_new
        return carry
    lax.fori_loop(0, nkc, att_chunk, 0)

    for h in range(N_HEADS):
        hs = slice(h * HEAD_DIM, (h + 1) * HEAD_DIM)
        ot_ref[hs, :] = ot_ref[hs, :] / l_ref[h]
    o_ref[...] = ot_ref[...].T.astype(BF16)


def dsa_prompt(zall, batch, seq):
    tq = 128
    kc = 256
    nq = seq // tq
    n_sel = min(TOPK_MAX, seq // 4)
    kvw = N_KV_HEADS * HEAD_DIM
    qw = N_HEADS * HEAD_DIM
    qiw = IDX_HEADS * IDX_DIM
    body = functools.partial(_dsa_prompt_body, tq=tq, kc=kc, seq=seq, n_sel=n_sel)
    return pl.pallas_call(
        body,
        grid=(batch, nq),
        in_specs=[pl.BlockSpec((tq, qw), lambda b, i: (b * nq + i, OFF_Q // qw)),
                  pl.BlockSpec((tq, qiw), lambda b, i: (b * nq + i, OFF_QI // qiw)),
                  pl.BlockSpec((tq, LANES), lambda b, i: (b * nq + i, OFF_SM // LANES)),
                  pl.BlockSpec((seq, kvw), lambda b, i: (b, OFF_K // kvw)),
                  pl.BlockSpec((seq, kvw), lambda b, i: (b, OFF_V // kvw)),
                  pl.BlockSpec((seq, LANES), lambda b, i: (b, OFF_SM // LANES))],
        out_specs=pl.BlockSpec((tq, qw), lambda b, i: (b * nq + i, 0)),
        out_shape=jax.ShapeDtypeStruct((batch * seq, qw), BF16),
        scratch_shapes=[pltpu.VMEM((N_KV_HEADS, seq, HEAD_DIM), BF16),
                        pltpu.VMEM((seq // kc, kvw, kc), BF16),
                        pltpu.VMEM((seq, IDX_DIM), BF16),
                        pltpu.VMEM((HEAD_DIM, N_HEADS * tq), BF16),
                        pltpu.VMEM((qiw, tq), BF16),
                        pltpu.VMEM((seq // kc, kc, tq), I32),
                        pltpu.VMEM((seq // kc, kc, tq), F32),
                        pltpu.VMEM((qw, tq), F32),
                        pltpu.VMEM((N_HEADS, 1, tq), F32),
                        pltpu.VMEM((N_HEADS, 1, tq), F32)],
        compiler_params=_cparams("parallel", "arbitrary"),
        name="dsa_prompt",
    )(zall, zall, zall, zall, zall, zall)


def _dsa_s_score_body(pt_ref, qs_ref, w_ref, *refs, pg):
    ki_refs, o_ref = refs[:pg], refs[pg]
    qs = qs_ref[0]
    wcol = w_ref[0] * (IDX_HEADS ** -0.5)
    t = qs.shape[0] // IDX_HEADS
    for p in range(pg):
        d = _nt_dot(qs, ki_refs[p][0].astype(BF16))
        r = jnp.maximum(d * (IDX_DIM ** -0.5), 0.0) * wcol
        sc = r[0:t, :]
        for h in range(1, IDX_HEADS):
            sc = sc + r[h * t:(h + 1) * t, :]
        o_ref[0, :, p * PAGE_SIZE:(p + 1) * PAGE_SIZE] = sc + 0.0


def _dsa_s_select_body(sc_ref, qs_ref, w_ref, smn_ref, o_ref, *, t, past, n_sel):
    qs = qs_ref[0]
    wcol = w_ref[0] * (IDX_HEADS ** -0.5)
    ki_new = smn_ref[:, SM_KI:SM_KI + IDX_DIM].astype(BF16)
    ki_new = jnp.concatenate([ki_new, jnp.zeros((LANES - t, IDX_DIM), BF16)], axis=0)
    d = _nt_dot(qs, ki_new)
    r = jnp.maximum(d * (IDX_DIM ** -0.5), 0.0) * wcol
    sc_new = r[0:t, :]
    for h in range(1, IDX_HEADS):
        sc_new = sc_new + r[h * t:(h + 1) * t, :]
    lane_n = lax.broadcasted_iota(I32, (t, LANES), 1)
    vis_n = lane_n <= lax.broadcasted_iota(I32, (t, LANES), 0)
    key_n = jnp.where(vis_n, _float_key(sc_new + 0.0), INT_MIN)
    key_p = _float_key(sc_ref[0])
    pos_p = lax.broadcasted_iota(I32, (t, past), 1)
    pos_n = past + lane_n

    def count(pred):
        return (jnp.sum(jnp.where(pred(key_p, pos_p), 1.0, 0.0), axis=1, keepdims=True)
                + jnp.sum(jnp.where(pred(key_n, pos_n), 1.0, 0.0), axis=1, keepdims=True))

    thr = _kth_largest_key(lambda cand: count(lambda key, kpos: key >= cand), (t, 1), n_sel)
    need = n_sel - count(lambda key, kpos: key > thr)
    nbits = max(1, int(past + t - 1).bit_length())
    cut = _tie_cut(lambda cand: count(lambda key, kpos: (key == thr) & (kpos < cand)), need, (t, 1), nbits)
    o_ref[0, :, 0:past] = _select_bias(key_p, pos_p, thr, cut, pos_p >= 0)
    o_ref[0, :, past:past + LANES] = _select_bias(key_n, pos_n, thr, cut, vis_n)


def _dsa_s_attn_body(pt_ref, qbd_ref, bias_ref, biasn_ref, kn_ref, vn_ref, *refs, pg, t):
    k_refs, v_refs = refs[:pg], refs[pg:2 * pg]
    o_ref, m_ref, l_ref, acc_ref = refs[2 * pg:]
    j = pl.program_id(1)
    rows = qbd_ref.shape[1]
    rep = rows // t

    @pl.when(j == 0)
    def _():
        m_ref[...] = jnp.full(m_ref.shape, NEG, F32)
        l_ref[...] = jnp.zeros(l_ref.shape, F32)
        acc_ref[...] = jnp.zeros(acc_ref.shape, F32)

    qbd = qbd_ref[0]

    def update(kcat, vcat, bias):
        s = _nt_dot(qbd, kcat) * (HEAD_DIM ** -0.5) + jnp.concatenate([bias] * rep, axis=0)
        m = m_ref[...]
        m_new = jnp.maximum(m, jnp.max(s, axis=1, keepdims=True))
        p = jnp.exp(s - m_new)
        alpha = jnp.exp(m - m_new)
        l_ref[...] = alpha * l_ref[...] + jnp.sum(p, axis=1, keepdims=True)
        acc_ref[...] = alpha * acc_ref[...] + jnp.dot(p.astype(BF16), vcat, preferred_element_type=F32)
        m_ref[...] = m_new

    kcat = jnp.concatenate([r[0] for r in k_refs], axis=0).astype(BF16)
    vcat = jnp.concatenate([r[0] for r in v_refs], axis=0).astype(BF16)
    update(kcat, vcat, bias_ref[0])

    @pl.when(j == pl.num_programs(1) - 1)
    def _():
        kvw = N_KV_HEADS * HEAD_DIM
        zpad = jnp.zeros((LANES - t, kvw), BF16)
        update(jnp.concatenate([kn_ref[...].astype(BF16), zpad], axis=0),
               jnp.concatenate([vn_ref[...].astype(BF16), zpad], axis=0), biasn_ref[0])
        o_ref[0] = acc_ref[...] / l_ref[...]


def dsa_sample(zs, batch, t, cache_k, cache_v, cache_ki, page_table):
    n_pages = page_table.shape[1]
    past = n_pages * PAGE_SIZE
    n_sel = min(TOPK_MAX, (past + t) // 4)
    pg = 8
    nj = n_pages // pg
    n_pool = cache_k.shape[0]
    kvw = N_KV_HEADS * HEAD_DIM
    grp = N_HEADS // N_KV_HEADS

    qi = zs[:, OFF_QI:OFF_QI + IDX_HEADS * IDX_DIM].reshape(batch, t, IDX_HEADS, IDX_DIM)
    qs = jnp.transpose(qi, (0, 2, 1, 3)).reshape(batch, IDX_HEADS * t, IDX_DIM).astype(BF16)
    wi = zs[:, OFF_SM + SM_WI:OFF_SM + SM_WI + IDX_HEADS].reshape(batch, t, IDX_HEADS)
    wcol = jnp.transpose(wi, (0, 2, 1)).reshape(batch, IDX_HEADS * t, 1)

    def page_spec(shape, p):
        return pl.BlockSpec(shape, lambda b, j, pt: (pt[b, j * pg + p],) + (0,) * (len(shape) - 1))

    scores = pl.pallas_call(
        functools.partial(_dsa_s_score_body, pg=pg),
        grid_spec=pltpu.PrefetchScalarGridSpec(
            num_scalar_prefetch=1,
            grid=(batch, nj),
            in_specs=[pl.BlockSpec((1, IDX_HEADS * t, IDX_DIM), lambda b, j, pt: (b, 0, 0)),
                      pl.BlockSpec((1, IDX_HEADS * t, 1), lambda b, j, pt: (b, 0, 0))]
                     + [page_spec((1, PAGE_SIZE, IDX_DIM), p) for p in range(pg)],
            out_specs=pl.BlockSpec((1, t, pg * PAGE_SIZE), lambda b, j, pt: (b, 0, j))),
        out_shape=jax.ShapeDtypeStruct((batch, t, past), F32),
        compiler_params=_cparams("parallel", "arbitrary"),
        name="dsa_sample_scores",
    )(page_table, qs, wcol, *([cache_ki] * pg))

    bias = pl.pallas_call(
        functools.partial(_dsa_s_select_body, t=t, past=past, n_sel=n_sel),
        grid=(batch,),
        in_specs=[pl.BlockSpec((1, t, past), lambda b: (b, 0, 0)),
                  pl.BlockSpec((1, IDX_HEADS * t, IDX_DIM), lambda b: (b, 0, 0)),
                  pl.BlockSpec((1, IDX_HEADS * t, 1), lambda b: (b, 0, 0)),
                  pl.BlockSpec((t, LANES), lambda b: (b, OFF_SM // LANES))],
        out_specs=pl.BlockSpec((1, t, past + LANES), lambda b: (b, 0, 0)),
        out_shape=jax.ShapeDtypeStruct((batch, t, past + LANES), F32),
        compiler_params=_cparams("parallel"),
        name="dsa_sample_select",
    )(scores, qs, wcol, zs)

    q = zs[:, OFF_Q:OFF_Q + N_HEADS * HEAD_DIM].reshape(batch, t, N_KV_HEADS, grp, HEAD_DIM)
    q = jnp.transpose(q, (0, 2, 3, 1, 4))
    eye = jnp.eye(N_KV_HEADS, dtype=F32)
    qbd = (q[:, :, :, :, None, :] * eye[None, :, None, None, :, None]).reshape(batch, N_HEADS * t, kvw).astype(BF16)

    ck = cache_k.reshape(n_pool, PAGE_SIZE, kvw)
    cv = cache_v.reshape(n_pool, PAGE_SIZE, kvw)
    rows = N_HEADS * t
    out = pl.pallas_call(
        functools.partial(_dsa_s_attn_body, pg=pg, t=t),
        grid_spec=pltpu.PrefetchScalarGridSpec(
            num_scalar_prefetch=1,
            grid=(batch, nj),
            in_specs=[pl.BlockSpec((1, rows, kvw), lambda b, j, pt: (b, 0, 0)),
                      pl.BlockSpec((1, t, pg * PAGE_SIZE), lambda b, j, pt: (b, 0, j)),
                      pl.BlockSpec((1, t, LANES), lambda b, j, pt: (b, 0, past // LANES)),
                      pl.BlockSpec((t, kvw), lambda b, j, pt: (b, OFF_K // kvw)),
                      pl.BlockSpec((t, kvw), lambda b, j, pt: (b, OFF_V // kvw))]
                     + [page_spec((1, PAGE_SIZE, kvw), p) for p in range(pg)]
                     + [page_spec((1, PAGE_SIZE, kvw), p) for p in range(pg)],
            out_specs=pl.BlockSpec((1, rows, kvw), lambda b, j, pt: (b, 0, 0)),
            scratch_shapes=[pltpu.VMEM((rows, 1), F32), pltpu.VMEM((rows, 1), F32), pltpu.VMEM((rows, kvw), F32)]),
        out_shape=jax.ShapeDtypeStruct((batch, rows, kvw), F32),
        compiler_params=_cparams("parallel", "arbitrary"),
        name="dsa_sample_attn",
    )(page_table, qbd, bias, bias, zs, zs, *([ck] * pg), *([cv] * pg))

    o = out.reshape(batch, N_KV_HEADS, grp, t, N_KV_HEADS, HEAD_DIM)
    o = jnp.stack([o[:, kh, :, :, kh, :] for kh in range(N_KV_HEADS)], axis=1)
    return jnp.transpose(o, (0, 3, 1, 2, 4)).reshape(batch * t, N_HEADS * HEAD_DIM).astype(BF16)


def _merge_body(x_ref, gate_ref, ys_ref, oa_ref, om_ref, bg_ref, ws_ref, wa_ref, wm_ref, wo_ref, g2_ref, wr_ref, br_ref,
                x1_ref, h2_ref, te_ref, gw_ref):
    gates = jax.nn.sigmoid(gate_ref[...] + bg_ref[...])
    merged = (gates[:, 0:D_MODEL] * jnp.dot(ys_ref[...], ws_ref[...], preferred_element_type=F32)
              + gates[:, D_MODEL:2 * D_MODEL] * jnp.dot(oa_ref[...], wa_ref[...], preferred_element_type=F32)
              + gates[:, 2 * D_MODEL:] * jnp.dot(om_ref[...], wm_ref[...], preferred_element_type=F32))
    x1 = x_ref[...] + jnp.dot(merged.astype(BF16), wo_ref[...], preferred_element_type=F32)
    x1_ref[...] = x1
    h2 = x1 * lax.rsqrt(jnp.mean(x1 * x1, axis=-1, keepdims=True) + EPS)
    h2 = h2 * g2_ref[...]
    h2_ref[...] = h2
    logits = jnp.dot(h2, wr_ref[...], precision=HIGHEST, preferred_element_type=F32) + br_ref[...]
    lane = lax.broadcasted_iota(I32, logits.shape, 1)
    te = jnp.zeros(logits.shape, I32)
    tv = []
    for k in range(TOP_K):
        m = jnp.max(logits, axis=1, keepdims=True)
        idx = jnp.min(jnp.where(logits == m, lane, LANES), axis=1, keepdims=True)
        te = jnp.where(lane == k, idx, te)
        tv.append(m)
        logits = jnp.where(lane == idx, -jnp.inf, logits)
    ex = [jnp.exp(v - tv[0]) for v in tv]
    den = ex[0] + ex[1] + ex[2] + ex[3]
    gw = jnp.zeros(logits.shape, F32)
    for k in range(TOP_K):
        gw = jnp.where(lane == k, ex[k] / den, gw)
    te_ref[...] = te
    gw_ref[...] = gw


def merge(x, zall, ys, oa, om, bg, ws, wa, wm, wo, g2, wr, br, tm):
    n = x.shape[0]
    gw3 = N_BRANCH * D_MODEL
    row = lambda i: (i, 0)
    const = lambda i: (0, 0)
    return pl.pallas_call(
        _merge_body,
        grid=(n // tm,),
        in_specs=[pl.BlockSpec((tm, D_MODEL), row),
                  pl.BlockSpec((tm, gw3), lambda i: (i, OFF_GATE // gw3)),
                  pl.BlockSpec((tm, D_INNER), row),
                  pl.BlockSpec((tm, N_HEADS * HEAD_DIM), row),
                  pl.BlockSpec((tm, MEM_WIDTH), row),
                  pl.BlockSpec((1, gw3), const),
                  pl.BlockSpec((D_INNER, D_MODEL), const),
                  pl.BlockSpec((N_HEADS * HEAD_DIM, D_MODEL), const),
                  pl.BlockSpec((MEM_WIDTH, D_MODEL), const),
                  pl.BlockSpec((D_MODEL, D_MODEL), const),
                  pl.BlockSpec((1, D_MODEL), const),
                  pl.BlockSpec((D_MODEL, LANES), const),
                  pl.BlockSpec((1, LANES), const)],
        out_specs=[pl.BlockSpec((tm, D_MODEL), row), pl.BlockSpec((tm, D_MODEL), row),
                   pl.BlockSpec((tm, LANES), row), pl.BlockSpec((tm, LANES), row)],
        out_shape=[jax.ShapeDtypeStruct((n, D_MODEL), F32), jax.ShapeDtypeStruct((n, D_MODEL), F32),
                   jax.ShapeDtypeStruct((n, LANES), I32), jax.ShapeDtypeStruct((n, LANES), F32)],
        compiler_params=_cparams("parallel"),
        name="merge",
    )(x, zall, ys, oa, om, bg, ws, wa, wm, wo, g2, wr, br)


def _moe_pos_body(te_ref, pos_ref, cnt_ref, carry_ref):
    i = pl.program_id(0)
    tt = te_ref.shape[0]

    @pl.when(i == 0)
    def _():
        carry_ref[...] = jnp.zeros(carry_ref.shape, F32)

    te = te_ref[...]
    lane = lax.broadcasted_iota(I32, (tt, LANES), 1)
    onehot = [lane == te[:, k:k + 1] for k in range(TOP_K)]
    msum = jnp.zeros((tt, LANES), F32)
    for k in range(TOP_K):
        msum = msum + jnp.where(onehot[k], 1.0, 0.0)
    strict = (lax.broadcasted_iota(I32, (tt, tt), 0) > lax.broadcasted_iota(I32, (tt, tt), 1))
    prefix = jnp.dot(jnp.where(strict, 1.0, 0.0).astype(BF16), msum.astype(BF16), preferred_element_type=F32)
    prefix = prefix + carry_ref[0:1, :]
    pos = jnp.zeros((tt, LANES), F32)
    for k in range(TOP_K):
        pk = jnp.sum(jnp.where(onehot[k], prefix, 0.0), axis=1, keepdims=True)
        pos = jnp.where(lane == k, pk, pos)
    pos_ref[...] = pos
    carry_ref[...] = carry_ref[...] + jnp.sum(msum, axis=0, keepdims=True)
    cnt_ref[...] = carry_ref[...]


def _moe_dest_body(te_ref, pos_ref, cnt_ref, dest_ref, be_ref, nu_ref, *, bm):
    tt = te_ref.shape[0]
    cnt = cnt_ref[...]
    padded = jnp.floor((cnt + (bm - 1)) * (1.0 / bm)) * bm
    upper = (lax.broadcasted_iota(I32, (LANES, LANES), 0) < lax.broadcasted_iota(I32, (LANES, LANES), 1))
    pad_start = jnp.dot(padded, jnp.where(upper, 1.0, 0.0), precision=HIGHEST, preferred_element_type=F32)
    pad_end = pad_start + padded
    te = te_ref[...]
    pos = pos_ref[...]
    lane = lax.broadcasted_iota(I32, (tt, LANES), 1)
    dest = jnp.zeros((tt, LANES), F32)
    for k in range(TOP_K):
        ps = jnp.sum(jnp.where(lane == te[:, k:k + 1], pad_start[0:1, :], 0.0), axis=1, keepdims=True)
        dest = jnp.where(lane == k, ps + pos[:, k:k + 1], dest)
    dest_ref[...] = dest.astype(I32)
    nb = be_ref.shape[0]
    bstart = (lax.broadcasted_iota(I32, (nb, LANES), 0) * bm).astype(F32)
    lane_b = lax.broadcasted_iota(I32, (nb, LANES), 1)
    done = jnp.where((pad_end[0:1, :] <= bstart) & (lane_b < N_EXPERTS), 1.0, 0.0)
    be = jnp.minimum(jnp.sum(done, axis=1, keepdims=True), N_EXPERTS - 1.0)
    be_ref[...] = jnp.broadcast_to(be, (nb, LANES)).astype(I32)
    total = jnp.sum(padded[0:1, :], axis=1, keepdims=True)
    nu_ref[...] = jnp.broadcast_to(total * (1.0 / bm), nu_ref.shape).astype(I32)


def _moe_dispatch_body(dest_ref, h_ref, xs_in_ref, xs_ref, sem):
    del xs_in_ref
    tt = h_ref.shape[0]

    def copy(r, k):
        d = dest_ref[r * TOP_K + k]
        return pltpu.make_async_copy(h_ref.at[pl.ds(r, 1), :], xs_ref.at[pl.ds(d, 1), :], sem)

    def issue(r, carry):
        for k in range(TOP_K):
            copy(r, k).start()
        return carry
    lax.fori_loop(0, tt, issue, 0)

    def drain(r, carry):
        for k in range(TOP_K):
            copy(r, k).wait()
        return carry
    lax.fori_loop(0, tt, drain, 0)


def _moe_expert_body(be_ref, nu_ref, xs_ref, w1_ref, b1_ref, w2_ref, b2_ref, o_ref, w1s_ref, w2s_ref):
    i = pl.program_id(0)
    used = i < nu_ref[0]
    e = be_ref[i]
    prev = be_ref[jnp.maximum(i - 1, 0)]
    half = LANES

    @pl.when(used & ((i == 0) | (e != prev)))
    def _():
        r = lax.broadcasted_iota(I32, (2 * half, 2 * half), 0)
        c = lax.broadcasted_iota(I32, (2 * half, 2 * half), 1)
        src_col = jnp.where(c < half, 2 * c, 2 * (c - half) + 1)
        perm = jnp.where(r == src_col, 1.0, 0.0).astype(BF16)
        for j in range(2 * D_FF // (2 * half)):
            sl = slice(j * 2 * half, (j + 1) * 2 * half)
            w1s_ref[:, sl] = jnp.dot(w1_ref[0, :, sl].astype(BF16), perm, preferred_element_type=F32).astype(BF16)
        w2s_ref[...] = w2_ref[0].astype(BF16)

    @pl.when(used)
    def _():
        u = jnp.dot(xs_ref[...].astype(BF16), w1s_ref[...], preferred_element_type=F32) + b1_ref[0]
        acts = []
        for j in range(D_FF // half):
            glu = jnp.minimum(u[:, 2 * j * half:(2 * j + 1) * half], SWIGLU_LIMIT)
            lin = jnp.clip(u[:, (2 * j + 1) * half:(2 * j + 2) * half], -SWIGLU_LIMIT, SWIGLU_LIMIT)
            acts.append((glu * jax.nn.sigmoid(SWIGLU_ALPHA * glu) * (lin + 1.0)).astype(BF16))
        act = jnp.concatenate(acts, axis=1)
        o_ref[...] = jnp.dot(act, w2s_ref[...], preferred_element_type=F32) + b2_ref[0]

    @pl.when(jnp.logical_not(used))
    def _():
        o_ref[...] = jnp.zeros(o_ref.shape, F32)


def _moe_combine_body(dest_ref, gw_ref, x1_ref, gf_ref, os_ref, y_ref, buf_ref, sem):
    tt = x1_ref.shape[0]

    def copy(r, k):
        d = dest_ref[r * TOP_K + k]
        return pltpu.make_async_copy(os_ref.at[pl.ds(d, 1), :], buf_ref.at[k, pl.ds(r, 1), :], sem)

    def issue(r, carry):
        for k in range(TOP_K):
            copy(r, k).start()
        return carry
    lax.fori_loop(0, tt, issue, 0)

    def drain(r, carry):
        for k in range(TOP_K):
            copy(r, k).wait()
        return carry
    lax.fori_loop(0, tt, drain, 0)

    gw = gw_ref[...]
    y = gw[:, 0:1] * buf_ref[0]
    for k in range(1, TOP_K):
        y = y + gw[:, k:k + 1] * buf_ref[k]
    x2 = x1_ref[...] + y
    out = x2 * lax.rsqrt(jnp.mean(x2 * x2, axis=-1, keepdims=True) + EPS)
    y_ref[...] = out * gf_ref[...]


def moe_and_final_norm(x1, h2, te, gw, w1, b1p, w2, b2, g_final):
    n = x1.shape[0]
    tt = MOE_T
    bm = MOE_BM
    nb = -(-(n * TOP_K + N_EXPERTS * (bm - 1)) // bm)
    nbp = -(-nb // SUBLANES) * SUBLANES
    row = lambda i: (i, 0)
    const = lambda i: (0, 0)

    pos, cnt = pl.pallas_call(
        _moe_pos_body,
        grid=(n // tt,),
        in_specs=[pl.BlockSpec((tt, LANES), row)],
        out_specs=[pl.BlockSpec((tt, LANES), row), pl.BlockSpec((SUBLANES, LANES), const)],
        out_shape=[jax.ShapeDtypeStruct((n, LANES), F32), jax.ShapeDtypeStruct((SUBLANES, LANES), F32)],
        scratch_shapes=[pltpu.VMEM((SUBLANES, LANES), F32)],
        compiler_params=_cparams("arbitrary"),
        name="moe_positions",
    )(te)

    dest, be, nu = pl.pallas_call(
        functools.partial(_moe_dest_body, bm=bm),
        grid=(n // tt,),
        in_specs=[pl.BlockSpec((tt, LANES), row), pl.BlockSpec((tt, LANES), row),
                  pl.BlockSpec((SUBLANES, LANES), const)],
        out_specs=[pl.BlockSpec((tt, LANES), row), pl.BlockSpec((nbp, LANES), const),
                   pl.BlockSpec((SUBLANES, LANES), const)],
        out_shape=[jax.ShapeDtypeStruct((n, LANES), I32), jax.ShapeDtypeStruct((nbp, LANES), I32),
                   jax.ShapeDtypeStruct((SUBLANES, LANES), I32)],
        compiler_params=_cparams("arbitrary"),
        name="moe_destinations",
    )(te, pos, cnt)
    dest_flat = dest[:, :TOP_K].reshape(n * TOP_K)
    block_e = be[:nb, 0]
    n_used = nu[0, 0:1]

    xs = pl.pallas_call(
        _moe_dispatch_body,
        grid=(n // tt,),
        in_specs=[pl.BlockSpec((tt * TOP_K,), lambda i: (i,), memory_space=pltpu.SMEM),
                  pl.BlockSpec((tt, D_MODEL), row),
                  pl.BlockSpec(memory_space=pl.ANY)],
        out_specs=pl.BlockSpec(memory_space=pl.ANY),
        out_shape=jax.ShapeDtypeStruct((nb * bm, D_MODEL), F32),
        scratch_shapes=[pltpu.SemaphoreType.DMA(())],
        input_output_aliases={2: 0},
        compiler_params=_cparams("arbitrary"),
        name="moe_dispatch",
    )(dest_flat, h2, jnp.zeros((nb * bm, D_MODEL), F32))

    out_sorted = pl.pallas_call(
        _moe_expert_body,
        grid_spec=pltpu.PrefetchScalarGridSpec(
            num_scalar_prefetch=2,
            grid=(nb,),
            in_specs=[pl.BlockSpec((bm, D_MODEL), lambda i, be_, nu_: (i, 0)),
                      pl.BlockSpec((1, D_MODEL, 2 * D_FF), lambda i, be_, nu_: (be_[i], 0, 0)),
                      pl.BlockSpec((1, 1, 2 * D_FF), lambda i, be_, nu_: (be_[i], 0, 0)),
                      pl.BlockSpec((1, D_FF, D_MODEL), lambda i, be_, nu_: (be_[i], 0, 0)),
                      pl.BlockSpec((1, 1, D_MODEL), lambda i, be_, nu_: (be_[i], 0, 0))],
            out_specs=pl.BlockSpec((bm, D_MODEL), lambda i, be_, nu_: (i, 0)),
            scratch_shapes=[pltpu.VMEM((D_MODEL, 2 * D_FF), BF16), pltpu.VMEM((D_FF, D_MODEL), BF16)]),
        out_shape=jax.ShapeDtypeStruct((nb * bm, D_MODEL), F32),
        compiler_params=_cparams("arbitrary"),
        name="moe_experts",
    )(block_e, n_used, xs, w1, b1p, w2, b2)

    return pl.pallas_call(
        _moe_combine_body,
        grid=(n // tt,),
        in_specs=[pl.BlockSpec((tt * TOP_K,), lambda i: (i,), memory_space=pltpu.SMEM),
                  pl.BlockSpec((tt, LANES), row),
                  pl.BlockSpec((tt, D_MODEL), row),
                  pl.BlockSpec((1, D_MODEL), const),
                  pl.BlockSpec(memory_space=pl.ANY)],
        out_specs=pl.BlockSpec((tt, D_MODEL), row),
        out_shape=jax.ShapeDtypeStruct((n, D_MODEL), F32),
        scratch_shapes=[pltpu.VMEM((TOP_K, tt, D_MODEL), F32), pltpu.SemaphoreType.DMA(())],
        compiler_params=_cparams("arbitrary"),
        name="moe_combine",
    )(dest_flat, gw, x1, g_final.reshape(1, D_MODEL), out_sorted)


def _split_cols(w):
    outs, off = [], 0
    for wd in IN_WIDTHS:
        outs.append(w[:, off:off + wd])
        off += wd
    return outs


def _lane_row(v, off):
    return jnp.zeros((1, LANES), F32).at[0, off:off + v.shape[0]].set(v.astype(F32))


def kernel(x_prompt, x_sample, cache_k, cache_v, cache_idx_k, state_conv, state_ssm, cache_mem_k, cache_mem_v,
           page_table, mem_prompt, g_norm1, w_in, b_gate, conv_w, conv_b, dt_bias, a_log, d_skip, g_ssd_norm,
           g_mem, w_mem_kv, w_ssd_out, w_attn_out, w_mem_out, w_out, g_norm2, w_router, b_router, w_exp1,
           b_exp1, w_exp2, b_exp2, g_final):
    assert w_in.shape[0] == 1, "single-layer trunk"
    bp, lp, _ = x_prompt.shape
    bs, ls, _ = x_sample.shape
    np_, ns = bp * lp, bs * ls

    wz, wxbc, wdt, wq, wk, wv, wqi, wki, wwi, wqm, wgate = _split_cols(w_in[0])
    w_all = jnp.concatenate(
        [wxbc, wgate, wz, wq, wqm, wqi, wk, wv, wki, wdt, wwi,
         jnp.zeros((D_MODEL, W_ALL - OFF_SM - SM_WI - IDX_HEADS), F32)], axis=1).astype(BF16)
    dtb_row = _lane_row(dt_bias[0], SM_DT)
    aneg_row = _lane_row(-jnp.exp(a_log[0].astype(F32)), SM_DT)
    dsk_row = jnp.repeat(d_skip[0].astype(F32), SSD_HEAD_DIM).reshape(1, D_INNER)
    gs_row = g_ssd_norm[0].reshape(1, D_INNER)
    cb_row = conv_b[0].reshape(1, CONV_DIM)
    wr_pad = jnp.zeros((D_MODEL, LANES), F32).at[:, :N_EXPERTS].set(w_router[0])
    br_pad = jnp.full((1, LANES), NEG, F32).at[0, :N_EXPERTS].set(b_router[0])
    b1p = b_exp1[0].reshape(N_EXPERTS, D_FF // LANES, LANES, 2).transpose(0, 1, 3, 2).reshape(N_EXPERTS, 1, 2 * D_FF)
    b2 = b_exp2[0].reshape(N_EXPERTS, 1, D_MODEL)

    xp = x_prompt.reshape(np_, D_MODEL)
    xs = x_sample.reshape(ns, D_MODEL)
    zp = norm_matmul(xp, g_norm1[0], w_all, 1024, IN_PROJ_TN)
    zs = norm_matmul(xs, g_norm1[0], w_all, ns, IN_PROJ_TN)

    kv_p = norm_matmul(mem_prompt.reshape(bp * N_MEM, D_MODEL), g_mem[0], w_mem_kv[0].astype(BF16),
                       min(1024, bp * N_MEM), MEM_WIDTH)
    om_p = mem_attn(zp, bp, lp, kv_p, 0, kv_p, 1, 512)
    om_s = mem_attn(zs, bs, ls, cache_mem_k[0].reshape(bs * N_MEM, MEM_WIDTH), 0,
                    cache_mem_v[0].reshape(bs * N_MEM, MEM_WIDTH), 0, ls)

    conv_prev_p = jnp.zeros((bp, SUBLANES, CONV_DIM), F32)
    conv_prev_s = jnp.concatenate(
        [jnp.zeros((bs, SUBLANES - (CONV_WIDTH - 1), CONV_DIM), F32), state_conv[0]], axis=1)
    ssm0_p = jnp.zeros((bp, D_INNER, D_STATE), F32)
    ssm0_s = state_ssm[0].reshape(bs, D_INNER, D_STATE)
    ys_p, ssm_p = ssd(zp, bp, lp, conv_prev_p, ssm0_p, conv_w[0], cb_row, dtb_row, aneg_row, dsk_row, gs_row)
    ys_s, ssm_s = ssd(zs, bs, ls, conv_prev_s, ssm0_s, conv_w[0], cb_row, dtb_row, aneg_row, dsk_row, gs_row)

    oa_p = dsa_prompt(zp, bp, lp)
    oa_s = dsa_sample(zs, bs, ls, cache_k[0], cache_v[0], cache_idx_k[0], page_table)

    mw = (b_gate[0].reshape(1, -1), w_ssd_out[0].astype(BF16), w_attn_out[0].astype(BF16),
          w_mem_out[0].astype(BF16), w_out[0].astype(BF16), g_norm2[0].reshape(1, D_MODEL), wr_pad, br_pad)
    x1_p, h2_p, te_p, gw_p = merge(xp, zp, ys_p, oa_p, om_p, *mw, 512)
    x1_s, h2_s, te_s, gw_s = merge(xs, zs, ys_s, oa_s, om_s, *mw, ns)

    cat = lambda a, b: jnp.concatenate([a, b], axis=0)
    y_all = moe_and_final_norm(cat(x1_p, x1_s), cat(h2_p, h2_s), cat(te_p, te_s), cat(gw_p, gw_s),
                               w_exp1[0], b1p, w_exp2[0], b2, g_final)
    y_prompt = y_all[:np_].reshape(bp, lp, D_MODEL)
    y_sample = y_all[np_:].reshape(bs, ls, D_MODEL)

    def kvi(z, b, l):
        k = z[:, OFF_K:OFF_K + N_KV_HEADS * HEAD_DIM].reshape(1, b, l, N_KV_HEADS, HEAD_DIM)
        v = z[:, OFF_V:OFF_V + N_KV_HEADS * HEAD_DIM].reshape(1, b, l, N_KV_HEADS, HEAD_DIM)
        ki = z[:, OFF_SM + SM_KI:OFF_SM + SM_KI + IDX_DIM].reshape(1, b, l, IDX_DIM)
        conv = z[:, OFF_XBC:OFF_XBC + CONV_DIM].reshape(b, l, CONV_DIM)[:, l - (CONV_WIDTH - 1):, :][None]
        return k, v, ki, conv

    k_p, v_p, ki_p, conv_p = kvi(zp, bp, lp)
    k_s, v_s, ki_s, conv_s = kvi(zs, bs, ls)
    mk_p = kv_p[:, :MEM_WIDTH].reshape(1, bp, N_MEM, MEM_HEADS, MEM_HEAD_DIM)
    mv_p = kv_p[:, MEM_WIDTH:].reshape(1, bp, N_MEM, MEM_HEADS, MEM_HEAD_DIM)
    ssm_shape = (1, -1, SSD_HEADS, SSD_HEAD_DIM, D_STATE)
    return (y_prompt, y_sample, k_p, v_p, ki_p, conv_p, ssm_p.reshape(ssm_shape), mk_p, mv_p,
            k_s, v_s, ki_s, conv_s, ssm_s.reshape(ssm_shape))
```

```python
import functools

import numpy as np
import jax
import jax.numpy as jnp
from jax import lax
from jax.experimental import pallas as pl
from jax.experimental.pallas import tpu as pltpu

F32 = jnp.float32
BF16 = jnp.bfloat16
I32 = jnp.int32
HIGHEST = lax.Precision.HIGHEST

D_MODEL = 1024
D_INNER = 2048
SSD_HEAD_DIM = 64
SSD_HEADS = 32
SSD_GROUPS = 4
D_STATE = 128
CONV_WIDTH = 4
CONV_DIM = D_INNER + 2 * SSD_GROUPS * D_STATE
SSD_CHUNK = 128
N_HEADS = 16
N_KV_HEADS = 4
HEAD_DIM = 64
IDX_HEADS = 8
IDX_DIM = 64
TOPK_MAX = 256
N_MEM = 256
MEM_HEADS = 4
MEM_HEAD_DIM = 256
MEM_WIDTH = MEM_HEADS * MEM_HEAD_DIM
N_EXPERTS = 32
TOP_K = 4
D_FF = D_MODEL
SWIGLU_LIMIT = 7.0
SWIGLU_ALPHA = 1.702
N_BRANCH = 3
EPS = 1e-6
PAGE_SIZE = 128
IN_WIDTHS = (D_INNER, CONV_DIM, SSD_HEADS, N_HEADS * HEAD_DIM, N_KV_HEADS * HEAD_DIM, N_KV_HEADS * HEAD_DIM,
             IDX_HEADS * IDX_DIM, IDX_DIM, IDX_HEADS, MEM_WIDTH, N_BRANCH * D_MODEL)

LANES = 128
SUBLANES = 8
VMEM_LIMIT = 56 * 1024 * 1024

OFF_XBC = 0
OFF_GATE = OFF_XBC + CONV_DIM
OFF_Z = OFF_GATE + N_BRANCH * D_MODEL
OFF_Q = OFF_Z + D_INNER
OFF_QM = OFF_Q + N_HEADS * HEAD_DIM
OFF_QI = OFF_QM + MEM_WIDTH
OFF_K = OFF_QI + IDX_HEADS * IDX_DIM
OFF_V = OFF_K + N_KV_HEADS * HEAD_DIM
OFF_SM = OFF_V + N_KV_HEADS * HEAD_DIM
SM_KI = 0
SM_DT = SM_KI + IDX_DIM
SM_WI = SM_DT + SSD_HEADS
IN_PROJ_TN = 1280
W_ALL = OFF_SM + 2 * LANES

NEG = -1e30
INT_MIN = np.int32(-2 ** 31)
INT_MAX = np.int32(2 ** 31 - 1)

MOE_BM = 512
MOE_T = 256


def _cparams(*sem):
    return pltpu.CompilerParams(dimension_semantics=sem, vmem_limit_bytes=VMEM_LIMIT)


def _nt_dot(a, b):
    return lax.dot_general(a, b, (((1,), (1,)), ((), ())), preferred_element_type=F32)


def _float_key(x):
    bits = lax.bitcast_convert_type(x, I32)
    return jnp.where(bits < 0, bits ^ INT_MAX, bits)


def _norm_matmul_body(x_ref, g_ref, w_ref, o_ref, h_ref):
    @pl.when(pl.program_id(1) == 0)
    def _():
        x = x_ref[...]
        h = x * lax.rsqrt(jnp.mean(x * x, axis=-1, keepdims=True) + EPS)
        h_ref[...] = (h * g_ref[...]).astype(BF16)

    o_ref[...] = jnp.dot(h_ref[...], w_ref[...], preferred_element_type=F32)


def norm_matmul(x, g, w, tm, tn):
    n, d = x.shape
    wn = w.shape[1]
    return pl.pallas_call(
        _norm_matmul_body,
        grid=(n // tm, wn // tn),
        in_specs=[pl.BlockSpec((tm, d), lambda i, j: (i, 0)),
                  pl.BlockSpec((1, d), lambda i, j: (0, 0)),
                  pl.BlockSpec((d, tn), lambda i, j: (0, j))],
        out_specs=pl.BlockSpec((tm, tn), lambda i, j: (i, j)),
        out_shape=jax.ShapeDtypeStruct((n, wn), F32),
        scratch_shapes=[pltpu.VMEM((tm, d), BF16)],
        compiler_params=_cparams("parallel", "arbitrary"),
        name="norm_matmul",
    )(x, g.reshape(1, d), w)


def _softplus(x):
    return jnp.maximum(x, 0.0) + jnp.log1p(jnp.exp(-jnp.abs(x)))


def _silu(x):
    return x * jax.nn.sigmoid(x)


def _expand_heads(v, q):
    lane = lax.broadcasted_iota(I32, (q, LANES), 1)
    cols = []
    for t in range(SSD_HEADS // 2):
        c0 = jnp.broadcast_to(v[:, SM_DT + 2 * t:SM_DT + 2 * t + 1], (q, LANES))
        c1 = jnp.broadcast_to(v[:, SM_DT + 2 * t + 1:SM_DT + 2 * t + 2], (q, LANES))
        cols.append(jnp.where(lane < SSD_HEAD_DIM, c0, c1))
    return jnp.concatenate(cols, axis=1)


def _ssd_body(xbc_ref, z_ref, sm_ref, convp_ref, init_ref, cw_ref, cb_ref, dtb_ref, aneg_ref, dsk_ref, gs_ref,
              y_ref, st_ref, xpad_ref, state_ref, *, rows_in, q):
    c = pl.program_id(1)
    pad = SUBLANES

    @pl.when(c == 0)
    def _():
        xpad_ref[0:pad, :] = convp_ref[0]
        state_ref[...] = init_ref[0]

    xpad_ref[pad:pad + rows_in, :] = xbc_ref[...]
    if rows_in < q:
        xpad_ref[pad + rows_in:pad + q, :] = jnp.zeros((q - rows_in, CONV_DIM), F32)

    acc = cb_ref[...]
    for j in range(CONV_WIDTH):
        lo = pad - (CONV_WIDTH - 1) + j
        acc = acc + xpad_ref[lo:lo + q, :] * cw_ref[j:j + 1, :]
    xc = _silu(acc)
    xpad_ref[0:pad, :] = xpad_ref[q:q + pad, :]

    xs = xc[:, :D_INNER]
    gn = SSD_GROUPS * D_STATE
    bm = xc[:, D_INNER:D_INNER + gn].astype(BF16)
    cm = xc[:, D_INNER + gn:].astype(BF16)

    sm = sm_ref[...]
    zz = z_ref[...]
    if rows_in < q:
        sm = jnp.concatenate([sm, jnp.zeros((q - rows_in, LANES), F32)], axis=0)
        zz = jnp.concatenate([zz, jnp.zeros((q - rows_in, D_INNER), F32)], axis=0)
    row = lax.broadcasted_iota(I32, (q, LANES), 0)
    dt = _softplus(sm + dtb_ref[...])
    if rows_in < q:
        dt = jnp.where(row < rows_in, dt, 0.0)
    a = dt * aneg_ref[...]
    tri = (lax.broadcasted_iota(I32, (q, q), 0) >= lax.broadcasted_iota(I32, (q, q), 1)).astype(F32)
    a_cs = jnp.dot(tri, a, precision=HIGHEST, preferred_element_type=F32)
    a_t = a_cs.T
    a_last = a_cs[q - 1:q, :]
    dte = jnp.exp(a_last - a_cs)
    e_in = jnp.exp(a_cs)

    xdt = xs * _expand_heads(dt, q)
    xdt_bf = xdt.astype(BF16)
    xw_bf = (xdt * _expand_heads(dte, q)).astype(BF16)
    ein_x = _expand_heads(e_in, q)

    causal = lax.broadcasted_iota(I32, (q, q), 0) >= lax.broadcasted_iota(I32, (q, q), 1)
    lane = lax.broadcasted_iota(I32, (q, LANES), 1)
    hpg = SSD_HEADS // SSD_GROUPS
    gw = hpg * SSD_HEAD_DIM
    y_parts = []
    for g in range(SSD_GROUPS):
        bg = bm[:, g * D_STATE:(g + 1) * D_STATE]
        cg = cm[:, g * D_STATE:(g + 1) * D_STATE]
        cb = _nt_dot(cg, bg)
        m_h = []
        for e in range(hpg):
            h = g * hpg + e
            col = a_cs[:, SM_DT + h:SM_DT + h + 1]
            rw = a_t[SM_DT + h:SM_DT + h + 1, :]
            decay = jnp.exp(jnp.where(causal, col - rw, -jnp.inf))
            m_h.append((cb * decay).astype(BF16))
        yd = []
        for t in range(hpg // 2):
            pair = g * (hpg // 2) + t
            slab = xdt_bf[:, pair * LANES:(pair + 1) * LANES]
            ya = jnp.dot(m_h[2 * t], slab, preferred_element_type=F32)
            yb = jnp.dot(m_h[2 * t + 1], slab, preferred_element_type=F32)
            yd.append(jnp.where(lane < SSD_HEAD_DIM, ya, yb))
        s_old = state_ref[g * gw:(g + 1) * gw, :]
        y_off = _nt_dot(cg, s_old.astype(BF16)) * ein_x[:, g * gw:(g + 1) * gw]
        y_parts.append(jnp.concatenate(yd, axis=1) + y_off)
        new = lax.dot_general(xw_bf[:, g * gw:(g + 1) * gw], bg, (((0,), (0,)), ((), ())),
                              preferred_element_type=F32)
        for e in range(hpg):
            h = g * hpg + e
            dec = jnp.exp(a_t[SM_DT + h:SM_DT + h + 1, q - 1:q])
            lo = e * SSD_HEAD_DIM
            state_ref[h * SSD_HEAD_DIM:(h + 1) * SSD_HEAD_DIM, :] = (
                s_old[lo:lo + SSD_HEAD_DIM, :] * dec + new[lo:lo + SSD_HEAD_DIM, :])

    y = jnp.concatenate(y_parts, axis=1) + dsk_ref[...] * xs
    y = y * _silu(zz)
    outs = []
    for g in range(SSD_GROUPS):
        yg = y[:, g * gw:(g + 1) * gw]
        outs.append(yg * lax.rsqrt(jnp.mean(yg * yg, axis=-1, keepdims=True) + EPS))
    y = jnp.concatenate(outs, axis=1) * gs_ref[...]
    y_ref[...] = y[:rows_in].astype(BF16)

    @pl.when(c == pl.num_programs(1) - 1)
    def _():
        st_ref[0] = state_ref[...]


def ssd(zall, batch, seq, conv_prev8, ssm_init, conv_w, conv_b, dtb_row, aneg_row, dsk_row, gs_row):
    q = SSD_CHUNK
    rows_in = min(seq, q)
    nch = seq // rows_in
    row_map = lambda b, c: (b * nch + c)
    const2 = lambda b, c: (0, 0)
    body = functools.partial(_ssd_body, rows_in=rows_in, q=q)
    return pl.pallas_call(
        body,
        grid=(batch, nch),
        in_specs=[pl.BlockSpec((rows_in, CONV_DIM), lambda b, c: (row_map(b, c), OFF_XBC // CONV_DIM)),
                  pl.BlockSpec((rows_in, D_INNER), lambda b, c: (row_map(b, c), OFF_Z // D_INNER)),
                  pl.BlockSpec((rows_in, LANES), lambda b, c: (row_map(b, c), OFF_SM // LANES)),
                  pl.BlockSpec((1, SUBLANES, CONV_DIM), lambda b, c: (b, 0, 0)),
                  pl.BlockSpec((1, D_INNER, D_STATE), lambda b, c: (b, 0, 0)),
                  pl.BlockSpec((CONV_WIDTH, CONV_DIM), const2),
                  pl.BlockSpec((1, CONV_DIM), const2),
                  pl.BlockSpec((1, LANES), const2),
                  pl.BlockSpec((1, LANES), const2),
                  pl.BlockSpec((1, D_INNER), const2),
                  pl.BlockSpec((1, D_INNER), const2)],
        out_specs=[pl.BlockSpec((rows_in, D_INNER), lambda b, c: (row_map(b, c), 0)),
                   pl.BlockSpec((1, D_INNER, D_STATE), lambda b, c: (b, 0, 0))],
        out_shape=[jax.ShapeDtypeStruct((batch * seq, D_INNER), BF16),
                   jax.ShapeDtypeStruct((batch, D_INNER, D_STATE), F32)],
        scratch_shapes=[pltpu.VMEM((q + 2 * SUBLANES, CONV_DIM), F32),
                        pltpu.VMEM((D_INNER, D_STATE), F32)],
        compiler_params=_cparams("parallel", "arbitrary"),
        name="ssd",
    )(zall, zall, zall, conv_prev8, ssm_init, conv_w, conv_b, dtb_row, aneg_row, dsk_row, gs_row)


def _mem_attn_body(q_ref, k_ref, v_ref, o_ref):
    for h in range(MEM_HEADS):
        sl = slice(h * MEM_HEAD_DIM, (h + 1) * MEM_HEAD_DIM)
        s = _nt_dot(q_ref[:, sl].astype(BF16), k_ref[:, sl].astype(BF16)) * (MEM_HEAD_DIM ** -0.5)
        m = jnp.max(s, axis=-1, keepdims=True)
        p = jnp.exp(s - m)
        p = p / jnp.sum(p, axis=-1, keepdims=True)
        o = jnp.dot(p.astype(BF16), v_ref[:, sl].astype(BF16), preferred_element_type=F32)
        o_ref[:, sl] = o.astype(BF16)


def mem_attn(zall, batch, seq, k_arr, k_col, v_arr, v_col, tm):
    nt = seq // tm
    return pl.pallas_call(
        _mem_attn_body,
        grid=(batch, nt),
        in_specs=[pl.BlockSpec((tm, MEM_WIDTH), lambda b, i: (b * nt + i, OFF_QM // MEM_WIDTH)),
                  pl.BlockSpec((N_MEM, MEM_WIDTH), lambda b, i: (b, k_col)),
                  pl.BlockSpec((N_MEM, MEM_WIDTH), lambda b, i: (b, v_col))],
        out_specs=pl.BlockSpec((tm, MEM_WIDTH), lambda b, i: (b * nt + i, 0)),
        out_shape=jax.ShapeDtypeStruct((batch * seq, MEM_WIDTH), BF16),
        compiler_params=_cparams("parallel", "arbitrary"),
        name="mem_attn",
    )(zall, k_arr, v_arr)


def _kth_largest_key(count_ge, shape, n_sel):
    def bit_body(t, ans):
        cand = ans | jnp.left_shift(jnp.int32(1), 31 - t)
        cnt = count_ge(cand ^ INT_MIN)
        return jnp.where(cnt >= n_sel, cand, ans)

    ans = lax.fori_loop(0, 32, bit_body, jnp.zeros(shape, I32))
    return ans ^ INT_MIN


def _tie_cut(count_eq_below, need, shape, nbits):
    def bit_body(t, lo):
        cand = lo | jnp.left_shift(jnp.int32(1), nbits - 1 - t)
        cnt = count_eq_below(cand)
        return jnp.where(cnt < need, cand, lo)

    return lax.fori_loop(0, nbits, bit_body, jnp.zeros(shape, I32))


def _select_bias(key, kpos, thr, cut, visible):
    sel = (key > thr) | ((key == thr) & (kpos <= cut))
    return jnp.where(sel & visible, 0.0, NEG)


def _dsa_prompt_body(q_ref, qi_ref, smq_ref, k_ref, v_ref, smk_ref, o_ref,
                     kh_ref, vt_ref, kis_ref, qt2_ref, qit_ref, keys_ref, bias_ref, ot_ref, *, tq, kc, seq, n_sel):
    i = pl.program_id(1)

    @pl.when(i == 0)
    def _():
        def cast_rows(r, carry):
            rs = pl.ds(pl.multiple_of(r * kc, kc), kc)
            kk = k_ref[rs, :]
            for h in range(N_KV_HEADS):
                kh_ref[h, rs, :] = kk[:, h * HEAD_DIM:(h + 1) * HEAD_DIM].astype(BF16)
            vt_ref[r] = v_ref[rs, :].T.astype(BF16)
            kis_ref[rs, :] = smk_ref[rs, SM_KI:SM_KI + IDX_DIM].astype(BF16)
            return carry
        lax.fori_loop(0, seq // kc, cast_rows, 0)

    nkc = (i * tq + tq - 1) // kc + 1
    qpos = i * tq + lax.broadcasted_iota(I32, (kc, tq), 1)
    krow = lax.broadcasted_iota(I32, (kc, tq), 0)

    qt = (q_ref[...] * (HEAD_DIM ** -0.5)).T.astype(BF16)
    for h in range(N_HEADS):
        qt2_ref[:, h * tq:(h + 1) * tq] = qt[h * HEAD_DIM:(h + 1) * HEAD_DIM, :]
    qit_ref[...] = (qi_ref[...] * (IDX_DIM ** -0.5)).T.astype(BF16)
    wt = smq_ref[...].T[SM_WI:SM_WI + IDX_HEADS, :] * (IDX_HEADS ** -0.5)

    def score_chunk(c, carry):
        ks = pl.ds(pl.multiple_of(c * kc, kc), kc)
        kic = kis_ref[ks, :]
        sc = jnp.zeros((kc, tq), F32)
        for h in range(IDX_HEADS):
            d = jnp.dot(kic, qit_ref[h * IDX_DIM:(h + 1) * IDX_DIM, :], preferred_element_type=F32)
            sc = sc + jnp.maximum(d, 0.0) * wt[h:h + 1, :]
        key = _float_key(sc + 0.0)
        keys_ref[c] = jnp.where(c * kc + krow <= qpos, key, INT_MIN)
        return carry
    lax.fori_loop(0, nkc, score_chunk, 0)

    def count(pred):
        def body(c, acc):
            return acc + jnp.where(pred(keys_ref[c], c * kc + krow), 1.0, 0.0)
        acc = lax.fori_loop(0, nkc, body, jnp.zeros((kc, tq), F32))
        return jnp.sum(acc, axis=0, keepdims=True)

    vec = (1, tq)
    thr = _kth_largest_key(lambda cand: count(lambda key, kpos: key >= cand), vec, n_sel)
    n_gt = count(lambda key, kpos: key > thr)
    n_eq = count(lambda key, kpos: key == thr)
    need = n_sel - n_gt
    excess = jnp.max(jnp.where((n_eq > need) & (thr != INT_MIN), 1.0, 0.0))
    nbits = max(1, int(seq - 1).bit_length())
    cut = lax.cond(
        excess > 0.0,
        lambda: _tie_cut(lambda cand: count(lambda key, kpos: (key == thr) & (kpos < cand)), need, vec, nbits),
        lambda: jnp.full(vec, INT_MAX, I32))

    def bias_chunk(c, carry):
        kpos = c * kc + krow
        bias_ref[c] = _select_bias(keys_ref[c], kpos, thr, cut, kpos <= qpos)
        return carry
    lax.fori_loop(0, nkc, bias_chunk, 0)

    grp = N_HEADS // N_KV_HEADS
    def att_chunk(c, carry):
        ms, ls, accs = carry
        ks = pl.ds(pl.multiple_of(c * kc, kc), kc)
        bias = bias_ref[c]
        s4 = [jnp.dot(kh_ref[kh, ks, :], qt2_ref[:, kh * grp * tq:(kh + 1) * grp * tq],
                      preferred_element_type=F32) for kh in range(N_KV_HEADS)]
        ms_n, ls_n, ps, alphas = [], [], [], []
        for h in range(N_HEADS):
            s = s4[h // grp][:, (h % grp) * tq:(h % grp + 1) * tq] + bias
            m_new = jnp.maximum(ms[h], jnp.max(s, axis=0, keepdims=True))
            p = jnp.exp(s - m_new)
            alpha = jnp.exp(ms[h] - m_new)
            ms_n.append(m_new)
            ls_n.append(alpha * ls[h] + jnp.sum(p, axis=0, keepdims=True))
            ps.append(p.astype(BF16))
            alphas.append(alpha)
        accs_n = []
        for h in range(N_HEADS):
            kh = h // grp
            vtc = vt_ref[c, kh * HEAD_DIM:(kh + 1) * HEAD_DIM, :]
            accs_n.append(alphas[h] * accs[h] + jnp.dot(vtc, ps[h], preferred_element_type=F32))
        return tuple(ms_n), tuple(ls_n), tuple(accs_n)

    init = (tuple(jnp.full(vec, NEG, F32) for _ in range(N_HEADS)),
            tuple(jnp.zeros(vec, F32) for _ in range(N_HEADS)),
            tuple(jnp.zeros((HEAD_DIM, tq), F32) for _ in range(N_HEADS)))
    _, ls, accs = lax.fori_loop(0, nkc, att_chunk, init)
    for h in range(N_HEADS):
        ot_ref[h * HEAD_DIM:(h + 1) * HEAD_DIM, :] = accs[h] / ls[h]
    o_ref[...] = ot_ref[...].T.astype(BF16)


def dsa_prompt(zall, batch, seq):
    tq = 128
    kc = 256
    nq = seq // tq
    n_sel = min(TOPK_MAX, seq // 4)
    kvw = N_KV_HEADS * HEAD_DIM
    qw = N_HEADS * HEAD_DIM
    qiw = IDX_HEADS * IDX_DIM
    body = functools.partial(_dsa_prompt_body, tq=tq, kc=kc, seq=seq, n_sel=n_sel)
    return pl.pallas_call(
        body,
        grid=(batch, nq),
        in_specs=[pl.BlockSpec((tq, qw), lambda b, i: (b * nq + i, OFF_Q // qw)),
                  pl.BlockSpec((tq, qiw), lambda b, i: (b * nq + i, OFF_QI // qiw)),
                  pl.BlockSpec((tq, LANES), lambda b, i: (b * nq + i, OFF_SM // LANES)),
                  pl.BlockSpec((seq, kvw), lambda b, i: (b, OFF_K // kvw)),
                  pl.BlockSpec((seq, kvw), lambda b, i: (b, OFF_V // kvw)),
                  pl.BlockSpec((seq, LANES), lambda b, i: (b, OFF_SM // LANES))],
        out_specs=pl.BlockSpec((tq, qw), lambda b, i: (b * nq + i, 0)),
        out_shape=jax.ShapeDtypeStruct((batch * seq, qw), BF16),
        scratch_shapes=[pltpu.VMEM((N_KV_HEADS, seq, HEAD_DIM), BF16),
                        pltpu.VMEM((seq // kc, kvw, kc), BF16),
                        pltpu.VMEM((seq, IDX_DIM), BF16),
                        pltpu.VMEM((HEAD_DIM, N_HEADS * tq), BF16),
                        pltpu.VMEM((qiw, tq), BF16),
                        pltpu.VMEM((seq // kc, kc, tq), I32),
                        pltpu.VMEM((seq // kc, kc, tq), F32),
                        pltpu.VMEM((qw, tq), F32)],
        compiler_params=_cparams("parallel", "arbitrary"),
        name="dsa_prompt",
    )(zall, zall, zall, zall, zall, zall)


def _dsa_s_score_body(pt_ref, qs_ref, w_ref, *refs, pg):
    ki_refs, o_ref = refs[:pg], refs[pg]
    qs = qs_ref[0]
    wcol = w_ref[0] * (IDX_HEADS ** -0.5)
    t = qs.shape[0] // IDX_HEADS
    for p in range(pg):
        d = jnp.dot(qs, ki_refs[p][0].astype(BF16), preferred_element_type=F32)
        r = jnp.maximum(d * (IDX_DIM ** -0.5), 0.0) * wcol
        sc = r[0:t, :]
        for h in range(1, IDX_HEADS):
            sc = sc + r[h * t:(h + 1) * t, :]
        o_ref[0, :, p * PAGE_SIZE:(p + 1) * PAGE_SIZE] = sc + 0.0


def _dsa_s_select_body(sc_ref, qs_ref, w_ref, smn_ref, o_ref, *, t, past, n_sel):
    qs = qs_ref[0]
    wcol = w_ref[0] * (IDX_HEADS ** -0.5)
    ki_new = smn_ref[:, SM_KI:SM_KI + IDX_DIM].astype(BF16)
    ki_new = jnp.concatenate([ki_new, jnp.zeros((LANES - t, IDX_DIM), BF16)], axis=0)
    d = _nt_dot(qs, ki_new)
    r = jnp.maximum(d * (IDX_DIM ** -0.5), 0.0) * wcol
    sc_new = r[0:t, :]
    for h in range(1, IDX_HEADS):
        sc_new = sc_new + r[h * t:(h + 1) * t, :]
    lane_n = lax.broadcasted_iota(I32, (t, LANES), 1)
    vis_n = lane_n <= lax.broadcasted_iota(I32, (t, LANES), 0)
    key_n = jnp.where(vis_n, _float_key(sc_new + 0.0), INT_MIN)
    key_p = _float_key(sc_ref[0])
    pos_p = lax.broadcasted_iota(I32, (t, past), 1)
    pos_n = past + lane_n

    def count(pred):
        return (jnp.sum(jnp.where(pred(key_p, pos_p), 1.0, 0.0), axis=1, keepdims=True)
                + jnp.sum(jnp.where(pred(key_n, pos_n), 1.0, 0.0), axis=1, keepdims=True))

    thr = _kth_largest_key(lambda cand: count(lambda key, kpos: key >= cand), (t, 1), n_sel)
    need = n_sel - count(lambda key, kpos: key > thr)
    nbits = max(1, int(past + t - 1).bit_length())
    cut = _tie_cut(lambda cand: count(lambda key, kpos: (key == thr) & (kpos < cand)), need, (t, 1), nbits)
    o_ref[0, :, 0:past] = _select_bias(key_p, pos_p, thr, cut, pos_p >= 0)
    o_ref[0, :, past:past + LANES] = _select_bias(key_n, pos_n, thr, cut, vis_n)


def _dsa_s_attn_body(pt_ref, qbd_ref, bias_ref, biasn_ref, kn_ref, vn_ref, *refs, pg, t):
    k_refs, v_refs = refs[:pg], refs[pg:2 * pg]
    o_ref, m_ref, l_ref, acc_ref = refs[2 * pg:]
    j = pl.program_id(1)
    rows = qbd_ref.shape[1]
    rep = rows // t

    @pl.when(j == 0)
    def _():
        m_ref[...] = jnp.full(m_ref.shape, NEG, F32)
        l_ref[...] = jnp.zeros(l_ref.shape, F32)
        acc_ref[...] = jnp.zeros(acc_ref.shape, F32)

    qbd = qbd_ref[0]

    def update(kt, vt, bias):
        s = jnp.dot(qbd, kt, preferred_element_type=F32) * (HEAD_DIM ** -0.5) + jnp.concatenate([bias] * rep, axis=0)
        m = m_ref[...]
        m_new = jnp.maximum(m, jnp.max(s, axis=1, keepdims=True))
        p = jnp.exp(s - m_new)
        alpha = jnp.exp(m - m_new)
        l_ref[...] = alpha * l_ref[...] + jnp.sum(p, axis=1, keepdims=True)
        acc_ref[...] = alpha * acc_ref[...] + _nt_dot(p.astype(BF16), vt)
        m_ref[...] = m_new

    kt = jnp.concatenate([r[0] for r in k_refs], axis=1).astype(BF16)
    vt = jnp.concatenate([r[0] for r in v_refs], axis=1).astype(BF16)
    update(kt, vt, bias_ref[0])

    @pl.when(j == pl.num_programs(1) - 1)
    def _():
        kvw = N_KV_HEADS * HEAD_DIM
        zpad = jnp.zeros((LANES - t, kvw), F32)
        update(jnp.concatenate([kn_ref[...], zpad], axis=0).T.astype(BF16),
               jnp.concatenate([vn_ref[...], zpad], axis=0).T.astype(BF16), biasn_ref[0])
        o_ref[0] = acc_ref[...] / l_ref[...]


def dsa_sample(zs, batch, t, cache_k, cache_v, cache_ki, page_table):
    n_pages = page_table.shape[1]
    past = n_pages * PAGE_SIZE
    n_sel = min(TOPK_MAX, (past + t) // 4)
    pg = 8
    nj = n_pages // pg
    n_pool = cache_k.shape[0]
    kvw = N_KV_HEADS * HEAD_DIM
    grp = N_HEADS // N_KV_HEADS

    qi = zs[:, OFF_QI:OFF_QI + IDX_HEADS * IDX_DIM].reshape(batch, t, IDX_HEADS, IDX_DIM)
    qs = jnp.transpose(qi, (0, 2, 1, 3)).reshape(batch, IDX_HEADS * t, IDX_DIM).astype(BF16)
    wi = zs[:, OFF_SM + SM_WI:OFF_SM + SM_WI + IDX_HEADS].reshape(batch, t, IDX_HEADS)
    wcol = jnp.transpose(wi, (0, 2, 1)).reshape(batch, IDX_HEADS * t, 1)

    def page_spec(shape, p):
        return pl.BlockSpec(shape, lambda b, j, pt: (pt[b, j * pg + p],) + (0,) * (len(shape) - 1))

    scores = pl.pallas_call(
        functools.partial(_dsa_s_score_body, pg=pg),
        grid_spec=pltpu.PrefetchScalarGridSpec(
            num_scalar_prefetch=1,
            grid=(batch, nj),
            in_specs=[pl.BlockSpec((1, IDX_HEADS * t, IDX_DIM), lambda b, j, pt: (b, 0, 0)),
                      pl.BlockSpec((1, IDX_HEADS * t, 1), lambda b, j, pt: (b, 0, 0))]
                     + [page_spec((1, IDX_DIM, PAGE_SIZE), p) for p in range(pg)],
            out_specs=pl.BlockSpec((1, t, pg * PAGE_SIZE), lambda b, j, pt: (b, 0, j))),
        out_shape=jax.ShapeDtypeStruct((batch, t, past), F32),
        compiler_params=_cparams("parallel", "arbitrary"),
        name="dsa_sample_scores",
    )(page_table, qs, wcol, *([jnp.swapaxes(cache_ki, 1, 2)] * pg))

    bias = pl.pallas_call(
        functools.partial(_dsa_s_select_body, t=t, past=past, n_sel=n_sel),
        grid=(batch,),
        in_specs=[pl.BlockSpec((1, t, past), lambda b: (b, 0, 0)),
                  pl.BlockSpec((1, IDX_HEADS * t, IDX_DIM), lambda b: (b, 0, 0)),
                  pl.BlockSpec((1, IDX_HEADS * t, 1), lambda b: (b, 0, 0)),
                  pl.BlockSpec((t, LANES), lambda b: (b, OFF_SM // LANES))],
        out_specs=pl.BlockSpec((1, t, past + LANES), lambda b: (b, 0, 0)),
        out_shape=jax.ShapeDtypeStruct((batch, t, past + LANES), F32),
        compiler_params=_cparams("parallel"),
        name="dsa_sample_select",
    )(scores, qs, wcol, zs)

    q = zs[:, OFF_Q:OFF_Q + N_HEADS * HEAD_DIM].reshape(batch, t, N_KV_HEADS, grp, HEAD_DIM)
    q = jnp.transpose(q, (0, 2, 3, 1, 4))
    eye = jnp.eye(N_KV_HEADS, dtype=F32)
    qbd = (q[:, :, :, :, None, :] * eye[None, :, None, None, :, None]).reshape(batch, N_HEADS * t, kvw).astype(BF16)

    ck = jnp.transpose(cache_k, (0, 2, 3, 1)).reshape(n_pool, kvw, PAGE_SIZE)
    cv = jnp.transpose(cache_v, (0, 2, 3, 1)).reshape(n_pool, kvw, PAGE_SIZE)
    rows = N_HEADS * t
    out = pl.pallas_call(
        functools.partial(_dsa_s_attn_body, pg=pg, t=t),
        grid_spec=pltpu.PrefetchScalarGridSpec(
            num_scalar_prefetch=1,
            grid=(batch, nj),
            in_specs=[pl.BlockSpec((1, rows, kvw), lambda b, j, pt: (b, 0, 0)),
                      pl.BlockSpec((1, t, pg * PAGE_SIZE), lambda b, j, pt: (b, 0, j)),
                      pl.BlockSpec((1, t, LANES), lambda b, j, pt: (b, 0, past // LANES)),
                      pl.BlockSpec((t, kvw), lambda b, j, pt: (b, OFF_K // kvw)),
                      pl.BlockSpec((t, kvw), lambda b, j, pt: (b, OFF_V // kvw))]
                     + [page_spec((1, kvw, PAGE_SIZE), p) for p in range(pg)]
                     + [page_spec((1, kvw, PAGE_SIZE), p) for p in range(pg)],
            out_specs=pl.BlockSpec((1, rows, kvw), lambda b, j, pt: (b, 0, 0)),
            scratch_shapes=[pltpu.VMEM((rows, 1), F32), pltpu.VMEM((rows, 1), F32), pltpu.VMEM((rows, kvw), F32)]),
        out_shape=jax.ShapeDtypeStruct((batch, rows, kvw), F32),
        compiler_params=_cparams("parallel", "arbitrary"),
        name="dsa_sample_attn",
    )(page_table, qbd, bias, bias, zs, zs, *([ck] * pg), *([cv] * pg))

    o = out.reshape(batch, N_KV_HEADS, grp, t, N_KV_HEADS, HEAD_DIM)
    o = jnp.stack([o[:, kh, :, :, kh, :] for kh in range(N_KV_HEADS)], axis=1)
    return jnp.transpose(o, (0, 3, 1, 2, 4)).reshape(batch * t, N_HEADS * HEAD_DIM).astype(BF16)


def _merge_body(x_ref, gate_ref, ys_ref, oa_ref, om_ref, bg_ref, ws_ref, wa_ref, wm_ref, wo_ref, g2_ref, wr_ref, br_ref,
                x1_ref, h2_ref, te_ref, gw_ref):
    gates = jax.nn.sigmoid(gate_ref[...] + bg_ref[...])
    merged = (gates[:, 0:D_MODEL] * jnp.dot(ys_ref[...], ws_ref[...], preferred_element_type=F32)
              + gates[:, D_MODEL:2 * D_MODEL] * jnp.dot(oa_ref[...], wa_ref[...], preferred_element_type=F32)
              + gates[:, 2 * D_MODEL:] * jnp.dot(om_ref[...], wm_ref[...], preferred_element_type=F32))
    x1 = x_ref[...] + jnp.dot(merged.astype(BF16), wo_ref[...], preferred_element_type=F32)
    x1_ref[...] = x1
    h2 = x1 * lax.rsqrt(jnp.mean(x1 * x1, axis=-1, keepdims=True) + EPS)
    h2 = h2 * g2_ref[...]
    h2_ref[...] = h2
    logits = jnp.dot(h2, wr_ref[...], precision=HIGHEST, preferred_element_type=F32) + br_ref[...]
    lane = lax.broadcasted_iota(I32, logits.shape, 1)
    te = jnp.zeros(logits.shape, I32)
    tv = []
    for k in range(TOP_K):
        m = jnp.max(logits, axis=1, keepdims=True)
        idx = jnp.min(jnp.where(logits == m, lane, LANES), axis=1, keepdims=True)
        te = jnp.where(lane == k, idx, te)
        tv.append(m)
        logits = jnp.where(lane == idx, -jnp.inf, logits)
    ex = [jnp.exp(v - tv[0]) for v in tv]
    den = ex[0] + ex[1] + ex[2] + ex[3]
    gw = jnp.zeros(logits.shape, F32)
    for k in range(TOP_K):
        gw = jnp.where(lane == k, ex[k] / den, gw)
    te_ref[...] = te
    gw_ref[...] = gw


def merge(x, zall, ys, oa, om, bg, ws, wa, wm, wo, g2, wr, br, tm):
    n = x.shape[0]
    gw3 = N_BRANCH * D_MODEL
    row = lambda i: (i, 0)
    const = lambda i: (0, 0)
    return pl.pallas_call(
        _merge_body,
        grid=(n // tm,),
        in_specs=[pl.BlockSpec((tm, D_MODEL), row),
                  pl.BlockSpec((tm, gw3), lambda i: (i, OFF_GATE // gw3)),
                  pl.BlockSpec((tm, D_INNER), row),
                  pl.BlockSpec((tm, N_HEADS * HEAD_DIM), row),
                  pl.BlockSpec((tm, MEM_WIDTH), row),
                  pl.BlockSpec((1, gw3), const),
                  pl.BlockSpec((D_INNER, D_MODEL), const),
                  pl.BlockSpec((N_HEADS * HEAD_DIM, D_MODEL), const),
                  pl.BlockSpec((MEM_WIDTH, D_MODEL), const),
                  pl.BlockSpec((D_MODEL, D_MODEL), const),
                  pl.BlockSpec((1, D_MODEL), const),
                  pl.BlockSpec((D_MODEL, LANES), const),
                  pl.BlockSpec((1, LANES), const)],
        out_specs=[pl.BlockSpec((tm, D_MODEL), row), pl.BlockSpec((tm, D_MODEL), row),
                   pl.BlockSpec((tm, LANES), row), pl.BlockSpec((tm, LANES), row)],
        out_shape=[jax.ShapeDtypeStruct((n, D_MODEL), F32), jax.ShapeDtypeStruct((n, D_MODEL), F32),
                   jax.ShapeDtypeStruct((n, LANES), I32), jax.ShapeDtypeStruct((n, LANES), F32)],
        compiler_params=_cparams("parallel"),
        name="merge",
    )(x, zall, ys, oa, om, bg, ws, wa, wm, wo, g2, wr, br)


def _moe_pos_body(te_ref, pos_ref, cnt_ref, carry_ref):
    i = pl.program_id(0)
    tt = te_ref.shape[0]

    @pl.when(i == 0)
    def _():
        carry_ref[...] = jnp.zeros(carry_ref.shape, F32)

    te = te_ref[...]
    lane = lax.broadcasted_iota(I32, (tt, LANES), 1)
    onehot = [lane == te[:, k:k + 1] for k in range(TOP_K)]
    msum = jnp.zeros((tt, LANES), F32)
    for k in range(TOP_K):
        msum = msum + jnp.where(onehot[k], 1.0, 0.0)
    strict = (lax.broadcasted_iota(I32, (tt, tt), 0) > lax.broadcasted_iota(I32, (tt, tt), 1))
    prefix = jnp.dot(jnp.where(strict, 1.0, 0.0).astype(BF16), msum.astype(BF16), preferred_element_type=F32)
    prefix = prefix + carry_ref[0:1, :]
    pos = jnp.zeros((tt, LANES), F32)
    for k in range(TOP_K):
        pk = jnp.sum(jnp.where(onehot[k], prefix, 0.0), axis=1, keepdims=True)
        pos = jnp.where(lane == k, pk, pos)
    pos_ref[...] = pos
    carry_ref[...] = carry_ref[...] + jnp.sum(msum, axis=0, keepdims=True)
    cnt_ref[...] = carry_ref[...]


def _moe_dest_body(te_ref, pos_ref, cnt_ref, dest_ref, be_ref, nu_ref, *, bm):
    tt = te_ref.shape[0]
    cnt = cnt_ref[...]
    padded = jnp.floor((cnt + (bm - 1)) * (1.0 / bm)) * bm
    upper = (lax.broadcasted_iota(I32, (LANES, LANES), 0) < lax.broadcasted_iota(I32, (LANES, LANES), 1))
    pad_start = jnp.dot(padded, jnp.where(upper, 1.0, 0.0), precision=HIGHEST, preferred_element_type=F32)
    pad_end = pad_start + padded
    te = te_ref[...]
    pos = pos_ref[...]
    lane = lax.broadcasted_iota(I32, (tt, LANES), 1)
    dest = jnp.zeros((tt, LANES), F32)
    for k in range(TOP_K):
        ps = jnp.sum(jnp.where(lane == te[:, k:k + 1], pad_start[0:1, :], 0.0), axis=1, keepdims=True)
        dest = jnp.where(lane == k, ps + pos[:, k:k + 1], dest)
    dest_ref[...] = dest.astype(I32)
    nb = be_ref.shape[0]
    bstart = (lax.broadcasted_iota(I32, (nb, LANES), 0) * bm).astype(F32)
    lane_b = lax.broadcasted_iota(I32, (nb, LANES), 1)
    done = jnp.where((pad_end[0:1, :] <= bstart) & (lane_b < N_EXPERTS), 1.0, 0.0)
    be = jnp.minimum(jnp.sum(done, axis=1, keepdims=True), N_EXPERTS - 1.0)
    be_ref[...] = jnp.broadcast_to(be, (nb, LANES)).astype(I32)
    total = jnp.sum(padded[0:1, :], axis=1, keepdims=True)
    nu_ref[...] = jnp.broadcast_to(total * (1.0 / bm), nu_ref.shape).astype(I32)


def _moe_dispatch_body(dest_ref, h_ref, xs_in_ref, xs_ref, sem):
    del xs_in_ref
    tt = h_ref.shape[0]

    def copy(r, k):
        d = dest_ref[r * TOP_K + k]
        return pltpu.make_async_copy(h_ref.at[pl.ds(r, 1), :], xs_ref.at[pl.ds(d, 1), :], sem)

    def issue(r, carry):
        for k in range(TOP_K):
            copy(r, k).start()
        return carry
    lax.fori_loop(0, tt, issue, 0)

    def drain(r, carry):
        for k in range(TOP_K):
            copy(r, k).wait()
        return carry
    lax.fori_loop(0, tt, drain, 0)


def _moe_expert_body(be_ref, nu_ref, xs_ref, w1_ref, b1_ref, w2_ref, b2_ref, o_ref, w1s_ref, w2s_ref):
    i = pl.program_id(0)
    used = i < nu_ref[0]
    e = be_ref[i]
    prev = be_ref[jnp.maximum(i - 1, 0)]
    half = LANES

    @pl.when(used & ((i == 0) | (e != prev)))
    def _():
        r = lax.broadcasted_iota(I32, (2 * half, 2 * half), 0)
        c = lax.broadcasted_iota(I32, (2 * half, 2 * half), 1)
        src_col = jnp.where(c < half, 2 * c, 2 * (c - half) + 1)
        perm = jnp.where(r == src_col, 1.0, 0.0).astype(BF16)
        for j in range(2 * D_FF // (2 * half)):
            sl = slice(j * 2 * half, (j + 1) * 2 * half)
            w1s_ref[:, sl] = jnp.dot(w1_ref[0, :, sl].astype(BF16), perm, preferred_element_type=F32).astype(BF16)
        w2s_ref[...] = w2_ref[0].astype(BF16)

    @pl.when(used)
    def _():
        u = jnp.dot(xs_ref[...].astype(BF16), w1s_ref[...], preferred_element_type=F32) + b1_ref[0]
        acts = []
        for j in range(D_FF // half):
            glu = jnp.minimum(u[:, 2 * j * half:(2 * j + 1) * half], SWIGLU_LIMIT)
            lin = jnp.clip(u[:, (2 * j + 1) * half:(2 * j + 2) * half], -SWIGLU_LIMIT, SWIGLU_LIMIT)
            acts.append((glu * jax.nn.sigmoid(SWIGLU_ALPHA * glu) * (lin + 1.0)).astype(BF16))
        act = jnp.concatenate(acts, axis=1)
        o_ref[...] = jnp.dot(act, w2s_ref[...], preferred_element_type=F32) + b2_ref[0]

    @pl.when(jnp.logical_not(used))
    def _():
        o_ref[...] = jnp.zeros(o_ref.shape, F32)


def _moe_combine_body(dest_ref, gw_ref, x1_ref, gf_ref, os_ref, y_ref, buf_ref, sem):
    tt = x1_ref.shape[0]

    def copy(r, k):
        d = dest_ref[r * TOP_K + k]
        return pltpu.make_async_copy(os_ref.at[pl.ds(d, 1), :], buf_ref.at[k, pl.ds(r, 1), :], sem)

    def issue(r, carry):
        for k in range(TOP_K):
            copy(r, k).start()
        return carry
    lax.fori_loop(0, tt, issue, 0)

    def drain(r, carry):
        for k in range(TOP_K):
            copy(r, k).wait()
        return carry
    lax.fori_loop(0, tt, drain, 0)

    gw = gw_ref[...]
    y = gw[:, 0:1] * buf_ref[0]
    for k in range(1, TOP_K):
        y = y + gw[:, k:k + 1] * buf_ref[k]
    x2 = x1_ref[...] + y
    out = x2 * lax.rsqrt(jnp.mean(x2 * x2, axis=-1, keepdims=True) + EPS)
    y_ref[...] = out * gf_ref[...]


def moe_and_final_norm(x1, h2, te, gw, w1, b1p, w2, b2, g_final):
    n = x1.shape[0]
    tt = MOE_T
    bm = MOE_BM
    nb = -(-(n * TOP_K + N_EXPERTS * (bm - 1)) // bm)
    nbp = -(-nb // SUBLANES) * SUBLANES
    row = lambda i: (i, 0)
    const = lambda i: (0, 0)

    pos, cnt = pl.pallas_call(
        _moe_pos_body,
        grid=(n // tt,),
        in_specs=[pl.BlockSpec((tt, LANES), row)],
        out_specs=[pl.BlockSpec((tt, LANES), row), pl.BlockSpec((SUBLANES, LANES), const)],
        out_shape=[jax.ShapeDtypeStruct((n, LANES), F32), jax.ShapeDtypeStruct((SUBLANES, LANES), F32)],
        scratch_shapes=[pltpu.VMEM((SUBLANES, LANES), F32)],
        compiler_params=_cparams("arbitrary"),
        name="moe_positions",
    )(te)

    dest, be, nu = pl.pallas_call(
        functools.partial(_moe_dest_body, bm=bm),
        grid=(n // tt,),
        in_specs=[pl.BlockSpec((tt, LANES), row), pl.BlockSpec((tt, LANES), row),
                  pl.BlockSpec((SUBLANES, LANES), const)],
        out_specs=[pl.BlockSpec((tt, LANES), row), pl.BlockSpec((nbp, LANES), const),
                   pl.BlockSpec((SUBLANES, LANES), const)],
        out_shape=[jax.ShapeDtypeStruct((n, LANES), I32), jax.ShapeDtypeStruct((nbp, LANES), I32),
                   jax.ShapeDtypeStruct((SUBLANES, LANES), I32)],
        compiler_params=_cparams("arbitrary"),
        name="moe_destinations",
    )(te, pos, cnt)
    dest_flat = dest[:, :TOP_K].reshape(n * TOP_K)
    block_e = be[:nb, 0]
    n_used = nu[0, 0:1]

    xs = pl.pallas_call(
        _moe_dispatch_body,
        grid=(n // tt,),
        in_specs=[pl.BlockSpec((tt * TOP_K,), lambda i: (i,), memory_space=pltpu.SMEM),
                  pl.BlockSpec((tt, D_MODEL), row),
                  pl.BlockSpec(memory_space=pl.ANY)],
        out_specs=pl.BlockSpec(memory_space=pl.ANY),
        out_shape=jax.ShapeDtypeStruct((nb * bm, D_MODEL), F32),
        scratch_shapes=[pltpu.SemaphoreType.DMA(())],
        input_output_aliases={2: 0},
        compiler_params=_cparams("arbitrary"),
        name="moe_dispatch",
    )(dest_flat, h2, jnp.zeros((nb * bm, D_MODEL), F32))

    out_sorted = pl.pallas_call(
        _moe_expert_body,
        grid_spec=pltpu.PrefetchScalarGridSpec(
            num_scalar_prefetch=2,
            grid=(nb,),
            in_specs=[pl.BlockSpec((bm, D_MODEL), lambda i, be_, nu_: (i, 0)),
                      pl.BlockSpec((1, D_MODEL, 2 * D_FF), lambda i, be_, nu_: (be_[i], 0, 0)),
                      pl.BlockSpec((1, 1, 2 * D_FF), lambda i, be_, nu_: (be_[i], 0, 0)),
                      pl.BlockSpec((1, D_FF, D_MODEL), lambda i, be_, nu_: (be_[i], 0, 0)),
                      pl.BlockSpec((1, 1, D_MODEL), lambda i, be_, nu_: (be_[i], 0, 0))],
            out_specs=pl.BlockSpec((bm, D_MODEL), lambda i, be_, nu_: (i, 0)),
            scratch_shapes=[pltpu.VMEM((D_MODEL, 2 * D_FF), BF16), pltpu.VMEM((D_FF, D_MODEL), BF16)]),
        out_shape=jax.ShapeDtypeStruct((nb * bm, D_MODEL), F32),
        compiler_params=_cparams("arbitrary"),
        name="moe_experts",
    )(block_e, n_used, xs, w1, b1p, w2, b2)

    return pl.pallas_call(
        _moe_combine_body,
        grid=(n // tt,),
        in_specs=[pl.BlockSpec((tt * TOP_K,), lambda i: (i,), memory_space=pltpu.SMEM),
                  pl.BlockSpec((tt, LANES), row),
                  pl.BlockSpec((tt, D_MODEL), row),
                  pl.BlockSpec((1, D_MODEL), const),
                  pl.BlockSpec(memory_space=pl.ANY)],
        out_specs=pl.BlockSpec((tt, D_MODEL), row),
        out_shape=jax.ShapeDtypeStruct((n, D_MODEL), F32),
        scratch_shapes=[pltpu.VMEM((TOP_K, tt, D_MODEL), F32), pltpu.SemaphoreType.DMA(())],
        compiler_params=_cparams("arbitrary"),
        name="moe_combine",
    )(dest_flat, gw, x1, g_final.reshape(1, D_MODEL), out_sorted)


def _split_cols(w):
    outs, off = [], 0
    for wd in IN_WIDTHS:
        outs.append(w[:, off:off + wd])
        off += wd
    return outs


def _lane_row(v, off):
    return jnp.zeros((1, LANES), F32).at[0, off:off + v.shape[0]].set(v.astype(F32))


def kernel(x_prompt, x_sample, cache_k, cache_v, cache_idx_k, state_conv, state_ssm, cache_mem_k, cache_mem_v,
           page_table, mem_prompt, g_norm1, w_in, b_gate, conv_w, conv_b, dt_bias, a_log, d_skip, g_ssd_norm,
           g_mem, w_mem_kv, w_ssd_out, w_attn_out, w_mem_out, w_out, g_norm2, w_router, b_router, w_exp1,
           b_exp1, w_exp2, b_exp2, g_final):
    assert w_in.shape[0] == 1, "single-layer trunk"
    bp, lp, _ = x_prompt.shape
    bs, ls, _ = x_sample.shape
    np_, ns = bp * lp, bs * ls

    wz, wxbc, wdt, wq, wk, wv, wqi, wki, wwi, wqm, wgate = _split_cols(w_in[0])
    w_all = jnp.concatenate(
        [wxbc, wgate, wz, wq, wqm, wqi, wk, wv, wki, wdt, wwi,
         jnp.zeros((D_MODEL, W_ALL - OFF_SM - SM_WI - IDX_HEADS), F32)], axis=1).astype(BF16)
    dtb_row = _lane_row(dt_bias[0], SM_DT)
    aneg_row = _lane_row(-jnp.exp(a_log[0].astype(F32)), SM_DT)
    dsk_row = jnp.repeat(d_skip[0].astype(F32), SSD_HEAD_DIM).reshape(1, D_INNER)
    gs_row = g_ssd_norm[0].reshape(1, D_INNER)
    cb_row = conv_b[0].reshape(1, CONV_DIM)
    wr_pad = jnp.zeros((D_MODEL, LANES), F32).at[:, :N_EXPERTS].set(w_router[0])
    br_pad = jnp.full((1, LANES), NEG, F32).at[0, :N_EXPERTS].set(b_router[0])
    b1p = b_exp1[0].reshape(N_EXPERTS, D_FF // LANES, LANES, 2).transpose(0, 1, 3, 2).reshape(N_EXPERTS, 1, 2 * D_FF)
    b2 = b_exp2[0].reshape(N_EXPERTS, 1, D_MODEL)

    xp = x_prompt.reshape(np_, D_MODEL)
    xs = x_sample.reshape(ns, D_MODEL)
    zp = norm_matmul(xp, g_norm1[0], w_all, 1024, IN_PROJ_TN)
    zs = norm_matmul(xs, g_norm1[0], w_all, ns, IN_PROJ_TN)

    kv_p = norm_matmul(mem_prompt.reshape(bp * N_MEM, D_MODEL), g_mem[0], w_mem_kv[0].astype(BF16),
                       min(1024, bp * N_MEM), MEM_WIDTH)
    om_p = mem_attn(zp, bp, lp, kv_p, 0, kv_p, 1, 512)
    om_s = mem_attn(zs, bs, ls, cache_mem_k[0].reshape(bs * N_MEM, MEM_WIDTH), 0,
                    cache_mem_v[0].reshape(bs * N_MEM, MEM_WIDTH), 0, ls)

    conv_prev_p = jnp.zeros((bp, SUBLANES, CONV_DIM), F32)
    conv_prev_s = jnp.concatenate(
        [jnp.zeros((bs, SUBLANES - (CONV_WIDTH - 1), CONV_DIM), F32), state_conv[0]], axis=1)
    ssm0_p = jnp.zeros((bp, D_INNER, D_STATE), F32)
    ssm0_s = state_ssm[0].reshape(bs, D_INNER, D_STATE)
    ys_p, ssm_p = ssd(zp, bp, lp, conv_prev_p, ssm0_p, conv_w[0], cb_row, dtb_row, aneg_row, dsk_row, gs_row)
    ys_s, ssm_s = ssd(zs, bs, ls, conv_prev_s, ssm0_s, conv_w[0], cb_row, dtb_row, aneg_row, dsk_row, gs_row)

    oa_p = dsa_prompt(zp, bp, lp)
    oa_s = dsa_sample(zs, bs, ls, cache_k[0], cache_v[0], cache_idx_k[0], page_table)

    mw = (b_gate[0].reshape(1, -1), w_ssd_out[0].astype(BF16), w_attn_out[0].astype(BF16),
          w_mem_out[0].astype(BF16), w_out[0].astype(BF16), g_norm2[0].reshape(1, D_MODEL), wr_pad, br_pad)
    x1_p, h2_p, te_p, gw_p = merge(xp, zp, ys_p, oa_p, om_p, *mw, 512)
    x1_s, h2_s, te_s, gw_s = merge(xs, zs, ys_s, oa_s, om_s, *mw, ns)

    cat = lambda a, b: jnp.concatenate([a, b], axis=0)
    y_all = moe_and_final_norm(cat(x1_p, x1_s), cat(h2_p, h2_s), cat(te_p, te_s), cat(gw_p, gw_s),
                               w_exp1[0], b1p, w_exp2[0], b2, g_final)
    y_prompt = y_all[:np_].reshape(bp, lp, D_MODEL)
    y_sample = y_all[np_:].reshape(bs, ls, D_MODEL)

    def kvi(z, b, l):
        k = z[:, OFF_K:OFF_K + N_KV_HEADS * HEAD_DIM].reshape(1, b, l, N_KV_HEADS, HEAD_DIM)
        v = z[:, OFF_V:OFF_V + N_KV_HEADS * HEAD_DIM].reshape(1, b, l, N_KV_HEADS, HEAD_DIM)
        ki = z[:, OFF_SM + SM_KI:OFF_SM + SM_KI + IDX_DIM].reshape(1, b, l, IDX_DIM)
        conv = z.reshape(b, l, W_ALL)[:, l - (CONV_WIDTH - 1):, OFF_XBC:OFF_XBC + CONV_DIM][None]
        return k, v, ki, conv

    k_p, v_p, ki_p, conv_p = kvi(zp, bp, lp)
    k_s, v_s, ki_s, conv_s = kvi(zs, bs, ls)
    mk_p = kv_p[:, :MEM_WIDTH].reshape(1, bp, N_MEM, MEM_HEADS, MEM_HEAD_DIM)
    mv_p = kv_p[:, MEM_WIDTH:].reshape(1, bp, N_MEM, MEM_HEADS, MEM_HEAD_DIM)
    ssm_shape = (1, -1, SSD_HEADS, SSD_HEAD_DIM, D_STATE)
    return (y_prompt, y_sample, k_p, v_p, ki_p, conv_p, ssm_p.reshape(ssm_shape), mk_p, mv_p,
            k_s, v_s, ki_s, conv_s, ssm_s.reshape(ssm_shape))
```

```python
import functools

import numpy as np
import jax
import jax.numpy as jnp
from jax import lax
from jax.experimental import pallas as pl
from jax.experimental.pallas import tpu as pltpu

F32 = jnp.float32
BF16 = jnp.bfloat16
I32 = jnp.int32
HIGHEST = lax.Precision.HIGHEST

D_MODEL = 1024
D_INNER = 2048
SSD_HEAD_DIM = 64
SSD_HEADS = 32
SSD_GROUPS = 4
D_STATE = 128
CONV_WIDTH = 4
CONV_DIM = D_INNER + 2 * SSD_GROUPS * D_STATE
SSD_CHUNK = 128
N_HEADS = 16
N_KV_HEADS = 4
HEAD_DIM = 64
IDX_HEADS = 8
IDX_DIM = 64
TOPK_MAX = 256
N_MEM = 256
MEM_HEADS = 4
MEM_HEAD_DIM = 256
MEM_WIDTH = MEM_HEADS * MEM_HEAD_DIM
N_EXPERTS = 32
TOP_K = 4
D_FF = D_MODEL
SWIGLU_LIMIT = 7.0
SWIGLU_ALPHA = 1.702
N_BRANCH = 3
EPS = 1e-6
PAGE_SIZE = 128
IN_WIDTHS = (D_INNER, CONV_DIM, SSD_HEADS, N_HEADS * HEAD_DIM, N_KV_HEADS * HEAD_DIM, N_KV_HEADS * HEAD_DIM,
             IDX_HEADS * IDX_DIM, IDX_DIM, IDX_HEADS, MEM_WIDTH, N_BRANCH * D_MODEL)

LANES = 128
SUBLANES = 8
VMEM_LIMIT = 56 * 1024 * 1024

OFF_XBC = 0
OFF_GATE = OFF_XBC + CONV_DIM
OFF_Z = OFF_GATE + N_BRANCH * D_MODEL
OFF_Q = OFF_Z + D_INNER
OFF_QM = OFF_Q + N_HEADS * HEAD_DIM
OFF_QI = OFF_QM + MEM_WIDTH
OFF_K = OFF_QI + IDX_HEADS * IDX_DIM
OFF_V = OFF_K + N_KV_HEADS * HEAD_DIM
OFF_SM = OFF_V + N_KV_HEADS * HEAD_DIM
SM_KI = 0
SM_DT = SM_KI + IDX_DIM
SM_WI = SM_DT + SSD_HEADS
IN_PROJ_TN = 1280
W_ALL = OFF_SM + 2 * LANES

NEG = -1e30
INT_MIN = np.int32(-2 ** 31)
INT_MAX = np.int32(2 ** 31 - 1)

MOE_BM = 512
MOE_T = 256


def _cparams(*sem):
    return pltpu.CompilerParams(dimension_semantics=sem, vmem_limit_bytes=VMEM_LIMIT)


def _nt_dot(a, b):
    return lax.dot_general(a, b, (((1,), (1,)), ((), ())), preferred_element_type=F32)


def _float_key(x):
    bits = lax.bitcast_convert_type(x, I32)
    return jnp.where(bits < 0, bits ^ INT_MAX, bits)


def _norm_matmul_body(x_ref, g_ref, w_ref, o_ref, h_ref):
    @pl.when(pl.program_id(1) == 0)
    def _():
        x = x_ref[...]
        h = x * lax.rsqrt(jnp.mean(x * x, axis=-1, keepdims=True) + EPS)
        h_ref[...] = (h * g_ref[...]).astype(BF16)

    o_ref[...] = jnp.dot(h_ref[...], w_ref[...], preferred_element_type=F32)


def norm_matmul(x, g, w, tm, tn):
    n, d = x.shape
    wn = w.shape[1]
    return pl.pallas_call(
        _norm_matmul_body,
        grid=(n // tm, wn // tn),
        in_specs=[pl.BlockSpec((tm, d), lambda i, j: (i, 0)),
                  pl.BlockSpec((1, d), lambda i, j: (0, 0)),
                  pl.BlockSpec((d, tn), lambda i, j: (0, j))],
        out_specs=pl.BlockSpec((tm, tn), lambda i, j: (i, j)),
        out_shape=jax.ShapeDtypeStruct((n, wn), F32),
        scratch_shapes=[pltpu.VMEM((tm, d), BF16)],
        compiler_params=_cparams("parallel", "arbitrary"),
        name="norm_matmul",
    )(x, g.reshape(1, d), w)


def _softplus(x):
    return jnp.maximum(x, 0.0) + jnp.log1p(jnp.exp(-jnp.abs(x)))


def _silu(x):
    return x * jax.nn.sigmoid(x)


def _expand_heads(v, q):
    lane = lax.broadcasted_iota(I32, (q, LANES), 1)
    cols = []
    for t in range(SSD_HEADS // 2):
        c0 = jnp.broadcast_to(v[:, SM_DT + 2 * t:SM_DT + 2 * t + 1], (q, LANES))
        c1 = jnp.broadcast_to(v[:, SM_DT + 2 * t + 1:SM_DT + 2 * t + 2], (q, LANES))
        cols.append(jnp.where(lane < SSD_HEAD_DIM, c0, c1))
    return jnp.concatenate(cols, axis=1)


def _ssd_body(xbc_ref, z_ref, sm_ref, convp_ref, init_ref, cw_ref, cb_ref, dtb_ref, aneg_ref, dsk_ref, gs_ref,
              y_ref, st_ref, xpad_ref, state_ref, *, rows_in, q):
    c = pl.program_id(1)
    pad = SUBLANES

    @pl.when(c == 0)
    def _():
        xpad_ref[0:pad, :] = convp_ref[0]
        state_ref[...] = init_ref[0]

    xpad_ref[pad:pad + rows_in, :] = xbc_ref[...]
    if rows_in < q:
        xpad_ref[pad + rows_in:pad + q, :] = jnp.zeros((q - rows_in, CONV_DIM), F32)

    acc = cb_ref[...]
    for j in range(CONV_WIDTH):
        lo = pad - (CONV_WIDTH - 1) + j
        acc = acc + xpad_ref[lo:lo + q, :] * cw_ref[j:j + 1, :]
    xc = _silu(acc)
    xpad_ref[0:pad, :] = xpad_ref[q:q + pad, :]

    xs = xc[:, :D_INNER]
    gn = SSD_GROUPS * D_STATE
    bm = xc[:, D_INNER:D_INNER + gn].astype(BF16)
    cm = xc[:, D_INNER + gn:].astype(BF16)

    sm = sm_ref[...]
    zz = z_ref[...]
    if rows_in < q:
        sm = jnp.concatenate([sm, jnp.zeros((q - rows_in, LANES), F32)], axis=0)
        zz = jnp.concatenate([zz, jnp.zeros((q - rows_in, D_INNER), F32)], axis=0)
    row = lax.broadcasted_iota(I32, (q, LANES), 0)
    dt = _softplus(sm + dtb_ref[...])
    if rows_in < q:
        dt = jnp.where(row < rows_in, dt, 0.0)
    a = dt * aneg_ref[...]
    tri = (lax.broadcasted_iota(I32, (q, q), 0) >= lax.broadcasted_iota(I32, (q, q), 1)).astype(F32)
    a_cs = jnp.dot(tri, a, precision=HIGHEST, preferred_element_type=F32)
    a_t = a_cs.T
    a_last = a_cs[q - 1:q, :]
    dte = jnp.exp(a_last - a_cs)
    e_in = jnp.exp(a_cs)

    xdt = xs * _expand_heads(dt, q)
    xdt_bf = xdt.astype(BF16)
    xw_bf = (xs * _expand_heads(dt * dte, q)).astype(BF16)
    ein_x = _expand_heads(e_in, q)

    causal = lax.broadcasted_iota(I32, (q, q), 0) >= lax.broadcasted_iota(I32, (q, q), 1)
    lane = lax.broadcasted_iota(I32, (q, LANES), 1)
    hpg = SSD_HEADS // SSD_GROUPS
    gw = hpg * SSD_HEAD_DIM
    y_parts = []
    for g in range(SSD_GROUPS):
        bg = bm[:, g * D_STATE:(g + 1) * D_STATE]
        cg = cm[:, g * D_STATE:(g + 1) * D_STATE]
        cb = _nt_dot(cg, bg)
        m_h = []
        for e in range(hpg):
            h = g * hpg + e
            col = a_cs[:, SM_DT + h:SM_DT + h + 1]
            rw = a_t[SM_DT + h:SM_DT + h + 1, :]
            decay = jnp.exp(jnp.where(causal, col - rw, -jnp.inf))
            m_h.append((cb * decay).astype(BF16))
        yd = []
        for t in range(hpg // 2):
            pair = g * (hpg // 2) + t
            slab = xdt_bf[:, pair * LANES:(pair + 1) * LANES]
            ya = jnp.dot(m_h[2 * t], slab, preferred_element_type=F32)
            yb = jnp.dot(m_h[2 * t + 1], slab, preferred_element_type=F32)
            yd.append(jnp.where(lane < SSD_HEAD_DIM, ya, yb))
        s_old = state_ref[g * gw:(g + 1) * gw, :]
        y_off = _nt_dot(cg, s_old.astype(BF16)) * ein_x[:, g * gw:(g + 1) * gw]
        y_parts.append(jnp.concatenate(yd, axis=1) + y_off)
        new = lax.dot_general(xw_bf[:, g * gw:(g + 1) * gw], bg, (((0,), (0,)), ((), ())),
                              preferred_element_type=F32)
        for e in range(hpg):
            h = g * hpg + e
            dec = jnp.exp(a_t[SM_DT + h:SM_DT + h + 1, q - 1:q])
            lo = e * SSD_HEAD_DIM
            state_ref[h * SSD_HEAD_DIM:(h + 1) * SSD_HEAD_DIM, :] = (
                s_old[lo:lo + SSD_HEAD_DIM, :] * dec + new[lo:lo + SSD_HEAD_DIM, :])

    y = jnp.concatenate(y_parts, axis=1) + dsk_ref[...] * xs
    y = y * _silu(zz)
    outs = []
    for g in range(SSD_GROUPS):
        yg = y[:, g * gw:(g + 1) * gw]
        outs.append(yg * lax.rsqrt(jnp.mean(yg * yg, axis=-1, keepdims=True) + EPS))
    y = jnp.concatenate(outs, axis=1) * gs_ref[...]
    y_ref[...] = y[:rows_in].astype(BF16)

    @pl.when(c == pl.num_programs(1) - 1)
    def _():
        st_ref[0] = state_ref[...]


def ssd(zall, batch, seq, conv_prev8, ssm_init, conv_w, conv_b, dtb_row, aneg_row, dsk_row, gs_row):
    q = SSD_CHUNK
    rows_in = min(seq, q)
    nch = seq // rows_in
    row_map = lambda b, c: (b * nch + c)
    const2 = lambda b, c: (0, 0)
    body = functools.partial(_ssd_body, rows_in=rows_in, q=q)
    return pl.pallas_call(
        body,
        grid=(batch, nch),
        in_specs=[pl.BlockSpec((rows_in, CONV_DIM), lambda b, c: (row_map(b, c), OFF_XBC // CONV_DIM)),
                  pl.BlockSpec((rows_in, D_INNER), lambda b, c: (row_map(b, c), OFF_Z // D_INNER)),
                  pl.BlockSpec((rows_in, LANES), lambda b, c: (row_map(b, c), OFF_SM // LANES)),
                  pl.BlockSpec((1, SUBLANES, CONV_DIM), lambda b, c: (b, 0, 0)),
                  pl.BlockSpec((1, D_INNER, D_STATE), lambda b, c: (b, 0, 0)),
                  pl.BlockSpec((CONV_WIDTH, CONV_DIM), const2),
                  pl.BlockSpec((1, CONV_DIM), const2),
                  pl.BlockSpec((1, LANES), const2),
                  pl.BlockSpec((1, LANES), const2),
                  pl.BlockSpec((1, D_INNER), const2),
                  pl.BlockSpec((1, D_INNER), const2)],
        out_specs=[pl.BlockSpec((rows_in, D_INNER), lambda b, c: (row_map(b, c), 0)),
                   pl.BlockSpec((1, D_INNER, D_STATE), lambda b, c: (b, 0, 0))],
        out_shape=[jax.ShapeDtypeStruct((batch * seq, D_INNER), BF16),
                   jax.ShapeDtypeStruct((batch, D_INNER, D_STATE), F32)],
        scratch_shapes=[pltpu.VMEM((q + 2 * SUBLANES, CONV_DIM), F32),
                        pltpu.VMEM((D_INNER, D_STATE), F32)],
        compiler_params=_cparams("parallel", "arbitrary"),
        name="ssd",
    )(zall, zall, zall, conv_prev8, ssm_init, conv_w, conv_b, dtb_row, aneg_row, dsk_row, gs_row)


def _mem_attn_body(q_ref, k_ref, v_ref, o_ref):
    for h in range(MEM_HEADS):
        sl = slice(h * MEM_HEAD_DIM, (h + 1) * MEM_HEAD_DIM)
        s = _nt_dot(q_ref[:, sl].astype(BF16), k_ref[:, sl].astype(BF16)) * (MEM_HEAD_DIM ** -0.5)
        m = jnp.max(s, axis=-1, keepdims=True)
        p = jnp.exp(s - m)
        p = p / jnp.sum(p, axis=-1, keepdims=True)
        o = jnp.dot(p.astype(BF16), v_ref[:, sl].astype(BF16), preferred_element_type=F32)
        o_ref[:, sl] = o.astype(BF16)


def mem_attn(zall, batch, seq, k_arr, k_col, v_arr, v_col, tm):
    nt = seq // tm
    return pl.pallas_call(
        _mem_attn_body,
        grid=(batch, nt),
        in_specs=[pl.BlockSpec((tm, MEM_WIDTH), lambda b, i: (b * nt + i, OFF_QM // MEM_WIDTH)),
                  pl.BlockSpec((N_MEM, MEM_WIDTH), lambda b, i: (b, k_col)),
                  pl.BlockSpec((N_MEM, MEM_WIDTH), lambda b, i: (b, v_col))],
        out_specs=pl.BlockSpec((tm, MEM_WIDTH), lambda b, i: (b * nt + i, 0)),
        out_shape=jax.ShapeDtypeStruct((batch * seq, MEM_WIDTH), BF16),
        compiler_params=_cparams("parallel", "arbitrary"),
        name="mem_attn",
    )(zall, k_arr, v_arr)


def _kth_largest_key(count_ge, shape, n_sel):
    def bit_body(t, ans):
        cand = ans | jnp.left_shift(jnp.int32(1), 31 - t)
        cnt = count_ge(cand ^ INT_MIN)
        return jnp.where(cnt >= n_sel, cand, ans)

    ans = lax.fori_loop(0, 32, bit_body, jnp.zeros(shape, I32))
    return ans ^ INT_MIN


def _tie_cut(count_eq_below, need, shape, nbits):
    def bit_body(t, lo):
        cand = lo | jnp.left_shift(jnp.int32(1), nbits - 1 - t)
        cnt = count_eq_below(cand)
        return jnp.where(cnt < need, cand, lo)

    return lax.fori_loop(0, nbits, bit_body, jnp.zeros(shape, I32))


def _select_bias(key, kpos, thr, cut, visible):
    sel = (key > thr) | ((key == thr) & (kpos <= cut))
    return jnp.where(sel & visible, 0.0, NEG)


def _dsa_prompt_body(q_ref, qi_ref, smq_ref, k_ref, v_ref, smk_ref, o_ref,
                     kh_ref, vt_ref, kis_ref, qt2_ref, qit_ref, keys_ref, bias_ref, ot_ref, *, tq, kc, seq, n_sel):
    i = pl.program_id(1)

    @pl.when(i == 0)
    def _():
        def cast_rows(r, carry):
            rs = pl.ds(pl.multiple_of(r * kc, kc), kc)
            kk = k_ref[rs, :]
            for h in range(N_KV_HEADS):
                kh_ref[h, rs, :] = kk[:, h * HEAD_DIM:(h + 1) * HEAD_DIM].astype(BF16)
            vt_ref[r] = v_ref[rs, :].T.astype(BF16)
            kis_ref[rs, :] = smk_ref[rs, SM_KI:SM_KI + IDX_DIM].astype(BF16)
            return carry
        lax.fori_loop(0, seq // kc, cast_rows, 0)

    nkc = (i * tq + tq - 1) // kc + 1
    qpos = i * tq + lax.broadcasted_iota(I32, (kc, tq), 1)
    krow = lax.broadcasted_iota(I32, (kc, tq), 0)

    qt = (q_ref[...] * (HEAD_DIM ** -0.5)).T.astype(BF16)
    for h in range(N_HEADS):
        qt2_ref[:, h * tq:(h + 1) * tq] = qt[h * HEAD_DIM:(h + 1) * HEAD_DIM, :]
    qit_ref[...] = (qi_ref[...] * (IDX_DIM ** -0.5)).T.astype(BF16)
    wt = smq_ref[...].T[SM_WI:SM_WI + IDX_HEADS, :] * (IDX_HEADS ** -0.5)

    def score_chunk(c, carry):
        ks = pl.ds(pl.multiple_of(c * kc, kc), kc)
        kic = kis_ref[ks, :]
        sc = jnp.zeros((kc, tq), F32)
        for h in range(IDX_HEADS):
            d = jnp.dot(kic, qit_ref[h * IDX_DIM:(h + 1) * IDX_DIM, :], preferred_element_type=F32)
            sc = sc + jnp.maximum(d, 0.0) * wt[h:h + 1, :]
        key = _float_key(sc + 0.0)
        keys_ref[c] = jnp.where(c * kc + krow <= qpos, key, INT_MIN)
        return carry
    lax.fori_loop(0, nkc, score_chunk, 0)

    def count(pred):
        def body(c, acc):
            return acc + jnp.where(pred(keys_ref[c], c * kc + krow), 1.0, 0.0)
        acc = lax.fori_loop(0, nkc, body, jnp.zeros((kc, tq), F32))
        return jnp.sum(acc, axis=0, keepdims=True)

    vec = (1, tq)
    thr = _kth_largest_key(lambda cand: count(lambda key, kpos: key >= cand), vec, n_sel)
    n_gt = count(lambda key, kpos: key > thr)
    n_eq = count(lambda key, kpos: key == thr)
    need = n_sel - n_gt
    excess = jnp.max(jnp.where((n_eq > need) & (thr != INT_MIN), 1.0, 0.0))
    nbits = max(1, int(seq - 1).bit_length())
    cut = lax.cond(
        excess > 0.0,
        lambda: _tie_cut(lambda cand: count(lambda key, kpos: (key == thr) & (kpos < cand)), need, vec, nbits),
        lambda: jnp.full(vec, INT_MAX, I32))

    def bias_chunk(c, carry):
        kpos = c * kc + krow
        bias_ref[c] = _select_bias(keys_ref[c], kpos, thr, cut, kpos <= qpos)
        return carry
    lax.fori_loop(0, nkc, bias_chunk, 0)

    grp = N_HEADS // N_KV_HEADS
    def att_chunk(c, carry):
        ms, ls, accs = carry
        ks = pl.ds(pl.multiple_of(c * kc, kc), kc)
        bias = bias_ref[c]
        s4 = [jnp.dot(kh_ref[kh, ks, :], qt2_ref[:, kh * grp * tq:(kh + 1) * grp * tq],
                      preferred_element_type=F32) for kh in range(N_KV_HEADS)]
        ms_n, ls_n, ps, alphas = [], [], [], []
        for h in range(N_HEADS):
            s = s4[h // grp][:, (h % grp) * tq:(h % grp + 1) * tq] + bias
            m_new = jnp.maximum(ms[h], jnp.max(s, axis=0, keepdims=True))
            p = jnp.exp(s - m_new)
            alpha = jnp.exp(ms[h] - m_new)
            ms_n.append(m_new)
            ls_n.append(alpha * ls[h] + jnp.sum(p, axis=0, keepdims=True))
            ps.append(p.astype(BF16))
            alphas.append(alpha)
        accs_n = []
        for h in range(N_HEADS):
            kh = h // grp
            vtc = vt_ref[c, kh * HEAD_DIM:(kh + 1) * HEAD_DIM, :]
            accs_n.append(alphas[h] * accs[h] + jnp.dot(vtc, ps[h], preferred_element_type=F32))
        return tuple(ms_n), tuple(ls_n), tuple(accs_n)

    init = (tuple(jnp.full(vec, NEG, F32) for _ in range(N_HEADS)),
            tuple(jnp.zeros(vec, F32) for _ in range(N_HEADS)),
            tuple(jnp.zeros((HEAD_DIM, tq), F32) for _ in range(N_HEADS)))
    _, ls, accs = lax.fori_loop(0, nkc, att_chunk, init)
    for h in range(N_HEADS):
        ot_ref[h * HEAD_DIM:(h + 1) * HEAD_DIM, :] = accs[h] / ls[h]
    o_ref[...] = ot_ref[...].T.astype(BF16)


def dsa_prompt(zall, batch, seq):
    tq = 128
    kc = 256
    nq = seq // tq
    n_sel = min(TOPK_MAX, seq // 4)
    kvw = N_KV_HEADS * HEAD_DIM
    qw = N_HEADS * HEAD_DIM
    qiw = IDX_HEADS * IDX_DIM
    body = functools.partial(_dsa_prompt_body, tq=tq, kc=kc, seq=seq, n_sel=n_sel)
    return pl.pallas_call(
        body,
        grid=(batch, nq),
        in_specs=[pl.BlockSpec((tq, qw), lambda b, i: (b * nq + i, OFF_Q // qw)),
                  pl.BlockSpec((tq, qiw), lambda b, i: (b * nq + i, OFF_QI // qiw)),
                  pl.BlockSpec((tq, LANES), lambda b, i: (b * nq + i, OFF_SM // LANES)),
                  pl.BlockSpec((seq, kvw), lambda b, i: (b, OFF_K // kvw)),
                  pl.BlockSpec((seq, kvw), lambda b, i: (b, OFF_V // kvw)),
                  pl.BlockSpec((seq, LANES), lambda b, i: (b, OFF_SM // LANES))],
        out_specs=pl.BlockSpec((tq, qw), lambda b, i: (b * nq + i, 0)),
        out_shape=jax.ShapeDtypeStruct((batch * seq, qw), BF16),
        scratch_shapes=[pltpu.VMEM((N_KV_HEADS, seq, HEAD_DIM), BF16),
                        pltpu.VMEM((seq // kc, kvw, kc), BF16),
                        pltpu.VMEM((seq, IDX_DIM), BF16),
                        pltpu.VMEM((HEAD_DIM, N_HEADS * tq), BF16),
                        pltpu.VMEM((qiw, tq), BF16),
                        pltpu.VMEM((seq // kc, kc, tq), I32),
                        pltpu.VMEM((seq // kc, kc, tq), F32),
                        pltpu.VMEM((qw, tq), F32)],
        compiler_params=_cparams("parallel", "arbitrary"),
        name="dsa_prompt",
    )(zall, zall, zall, zall, zall, zall)


def _dsa_s_score_body(pt_ref, qs_ref, w_ref, *refs, pg):
    ki_refs, o_ref = refs[:pg], refs[pg]
    qs = qs_ref[0]
    wcol = w_ref[0] * (IDX_HEADS ** -0.5)
    t = qs.shape[0] // IDX_HEADS
    for p in range(pg):
        d = jnp.dot(qs, ki_refs[p][0].astype(BF16), preferred_element_type=F32)
        r = jnp.maximum(d * (IDX_DIM ** -0.5), 0.0) * wcol
        sc = r[0:t, :]
        for h in range(1, IDX_HEADS):
            sc = sc + r[h * t:(h + 1) * t, :]
        o_ref[0, :, p * PAGE_SIZE:(p + 1) * PAGE_SIZE] = sc + 0.0


def _dsa_s_select_body(sc_ref, qs_ref, w_ref, smn_ref, o_ref, *, t, past, n_sel):
    qs = qs_ref[0]
    wcol = w_ref[0] * (IDX_HEADS ** -0.5)
    ki_new = smn_ref[:, SM_KI:SM_KI + IDX_DIM].astype(BF16)
    ki_new = jnp.concatenate([ki_new, jnp.zeros((LANES - t, IDX_DIM), BF16)], axis=0)
    d = _nt_dot(qs, ki_new)
    r = jnp.maximum(d * (IDX_DIM ** -0.5), 0.0) * wcol
    sc_new = r[0:t, :]
    for h in range(1, IDX_HEADS):
        sc_new = sc_new + r[h * t:(h + 1) * t, :]
    lane_n = lax.broadcasted_iota(I32, (t, LANES), 1)
    vis_n = lane_n <= lax.broadcasted_iota(I32, (t, LANES), 0)
    key_n = jnp.where(vis_n, _float_key(sc_new + 0.0), INT_MIN)
    key_p = _float_key(sc_ref[0])
    pos_p = lax.broadcasted_iota(I32, (t, past), 1)
    pos_n = past + lane_n

    def count(pred):
        return (jnp.sum(jnp.where(pred(key_p, pos_p), 1.0, 0.0), axis=1, keepdims=True)
                + jnp.sum(jnp.where(pred(key_n, pos_n), 1.0, 0.0), axis=1, keepdims=True))

    thr = _kth_largest_key(lambda cand: count(lambda key, kpos: key >= cand), (t, 1), n_sel)
    need = n_sel - count(lambda key, kpos: key > thr)
    nbits = max(1, int(past + t - 1).bit_length())
    cut = _tie_cut(lambda cand: count(lambda key, kpos: (key == thr) & (kpos < cand)), need, (t, 1), nbits)
    o_ref[0, :, 0:past] = _select_bias(key_p, pos_p, thr, cut, pos_p >= 0)
    o_ref[0, :, past:past + LANES] = _select_bias(key_n, pos_n, thr, cut, vis_n)


def _dsa_s_attn_body(pt_ref, qbd_ref, bias_ref, biasn_ref, kn_ref, vn_ref, *refs, pg, t):
    k_refs, v_refs = refs[:pg], refs[pg:2 * pg]
    o_ref, m_ref, l_ref, acc_ref = refs[2 * pg:]
    j = pl.program_id(1)
    rows = qbd_ref.shape[1]
    rep = rows // t

    @pl.when(j == 0)
    def _():
        m_ref[...] = jnp.full(m_ref.shape, NEG, F32)
        l_ref[...] = jnp.zeros(l_ref.shape, F32)
        acc_ref[...] = jnp.zeros(acc_ref.shape, F32)

    qbd = qbd_ref[0]

    def update(kt, vt, bias):
        s = jnp.dot(qbd, kt, preferred_element_type=F32) * (HEAD_DIM ** -0.5) + jnp.concatenate([bias] * rep, axis=0)
        m = m_ref[...]
        m_new = jnp.maximum(m, jnp.max(s, axis=1, keepdims=True))
        p = jnp.exp(s - m_new)
        alpha = jnp.exp(m - m_new)
        l_ref[...] = alpha * l_ref[...] + jnp.sum(p, axis=1, keepdims=True)
        acc_ref[...] = alpha * acc_ref[...] + _nt_dot(p.astype(BF16), vt)
        m_ref[...] = m_new

    kt = jnp.concatenate([r[0] for r in k_refs], axis=1).astype(BF16)
    vt = jnp.concatenate([r[0] for r in v_refs], axis=1).astype(BF16)
    update(kt, vt, bias_ref[0])

    @pl.when(j == pl.num_programs(1) - 1)
    def _():
        kvw = N_KV_HEADS * HEAD_DIM
        zpad = jnp.zeros((LANES - t, kvw), F32)
        update(jnp.concatenate([kn_ref[...], zpad], axis=0).T.astype(BF16),
               jnp.concatenate([vn_ref[...], zpad], axis=0).T.astype(BF16), biasn_ref[0])
        o_ref[0] = acc_ref[...] / l_ref[...]


def dsa_sample(zs, batch, t, cache_k, cache_v, cache_ki, page_table):
    n_pages = page_table.shape[1]
    past = n_pages * PAGE_SIZE
    n_sel = min(TOPK_MAX, (past + t) // 4)
    pg = 16 if n_pages % 16 == 0 else 8
    nj = n_pages // pg
    n_pool = cache_k.shape[0]
    kvw = N_KV_HEADS * HEAD_DIM
    grp = N_HEADS // N_KV_HEADS

    qi = zs[:, OFF_QI:OFF_QI + IDX_HEADS * IDX_DIM].reshape(batch, t, IDX_HEADS, IDX_DIM)
    qs = jnp.transpose(qi, (0, 2, 1, 3)).reshape(batch, IDX_HEADS * t, IDX_DIM).astype(BF16)
    wi = zs[:, OFF_SM + SM_WI:OFF_SM + SM_WI + IDX_HEADS].reshape(batch, t, IDX_HEADS)
    wcol = jnp.transpose(wi, (0, 2, 1)).reshape(batch, IDX_HEADS * t, 1)

    def page_spec(shape, p):
        return pl.BlockSpec(shape, lambda b, j, pt: (pt[b, j * pg + p],) + (0,) * (len(shape) - 1))

    scores = pl.pallas_call(
        functools.partial(_dsa_s_score_body, pg=pg),
        grid_spec=pltpu.PrefetchScalarGridSpec(
            num_scalar_prefetch=1,
            grid=(batch, nj),
            in_specs=[pl.BlockSpec((1, IDX_HEADS * t, IDX_DIM), lambda b, j, pt: (b, 0, 0)),
                      pl.BlockSpec((1, IDX_HEADS * t, 1), lambda b, j, pt: (b, 0, 0))]
                     + [page_spec((1, IDX_DIM, PAGE_SIZE), p) for p in range(pg)],
            out_specs=pl.BlockSpec((1, t, pg * PAGE_SIZE), lambda b, j, pt: (b, 0, j))),
        out_shape=jax.ShapeDtypeStruct((batch, t, past), F32),
        compiler_params=_cparams("parallel", "arbitrary"),
        name="dsa_sample_scores",
    )(page_table, qs, wcol, *([jnp.swapaxes(cache_ki, 1, 2)] * pg))

    bias = pl.pallas_call(
        functools.partial(_dsa_s_select_body, t=t, past=past, n_sel=n_sel),
        grid=(batch,),
        in_specs=[pl.BlockSpec((1, t, past), lambda b: (b, 0, 0)),
                  pl.BlockSpec((1, IDX_HEADS * t, IDX_DIM), lambda b: (b, 0, 0)),
                  pl.BlockSpec((1, IDX_HEADS * t, 1), lambda b: (b, 0, 0)),
                  pl.BlockSpec((t, LANES), lambda b: (b, OFF_SM // LANES))],
        out_specs=pl.BlockSpec((1, t, past + LANES), lambda b: (b, 0, 0)),
        out_shape=jax.ShapeDtypeStruct((batch, t, past + LANES), F32),
        compiler_params=_cparams("parallel"),
        name="dsa_sample_select",
    )(scores, qs, wcol, zs)

    q = zs[:, OFF_Q:OFF_Q + N_HEADS * HEAD_DIM].reshape(batch, t, N_KV_HEADS, grp, HEAD_DIM)
    q = jnp.transpose(q, (0, 2, 3, 1, 4))
    eye = jnp.eye(N_KV_HEADS, dtype=F32)
    qbd = (q[:, :, :, :, None, :] * eye[None, :, None, None, :, None]).reshape(batch, N_HEADS * t, kvw).astype(BF16)

    ck = jnp.transpose(cache_k, (0, 2, 3, 1)).reshape(n_pool, kvw, PAGE_SIZE)
    cv = jnp.transpose(cache_v, (0, 2, 3, 1)).reshape(n_pool, kvw, PAGE_SIZE)
    rows = N_HEADS * t
    out = pl.pallas_call(
        functools.partial(_dsa_s_attn_body, pg=pg, t=t),
        grid_spec=pltpu.PrefetchScalarGridSpec(
            num_scalar_prefetch=1,
            grid=(batch, nj),
            in_specs=[pl.BlockSpec((1, rows, kvw), lambda b, j, pt: (b, 0, 0)),
                      pl.BlockSpec((1, t, pg * PAGE_SIZE), lambda b, j, pt: (b, 0, j)),
                      pl.BlockSpec((1, t, LANES), lambda b, j, pt: (b, 0, past // LANES)),
                      pl.BlockSpec((t, kvw), lambda b, j, pt: (b, OFF_K // kvw)),
                      pl.BlockSpec((t, kvw), lambda b, j, pt: (b, OFF_V // kvw))]
                     + [page_spec((1, kvw, PAGE_SIZE), p) for p in range(pg)]
                     + [page_spec((1, kvw, PAGE_SIZE), p) for p in range(pg)],
            out_specs=pl.BlockSpec((1, rows, kvw), lambda b, j, pt: (b, 0, 0)),
            scratch_shapes=[pltpu.VMEM((rows, 1), F32), pltpu.VMEM((rows, 1), F32), pltpu.VMEM((rows, kvw), F32)]),
        out_shape=jax.ShapeDtypeStruct((batch, rows, kvw), F32),
        compiler_params=_cparams("parallel", "arbitrary"),
        name="dsa_sample_attn",
    )(page_table, qbd, bias, bias, zs, zs, *([ck] * pg), *([cv] * pg))

    o = out.reshape(batch, N_KV_HEADS, grp, t, N_KV_HEADS, HEAD_DIM)
    o = jnp.stack([o[:, kh, :, :, kh, :] for kh in range(N_KV_HEADS)], axis=1)
    return jnp.transpose(o, (0, 3, 1, 2, 4)).reshape(batch * t, N_HEADS * HEAD_DIM).astype(BF16)


def _merge_body(x_ref, gate_ref, ys_ref, oa_ref, om_ref, bg_ref, ws_ref, wa_ref, wm_ref, wo_ref, g2_ref, wr_ref, br_ref,
                x1_ref, h2_ref, te_ref, gw_ref):
    gates = jax.nn.sigmoid(gate_ref[...] + bg_ref[...])
    merged = (gates[:, 0:D_MODEL] * jnp.dot(ys_ref[...], ws_ref[...], preferred_element_type=F32)
              + gates[:, D_MODEL:2 * D_MODEL] * jnp.dot(oa_ref[...], wa_ref[...], preferred_element_type=F32)
              + gates[:, 2 * D_MODEL:] * jnp.dot(om_ref[...], wm_ref[...], preferred_element_type=F32))
    x1 = x_ref[...] + jnp.dot(merged.astype(BF16), wo_ref[...], preferred_element_type=F32)
    x1_ref[...] = x1
    h2 = x1 * lax.rsqrt(jnp.mean(x1 * x1, axis=-1, keepdims=True) + EPS)
    h2 = h2 * g2_ref[...]
    h2_ref[...] = h2
    logits = jnp.dot(h2, wr_ref[...], precision=HIGHEST, preferred_element_type=F32) + br_ref[...]
    lane = lax.broadcasted_iota(I32, logits.shape, 1)
    te = jnp.zeros(logits.shape, I32)
    tv = []
    for k in range(TOP_K):
        m = jnp.max(logits, axis=1, keepdims=True)
        idx = jnp.min(jnp.where(logits == m, lane, LANES), axis=1, keepdims=True)
        te = jnp.where(lane == k, idx, te)
        tv.append(m)
        logits = jnp.where(lane == idx, -jnp.inf, logits)
    ex = [jnp.exp(v - tv[0]) for v in tv]
    den = ex[0] + ex[1] + ex[2] + ex[3]
    gw = jnp.zeros(logits.shape, F32)
    for k in range(TOP_K):
        gw = jnp.where(lane == k, ex[k] / den, gw)
    te_ref[...] = te
    gw_ref[...] = gw


def merge(x, zall, ys, oa, om, bg, ws, wa, wm, wo, g2, wr, br, tm):
    n = x.shape[0]
    gw3 = N_BRANCH * D_MODEL
    row = lambda i: (i, 0)
    const = lambda i: (0, 0)
    return pl.pallas_call(
        _merge_body,
        grid=(n // tm,),
        in_specs=[pl.BlockSpec((tm, D_MODEL), row),
                  pl.BlockSpec((tm, gw3), lambda i: (i, OFF_GATE // gw3)),
                  pl.BlockSpec((tm, D_INNER), row),
                  pl.BlockSpec((tm, N_HEADS * HEAD_DIM), row),
                  pl.BlockSpec((tm, MEM_WIDTH), row),
                  pl.BlockSpec((1, gw3), const),
                  pl.BlockSpec((D_INNER, D_MODEL), const),
                  pl.BlockSpec((N_HEADS * HEAD_DIM, D_MODEL), const),
                  pl.BlockSpec((MEM_WIDTH, D_MODEL), const),
                  pl.BlockSpec((D_MODEL, D_MODEL), const),
                  pl.BlockSpec((1, D_MODEL), const),
                  pl.BlockSpec((D_MODEL, LANES), const),
                  pl.BlockSpec((1, LANES), const)],
        out_specs=[pl.BlockSpec((tm, D_MODEL), row), pl.BlockSpec((tm, D_MODEL), row),
                   pl.BlockSpec((tm, LANES), row), pl.BlockSpec((tm, LANES), row)],
        out_shape=[jax.ShapeDtypeStruct((n, D_MODEL), F32), jax.ShapeDtypeStruct((n, D_MODEL), F32),
                   jax.ShapeDtypeStruct((n, LANES), I32), jax.ShapeDtypeStruct((n, LANES), F32)],
        compiler_params=_cparams("parallel"),
        name="merge",
    )(x, zall, ys, oa, om, bg, ws, wa, wm, wo, g2, wr, br)


def _moe_pos_body(te_ref, pos_ref, cnt_ref, carry_ref):
    i = pl.program_id(0)
    tt = te_ref.shape[0]

    @pl.when(i == 0)
    def _():
        carry_ref[...] = jnp.zeros(carry_ref.shape, F32)

    te = te_ref[...]
    lane = lax.broadcasted_iota(I32, (tt, LANES), 1)
    onehot = [lane == te[:, k:k + 1] for k in range(TOP_K)]
    msum = jnp.zeros((tt, LANES), F32)
    for k in range(TOP_K):
        msum = msum + jnp.where(onehot[k], 1.0, 0.0)
    strict = (lax.broadcasted_iota(I32, (tt, tt), 0) > lax.broadcasted_iota(I32, (tt, tt), 1))
    prefix = jnp.dot(jnp.where(strict, 1.0, 0.0).astype(BF16), msum.astype(BF16), preferred_element_type=F32)
    prefix = prefix + carry_ref[0:1, :]
    pos = jnp.zeros((tt, LANES), F32)
    for k in range(TOP_K):
        pk = jnp.sum(jnp.where(onehot[k], prefix, 0.0), axis=1, keepdims=True)
        pos = jnp.where(lane == k, pk, pos)
    pos_ref[...] = pos
    carry_ref[...] = carry_ref[...] + jnp.sum(msum, axis=0, keepdims=True)
    cnt_ref[...] = carry_ref[...]


def _moe_dest_body(te_ref, pos_ref, cnt_ref, dest_ref, be_ref, nu_ref, *, bm):
    tt = te_ref.shape[0]
    cnt = cnt_ref[...]
    padded = jnp.floor((cnt + (bm - 1)) * (1.0 / bm)) * bm
    upper = (lax.broadcasted_iota(I32, (LANES, LANES), 0) < lax.broadcasted_iota(I32, (LANES, LANES), 1))
    pad_start = jnp.dot(padded, jnp.where(upper, 1.0, 0.0), precision=HIGHEST, preferred_element_type=F32)
    pad_end = pad_start + padded
    te = te_ref[...]
    pos = pos_ref[...]
    lane = lax.broadcasted_iota(I32, (tt, LANES), 1)
    dest = jnp.zeros((tt, LANES), F32)
    for k in range(TOP_K):
        ps = jnp.sum(jnp.where(lane == te[:, k:k + 1], pad_start[0:1, :], 0.0), axis=1, keepdims=True)
        dest = jnp.where(lane == k, ps + pos[:, k:k + 1], dest)
    dest_ref[...] = dest.astype(I32)
    nb = be_ref.shape[0]
    bstart = (lax.broadcasted_iota(I32, (nb, LANES), 0) * bm).astype(F32)
    lane_b = lax.broadcasted_iota(I32, (nb, LANES), 1)
    done = jnp.where((pad_end[0:1, :] <= bstart) & (lane_b < N_EXPERTS), 1.0, 0.0)
    be = jnp.minimum(jnp.sum(done, axis=1, keepdims=True), N_EXPERTS - 1.0)
    be_ref[...] = jnp.broadcast_to(be, (nb, LANES)).astype(I32)
    total = jnp.sum(padded[0:1, :], axis=1, keepdims=True)
    nu_ref[...] = jnp.broadcast_to(total * (1.0 / bm), nu_ref.shape).astype(I32)


def _moe_dispatch_body(dest_ref, ha_ref, hb_ref, xs_in_ref, xs_ref, sem, *, tiles_a):
    del xs_in_ref
    tt = ha_ref.shape[0]

    def scatter_rows(h_ref):
        def copy(r, k):
            d = dest_ref[r * TOP_K + k]
            return pltpu.make_async_copy(h_ref.at[pl.ds(r, 1), :], xs_ref.at[pl.ds(d, 1), :], sem)

        def issue(r, carry):
            for k in range(TOP_K):
                copy(r, k).start(priority=k % 2)
            return carry
        lax.fori_loop(0, tt, issue, 0)

        def drain(r, carry):
            for k in range(TOP_K):
                copy(r, k).wait()
            return carry
        lax.fori_loop(0, tt, drain, 0)

    @pl.when(pl.program_id(0) < tiles_a)
    def _():
        scatter_rows(ha_ref)

    @pl.when(pl.program_id(0) >= tiles_a)
    def _():
        scatter_rows(hb_ref)


def _moe_expert_body(be_ref, nu_ref, xs_ref, w1_ref, b1_ref, w2_ref, b2_ref, o_ref, w1s_ref, w2s_ref):
    i = pl.program_id(0)
    used = i < nu_ref[0]
    e = be_ref[i]
    prev = be_ref[jnp.maximum(i - 1, 0)]
    half = LANES

    @pl.when(used & ((i == 0) | (e != prev)))
    def _():
        r = lax.broadcasted_iota(I32, (2 * half, 2 * half), 0)
        c = lax.broadcasted_iota(I32, (2 * half, 2 * half), 1)
        src_col = jnp.where(c < half, 2 * c, 2 * (c - half) + 1)
        perm = jnp.where(r == src_col, 1.0, 0.0).astype(BF16)
        for j in range(2 * D_FF // (2 * half)):
            sl = slice(j * 2 * half, (j + 1) * 2 * half)
            w1s_ref[:, sl] = jnp.dot(w1_ref[0, :, sl].astype(BF16), perm, preferred_element_type=F32).astype(BF16)
        w2s_ref[...] = w2_ref[0].astype(BF16)

    @pl.when(used)
    def _():
        u = jnp.dot(xs_ref[...].astype(BF16), w1s_ref[...], preferred_element_type=F32) + b1_ref[0]
        acts = []
        for j in range(D_FF // half):
            glu = jnp.minimum(u[:, 2 * j * half:(2 * j + 1) * half], SWIGLU_LIMIT)
            lin = jnp.clip(u[:, (2 * j + 1) * half:(2 * j + 2) * half], -SWIGLU_LIMIT, SWIGLU_LIMIT)
            acts.append((glu * jax.nn.sigmoid(SWIGLU_ALPHA * glu) * (lin + 1.0)).astype(BF16))
        act = jnp.concatenate(acts, axis=1)
        o_ref[...] = jnp.dot(act, w2s_ref[...], preferred_element_type=F32) + b2_ref[0]

    @pl.when(jnp.logical_not(used))
    def _():
        o_ref[...] = jnp.zeros(o_ref.shape, F32)


def _moe_combine_body(dest_ref, gw_ref, xa_ref, xb_ref, gf_ref, os_ref, y_ref, buf_ref, sem, *, tiles_a):
    tt = xa_ref.shape[0]

    def copy(r, k):
        d = dest_ref[r * TOP_K + k]
        return pltpu.make_async_copy(os_ref.at[pl.ds(d, 1), :], buf_ref.at[k, pl.ds(r, 1), :], sem)

    def issue(r, carry):
        for k in range(TOP_K):
            copy(r, k).start(priority=k % 2)
        return carry
    lax.fori_loop(0, tt, issue, 0)

    def drain(r, carry):
        for k in range(TOP_K):
            copy(r, k).wait()
        return carry
    lax.fori_loop(0, tt, drain, 0)

    gw = gw_ref[...]
    y = gw[:, 0:1] * buf_ref[0]
    for k in range(1, TOP_K):
        y = y + gw[:, k:k + 1] * buf_ref[k]
    x1 = jnp.where(pl.program_id(0) < tiles_a, xa_ref[...], xb_ref[...])
    x2 = x1 + y
    out = x2 * lax.rsqrt(jnp.mean(x2 * x2, axis=-1, keepdims=True) + EPS)
    y_ref[...] = out * gf_ref[...]


def moe_and_final_norm(x1a, x1b, h2a, h2b, te, gw, w1, b1p, w2, b2, g_final):
    n = te.shape[0]
    tiles_a = x1a.shape[0] // MOE_T
    row_a = lambda i: (jnp.minimum(i, tiles_a - 1), 0)
    row_b = lambda i: (jnp.maximum(i - tiles_a, 0), 0)
    tt = MOE_T
    bm = MOE_BM
    nb = -(-(n * TOP_K + N_EXPERTS * (bm - 1)) // bm)
    nbp = -(-nb // SUBLANES) * SUBLANES
    row = lambda i: (i, 0)
    const = lambda i: (0, 0)

    pos, cnt = pl.pallas_call(
        _moe_pos_body,
        grid=(n // tt,),
        in_specs=[pl.BlockSpec((tt, LANES), row)],
        out_specs=[pl.BlockSpec((tt, LANES), row), pl.BlockSpec((SUBLANES, LANES), const)],
        out_shape=[jax.ShapeDtypeStruct((n, LANES), F32), jax.ShapeDtypeStruct((SUBLANES, LANES), F32)],
        scratch_shapes=[pltpu.VMEM((SUBLANES, LANES), F32)],
        compiler_params=_cparams("arbitrary"),
        name="moe_positions",
    )(te)

    dest, be, nu = pl.pallas_call(
        functools.partial(_moe_dest_body, bm=bm),
        grid=(n // tt,),
        in_specs=[pl.BlockSpec((tt, LANES), row), pl.BlockSpec((tt, LANES), row),
                  pl.BlockSpec((SUBLANES, LANES), const)],
        out_specs=[pl.BlockSpec((tt, LANES), row), pl.BlockSpec((nbp, LANES), const),
                   pl.BlockSpec((SUBLANES, LANES), const)],
        out_shape=[jax.ShapeDtypeStruct((n, LANES), I32), jax.ShapeDtypeStruct((nbp, LANES), I32),
                   jax.ShapeDtypeStruct((SUBLANES, LANES), I32)],
        compiler_params=_cparams("arbitrary"),
        name="moe_destinations",
    )(te, pos, cnt)
    dest_flat = dest[:, :TOP_K].reshape(n * TOP_K)
    block_e = be[:nb, 0]
    n_used = nu[0, 0:1]

    xs = pl.pallas_call(
        functools.partial(_moe_dispatch_body, tiles_a=tiles_a),
        grid=(n // tt,),
        in_specs=[pl.BlockSpec((tt * TOP_K,), lambda i: (i,), memory_space=pltpu.SMEM),
                  pl.BlockSpec((tt, D_MODEL), row_a),
                  pl.BlockSpec((tt, D_MODEL), row_b),
                  pl.BlockSpec(memory_space=pl.ANY)],
        out_specs=pl.BlockSpec(memory_space=pl.ANY),
        out_shape=jax.ShapeDtypeStruct((nb * bm, D_MODEL), F32),
        scratch_shapes=[pltpu.SemaphoreType.DMA(())],
        input_output_aliases={3: 0},
        compiler_params=_cparams("arbitrary"),
        name="moe_dispatch",
    )(dest_flat, h2a, h2b, jnp.zeros((nb * bm, D_MODEL), F32))

    out_sorted = pl.pallas_call(
        _moe_expert_body,
        grid_spec=pltpu.PrefetchScalarGridSpec(
            num_scalar_prefetch=2,
            grid=(nb,),
            in_specs=[pl.BlockSpec((bm, D_MODEL), lambda i, be_, nu_: (i, 0)),
                      pl.BlockSpec((1, D_MODEL, 2 * D_FF), lambda i, be_, nu_: (be_[i], 0, 0)),
                      pl.BlockSpec((1, 1, 2 * D_FF), lambda i, be_, nu_: (be_[i], 0, 0)),
                      pl.BlockSpec((1, D_FF, D_MODEL), lambda i, be_, nu_: (be_[i], 0, 0)),
                      pl.BlockSpec((1, 1, D_MODEL), lambda i, be_, nu_: (be_[i], 0, 0))],
            out_specs=pl.BlockSpec((bm, D_MODEL), lambda i, be_, nu_: (i, 0)),
            scratch_shapes=[pltpu.VMEM((D_MODEL, 2 * D_FF), BF16), pltpu.VMEM((D_FF, D_MODEL), BF16)]),
        out_shape=jax.ShapeDtypeStruct((nb * bm, D_MODEL), F32),
        compiler_params=_cparams("arbitrary"),
        name="moe_experts",
    )(block_e, n_used, xs, w1, b1p, w2, b2)

    return pl.pallas_call(
        functools.partial(_moe_combine_body, tiles_a=tiles_a),
        grid=(n // tt,),
        in_specs=[pl.BlockSpec((tt * TOP_K,), lambda i: (i,), memory_space=pltpu.SMEM),
                  pl.BlockSpec((tt, LANES), row),
                  pl.BlockSpec((tt, D_MODEL), row_a),
                  pl.BlockSpec((tt, D_MODEL), row_b),
                  pl.BlockSpec((1, D_MODEL), const),
                  pl.BlockSpec(memory_space=pl.ANY)],
        out_specs=pl.BlockSpec((tt, D_MODEL), row),
        out_shape=jax.ShapeDtypeStruct((n, D_MODEL), F32),
        scratch_shapes=[pltpu.VMEM((TOP_K, tt, D_MODEL), F32), pltpu.SemaphoreType.DMA(())],
        compiler_params=_cparams("arbitrary"),
        name="moe_combine",
    )(dest_flat, gw, x1a, x1b, g_final.reshape(1, D_MODEL), out_sorted)


def _split_cols(w):
    outs, off = [], 0
    for wd in IN_WIDTHS:
        outs.append(w[:, off:off + wd])
        off += wd
    return outs


def _lane_row(v, off):
    return jnp.zeros((1, LANES), F32).at[0, off:off + v.shape[0]].set(v.astype(F32))


def kernel(x_prompt, x_sample, cache_k, cache_v, cache_idx_k, state_conv, state_ssm, cache_mem_k, cache_mem_v,
           page_table, mem_prompt, g_norm1, w_in, b_gate, conv_w, conv_b, dt_bias, a_log, d_skip, g_ssd_norm,
           g_mem, w_mem_kv, w_ssd_out, w_attn_out, w_mem_out, w_out, g_norm2, w_router, b_router, w_exp1,
           b_exp1, w_exp2, b_exp2, g_final):
    assert w_in.shape[0] == 1, "single-layer trunk"
    bp, lp, _ = x_prompt.shape
    bs, ls, _ = x_sample.shape
    np_, ns = bp * lp, bs * ls

    wz, wxbc, wdt, wq, wk, wv, wqi, wki, wwi, wqm, wgate = _split_cols(w_in[0])
    w_all = jnp.concatenate(
        [wxbc, wgate, wz, wq, wqm, wqi, wk, wv, wki, wdt, wwi,
         jnp.zeros((D_MODEL, W_ALL - OFF_SM - SM_WI - IDX_HEADS), F32)], axis=1).astype(BF16)
    dtb_row = _lane_row(dt_bias[0], SM_DT)
    aneg_row = _lane_row(-jnp.exp(a_log[0].astype(F32)), SM_DT)
    dsk_row = jnp.repeat(d_skip[0].astype(F32), SSD_HEAD_DIM).reshape(1, D_INNER)
    gs_row = g_ssd_norm[0].reshape(1, D_INNER)
    cb_row = conv_b[0].reshape(1, CONV_DIM)
    wr_pad = jnp.zeros((D_MODEL, LANES), F32).at[:, :N_EXPERTS].set(w_router[0])
    br_pad = jnp.full((1, LANES), NEG, F32).at[0, :N_EXPERTS].set(b_router[0])
    b1p = b_exp1[0].reshape(N_EXPERTS, D_FF // LANES, LANES, 2).transpose(0, 1, 3, 2).reshape(N_EXPERTS, 1, 2 * D_FF)
    b2 = b_exp2[0].reshape(N_EXPERTS, 1, D_MODEL)

    xp = x_prompt.reshape(np_, D_MODEL)
    xs = x_sample.reshape(ns, D_MODEL)
    zp = norm_matmul(xp, g_norm1[0], w_all, 1024, IN_PROJ_TN)
    zs = norm_matmul(xs, g_norm1[0], w_all, ns, IN_PROJ_TN)

    kv_p = norm_matmul(mem_prompt.reshape(bp * N_MEM, D_MODEL), g_mem[0], w_mem_kv[0].astype(BF16),
                       min(1024, bp * N_MEM), MEM_WIDTH)
    om_p = mem_attn(zp, bp, lp, kv_p, 0, kv_p, 1, 512)
    om_s = mem_attn(zs, bs, ls, cache_mem_k[0].reshape(bs * N_MEM, MEM_WIDTH), 0,
                    cache_mem_v[0].reshape(bs * N_MEM, MEM_WIDTH), 0, ls)

    conv_prev_p = jnp.zeros((bp, SUBLANES, CONV_DIM), F32)
    conv_prev_s = jnp.concatenate(
        [jnp.zeros((bs, SUBLANES - (CONV_WIDTH - 1), CONV_DIM), F32), state_conv[0]], axis=1)
    ssm0_p = jnp.zeros((bp, D_INNER, D_STATE), F32)
    ssm0_s = state_ssm[0].reshape(bs, D_INNER, D_STATE)
    ys_p, ssm_p = ssd(zp, bp, lp, conv_prev_p, ssm0_p, conv_w[0], cb_row, dtb_row, aneg_row, dsk_row, gs_row)
    ys_s, ssm_s = ssd(zs, bs, ls, conv_prev_s, ssm0_s, conv_w[0], cb_row, dtb_row, aneg_row, dsk_row, gs_row)

    oa_p = dsa_prompt(zp, bp, lp)
    oa_s = dsa_sample(zs, bs, ls, cache_k[0], cache_v[0], cache_idx_k[0], page_table)

    mw = (b_gate[0].reshape(1, -1), w_ssd_out[0].astype(BF16), w_attn_out[0].astype(BF16),
          w_mem_out[0].astype(BF16), w_out[0].astype(BF16), g_norm2[0].reshape(1, D_MODEL), wr_pad, br_pad)
    x1_p, h2_p, te_p, gw_p = merge(xp, zp, ys_p, oa_p, om_p, *mw, 512)
    x1_s, h2_s, te_s, gw_s = merge(xs, zs, ys_s, oa_s, om_s, *mw, ns)

    cat = lambda a, b: jnp.concatenate([a, b], axis=0)
    y_all = moe_and_final_norm(x1_p, x1_s, h2_p, h2_s, cat(te_p, te_s), cat(gw_p, gw_s),
                               w_exp1[0], b1p, w_exp2[0], b2, g_final)
    y_prompt = y_all[:np_].reshape(bp, lp, D_MODEL)
    y_sample = y_all[np_:].reshape(bs, ls, D_MODEL)

    def kvi(z, b, l):
        k = z[:, OFF_K:OFF_K + N_KV_HEADS * HEAD_DIM].reshape(1, b, l, N_KV_HEADS, HEAD_DIM)
        v = z[:, OFF_V:OFF_V + N_KV_HEADS * HEAD_DIM].reshape(1, b, l, N_KV_HEADS, HEAD_DIM)
        ki = z[:, OFF_SM + SM_KI:OFF_SM + SM_KI + IDX_DIM].reshape(1, b, l, IDX_DIM)
        conv = z.reshape(b, l, W_ALL)[:, l - (CONV_WIDTH - 1):, OFF_XBC:OFF_XBC + CONV_DIM][None]
        return k, v, ki, conv

    k_p, v_p, ki_p, conv_p = kvi(zp, bp, lp)
    k_s, v_s, ki_s, conv_s = kvi(zs, bs, ls)
    mk_p = kv_p[:, :MEM_WIDTH].reshape(1, bp, N_MEM, MEM_HEADS, MEM_HEAD_DIM)
    mv_p = kv_p[:, MEM_WIDTH:].reshape(1, bp, N_MEM, MEM_HEADS, MEM_HEAD_DIM)
    ssm_shape = (1, -1, SSD_HEADS, SSD_HEAD_DIM, D_STATE)
    return (y_prompt, y_sample, k_p, v_p, ki_p, conv_p, ssm_p.reshape(ssm_shape), mk_p, mv_p,
            k_s, v_s, ki_s, conv_s, ssm_s.reshape(ssm_shape))
```

```python
import functools

import numpy as np
import jax
import jax.numpy as jnp
from jax import lax
from jax.experimental import pallas as pl
from jax.experimental.pallas import tpu as pltpu

F32 = jnp.float32
BF16 = jnp.bfloat16
I32 = jnp.int32
HIGHEST = lax.Precision.HIGHEST

D_MODEL = 1024
D_INNER = 2048
SSD_HEAD_DIM = 64
SSD_HEADS = 32
SSD_GROUPS = 4
D_STATE = 128
CONV_WIDTH = 4
CONV_DIM = D_INNER + 2 * SSD_GROUPS * D_STATE
SSD_CHUNK = 128
N_HEADS = 16
N_KV_HEADS = 4
HEAD_DIM = 64
IDX_HEADS = 8
IDX_DIM = 64
TOPK_MAX = 256
N_MEM = 256
MEM_HEADS = 4
MEM_HEAD_DIM = 256
MEM_WIDTH = MEM_HEADS * MEM_HEAD_DIM
N_EXPERTS = 32
TOP_K = 4
D_FF = D_MODEL
SWIGLU_LIMIT = 7.0
SWIGLU_ALPHA = 1.702
N_BRANCH = 3
EPS = 1e-6
PAGE_SIZE = 128
IN_WIDTHS = (D_INNER, CONV_DIM, SSD_HEADS, N_HEADS * HEAD_DIM, N_KV_HEADS * HEAD_DIM, N_KV_HEADS * HEAD_DIM,
             IDX_HEADS * IDX_DIM, IDX_DIM, IDX_HEADS, MEM_WIDTH, N_BRANCH * D_MODEL)

LANES = 128
SUBLANES = 8
VMEM_LIMIT = 56 * 1024 * 1024

OFF_XBC = 0
OFF_GATE = OFF_XBC + CONV_DIM
OFF_Z = OFF_GATE + N_BRANCH * D_MODEL
OFF_Q = OFF_Z + D_INNER
OFF_QM = OFF_Q + N_HEADS * HEAD_DIM
OFF_QI = OFF_QM + MEM_WIDTH
OFF_K = OFF_QI + IDX_HEADS * IDX_DIM
OFF_V = OFF_K + N_KV_HEADS * HEAD_DIM
OFF_SM = OFF_V + N_KV_HEADS * HEAD_DIM
SM_KI = 0
SM_DT = SM_KI + IDX_DIM
SM_WI = SM_DT + SSD_HEADS
IN_PROJ_TN = 1280
W_ALL = OFF_SM + 2 * LANES

NEG = -1e30
INT_MIN = np.int32(-2 ** 31)
INT_MAX = np.int32(2 ** 31 - 1)

ATT_PASSES = 1
MOE_BM = 512
MOE_T = 256


def _cparams(*sem):
    return pltpu.CompilerParams(dimension_semantics=sem, vmem_limit_bytes=VMEM_LIMIT)


def _nt_dot(a, b):
    return lax.dot_general(a, b, (((1,), (1,)), ((), ())), preferred_element_type=F32)


def _float_key(x):
    bits = lax.bitcast_convert_type(x, I32)
    return jnp.where(bits < 0, bits ^ INT_MAX, bits)


def _norm_matmul_body(x_ref, g_ref, w_ref, o_ref, h_ref):
    @pl.when(pl.program_id(1) == 0)
    def _():
        x = x_ref[...]
        h = x * lax.rsqrt(jnp.mean(x * x, axis=-1, keepdims=True) + EPS)
        h_ref[...] = (h * g_ref[...]).astype(BF16)

    o_ref[...] = jnp.dot(h_ref[...], w_ref[...], preferred_element_type=F32)


def norm_matmul(x, g, w, tm, tn):
    n, d = x.shape
    wn = w.shape[1]
    return pl.pallas_call(
        _norm_matmul_body,
        grid=(n // tm, wn // tn),
        in_specs=[pl.BlockSpec((tm, d), lambda i, j: (i, 0)),
                  pl.BlockSpec((1, d), lambda i, j: (0, 0)),
                  pl.BlockSpec((d, tn), lambda i, j: (0, j))],
        out_specs=pl.BlockSpec((tm, tn), lambda i, j: (i, j)),
        out_shape=jax.ShapeDtypeStruct((n, wn), F32),
        scratch_shapes=[pltpu.VMEM((tm, d), BF16)],
        compiler_params=_cparams("parallel", "arbitrary"),
        name="norm_matmul",
    )(x, g.reshape(1, d), w)


def _softplus(x):
    return jnp.maximum(x, 0.0) + jnp.log1p(jnp.exp(-jnp.abs(x)))


def _silu(x):
    return x * jax.nn.sigmoid(x)


def _expand_heads(v, q):
    lane = lax.broadcasted_iota(I32, (q, LANES), 1)
    cols = []
    for t in range(SSD_HEADS // 2):
        c0 = jnp.broadcast_to(v[:, SM_DT + 2 * t:SM_DT + 2 * t + 1], (q, LANES))
        c1 = jnp.broadcast_to(v[:, SM_DT + 2 * t + 1:SM_DT + 2 * t + 2], (q, LANES))
        cols.append(jnp.where(lane < SSD_HEAD_DIM, c0, c1))
    return jnp.concatenate(cols, axis=1)


def _ssd_body(xbc_ref, z_ref, sm_ref, convp_ref, init_ref, cw_ref, cb_ref, dtb_ref, aneg_ref, dsk_ref, gs_ref,
              y_ref, st_ref, xpad_ref, state_ref, *, rows_in, q):
    c = pl.program_id(1)
    pad = SUBLANES

    @pl.when(c == 0)
    def _():
        xpad_ref[0:pad, :] = convp_ref[0]
        state_ref[...] = init_ref[0]

    xpad_ref[pad:pad + rows_in, :] = xbc_ref[...]
    if rows_in < q:
        xpad_ref[pad + rows_in:pad + q, :] = jnp.zeros((q - rows_in, CONV_DIM), F32)

    acc = cb_ref[...]
    for j in range(CONV_WIDTH):
        lo = pad - (CONV_WIDTH - 1) + j
        acc = acc + xpad_ref[lo:lo + q, :] * cw_ref[j:j + 1, :]
    xc = _silu(acc)
    xpad_ref[0:pad, :] = xpad_ref[q:q + pad, :]

    xs = xc[:, :D_INNER]
    gn = SSD_GROUPS * D_STATE
    bm = xc[:, D_INNER:D_INNER + gn].astype(BF16)
    cm = xc[:, D_INNER + gn:].astype(BF16)

    sm = sm_ref[...]
    zz = z_ref[...]
    if rows_in < q:
        sm = jnp.concatenate([sm, jnp.zeros((q - rows_in, LANES), F32)], axis=0)
        zz = jnp.concatenate([zz, jnp.zeros((q - rows_in, D_INNER), F32)], axis=0)
    row = lax.broadcasted_iota(I32, (q, LANES), 0)
    dt = _softplus(sm + dtb_ref[...])
    if rows_in < q:
        dt = jnp.where(row < rows_in, dt, 0.0)
    a = dt * aneg_ref[...]
    tri = (lax.broadcasted_iota(I32, (q, q), 0) >= lax.broadcasted_iota(I32, (q, q), 1)).astype(F32)
    a_cs = jnp.dot(tri, a, precision=HIGHEST, preferred_element_type=F32)
    a_t = a_cs.T
    a_last = a_cs[q - 1:q, :]
    dte = jnp.exp(a_last - a_cs)
    e_in = jnp.exp(a_cs)

    xdt = xs * _expand_heads(dt, q)
    xdt_bf = xdt.astype(BF16)
    xw_bf = (xs * _expand_heads(dt * dte, q)).astype(BF16)
    ein_x = _expand_heads(e_in, q)

    causal = lax.broadcasted_iota(I32, (q, q), 0) >= lax.broadcasted_iota(I32, (q, q), 1)
    lane = lax.broadcasted_iota(I32, (q, LANES), 1)
    hpg = SSD_HEADS // SSD_GROUPS
    gw = hpg * SSD_HEAD_DIM
    y_parts = []
    for g in range(SSD_GROUPS):
        bg = bm[:, g * D_STATE:(g + 1) * D_STATE]
        cg = cm[:, g * D_STATE:(g + 1) * D_STATE]
        cb = _nt_dot(cg, bg)
        m_h = []
        for e in range(hpg):
            h = g * hpg + e
            col = a_cs[:, SM_DT + h:SM_DT + h + 1]
            rw = a_t[SM_DT + h:SM_DT + h + 1, :]
            decay = jnp.exp(jnp.where(causal, col - rw, -jnp.inf))
            m_h.append((cb * decay).astype(BF16))
        yd = []
        for t in range(hpg // 2):
            pair = g * (hpg // 2) + t
            slab = xdt_bf[:, pair * LANES:(pair + 1) * LANES]
            ya = jnp.dot(m_h[2 * t], slab, preferred_element_type=F32)
            yb = jnp.dot(m_h[2 * t + 1], slab, preferred_element_type=F32)
            yd.append(jnp.where(lane < SSD_HEAD_DIM, ya, yb))
        s_old = state_ref[g * gw:(g + 1) * gw, :]
        y_off = _nt_dot(cg, s_old.astype(BF16)) * ein_x[:, g * gw:(g + 1) * gw]
        y_parts.append(jnp.concatenate(yd, axis=1) + y_off)
        new = lax.dot_general(xw_bf[:, g * gw:(g + 1) * gw], bg, (((0,), (0,)), ((), ())),
                              preferred_element_type=F32)
        for e in range(hpg):
            h = g * hpg + e
            dec = jnp.exp(a_t[SM_DT + h:SM_DT + h + 1, q - 1:q])
            lo = e * SSD_HEAD_DIM
            state_ref[h * SSD_HEAD_DIM:(h + 1) * SSD_HEAD_DIM, :] = (
                s_old[lo:lo + SSD_HEAD_DIM, :] * dec + new[lo:lo + SSD_HEAD_DIM, :])

    y = jnp.concatenate(y_parts, axis=1) + dsk_ref[...] * xs
    y = y * _silu(zz)
    outs = []
    for g in range(SSD_GROUPS):
        yg = y[:, g * gw:(g + 1) * gw]
        outs.append(yg * lax.rsqrt(jnp.mean(yg * yg, axis=-1, keepdims=True) + EPS))
    y = jnp.concatenate(outs, axis=1) * gs_ref[...]
    y_ref[...] = y[:rows_in].astype(BF16)

    @pl.when(c == pl.num_programs(1) - 1)
    def _():
        st_ref[0] = state_ref[...]


def ssd(zall, batch, seq, conv_prev8, ssm_init, conv_w, conv_b, dtb_row, aneg_row, dsk_row, gs_row):
    q = SSD_CHUNK
    rows_in = min(seq, q)
    nch = seq // rows_in
    row_map = lambda b, c: (b * nch + c)
    const2 = lambda b, c: (0, 0)
    body = functools.partial(_ssd_body, rows_in=rows_in, q=q)
    return pl.pallas_call(
        body,
        grid=(batch, nch),
        in_specs=[pl.BlockSpec((rows_in, CONV_DIM), lambda b, c: (row_map(b, c), OFF_XBC // CONV_DIM)),
                  pl.BlockSpec((rows_in, D_INNER), lambda b, c: (row_map(b, c), OFF_Z // D_INNER)),
                  pl.BlockSpec((rows_in, LANES), lambda b, c: (row_map(b, c), OFF_SM // LANES)),
                  pl.BlockSpec((1, SUBLANES, CONV_DIM), lambda b, c: (b, 0, 0)),
                  pl.BlockSpec((1, D_INNER, D_STATE), lambda b, c: (b, 0, 0)),
                  pl.BlockSpec((CONV_WIDTH, CONV_DIM), const2),
                  pl.BlockSpec((1, CONV_DIM), const2),
                  pl.BlockSpec((1, LANES), const2),
                  pl.BlockSpec((1, LANES), const2),
                  pl.BlockSpec((1, D_INNER), const2),
                  pl.BlockSpec((1, D_INNER), const2)],
        out_specs=[pl.BlockSpec((rows_in, D_INNER), lambda b, c: (row_map(b, c), 0)),
                   pl.BlockSpec((1, D_INNER, D_STATE), lambda b, c: (b, 0, 0))],
        out_shape=[jax.ShapeDtypeStruct((batch * seq, D_INNER), BF16),
                   jax.ShapeDtypeStruct((batch, D_INNER, D_STATE), F32)],
        scratch_shapes=[pltpu.VMEM((q + 2 * SUBLANES, CONV_DIM), F32),
                        pltpu.VMEM((D_INNER, D_STATE), F32)],
        compiler_params=_cparams("parallel", "arbitrary"),
        name="ssd",
    )(zall, zall, zall, conv_prev8, ssm_init, conv_w, conv_b, dtb_row, aneg_row, dsk_row, gs_row)


def _mem_attn_body(q_ref, k_ref, v_ref, o_ref):
    for h in range(MEM_HEADS):
        sl = slice(h * MEM_HEAD_DIM, (h + 1) * MEM_HEAD_DIM)
        s = _nt_dot(q_ref[:, sl].astype(BF16), k_ref[:, sl].astype(BF16)) * (MEM_HEAD_DIM ** -0.5)
        m = jnp.max(s, axis=-1, keepdims=True)
        p = jnp.exp(s - m)
        p = p / jnp.sum(p, axis=-1, keepdims=True)
        o = jnp.dot(p.astype(BF16), v_ref[:, sl].astype(BF16), preferred_element_type=F32)
        o_ref[:, sl] = o.astype(BF16)


def mem_attn(zall, batch, seq, k_arr, k_col, v_arr, v_col, tm):
    nt = seq // tm
    return pl.pallas_call(
        _mem_attn_body,
        grid=(batch, nt),
        in_specs=[pl.BlockSpec((tm, MEM_WIDTH), lambda b, i: (b * nt + i, OFF_QM // MEM_WIDTH)),
                  pl.BlockSpec((N_MEM, MEM_WIDTH), lambda b, i: (b, k_col)),
                  pl.BlockSpec((N_MEM, MEM_WIDTH), lambda b, i: (b, v_col))],
        out_specs=pl.BlockSpec((tm, MEM_WIDTH), lambda b, i: (b * nt + i, 0)),
        out_shape=jax.ShapeDtypeStruct((batch * seq, MEM_WIDTH), BF16),
        compiler_params=_cparams("parallel", "arbitrary"),
        name="mem_attn",
    )(zall, k_arr, v_arr)


def _kth_largest_key(count_ge, shape, n_sel):
    def bit_body(t, ans):
        cand = ans | jnp.left_shift(jnp.int32(1), 31 - t)
        cnt = count_ge(cand ^ INT_MIN)
        return jnp.where(cnt >= n_sel, cand, ans)

    ans = lax.fori_loop(0, 32, bit_body, jnp.zeros(shape, I32))
    return ans ^ INT_MIN


def _tie_cut(count_eq_below, need, shape, nbits):
    def bit_body(t, lo):
        cand = lo | jnp.left_shift(jnp.int32(1), nbits - 1 - t)
        cnt = count_eq_below(cand)
        return jnp.where(cnt < need, cand, lo)

    return lax.fori_loop(0, nbits, bit_body, jnp.zeros(shape, I32))


def _select_bias(key, kpos, thr, cut, visible):
    sel = (key > thr) | ((key == thr) & (kpos <= cut))
    return jnp.where(sel & visible, 0.0, NEG)


def _dsa_prompt_body(q_ref, qi_ref, smq_ref, k_ref, v_ref, smk_ref, o_ref,
                     kh_ref, vt_ref, kis_ref, qt2_ref, qit_ref, keys_ref, bias_ref, ot_ref, *, tq, kc, seq, n_sel):
    i = pl.program_id(1)

    @pl.when(i == 0)
    def _():
        def cast_rows(r, carry):
            rs = pl.ds(pl.multiple_of(r * kc, kc), kc)
            kk = k_ref[rs, :]
            for h in range(N_KV_HEADS):
                kh_ref[h, rs, :] = kk[:, h * HEAD_DIM:(h + 1) * HEAD_DIM].astype(BF16)
            vt_ref[r] = v_ref[rs, :].T.astype(BF16)
            kis_ref[rs, :] = smk_ref[rs, SM_KI:SM_KI + IDX_DIM].astype(BF16)
            return carry
        lax.fori_loop(0, seq // kc, cast_rows, 0)

    nkc = (i * tq + tq - 1) // kc + 1
    qpos = i * tq + lax.broadcasted_iota(I32, (kc, tq), 1)
    krow = lax.broadcasted_iota(I32, (kc, tq), 0)

    qt = (q_ref[...] * (HEAD_DIM ** -0.5)).T.astype(BF16)
    for h in range(N_HEADS):
        qt2_ref[:, h * tq:(h + 1) * tq] = qt[h * HEAD_DIM:(h + 1) * HEAD_DIM, :]
    qit_ref[...] = (qi_ref[...] * (IDX_DIM ** -0.5)).T.astype(BF16)
    wt = smq_ref[...].T[SM_WI:SM_WI + IDX_HEADS, :] * (IDX_HEADS ** -0.5)

    def score_chunk(c, carry):
        ks = pl.ds(pl.multiple_of(c * kc, kc), kc)
        kic = kis_ref[ks, :]
        sc = jnp.zeros((kc, tq), F32)
        for h in range(IDX_HEADS):
            d = jnp.dot(kic, qit_ref[h * IDX_DIM:(h + 1) * IDX_DIM, :], preferred_element_type=F32)
            sc = sc + jnp.maximum(d, 0.0) * wt[h:h + 1, :]
        key = _float_key(sc + 0.0)
        keys_ref[c] = jnp.where(c * kc + krow <= qpos, key, INT_MIN)
        return carry
    lax.fori_loop(0, nkc, score_chunk, 0)

    def count(pred):
        def body(c, acc):
            return acc + jnp.where(pred(keys_ref[c], c * kc + krow), 1.0, 0.0)
        acc = lax.fori_loop(0, nkc, body, jnp.zeros((kc, tq), F32))
        return jnp.sum(acc, axis=0, keepdims=True)

    vec = (1, tq)
    thr = _kth_largest_key(lambda cand: count(lambda key, kpos: key >= cand), vec, n_sel)
    n_gt = count(lambda key, kpos: key > thr)
    n_eq = count(lambda key, kpos: key == thr)
    need = n_sel - n_gt
    excess = jnp.max(jnp.where((n_eq > need) & (thr != INT_MIN), 1.0, 0.0))
    nbits = max(1, int(seq - 1).bit_length())
    cut = lax.cond(
        excess > 0.0,
        lambda: _tie_cut(lambda cand: count(lambda key, kpos: (key == thr) & (kpos < cand)), need, vec, nbits),
        lambda: jnp.full(vec, INT_MAX, I32))

    def bias_chunk(c, carry):
        kpos = c * kc + krow
        bias_ref[c] = _select_bias(keys_ref[c], kpos, thr, cut, kpos <= qpos)
        return carry
    lax.fori_loop(0, nkc, bias_chunk, 0)

    grp = N_HEADS // N_KV_HEADS
    kv_per_pass = N_KV_HEADS // ATT_PASSES
    hpp = kv_per_pass * grp
    for ps_i in range(ATT_PASSES):
        kv0 = ps_i * kv_per_pass

        def att_chunk(c, carry, kv0=kv0):
            ms, ls, accs = carry
            ks = pl.ds(pl.multiple_of(c * kc, kc), kc)
            bias = bias_ref[c]
            s4 = [jnp.dot(kh_ref[kv0 + j, ks, :], qt2_ref[:, (kv0 + j) * grp * tq:(kv0 + j + 1) * grp * tq],
                          preferred_element_type=F32) for j in range(kv_per_pass)]
            ms_n, ls_n, accs_n = [], [], []
            for h in range(hpp):
                s = s4[h // grp][:, (h % grp) * tq:(h % grp + 1) * tq] + bias
                m_new = jnp.maximum(ms[h], jnp.max(s, axis=0, keepdims=True))
                p = jnp.exp(s - m_new)
                alpha = jnp.exp(ms[h] - m_new)
                ms_n.append(m_new)
                ls_n.append(alpha * ls[h] + jnp.sum(p, axis=0, keepdims=True))
                kh = kv0 + h // grp
                vtc = vt_ref[c, kh * HEAD_DIM:(kh + 1) * HEAD_DIM, :]
                accs_n.append(alpha * accs[h] + jnp.dot(vtc, p.astype(BF16), preferred_element_type=F32))
            return tuple(ms_n), tuple(ls_n), tuple(accs_n)

        init = (tuple(jnp.full(vec, NEG, F32) for _ in range(hpp)),
                tuple(jnp.zeros(vec, F32) for _ in range(hpp)),
                tuple(jnp.zeros((HEAD_DIM, tq), F32) for _ in range(hpp)))
        _, ls, accs = lax.fori_loop(0, nkc, att_chunk, init)
        for h in range(hpp):
            hh = kv0 * grp + h
            ot_ref[hh * HEAD_DIM:(hh + 1) * HEAD_DIM, :] = accs[h] / ls[h]
    o_ref[...] = ot_ref[...].T.astype(BF16)


def dsa_prompt(zall, batch, seq):
    tq = 128
    kc = 256
    nq = seq // tq
    n_sel = min(TOPK_MAX, seq // 4)
    kvw = N_KV_HEADS * HEAD_DIM
    qw = N_HEADS * HEAD_DIM
    qiw = IDX_HEADS * IDX_DIM
    body = functools.partial(_dsa_prompt_body, tq=tq, kc=kc, seq=seq, n_sel=n_sel)
    return pl.pallas_call(
        body,
        grid=(batch, nq),
        in_specs=[pl.BlockSpec((tq, qw), lambda b, i: (b * nq + i, OFF_Q // qw)),
                  pl.BlockSpec((tq, qiw), lambda b, i: (b * nq + i, OFF_QI // qiw)),
                  pl.BlockSpec((tq, LANES), lambda b, i: (b * nq + i, OFF_SM // LANES)),
                  pl.BlockSpec((seq, kvw), lambda b, i: (b, OFF_K // kvw)),
                  pl.BlockSpec((seq, kvw), lambda b, i: (b, OFF_V // kvw)),
                  pl.BlockSpec((seq, LANES), lambda b, i: (b, OFF_SM // LANES))],
        out_specs=pl.BlockSpec((tq, qw), lambda b, i: (b * nq + i, 0)),
        out_shape=jax.ShapeDtypeStruct((batch * seq, qw), BF16),
        scratch_shapes=[pltpu.VMEM((N_KV_HEADS, seq, HEAD_DIM), BF16),
                        pltpu.VMEM((seq // kc, kvw, kc), BF16),
                        pltpu.VMEM((seq, IDX_DIM), BF16),
                        pltpu.VMEM((HEAD_DIM, N_HEADS * tq), BF16),
                        pltpu.VMEM((qiw, tq), BF16),
                        pltpu.VMEM((seq // kc, kc, tq), I32),
                        pltpu.VMEM((seq // kc, kc, tq), F32),
                        pltpu.VMEM((qw, tq), F32)],
        compiler_params=_cparams("parallel", "arbitrary"),
        name="dsa_prompt",
    )(zall, zall, zall, zall, zall, zall)


def _dsa_s_score_body(pt_ref, qs_ref, w_ref, *refs, pg):
    ki_refs, o_ref = refs[:pg], refs[pg]
    qs = qs_ref[0]
    wcol = w_ref[0] * (IDX_HEADS ** -0.5)
    t = qs.shape[0] // IDX_HEADS
    for p in range(pg):
        d = jnp.dot(qs, ki_refs[p][0].astype(BF16), preferred_element_type=F32)
        r = jnp.maximum(d * (IDX_DIM ** -0.5), 0.0) * wcol
        sc = r[0:t, :]
        for h in range(1, IDX_HEADS):
            sc = sc + r[h * t:(h + 1) * t, :]
        o_ref[0, :, p * PAGE_SIZE:(p + 1) * PAGE_SIZE] = sc + 0.0


def _dsa_s_select_body(sc_ref, qs_ref, w_ref, smn_ref, o_ref, *, t, past, n_sel, nbat):
    rows = nbat * t
    sc_new = []
    for b in range(nbat):
        wcol = w_ref[b] * (IDX_HEADS ** -0.5)
        ki_new = smn_ref[b * t:(b + 1) * t, SM_KI:SM_KI + IDX_DIM].astype(BF16)
        ki_new = jnp.concatenate([ki_new, jnp.zeros((LANES - t, IDX_DIM), BF16)], axis=0)
        d = _nt_dot(qs_ref[b], ki_new)
        r = jnp.maximum(d * (IDX_DIM ** -0.5), 0.0) * wcol
        sc = r[0:t, :]
        for h in range(1, IDX_HEADS):
            sc = sc + r[h * t:(h + 1) * t, :]
        sc_new.append(sc)
    sc_new = jnp.concatenate(sc_new, axis=0)
    lane_t = lax.broadcasted_iota(I32, (t, LANES), 1)
    vis_t = lane_t <= lax.broadcasted_iota(I32, (t, LANES), 0)
    vis_n = jnp.concatenate([vis_t] * nbat, axis=0)
    lane_n = lax.broadcasted_iota(I32, (rows, LANES), 1)
    key_n = jnp.where(vis_n, _float_key(sc_new + 0.0), INT_MIN)
    key_p = _float_key(sc_ref[...].reshape(rows, past))
    pos_p = lax.broadcasted_iota(I32, (rows, past), 1)
    pos_n = past + lane_n

    def count(pred):
        return (jnp.sum(jnp.where(pred(key_p, pos_p), 1.0, 0.0), axis=1, keepdims=True)
                + jnp.sum(jnp.where(pred(key_n, pos_n), 1.0, 0.0), axis=1, keepdims=True))

    vec = (rows, 1)
    thr = _kth_largest_key(lambda cand: count(lambda key, kpos: key >= cand), vec, n_sel)
    need = n_sel - count(lambda key, kpos: key > thr)
    n_eq = count(lambda key, kpos: key == thr)
    excess = jnp.max(jnp.where((n_eq > need) & (thr != INT_MIN), 1.0, 0.0))
    nbits = max(1, int(past + t - 1).bit_length())
    cut = lax.cond(
        excess > 0.0,
        lambda: _tie_cut(lambda cand: count(lambda key, kpos: (key == thr) & (kpos < cand)), need, vec, nbits),
        lambda: jnp.full(vec, INT_MAX, I32))
    o_ref[:, :, 0:past] = _select_bias(key_p, pos_p, thr, cut, pos_p >= 0).reshape(nbat, t, past)
    o_ref[:, :, past:past + LANES] = _select_bias(key_n, pos_n, thr, cut, vis_n).reshape(nbat, t, LANES)


def _dsa_s_attn_body(pt_ref, qbd_ref, bias_ref, biasn_ref, kn_ref, vn_ref, *refs, pg, t):
    k_refs, v_refs = refs[:pg], refs[pg:2 * pg]
    o_ref, m_ref, l_ref, acc_ref = refs[2 * pg:]
    j = pl.program_id(1)
    rows = qbd_ref.shape[1]
    rep = rows // t

    @pl.when(j == 0)
    def _():
        m_ref[...] = jnp.full(m_ref.shape, NEG, F32)
        l_ref[...] = jnp.zeros(l_ref.shape, F32)
        acc_ref[...] = jnp.zeros(acc_ref.shape, F32)

    qbd = qbd_ref[0]

    def update(kt, vt, bias):
        s = jnp.dot(qbd, kt, preferred_element_type=F32) * (HEAD_DIM ** -0.5) + jnp.concatenate([bias] * rep, axis=0)
        m = m_ref[...]
        m_new = jnp.maximum(m, jnp.max(s, axis=1, keepdims=True))
        p = jnp.exp(s - m_new)
        alpha = jnp.exp(m - m_new)
        l_ref[...] = alpha * l_ref[...] + jnp.sum(p, axis=1, keepdims=True)
        acc_ref[...] = alpha * acc_ref[...] + _nt_dot(p.astype(BF16), vt)
        m_ref[...] = m_new

    kt = jnp.concatenate([r[0] for r in k_refs], axis=1).astype(BF16)
    vt = jnp.concatenate([r[0] for r in v_refs], axis=1).astype(BF16)
    update(kt, vt, bias_ref[0])

    @pl.when(j == pl.num_programs(1) - 1)
    def _():
        kvw = N_KV_HEADS * HEAD_DIM
        zpad = jnp.zeros((LANES - t, kvw), F32)
        update(jnp.concatenate([kn_ref[...], zpad], axis=0).T.astype(BF16),
               jnp.concatenate([vn_ref[...], zpad], axis=0).T.astype(BF16), biasn_ref[0])
        o_ref[0] = acc_ref[...] / l_ref[...]


def dsa_sample(zs, batch, t, cache_k, cache_v, cache_ki, page_table):
    n_pages = page_table.shape[1]
    past = n_pages * PAGE_SIZE
    n_sel = min(TOPK_MAX, (past + t) // 4)
    pg = 16 if n_pages % 16 == 0 else 8
    nj = n_pages // pg
    n_pool = cache_k.shape[0]
    kvw = N_KV_HEADS * HEAD_DIM
    grp = N_HEADS // N_KV_HEADS

    qi = zs[:, OFF_QI:OFF_QI + IDX_HEADS * IDX_DIM].reshape(batch, t, IDX_HEADS, IDX_DIM)
    qs = jnp.transpose(qi, (0, 2, 1, 3)).reshape(batch, IDX_HEADS * t, IDX_DIM).astype(BF16)
    wi = zs[:, OFF_SM + SM_WI:OFF_SM + SM_WI + IDX_HEADS].reshape(batch, t, IDX_HEADS)
    wcol = jnp.transpose(wi, (0, 2, 1)).reshape(batch, IDX_HEADS * t, 1)

    def page_spec(shape, p):
        return pl.BlockSpec(shape, lambda b, j, pt: (pt[b, j * pg + p],) + (0,) * (len(shape) - 1))

    scores = pl.pallas_call(
        functools.partial(_dsa_s_score_body, pg=pg),
        grid_spec=pltpu.PrefetchScalarGridSpec(
            num_scalar_prefetch=1,
            grid=(batch, nj),
            in_specs=[pl.BlockSpec((1, IDX_HEADS * t, IDX_DIM), lambda b, j, pt: (b, 0, 0)),
                      pl.BlockSpec((1, IDX_HEADS * t, 1), lambda b, j, pt: (b, 0, 0))]
                     + [page_spec((1, IDX_DIM, PAGE_SIZE), p) for p in range(pg)],
            out_specs=pl.BlockSpec((1, t, pg * PAGE_SIZE), lambda b, j, pt: (b, 0, j))),
        out_shape=jax.ShapeDtypeStruct((batch, t, past), F32),
        compiler_params=_cparams("parallel", "arbitrary"),
        name="dsa_sample_scores",
    )(page_table, qs, wcol, *([jnp.swapaxes(cache_ki, 1, 2)] * pg))

    nbat = 4 if (batch % 4 == 0 and t % SUBLANES == 0) else 1
    bias = pl.pallas_call(
        functools.partial(_dsa_s_select_body, t=t, past=past, n_sel=n_sel, nbat=nbat),
        grid=(batch // nbat,),
        in_specs=[pl.BlockSpec((nbat, t, past), lambda b: (b, 0, 0)),
                  pl.BlockSpec((nbat, IDX_HEADS * t, IDX_DIM), lambda b: (b, 0, 0)),
                  pl.BlockSpec((nbat, IDX_HEADS * t, 1), lambda b: (b, 0, 0)),
                  pl.BlockSpec((nbat * t, LANES), lambda b: (b, OFF_SM // LANES))],
        out_specs=pl.BlockSpec((nbat, t, past + LANES), lambda b: (b, 0, 0)),
        out_shape=jax.ShapeDtypeStruct((batch, t, past + LANES), F32),
        compiler_params=_cparams("parallel"),
        name="dsa_sample_select",
    )(scores, qs, wcol, zs)

    q = zs[:, OFF_Q:OFF_Q + N_HEADS * HEAD_DIM].reshape(batch, t, N_KV_HEADS, grp, HEAD_DIM)
    q = jnp.transpose(q, (0, 2, 3, 1, 4))
    eye = jnp.eye(N_KV_HEADS, dtype=F32)
    qbd = (q[:, :, :, :, None, :] * eye[None, :, None, None, :, None]).reshape(batch, N_HEADS * t, kvw).astype(BF16)

    ck = jnp.transpose(cache_k, (0, 2, 3, 1)).reshape(n_pool, kvw, PAGE_SIZE)
    cv = jnp.transpose(cache_v, (0, 2, 3, 1)).reshape(n_pool, kvw, PAGE_SIZE)
    rows = N_HEADS * t
    out = pl.pallas_call(
        functools.partial(_dsa_s_attn_body, pg=pg, t=t),
        grid_spec=pltpu.PrefetchScalarGridSpec(
            num_scalar_prefetch=1,
            grid=(batch, nj),
            in_specs=[pl.BlockSpec((1, rows, kvw), lambda b, j, pt: (b, 0, 0)),
                      pl.BlockSpec((1, t, pg * PAGE_SIZE), lambda b, j, pt: (b, 0, j)),
                      pl.BlockSpec((1, t, LANES), lambda b, j, pt: (b, 0, past // LANES)),
                      pl.BlockSpec((t, kvw), lambda b, j, pt: (b, OFF_K // kvw)),
                      pl.BlockSpec((t, kvw), lambda b, j, pt: (b, OFF_V // kvw))]
                     + [page_spec((1, kvw, PAGE_SIZE), p) for p in range(pg)]
                     + [page_spec((1, kvw, PAGE_SIZE), p) for p in range(pg)],
            out_specs=pl.BlockSpec((1, rows, kvw), lambda b, j, pt: (b, 0, 0)),
            scratch_shapes=[pltpu.VMEM((rows, 1), F32), pltpu.VMEM((rows, 1), F32), pltpu.VMEM((rows, kvw), F32)]),
        out_shape=jax.ShapeDtypeStruct((batch, rows, kvw), F32),
        compiler_params=_cparams("parallel", "arbitrary"),
        name="dsa_sample_attn",
    )(page_table, qbd, bias, bias, zs, zs, *([ck] * pg), *([cv] * pg))

    o = out.reshape(batch, N_KV_HEADS, grp, t, N_KV_HEADS, HEAD_DIM)
    o = jnp.stack([o[:, kh, :, :, kh, :] for kh in range(N_KV_HEADS)], axis=1)
    return jnp.transpose(o, (0, 3, 1, 2, 4)).reshape(batch * t, N_HEADS * HEAD_DIM).astype(BF16)


def _merge_body(x_ref, gate_ref, ys_ref, oa_ref, om_ref, bg_ref, ws_ref, wa_ref, wm_ref, wo_ref, g2_ref, wr_ref, br_ref,
                x1_ref, h2_ref, te_ref, gw_ref):
    gates = jax.nn.sigmoid(gate_ref[...] + bg_ref[...])
    merged = (gates[:, 0:D_MODEL] * jnp.dot(ys_ref[...], ws_ref[...], preferred_element_type=F32)
              + gates[:, D_MODEL:2 * D_MODEL] * jnp.dot(oa_ref[...], wa_ref[...], preferred_element_type=F32)
              + gates[:, 2 * D_MODEL:] * jnp.dot(om_ref[...], wm_ref[...], preferred_element_type=F32))
    x1 = x_ref[...] + jnp.dot(merged.astype(BF16), wo_ref[...], preferred_element_type=F32)
    x1_ref[...] = x1
    h2 = x1 * lax.rsqrt(jnp.mean(x1 * x1, axis=-1, keepdims=True) + EPS)
    h2 = h2 * g2_ref[...]
    h2_ref[...] = h2
    logits = jnp.dot(h2, wr_ref[...], precision=HIGHEST, preferred_element_type=F32) + br_ref[...]
    lane = lax.broadcasted_iota(I32, logits.shape, 1)
    te = jnp.zeros(logits.shape, I32)
    tv = []
    for k in range(TOP_K):
        m = jnp.max(logits, axis=1, keepdims=True)
        idx = jnp.min(jnp.where(logits == m, lane, LANES), axis=1, keepdims=True)
        te = jnp.where(lane == k, idx, te)
        tv.append(m)
        logits = jnp.where(lane == idx, -jnp.inf, logits)
    ex = [jnp.exp(v - tv[0]) for v in tv]
    den = ex[0] + ex[1] + ex[2] + ex[3]
    gw = jnp.zeros(logits.shape, F32)
    for k in range(TOP_K):
        gw = jnp.where(lane == k, ex[k] / den, gw)
    te_ref[...] = te
    gw_ref[...] = gw


def merge(x, zall, ys, oa, om, bg, ws, wa, wm, wo, g2, wr, br, tm):
    n = x.shape[0]
    gw3 = N_BRANCH * D_MODEL
    row = lambda i: (i, 0)
    const = lambda i: (0, 0)
    return pl.pallas_call(
        _merge_body,
        grid=(n // tm,),
        in_specs=[pl.BlockSpec((tm, D_MODEL), row),
                  pl.BlockSpec((tm, gw3), lambda i: (i, OFF_GATE // gw3)),
                  pl.BlockSpec((tm, D_INNER), row),
                  pl.BlockSpec((tm, N_HEADS * HEAD_DIM), row),
                  pl.BlockSpec((tm, MEM_WIDTH), row),
                  pl.BlockSpec((1, gw3), const),
                  pl.BlockSpec((D_INNER, D_MODEL), const),
                  pl.BlockSpec((N_HEADS * HEAD_DIM, D_MODEL), const),
                  pl.BlockSpec((MEM_WIDTH, D_MODEL), const),
                  pl.BlockSpec((D_MODEL, D_MODEL), const),
                  pl.BlockSpec((1, D_MODEL), const),
                  pl.BlockSpec((D_MODEL, LANES), const),
                  pl.BlockSpec((1, LANES), const)],
        out_specs=[pl.BlockSpec((tm, D_MODEL), row), pl.BlockSpec((tm, D_MODEL), row),
                   pl.BlockSpec((tm, LANES), row), pl.BlockSpec((tm, LANES), row)],
        out_shape=[jax.ShapeDtypeStruct((n, D_MODEL), F32), jax.ShapeDtypeStruct((n, D_MODEL), F32),
                   jax.ShapeDtypeStruct((n, LANES), I32), jax.ShapeDtypeStruct((n, LANES), F32)],
        compiler_params=_cparams("parallel"),
        name="merge",
    )(x, zall, ys, oa, om, bg, ws, wa, wm, wo, g2, wr, br)


def _moe_pos_body(te_ref, pos_ref, cnt_ref, carry_ref):
    i = pl.program_id(0)
    tt = te_ref.shape[0]

    @pl.when(i == 0)
    def _():
        carry_ref[...] = jnp.zeros(carry_ref.shape, F32)

    te = te_ref[...]
    lane = lax.broadcasted_iota(I32, (tt, LANES), 1)
    onehot = [lane == te[:, k:k + 1] for k in range(TOP_K)]
    msum = jnp.zeros((tt, LANES), F32)
    for k in range(TOP_K):
        msum = msum + jnp.where(onehot[k], 1.0, 0.0)
    strict = (lax.broadcasted_iota(I32, (tt, tt), 0) > lax.broadcasted_iota(I32, (tt, tt), 1))
    prefix = jnp.dot(jnp.where(strict, 1.0, 0.0).astype(BF16), msum.astype(BF16), preferred_element_type=F32)
    prefix = prefix + carry_ref[0:1, :]
    pos = jnp.zeros((tt, LANES), F32)
    for k in range(TOP_K):
        pk = jnp.sum(jnp.where(onehot[k], prefix, 0.0), axis=1, keepdims=True)
        pos = jnp.where(lane == k, pk, pos)
    pos_ref[...] = pos
    carry_ref[...] = carry_ref[...] + jnp.sum(msum, axis=0, keepdims=True)
    cnt_ref[...] = carry_ref[...]


def _moe_dest_body(te_ref, pos_ref, cnt_ref, dest_ref, be_ref, nu_ref, *, bm):
    tt = te_ref.shape[0]
    cnt = cnt_ref[...]
    padded = jnp.floor((cnt + (bm - 1)) * (1.0 / bm)) * bm
    upper = (lax.broadcasted_iota(I32, (LANES, LANES), 0) < lax.broadcasted_iota(I32, (LANES, LANES), 1))
    pad_start = jnp.dot(padded, jnp.where(upper, 1.0, 0.0), precision=HIGHEST, preferred_element_type=F32)
    pad_end = pad_start + padded
    te = te_ref[...]
    pos = pos_ref[...]
    lane = lax.broadcasted_iota(I32, (tt, LANES), 1)
    dest = jnp.zeros((tt, LANES), F32)
    for k in range(TOP_K):
        ps = jnp.sum(jnp.where(lane == te[:, k:k + 1], pad_start[0:1, :], 0.0), axis=1, keepdims=True)
        dest = jnp.where(lane == k, ps + pos[:, k:k + 1], dest)
    dest_ref[...] = dest.astype(I32)
    nb = be_ref.shape[0]
    bstart = (lax.broadcasted_iota(I32, (nb, LANES), 0) * bm).astype(F32)
    lane_b = lax.broadcasted_iota(I32, (nb, LANES), 1)
    done = jnp.where((pad_end[0:1, :] <= bstart) & (lane_b < N_EXPERTS), 1.0, 0.0)
    be = jnp.minimum(jnp.sum(done, axis=1, keepdims=True), N_EXPERTS - 1.0)
    be_ref[...] = jnp.broadcast_to(be, (nb, LANES)).astype(I32)
    total = jnp.sum(padded[0:1, :], axis=1, keepdims=True)
    nu_ref[...] = jnp.broadcast_to(total * (1.0 / bm), nu_ref.shape).astype(I32)


def _moe_dispatch_body(dest_ref, ha_ref, hb_ref, xs_in_ref, xs_ref, sem, *, tiles_a):
    del xs_in_ref
    tt = ha_ref.shape[0]

    def scatter_rows(h_ref):
        def copy(r, k):
            d = dest_ref[r * TOP_K + k]
            return pltpu.make_async_copy(h_ref.at[pl.ds(r, 1), :], xs_ref.at[pl.ds(d, 1), :], sem)

        def issue(r, carry):
            for k in range(TOP_K):
                copy(r, k).start()
            return carry
        lax.fori_loop(0, tt, issue, 0)

        def drain(r, carry):
            for k in range(TOP_K):
                copy(r, k).wait()
            return carry
        lax.fori_loop(0, tt, drain, 0)

    @pl.when(pl.program_id(0) < tiles_a)
    def _():
        scatter_rows(ha_ref)

    @pl.when(pl.program_id(0) >= tiles_a)
    def _():
        scatter_rows(hb_ref)


def _moe_expert_body(be_ref, nu_ref, xs_ref, w1_ref, b1_ref, w2_ref, b2_ref, o_ref, w1s_ref, w2s_ref):
    i = pl.program_id(0)
    used = i < nu_ref[0]
    e = be_ref[i]
    prev = be_ref[jnp.maximum(i - 1, 0)]
    half = LANES

    @pl.when(used & ((i == 0) | (e != prev)))
    def _():
        r = lax.broadcasted_iota(I32, (2 * half, 2 * half), 0)
        c = lax.broadcasted_iota(I32, (2 * half, 2 * half), 1)
        src_col = jnp.where(c < half, 2 * c, 2 * (c - half) + 1)
        perm = jnp.where(r == src_col, 1.0, 0.0).astype(BF16)
        for j in range(2 * D_FF // (2 * half)):
            sl = slice(j * 2 * half, (j + 1) * 2 * half)
            w1s_ref[:, sl] = jnp.dot(w1_ref[0, :, sl].astype(BF16), perm, preferred_element_type=F32).astype(BF16)
        w2s_ref[...] = w2_ref[0].astype(BF16)

    @pl.when(used)
    def _():
        u = jnp.dot(xs_ref[...].astype(BF16), w1s_ref[...], preferred_element_type=F32) + b1_ref[0]
        acts = []
        for j in range(D_FF // half):
            glu = jnp.minimum(u[:, 2 * j * half:(2 * j + 1) * half], SWIGLU_LIMIT)
            lin = jnp.clip(u[:, (2 * j + 1) * half:(2 * j + 2) * half], -SWIGLU_LIMIT, SWIGLU_LIMIT)
            acts.append((glu * jax.nn.sigmoid(SWIGLU_ALPHA * glu) * (lin + 1.0)).astype(BF16))
        act = jnp.concatenate(acts, axis=1)
        o_ref[...] = jnp.dot(act, w2s_ref[...], preferred_element_type=F32) + b2_ref[0]

    @pl.when(jnp.logical_not(used))
    def _():
        o_ref[...] = jnp.zeros(o_ref.shape, F32)


def _moe_combine_body(dest_ref, gw_ref, x1_ref, gf_ref, os_ref, y_ref, buf_ref, sem):
    tt = x1_ref.shape[0]

    def copy(r, k):
        d = dest_ref[r * TOP_K + k]
        return pltpu.make_async_copy(os_ref.at[pl.ds(d, 1), :], buf_ref.at[k, pl.ds(r, 1), :], sem)

    def issue(r, carry):
        for k in range(TOP_K):
            copy(r, k).start()
        return carry
    lax.fori_loop(0, tt, issue, 0)

    def drain(r, carry):
        for k in range(TOP_K):
            copy(r, k).wait()
        return carry
    lax.fori_loop(0, tt, drain, 0)

    gw = gw_ref[...]
    y = gw[:, 0:1] * buf_ref[0]
    for k in range(1, TOP_K):
        y = y + gw[:, k:k + 1] * buf_ref[k]
    x2 = x1_ref[...] + y
    out = x2 * lax.rsqrt(jnp.mean(x2 * x2, axis=-1, keepdims=True) + EPS)
    y_ref[...] = out * gf_ref[...]


def moe_and_final_norm(x1a, x1b, h2a, h2b, te, gw, w1, b1p, w2, b2, g_final):
    n = te.shape[0]
    tiles_a = x1a.shape[0] // MOE_T
    row_a = lambda i: (jnp.minimum(i, tiles_a - 1), 0)
    row_b = lambda i: (jnp.maximum(i - tiles_a, 0), 0)
    tt = MOE_T
    bm = MOE_BM
    nb = -(-(n * TOP_K + N_EXPERTS * (bm - 1)) // bm)
    nbp = -(-nb // SUBLANES) * SUBLANES
    row = lambda i: (i, 0)
    const = lambda i: (0, 0)

    pos, cnt = pl.pallas_call(
        _moe_pos_body,
        grid=(n // tt,),
        in_specs=[pl.BlockSpec((tt, LANES), row)],
        out_specs=[pl.BlockSpec((tt, LANES), row), pl.BlockSpec((SUBLANES, LANES), const)],
        out_shape=[jax.ShapeDtypeStruct((n, LANES), F32), jax.ShapeDtypeStruct((SUBLANES, LANES), F32)],
        scratch_shapes=[pltpu.VMEM((SUBLANES, LANES), F32)],
        compiler_params=_cparams("arbitrary"),
        name="moe_positions",
    )(te)

    dest, be, nu = pl.pallas_call(
        functools.partial(_moe_dest_body, bm=bm),
        grid=(n // tt,),
        in_specs=[pl.BlockSpec((tt, LANES), row), pl.BlockSpec((tt, LANES), row),
                  pl.BlockSpec((SUBLANES, LANES), const)],
        out_specs=[pl.BlockSpec((tt, LANES), row), pl.BlockSpec((nbp, LANES), const),
                   pl.BlockSpec((SUBLANES, LANES), const)],
        out_shape=[jax.ShapeDtypeStruct((n, LANES), I32), jax.ShapeDtypeStruct((nbp, LANES), I32),
                   jax.ShapeDtypeStruct((SUBLANES, LANES), I32)],
        compiler_params=_cparams("arbitrary"),
        name="moe_destinations",
    )(te, pos, cnt)
    dest_flat = dest[:, :TOP_K].reshape(n * TOP_K)
    block_e = be[:nb, 0]
    n_used = nu[0, 0:1]

    xs = pl.pallas_call(
        functools.partial(_moe_dispatch_body, tiles_a=tiles_a),
        grid=(n // tt,),
        in_specs=[pl.BlockSpec((tt * TOP_K,), lambda i: (i,), memory_space=pltpu.SMEM),
                  pl.BlockSpec((tt, D_MODEL), row_a),
                  pl.BlockSpec((tt, D_MODEL), row_b),
                  pl.BlockSpec(memory_space=pl.ANY)],
        out_specs=pl.BlockSpec(memory_space=pl.ANY),
        out_shape=jax.ShapeDtypeStruct((nb * bm, D_MODEL), F32),
        scratch_shapes=[pltpu.SemaphoreType.DMA(())],
        input_output_aliases={3: 0},
        compiler_params=_cparams("arbitrary"),
        name="moe_dispatch",
    )(dest_flat, h2a, h2b, jnp.zeros((nb * bm, D_MODEL), F32))

    out_sorted = pl.pallas_call(
        _moe_expert_body,
        grid_spec=pltpu.PrefetchScalarGridSpec(
            num_scalar_prefetch=2,
            grid=(nb,),
            in_specs=[pl.BlockSpec((bm, D_MODEL), lambda i, be_, nu_: (i, 0)),
                      pl.BlockSpec((1, D_MODEL, 2 * D_FF), lambda i, be_, nu_: (be_[i], 0, 0)),
                      pl.BlockSpec((1, 1, 2 * D_FF), lambda i, be_, nu_: (be_[i], 0, 0)),
                      pl.BlockSpec((1, D_FF, D_MODEL), lambda i, be_, nu_: (be_[i], 0, 0)),
                      pl.BlockSpec((1, 1, D_MODEL), lambda i, be_, nu_: (be_[i], 0, 0))],
            out_specs=pl.BlockSpec((bm, D_MODEL), lambda i, be_, nu_: (i, 0)),
            scratch_shapes=[pltpu.VMEM((D_MODEL, 2 * D_FF), BF16), pltpu.VMEM((D_FF, D_MODEL), BF16)]),
        out_shape=jax.ShapeDtypeStruct((nb * bm, D_MODEL), F32),
        compiler_params=_cparams("arbitrary"),
        name="moe_experts",
    )(block_e, n_used, xs, w1, b1p, w2, b2)

    def combine(x1_part, tile0):
        return pl.pallas_call(
            _moe_combine_body,
            grid=(x1_part.shape[0] // tt,),
            in_specs=[pl.BlockSpec((tt * TOP_K,), lambda i: (i + tile0,), memory_space=pltpu.SMEM),
                      pl.BlockSpec((tt, LANES), lambda i: (i + tile0, 0)),
                      pl.BlockSpec((tt, D_MODEL), row),
                      pl.BlockSpec((1, D_MODEL), const),
                      pl.BlockSpec(memory_space=pl.ANY)],
            out_specs=pl.BlockSpec((tt, D_MODEL), row),
            out_shape=jax.ShapeDtypeStruct(x1_part.shape, F32),
            scratch_shapes=[pltpu.VMEM((TOP_K, tt, D_MODEL), F32), pltpu.SemaphoreType.DMA(())],
            compiler_params=_cparams("arbitrary"),
            name="moe_combine",
        )(dest_flat, gw, x1_part, g_final.reshape(1, D_MODEL), out_sorted)

    return combine(x1a, 0), combine(x1b, tiles_a)


def _split_cols(w):
    outs, off = [], 0
    for wd in IN_WIDTHS:
        outs.append(w[:, off:off + wd])
        off += wd
    return outs


def _lane_row(v, off):
    return jnp.zeros((1, LANES), F32).at[0, off:off + v.shape[0]].set(v.astype(F32))


def kernel(x_prompt, x_sample, cache_k, cache_v, cache_idx_k, state_conv, state_ssm, cache_mem_k, cache_mem_v,
           page_table, mem_prompt, g_norm1, w_in, b_gate, conv_w, conv_b, dt_bias, a_log, d_skip, g_ssd_norm,
           g_mem, w_mem_kv, w_ssd_out, w_attn_out, w_mem_out, w_out, g_norm2, w_router, b_router, w_exp1,
           b_exp1, w_exp2, b_exp2, g_final):
    assert w_in.shape[0] == 1, "single-layer trunk"
    bp, lp, _ = x_prompt.shape
    bs, ls, _ = x_sample.shape
    np_, ns = bp * lp, bs * ls

    wz, wxbc, wdt, wq, wk, wv, wqi, wki, wwi, wqm, wgate = _split_cols(w_in[0])
    w_all = jnp.concatenate(
        [wxbc, wgate, wz, wq, wqm, wqi, wk, wv, wki, wdt, wwi,
         jnp.zeros((D_MODEL, W_ALL - OFF_SM - SM_WI - IDX_HEADS), F32)], axis=1).astype(BF16)
    dtb_row = _lane_row(dt_bias[0], SM_DT)
    aneg_row = _lane_row(-jnp.exp(a_log[0].astype(F32)), SM_DT)
    dsk_row = jnp.repeat(d_skip[0].astype(F32), SSD_HEAD_DIM).reshape(1, D_INNER)
    gs_row = g_ssd_norm[0].reshape(1, D_INNER)
    cb_row = conv_b[0].reshape(1, CONV_DIM)
    wr_pad = jnp.zeros((D_MODEL, LANES), F32).at[:, :N_EXPERTS].set(w_router[0])
    br_pad = jnp.full((1, LANES), NEG, F32).at[0, :N_EXPERTS].set(b_router[0])
    b1p = b_exp1[0].reshape(N_EXPERTS, D_FF // LANES, LANES, 2).transpose(0, 1, 3, 2).reshape(N_EXPERTS, 1, 2 * D_FF)
    b2 = b_exp2[0].reshape(N_EXPERTS, 1, D_MODEL)

    xp = x_prompt.reshape(np_, D_MODEL)
    xs = x_sample.reshape(ns, D_MODEL)
    zp = norm_matmul(xp, g_norm1[0], w_all, 1024, IN_PROJ_TN)
    zs = norm_matmul(xs, g_norm1[0], w_all, ns, IN_PROJ_TN)

    kv_p = norm_matmul(mem_prompt.reshape(bp * N_MEM, D_MODEL), g_mem[0], w_mem_kv[0].astype(BF16),
                       min(1024, bp * N_MEM), MEM_WIDTH)
    om_p = mem_attn(zp, bp, lp, kv_p, 0, kv_p, 1, 512)
    om_s = mem_attn(zs, bs, ls, cache_mem_k[0].reshape(bs * N_MEM, MEM_WIDTH), 0,
                    cache_mem_v[0].reshape(bs * N_MEM, MEM_WIDTH), 0, ls)

    conv_prev_p = jnp.zeros((bp, SUBLANES, CONV_DIM), F32)
    conv_prev_s = jnp.concatenate(
        [jnp.zeros((bs, SUBLANES - (CONV_WIDTH - 1), CONV_DIM), F32), state_conv[0]], axis=1)
    ssm0_p = jnp.zeros((bp, D_INNER, D_STATE), F32)
    ssm0_s = state_ssm[0].reshape(bs, D_INNER, D_STATE)
    ys_p, ssm_p = ssd(zp, bp, lp, conv_prev_p, ssm0_p, conv_w[0], cb_row, dtb_row, aneg_row, dsk_row, gs_row)
    ys_s, ssm_s = ssd(zs, bs, ls, conv_prev_s, ssm0_s, conv_w[0], cb_row, dtb_row, aneg_row, dsk_row, gs_row)

    oa_p = dsa_prompt(zp, bp, lp)
    oa_s = dsa_sample(zs, bs, ls, cache_k[0], cache_v[0], cache_idx_k[0], page_table)

    mw = (b_gate[0].reshape(1, -1), w_ssd_out[0].astype(BF16), w_attn_out[0].astype(BF16),
          w_mem_out[0].astype(BF16), w_out[0].astype(BF16), g_norm2[0].reshape(1, D_MODEL), wr_pad, br_pad)
    x1_p, h2_p, te_p, gw_p = merge(xp, zp, ys_p, oa_p, om_p, *mw, 512)
    x1_s, h2_s, te_s, gw_s = merge(xs, zs, ys_s, oa_s, om_s, *mw, ns)

    cat = lambda a, b: jnp.concatenate([a, b], axis=0)
    y_all = moe_and_final_norm(x1_p, x1_s, h2_p, h2_s, cat(te_p, te_s), cat(gw_p, gw_s),
                               w_exp1[0], b1p, w_exp2[0], b2, g_final)
    y_prompt = y_all[0].reshape(bp, lp, D_MODEL)
    y_sample = y_all[1].reshape(bs, ls, D_MODEL)

    def kvi(z, b, l):
        k = z[:, OFF_K:OFF_K + N_KV_HEADS * HEAD_DIM].reshape(1, b, l, N_KV_HEADS, HEAD_DIM)
        v = z[:, OFF_V:OFF_V + N_KV_HEADS * HEAD_DIM].reshape(1, b, l, N_KV_HEADS, HEAD_DIM)
        ki = z[:, OFF_SM + SM_KI:OFF_SM + SM_KI + IDX_DIM].reshape(1, b, l, IDX_DIM)
        conv = z.reshape(b, l, W_ALL)[:, l - (CONV_WIDTH - 1):, OFF_XBC:OFF_XBC + CONV_DIM][None]
        return k, v, ki, conv

    k_p, v_p, ki_p, conv_p = kvi(zp, bp, lp)
    k_s, v_s, ki_s, conv_s = kvi(zs, bs, ls)
    mk_p = kv_p[:, :MEM_WIDTH].reshape(1, bp, N_MEM, MEM_HEADS, MEM_HEAD_DIM)
    mv_p = kv_p[:, MEM_WIDTH:].reshape(1, bp, N_MEM, MEM_HEADS, MEM_HEAD_DIM)
    ssm_shape = (1, -1, SSD_HEADS, SSD_HEAD_DIM, D_STATE)
    return (y_prompt, y_sample, k_p, v_p, ki_p, conv_p, ssm_p.reshape(ssm_shape), mk_p, mv_p,
            k_s, v_s, ki_s, conv_s, ssm_s.reshape(ssm_shape))
```

```python
import functools

import numpy as np
import jax
import jax.numpy as jnp
from jax import lax
from jax.experimental import pallas as pl
from jax.experimental.pallas import tpu as pltpu

F32 = jnp.float32
BF16 = jnp.bfloat16
I32 = jnp.int32
HIGHEST = lax.Precision.HIGHEST

D_MODEL = 1024
D_INNER = 2048
SSD_HEAD_DIM = 64
SSD_HEADS = 32
SSD_GROUPS = 4
D_STATE = 128
CONV_WIDTH = 4
CONV_DIM = D_INNER + 2 * SSD_GROUPS * D_STATE
SSD_CHUNK = 128
N_HEADS = 16
N_KV_HEADS = 4
HEAD_DIM = 64
IDX_HEADS = 8
IDX_DIM = 64
TOPK_MAX = 256
N_MEM = 256
MEM_HEADS = 4
MEM_HEAD_DIM = 256
MEM_WIDTH = MEM_HEADS * MEM_HEAD_DIM
N_EXPERTS = 32
TOP_K = 4
D_FF = D_MODEL
SWIGLU_LIMIT = 7.0
SWIGLU_ALPHA = 1.702
N_BRANCH = 3
EPS = 1e-6
PAGE_SIZE = 128
IN_WIDTHS = (D_INNER, CONV_DIM, SSD_HEADS, N_HEADS * HEAD_DIM, N_KV_HEADS * HEAD_DIM, N_KV_HEADS * HEAD_DIM,
             IDX_HEADS * IDX_DIM, IDX_DIM, IDX_HEADS, MEM_WIDTH, N_BRANCH * D_MODEL)

LANES = 128
SUBLANES = 8
VMEM_LIMIT = 56 * 1024 * 1024

OFF_XBC = 0
OFF_GATE = OFF_XBC + CONV_DIM
OFF_Z = OFF_GATE + N_BRANCH * D_MODEL
OFF_Q = OFF_Z + D_INNER
OFF_QM = OFF_Q + N_HEADS * HEAD_DIM
OFF_QI = OFF_QM + MEM_WIDTH
OFF_K = OFF_QI + IDX_HEADS * IDX_DIM
OFF_V = OFF_K + N_KV_HEADS * HEAD_DIM
OFF_SM = OFF_V + N_KV_HEADS * HEAD_DIM
SM_KI = 0
SM_DT = SM_KI + IDX_DIM
SM_WI = SM_DT + SSD_HEADS
IN_PROJ_TN = 1280
W_ALL = OFF_SM + 2 * LANES

NEG = -1e30
INT_MIN = np.int32(-2 ** 31)
INT_MAX = np.int32(2 ** 31 - 1)

ATT_PASSES = 2
MOE_BM = 512
MOE_T = 256


def _cparams(*sem):
    return pltpu.CompilerParams(dimension_semantics=sem, vmem_limit_bytes=VMEM_LIMIT)


def _nt_dot(a, b):
    return lax.dot_general(a, b, (((1,), (1,)), ((), ())), preferred_element_type=F32)


def _float_key(x):
    bits = lax.bitcast_convert_type(x, I32)
    return jnp.where(bits < 0, bits ^ INT_MAX, bits)


def _norm_matmul_body(x_ref, g_ref, w_ref, o_ref, h_ref):
    @pl.when(pl.program_id(1) == 0)
    def _():
        x = x_ref[...]
        h = x * lax.rsqrt(jnp.mean(x * x, axis=-1, keepdims=True) + EPS)
        h_ref[...] = (h * g_ref[...]).astype(BF16)

    o_ref[...] = jnp.dot(h_ref[...], w_ref[...], preferred_element_type=F32)


def norm_matmul(x, g, w, tm, tn):
    n, d = x.shape
    wn = w.shape[1]
    return pl.pallas_call(
        _norm_matmul_body,
        grid=(n // tm, wn // tn),
        in_specs=[pl.BlockSpec((tm, d), lambda i, j: (i, 0)),
                  pl.BlockSpec((1, d), lambda i, j: (0, 0)),
                  pl.BlockSpec((d, tn), lambda i, j: (0, j))],
        out_specs=pl.BlockSpec((tm, tn), lambda i, j: (i, j)),
        out_shape=jax.ShapeDtypeStruct((n, wn), F32),
        scratch_shapes=[pltpu.VMEM((tm, d), BF16)],
        compiler_params=_cparams("parallel", "arbitrary"),
        name="norm_matmul",
    )(x, g.reshape(1, d), w)


def _softplus(x):
    return jnp.maximum(x, 0.0) + jnp.log1p(jnp.exp(-jnp.abs(x)))


def _silu(x):
    return x * jax.nn.sigmoid(x)


def _expand_heads(v, q):
    lane = lax.broadcasted_iota(I32, (q, LANES), 1)
    cols = []
    for t in range(SSD_HEADS // 2):
        c0 = jnp.broadcast_to(v[:, SM_DT + 2 * t:SM_DT + 2 * t + 1], (q, LANES))
        c1 = jnp.broadcast_to(v[:, SM_DT + 2 * t + 1:SM_DT + 2 * t + 2], (q, LANES))
        cols.append(jnp.where(lane < SSD_HEAD_DIM, c0, c1))
    return jnp.concatenate(cols, axis=1)


def _ssd_body(xbc_ref, z_ref, sm_ref, convp_ref, init_ref, cw_ref, cb_ref, dtb_ref, aneg_ref, dsk_ref, gs_ref,
              y_ref, st_ref, xpad_ref, state_ref, *, rows_in, q):
    c = pl.program_id(1)
    pad = SUBLANES

    @pl.when(c == 0)
    def _():
        xpad_ref[0:pad, :] = convp_ref[0]
        state_ref[...] = init_ref[0]

    xpad_ref[pad:pad + rows_in, :] = xbc_ref[...]
    if rows_in < q:
        xpad_ref[pad + rows_in:pad + q, :] = jnp.zeros((q - rows_in, CONV_DIM), F32)

    acc = cb_ref[...]
    for j in range(CONV_WIDTH):
        lo = pad - (CONV_WIDTH - 1) + j
        acc = acc + xpad_ref[lo:lo + q, :] * cw_ref[j:j + 1, :]
    xc = _silu(acc)
    xpad_ref[0:pad, :] = xpad_ref[q:q + pad, :]

    xs = xc[:, :D_INNER]
    gn = SSD_GROUPS * D_STATE
    bm = xc[:, D_INNER:D_INNER + gn].astype(BF16)
    cm = xc[:, D_INNER + gn:].astype(BF16)

    sm = sm_ref[...]
    zz = z_ref[...]
    if rows_in < q:
        sm = jnp.concatenate([sm, jnp.zeros((q - rows_in, LANES), F32)], axis=0)
        zz = jnp.concatenate([zz, jnp.zeros((q - rows_in, D_INNER), F32)], axis=0)
    row = lax.broadcasted_iota(I32, (q, LANES), 0)
    dt = _softplus(sm + dtb_ref[...])
    if rows_in < q:
        dt = jnp.where(row < rows_in, dt, 0.0)
    a = dt * aneg_ref[...]
    tri = (lax.broadcasted_iota(I32, (q, q), 0) >= lax.broadcasted_iota(I32, (q, q), 1)).astype(F32)
    a_cs = jnp.dot(tri, a, precision=HIGHEST, preferred_element_type=F32)
    a_t = a_cs.T
    a_last = a_cs[q - 1:q, :]
    dte = jnp.exp(a_last - a_cs)
    e_in = jnp.exp(a_cs)

    xdt = xs * _expand_heads(dt, q)
    xdt_bf = xdt.astype(BF16)
    xw_bf = (xs * _expand_heads(dt * dte, q)).astype(BF16)
    ein_x = _expand_heads(e_in, q)

    causal = lax.broadcasted_iota(I32, (q, q), 0) >= lax.broadcasted_iota(I32, (q, q), 1)
    lane = lax.broadcasted_iota(I32, (q, LANES), 1)
    hpg = SSD_HEADS // SSD_GROUPS
    gw = hpg * SSD_HEAD_DIM
    y_parts = []
    for g in range(SSD_GROUPS):
        bg = bm[:, g * D_STATE:(g + 1) * D_STATE]
        cg = cm[:, g * D_STATE:(g + 1) * D_STATE]
        cb = _nt_dot(cg, bg)
        m_h = []
        for e in range(hpg):
            h = g * hpg + e
            col = a_cs[:, SM_DT + h:SM_DT + h + 1]
            rw = a_t[SM_DT + h:SM_DT + h + 1, :]
            decay = jnp.exp(jnp.where(causal, col - rw, -jnp.inf))
            m_h.append((cb * decay).astype(BF16))
        yd = []
        for t in range(hpg // 2):
            pair = g * (hpg // 2) + t
            slab = xdt_bf[:, pair * LANES:(pair + 1) * LANES]
            ya = jnp.dot(m_h[2 * t], slab, preferred_element_type=F32)
            yb = jnp.dot(m_h[2 * t + 1], slab, preferred_element_type=F32)
            yd.append(jnp.where(lane < SSD_HEAD_DIM, ya, yb))
        s_old = state_ref[g * gw:(g + 1) * gw, :]
        y_off = _nt_dot(cg, s_old.astype(BF16)) * ein_x[:, g * gw:(g + 1) * gw]
        y_parts.append(jnp.concatenate(yd, axis=1) + y_off)
        new = lax.dot_general(xw_bf[:, g * gw:(g + 1) * gw], bg, (((0,), (0,)), ((), ())),
                              preferred_element_type=F32)
        for e in range(hpg):
            h = g * hpg + e
            dec = jnp.exp(a_t[SM_DT + h:SM_DT + h + 1, q - 1:q])
            lo = e * SSD_HEAD_DIM
            state_ref[h * SSD_HEAD_DIM:(h + 1) * SSD_HEAD_DIM, :] = (
                s_old[lo:lo + SSD_HEAD_DIM, :] * dec + new[lo:lo + SSD_HEAD_DIM, :])

    y = jnp.concatenate(y_parts, axis=1) + dsk_ref[...] * xs
    y = y * _silu(zz)
    outs = []
    for g in range(SSD_GROUPS):
        yg = y[:, g * gw:(g + 1) * gw]
        outs.append(yg * lax.rsqrt(jnp.mean(yg * yg, axis=-1, keepdims=True) + EPS))
    y = jnp.concatenate(outs, axis=1) * gs_ref[...]
    y_ref[...] = y[:rows_in].astype(BF16)

    @pl.when(c == pl.num_programs(1) - 1)
    def _():
        st_ref[0] = state_ref[...]


def ssd(zall, batch, seq, conv_prev8, ssm_init, conv_w, conv_b, dtb_row, aneg_row, dsk_row, gs_row):
    q = SSD_CHUNK
    rows_in = min(seq, q)
    nch = seq // rows_in
    row_map = lambda b, c: (b * nch + c)
    const2 = lambda b, c: (0, 0)
    body = functools.partial(_ssd_body, rows_in=rows_in, q=q)
    return pl.pallas_call(
        body,
        grid=(batch, nch),
        in_specs=[pl.BlockSpec((rows_in, CONV_DIM), lambda b, c: (row_map(b, c), OFF_XBC // CONV_DIM)),
                  pl.BlockSpec((rows_in, D_INNER), lambda b, c: (row_map(b, c), OFF_Z // D_INNER)),
                  pl.BlockSpec((rows_in, LANES), lambda b, c: (row_map(b, c), OFF_SM // LANES)),
                  pl.BlockSpec((1, SUBLANES, CONV_DIM), lambda b, c: (b, 0, 0)),
                  pl.BlockSpec((1, D_INNER, D_STATE), lambda b, c: (b, 0, 0)),
                  pl.BlockSpec((CONV_WIDTH, CONV_DIM), const2),
                  pl.BlockSpec((1, CONV_DIM), const2),
                  pl.BlockSpec((1, LANES), const2),
                  pl.BlockSpec((1, LANES), const2),
                  pl.BlockSpec((1, D_INNER), const2),
                  pl.BlockSpec((1, D_INNER), const2)],
        out_specs=[pl.BlockSpec((rows_in, D_INNER), lambda b, c: (row_map(b, c), 0)),
                   pl.BlockSpec((1, D_INNER, D_STATE), lambda b, c: (b, 0, 0))],
        out_shape=[jax.ShapeDtypeStruct((batch * seq, D_INNER), BF16),
                   jax.ShapeDtypeStruct((batch, D_INNER, D_STATE), F32)],
        scratch_shapes=[pltpu.VMEM((q + 2 * SUBLANES, CONV_DIM), F32),
                        pltpu.VMEM((D_INNER, D_STATE), F32)],
        compiler_params=_cparams("parallel", "arbitrary"),
        name="ssd",
    )(zall, zall, zall, conv_prev8, ssm_init, conv_w, conv_b, dtb_row, aneg_row, dsk_row, gs_row)


def _mem_attn_body(q_ref, k_ref, v_ref, o_ref):
    for h in range(MEM_HEADS):
        sl = slice(h * MEM_HEAD_DIM, (h + 1) * MEM_HEAD_DIM)
        s = _nt_dot(q_ref[:, sl].astype(BF16), k_ref[:, sl].astype(BF16)) * (MEM_HEAD_DIM ** -0.5)
        m = jnp.max(s, axis=-1, keepdims=True)
        p = jnp.exp(s - m)
        p = p / jnp.sum(p, axis=-1, keepdims=True)
        o = jnp.dot(p.astype(BF16), v_ref[:, sl].astype(BF16), preferred_element_type=F32)
        o_ref[:, sl] = o.astype(BF16)


def mem_attn(zall, batch, seq, k_arr, k_col, v_arr, v_col, tm):
    nt = seq // tm
    return pl.pallas_call(
        _mem_attn_body,
        grid=(batch, nt),
        in_specs=[pl.BlockSpec((tm, MEM_WIDTH), lambda b, i: (b * nt + i, OFF_QM // MEM_WIDTH)),
                  pl.BlockSpec((N_MEM, MEM_WIDTH), lambda b, i: (b, k_col)),
                  pl.BlockSpec((N_MEM, MEM_WIDTH), lambda b, i: (b, v_col))],
        out_specs=pl.BlockSpec((tm, MEM_WIDTH), lambda b, i: (b * nt + i, 0)),
        out_shape=jax.ShapeDtypeStruct((batch * seq, MEM_WIDTH), BF16),
        compiler_params=_cparams("parallel", "arbitrary"),
        name="mem_attn",
    )(zall, k_arr, v_arr)


def _kth_largest_key(count_ge, shape, n_sel):
    def bit_body(t, ans):
        cand = ans | jnp.left_shift(jnp.int32(1), 31 - t)
        cnt = count_ge(cand ^ INT_MIN)
        return jnp.where(cnt >= n_sel, cand, ans)

    ans = lax.fori_loop(0, 32, bit_body, jnp.zeros(shape, I32))
    return ans ^ INT_MIN


def _tie_cut(count_eq_below, need, shape, nbits):
    def bit_body(t, lo):
        cand = lo | jnp.left_shift(jnp.int32(1), nbits - 1 - t)
        cnt = count_eq_below(cand)
        return jnp.where(cnt < need, cand, lo)

    return lax.fori_loop(0, nbits, bit_body, jnp.zeros(shape, I32))


def _select_bias(key, kpos, thr, cut, visible):
    sel = (key > thr) | ((key == thr) & (kpos <= cut))
    return jnp.where(sel & visible, 0.0, NEG)


def _dsa_prompt_body(q_ref, qi_ref, smq_ref, k_ref, v_ref, smk_ref, o_ref,
                     kh_ref, vt_ref, kis_ref, qt2_ref, qit_ref, keys_ref, bias_ref, ot_ref, *, tq, kc, seq, n_sel):
    i = pl.program_id(1)

    @pl.when(i == 0)
    def _():
        def cast_rows(r, carry):
            rs = pl.ds(pl.multiple_of(r * kc, kc), kc)
            kk = k_ref[rs, :]
            for h in range(N_KV_HEADS):
                kh_ref[h, rs, :] = kk[:, h * HEAD_DIM:(h + 1) * HEAD_DIM].astype(BF16)
            vt_ref[r] = v_ref[rs, :].T.astype(BF16)
            kis_ref[rs, :] = smk_ref[rs, SM_KI:SM_KI + IDX_DIM].astype(BF16)
            return carry
        lax.fori_loop(0, seq // kc, cast_rows, 0)

    nkc = (i * tq + tq - 1) // kc + 1
    qpos = i * tq + lax.broadcasted_iota(I32, (kc, tq), 1)
    krow = lax.broadcasted_iota(I32, (kc, tq), 0)

    qt = (q_ref[...] * (HEAD_DIM ** -0.5)).T.astype(BF16)
    for h in range(N_HEADS):
        qt2_ref[:, h * tq:(h + 1) * tq] = qt[h * HEAD_DIM:(h + 1) * HEAD_DIM, :]
    qit_ref[...] = (qi_ref[...] * (IDX_DIM ** -0.5)).T.astype(BF16)
    wt = smq_ref[...].T[SM_WI:SM_WI + IDX_HEADS, :] * (IDX_HEADS ** -0.5)

    def score_chunk(c, carry):
        ks = pl.ds(pl.multiple_of(c * kc, kc), kc)
        kic = kis_ref[ks, :]
        sc = jnp.zeros((kc, tq), F32)
        for h in range(IDX_HEADS):
            d = jnp.dot(kic, qit_ref[h * IDX_DIM:(h + 1) * IDX_DIM, :], preferred_element_type=F32)
            sc = sc + jnp.maximum(d, 0.0) * wt[h:h + 1, :]
        key = _float_key(sc + 0.0)
        keys_ref[c] = jnp.where(c * kc + krow <= qpos, key, INT_MIN)
        return carry
    lax.fori_loop(0, nkc, score_chunk, 0)

    def count(pred):
        def body(c, acc):
            return acc + jnp.where(pred(keys_ref[c], c * kc + krow), 1.0, 0.0)
        acc = lax.fori_loop(0, nkc, body, jnp.zeros((kc, tq), F32))
        return jnp.sum(acc, axis=0, keepdims=True)

    vec = (1, tq)
    thr = _kth_largest_key(lambda cand: count(lambda key, kpos: key >= cand), vec, n_sel)
    n_gt = count(lambda key, kpos: key > thr)
    n_eq = count(lambda key, kpos: key == thr)
    need = n_sel - n_gt
    excess = jnp.max(jnp.where((n_eq > need) & (thr != INT_MIN), 1.0, 0.0))
    nbits = max(1, int(seq - 1).bit_length())
    cut = lax.cond(
        excess > 0.0,
        lambda: _tie_cut(lambda cand: count(lambda key, kpos: (key == thr) & (kpos < cand)), need, vec, nbits),
        lambda: jnp.full(vec, INT_MAX, I32))

    def bias_chunk(c, carry):
        kpos = c * kc + krow
        bias_ref[c] = _select_bias(keys_ref[c], kpos, thr, cut, kpos <= qpos)
        return carry
    lax.fori_loop(0, nkc, bias_chunk, 0)

    grp = N_HEADS // N_KV_HEADS
    kv_per_pass = N_KV_HEADS // ATT_PASSES
    hpp = kv_per_pass * grp
    for ps_i in range(ATT_PASSES):
        kv0 = ps_i * kv_per_pass

        def att_chunk(c, carry, kv0=kv0):
            ms, ls, accs = carry
            ks = pl.ds(pl.multiple_of(c * kc, kc), kc)
            bias = bias_ref[c]
            s4 = [jnp.dot(kh_ref[kv0 + j, ks, :], qt2_ref[:, (kv0 + j) * grp * tq:(kv0 + j + 1) * grp * tq],
                          preferred_element_type=F32) for j in range(kv_per_pass)]
            ms_n, ls_n, accs_n = [], [], []
            for h in range(hpp):
                s = s4[h // grp][:, (h % grp) * tq:(h % grp + 1) * tq] + bias
                m_new = jnp.maximum(ms[h], jnp.max(s, axis=0, keepdims=True))
                p = jnp.exp(s - m_new)
                alpha = jnp.exp(ms[h] - m_new)
                ms_n.append(m_new)
                ls_n.append(alpha * ls[h] + jnp.sum(p, axis=0, keepdims=True))
                kh = kv0 + h // grp
                vtc = vt_ref[c, kh * HEAD_DIM:(kh + 1) * HEAD_DIM, :]
                accs_n.append(alpha * accs[h] + jnp.dot(vtc, p.astype(BF16), preferred_element_type=F32))
            return tuple(ms_n), tuple(ls_n), tuple(accs_n)

        init = (tuple(jnp.full(vec, NEG, F32) for _ in range(hpp)),
                tuple(jnp.zeros(vec, F32) for _ in range(hpp)),
                tuple(jnp.zeros((HEAD_DIM, tq), F32) for _ in range(hpp)))
        _, ls, accs = lax.fori_loop(0, nkc, att_chunk, init)
        for h in range(hpp):
            hh = kv0 * grp + h
            ot_ref[hh * HEAD_DIM:(hh + 1) * HEAD_DIM, :] = accs[h] / ls[h]
    o_ref[...] = ot_ref[...].T.astype(BF16)


def dsa_prompt(zall, batch, seq):
    tq = 128
    kc = 256
    nq = seq // tq
    n_sel = min(TOPK_MAX, seq // 4)
    kvw = N_KV_HEADS * HEAD_DIM
    qw = N_HEADS * HEAD_DIM
    qiw = IDX_HEADS * IDX_DIM
    body = functools.partial(_dsa_prompt_body, tq=tq, kc=kc, seq=seq, n_sel=n_sel)
    return pl.pallas_call(
        body,
        grid=(batch, nq),
        in_specs=[pl.BlockSpec((tq, qw), lambda b, i: (b * nq + i, OFF_Q // qw)),
                  pl.BlockSpec((tq, qiw), lambda b, i: (b * nq + i, OFF_QI // qiw)),
                  pl.BlockSpec((tq, LANES), lambda b, i: (b * nq + i, OFF_SM // LANES)),
                  pl.BlockSpec((seq, kvw), lambda b, i: (b, OFF_K // kvw)),
                  pl.BlockSpec((seq, kvw), lambda b, i: (b, OFF_V // kvw)),
                  pl.BlockSpec((seq, LANES), lambda b, i: (b, OFF_SM // LANES))],
        out_specs=pl.BlockSpec((tq, qw), lambda b, i: (b * nq + i, 0)),
        out_shape=jax.ShapeDtypeStruct((batch * seq, qw), BF16),
        scratch_shapes=[pltpu.VMEM((N_KV_HEADS, seq, HEAD_DIM), BF16),
                        pltpu.VMEM((seq // kc, kvw, kc), BF16),
                        pltpu.VMEM((seq, IDX_DIM), BF16),
                        pltpu.VMEM((HEAD_DIM, N_HEADS * tq), BF16),
                        pltpu.VMEM((qiw, tq), BF16),
                        pltpu.VMEM((seq // kc, kc, tq), I32),
                        pltpu.VMEM((seq // kc, kc, tq), F32),
                        pltpu.VMEM((qw, tq), F32)],
        compiler_params=_cparams("parallel", "arbitrary"),
        name="dsa_prompt",
    )(zall, zall, zall, zall, zall, zall)


def _dsa_s_score_body(pt_ref, qs_ref, w_ref, *refs, pg):
    ki_refs, o_ref = refs[:pg], refs[pg]
    qs = qs_ref[0]
    wcol = w_ref[0] * (IDX_HEADS ** -0.5)
    t = qs.shape[0] // IDX_HEADS
    for p in range(pg):
        d = jnp.dot(qs, ki_refs[p][0].astype(BF16), preferred_element_type=F32)
        r = jnp.maximum(d * (IDX_DIM ** -0.5), 0.0) * wcol
        sc = r[0:t, :]
        for h in range(1, IDX_HEADS):
            sc = sc + r[h * t:(h + 1) * t, :]
        o_ref[0, :, p * PAGE_SIZE:(p + 1) * PAGE_SIZE] = sc + 0.0


def _dsa_s_select_body(sc_ref, qs_ref, w_ref, smn_ref, o_ref, *, t, past, n_sel, nbat):
    rows = nbat * t
    sc_new = []
    for b in range(nbat):
        wcol = w_ref[b] * (IDX_HEADS ** -0.5)
        ki_new = smn_ref[b * t:(b + 1) * t, SM_KI:SM_KI + IDX_DIM].astype(BF16)
        ki_new = jnp.concatenate([ki_new, jnp.zeros((LANES - t, IDX_DIM), BF16)], axis=0)
        d = _nt_dot(qs_ref[b], ki_new)
        r = jnp.maximum(d * (IDX_DIM ** -0.5), 0.0) * wcol
        sc = r[0:t, :]
        for h in range(1, IDX_HEADS):
            sc = sc + r[h * t:(h + 1) * t, :]
        sc_new.append(sc)
    sc_new = jnp.concatenate(sc_new, axis=0)
    lane_t = lax.broadcasted_iota(I32, (t, LANES), 1)
    vis_t = lane_t <= lax.broadcasted_iota(I32, (t, LANES), 0)
    vis_n = jnp.concatenate([vis_t] * nbat, axis=0)
    lane_n = lax.broadcasted_iota(I32, (rows, LANES), 1)
    key_n = jnp.where(vis_n, _float_key(sc_new + 0.0), INT_MIN)
    key_p = _float_key(sc_ref[...].reshape(rows, past))
    pos_p = lax.broadcasted_iota(I32, (rows, past), 1)
    pos_n = past + lane_n

    def count(pred):
        return (jnp.sum(jnp.where(pred(key_p, pos_p), 1.0, 0.0), axis=1, keepdims=True)
                + jnp.sum(jnp.where(pred(key_n, pos_n), 1.0, 0.0), axis=1, keepdims=True))

    vec = (rows, 1)
    thr = _kth_largest_key(lambda cand: count(lambda key, kpos: key >= cand), vec, n_sel)
    need = n_sel - count(lambda key, kpos: key > thr)
    n_eq = count(lambda key, kpos: key == thr)
    excess = jnp.max(jnp.where((n_eq > need) & (thr != INT_MIN), 1.0, 0.0))
    nbits = max(1, int(past + t - 1).bit_length())
    cut = lax.cond(
        excess > 0.0,
        lambda: _tie_cut(lambda cand: count(lambda key, kpos: (key == thr) & (kpos < cand)), need, vec, nbits),
        lambda: jnp.full(vec, INT_MAX, I32))
    o_ref[:, :, 0:past] = _select_bias(key_p, pos_p, thr, cut, pos_p >= 0).reshape(nbat, t, past)
    o_ref[:, :, past:past + LANES] = _select_bias(key_n, pos_n, thr, cut, vis_n).reshape(nbat, t, LANES)


def _dsa_s_attn_body(pt_ref, qbd_ref, bias_ref, biasn_ref, kn_ref, vn_ref, *refs, pg, t):
    k_refs, v_refs = refs[:pg], refs[pg:2 * pg]
    o_ref, m_ref, l_ref, acc_ref = refs[2 * pg:]
    j = pl.program_id(1)
    rows = qbd_ref.shape[1]
    rep = rows // t

    @pl.when(j == 0)
    def _():
        m_ref[...] = jnp.full(m_ref.shape, NEG, F32)
        l_ref[...] = jnp.zeros(l_ref.shape, F32)
        acc_ref[...] = jnp.zeros(acc_ref.shape, F32)

    qbd = qbd_ref[0]

    def update(kt, vt, bias):
        s = jnp.dot(qbd, kt, preferred_element_type=F32) * (HEAD_DIM ** -0.5) + jnp.concatenate([bias] * rep, axis=0)
        m = m_ref[...]
        m_new = jnp.maximum(m, jnp.max(s, axis=1, keepdims=True))
        p = jnp.exp(s - m_new)
        alpha = jnp.exp(m - m_new)
        l_ref[...] = alpha * l_ref[...] + jnp.sum(p, axis=1, keepdims=True)
        acc_ref[...] = alpha * acc_ref[...] + _nt_dot(p.astype(BF16), vt)
        m_ref[...] = m_new

    kt = jnp.concatenate([r[0] for r in k_refs], axis=1).astype(BF16)
    vt = jnp.concatenate([r[0] for r in v_refs], axis=1).astype(BF16)
    update(kt, vt, bias_ref[0])

    @pl.when(j == pl.num_programs(1) - 1)
    def _():
        kvw = N_KV_HEADS * HEAD_DIM
        zpad = jnp.zeros((LANES - t, kvw), F32)
        update(jnp.concatenate([kn_ref[...], zpad], axis=0).T.astype(BF16),
               jnp.concatenate([vn_ref[...], zpad], axis=0).T.astype(BF16), biasn_ref[0])
        o_ref[0] = acc_ref[...] / l_ref[...]


def dsa_sample(zs, batch, t, cache_k, cache_v, cache_ki, page_table):
    n_pages = page_table.shape[1]
    past = n_pages * PAGE_SIZE
    n_sel = min(TOPK_MAX, (past + t) // 4)
    pg = 16 if n_pages % 16 == 0 else 8
    nj = n_pages // pg
    n_pool = cache_k.shape[0]
    kvw = N_KV_HEADS * HEAD_DIM
    grp = N_HEADS // N_KV_HEADS

    qi = zs[:, OFF_QI:OFF_QI + IDX_HEADS * IDX_DIM].reshape(batch, t, IDX_HEADS, IDX_DIM)
    qs = jnp.transpose(qi, (0, 2, 1, 3)).reshape(batch, IDX_HEADS * t, IDX_DIM).astype(BF16)
    wi = zs[:, OFF_SM + SM_WI:OFF_SM + SM_WI + IDX_HEADS].reshape(batch, t, IDX_HEADS)
    wcol = jnp.transpose(wi, (0, 2, 1)).reshape(batch, IDX_HEADS * t, 1)

    def page_spec(shape, p):
        return pl.BlockSpec(shape, lambda b, j, pt: (pt[b, j * pg + p],) + (0,) * (len(shape) - 1))

    scores = pl.pallas_call(
        functools.partial(_dsa_s_score_body, pg=pg),
        grid_spec=pltpu.PrefetchScalarGridSpec(
            num_scalar_prefetch=1,
            grid=(batch, nj),
            in_specs=[pl.BlockSpec((1, IDX_HEADS * t, IDX_DIM), lambda b, j, pt: (b, 0, 0)),
                      pl.BlockSpec((1, IDX_HEADS * t, 1), lambda b, j, pt: (b, 0, 0))]
                     + [page_spec((1, IDX_DIM, PAGE_SIZE), p) for p in range(pg)],
            out_specs=pl.BlockSpec((1, t, pg * PAGE_SIZE), lambda b, j, pt: (b, 0, j))),
        out_shape=jax.ShapeDtypeStruct((batch, t, past), F32),
        compiler_params=_cparams("parallel", "arbitrary"),
        name="dsa_sample_scores",
    )(page_table, qs, wcol, *([jnp.swapaxes(cache_ki, 1, 2)] * pg))

    nbat = 4 if (batch % 4 == 0 and t % SUBLANES == 0) else 1
    bias = pl.pallas_call(
        functools.partial(_dsa_s_select_body, t=t, past=past, n_sel=n_sel, nbat=nbat),
        grid=(batch // nbat,),
        in_specs=[pl.BlockSpec((nbat, t, past), lambda b: (b, 0, 0)),
                  pl.BlockSpec((nbat, IDX_HEADS * t, IDX_DIM), lambda b: (b, 0, 0)),
                  pl.BlockSpec((nbat, IDX_HEADS * t, 1), lambda b: (b, 0, 0)),
                  pl.BlockSpec((nbat * t, LANES), lambda b: (b, OFF_SM // LANES))],
        out_specs=pl.BlockSpec((nbat, t, past + LANES), lambda b: (b, 0, 0)),
        out_shape=jax.ShapeDtypeStruct((batch, t, past + LANES), F32),
        compiler_params=_cparams("parallel"),
        name="dsa_sample_select",
    )(scores, qs, wcol, zs)

    q = zs[:, OFF_Q:OFF_Q + N_HEADS * HEAD_DIM].reshape(batch, t, N_KV_HEADS, grp, HEAD_DIM)
    q = jnp.transpose(q, (0, 2, 3, 1, 4))
    eye = jnp.eye(N_KV_HEADS, dtype=F32)
    qbd = (q[:, :, :, :, None, :] * eye[None, :, None, None, :, None]).reshape(batch, N_HEADS * t, kvw).astype(BF16)

    ck = jnp.transpose(cache_k, (0, 2, 3, 1)).reshape(n_pool, kvw, PAGE_SIZE)
    cv = jnp.transpose(cache_v, (0, 2, 3, 1)).reshape(n_pool, kvw, PAGE_SIZE)
    rows = N_HEADS * t
    out = pl.pallas_call(
        functools.partial(_dsa_s_attn_body, pg=pg, t=t),
        grid_spec=pltpu.PrefetchScalarGridSpec(
            num_scalar_prefetch=1,
            grid=(batch, nj),
            in_specs=[pl.BlockSpec((1, rows, kvw), lambda b, j, pt: (b, 0, 0)),
                      pl.BlockSpec((1, t, pg * PAGE_SIZE), lambda b, j, pt: (b, 0, j)),
                      pl.BlockSpec((1, t, LANES), lambda b, j, pt: (b, 0, past // LANES)),
                      pl.BlockSpec((t, kvw), lambda b, j, pt: (b, OFF_K // kvw)),
                      pl.BlockSpec((t, kvw), lambda b, j, pt: (b, OFF_V // kvw))]
                     + [page_spec((1, kvw, PAGE_SIZE), p) for p in range(pg)]
                     + [page_spec((1, kvw, PAGE_SIZE), p) for p in range(pg)],
            out_specs=pl.BlockSpec((1, rows, kvw), lambda b, j, pt: (b, 0, 0)),
            scratch_shapes=[pltpu.VMEM((rows, 1), F32), pltpu.VMEM((rows, 1), F32), pltpu.VMEM((rows, kvw), F32)]),
        out_shape=jax.ShapeDtypeStruct((batch, rows, kvw), F32),
        compiler_params=_cparams("parallel", "arbitrary"),
        name="dsa_sample_attn",
    )(page_table, qbd, bias, bias, zs, zs, *([ck] * pg), *([cv] * pg))

    o = out.reshape(batch, N_KV_HEADS, grp, t, N_KV_HEADS, HEAD_DIM)
    o = jnp.stack([o[:, kh, :, :, kh, :] for kh in range(N_KV_HEADS)], axis=1)
    return jnp.transpose(o, (0, 3, 1, 2, 4)).reshape(batch * t, N_HEADS * HEAD_DIM).astype(BF16)


def _merge_body(x_ref, gate_ref, ys_ref, oa_ref, om_ref, bg_ref, ws_ref, wa_ref, wm_ref, wo_ref, g2_ref, wr_ref, br_ref,
                x1_ref, h2_ref, te_ref, gw_ref):
    gates = jax.nn.sigmoid(gate_ref[...] + bg_ref[...])
    merged = (gates[:, 0:D_MODEL] * jnp.dot(ys_ref[...], ws_ref[...], preferred_element_type=F32)
              + gates[:, D_MODEL:2 * D_MODEL] * jnp.dot(oa_ref[...], wa_ref[...], preferred_element_type=F32)
              + gates[:, 2 * D_MODEL:] * jnp.dot(om_ref[...], wm_ref[...], preferred_element_type=F32))
    x1 = x_ref[...] + jnp.dot(merged.astype(BF16), wo_ref[...], preferred_element_type=F32)
    x1_ref[...] = x1
    h2 = x1 * lax.rsqrt(jnp.mean(x1 * x1, axis=-1, keepdims=True) + EPS)
    h2 = h2 * g2_ref[...]
    h2_ref[...] = h2
    logits = jnp.dot(h2, wr_ref[...], precision=HIGHEST, preferred_element_type=F32) + br_ref[...]
    lane = lax.broadcasted_iota(I32, logits.shape, 1)
    te = jnp.zeros(logits.shape, I32)
    tv = []
    for k in range(TOP_K):
        m = jnp.max(logits, axis=1, keepdims=True)
        idx = jnp.min(jnp.where(logits == m, lane, LANES), axis=1, keepdims=True)
        te = jnp.where(lane == k, idx, te)
        tv.append(m)
        logits = jnp.where(lane == idx, -jnp.inf, logits)
    ex = [jnp.exp(v - tv[0]) for v in tv]
    den = ex[0] + ex[1] + ex[2] + ex[3]
    gw = jnp.zeros(logits.shape, F32)
    for k in range(TOP_K):
        gw = jnp.where(lane == k, ex[k] / den, gw)
    te_ref[...] = te
    gw_ref[...] = gw


def merge(x, zall, ys, oa, om, bg, ws, wa, wm, wo, g2, wr, br, tm):
    n = x.shape[0]
    gw3 = N_BRANCH * D_MODEL
    row = lambda i: (i, 0)
    const = lambda i: (0, 0)
    return pl.pallas_call(
        _merge_body,
        grid=(n // tm,),
        in_specs=[pl.BlockSpec((tm, D_MODEL), row),
                  pl.BlockSpec((tm, gw3), lambda i: (i, OFF_GATE // gw3)),
                  pl.BlockSpec((tm, D_INNER), row),
                  pl.BlockSpec((tm, N_HEADS * HEAD_DIM), row),
                  pl.BlockSpec((tm, MEM_WIDTH), row),
                  pl.BlockSpec((1, gw3), const),
                  pl.BlockSpec((D_INNER, D_MODEL), const),
                  pl.BlockSpec((N_HEADS * HEAD_DIM, D_MODEL), const),
                  pl.BlockSpec((MEM_WIDTH, D_MODEL), const),
                  pl.BlockSpec((D_MODEL, D_MODEL), const),
                  pl.BlockSpec((1, D_MODEL), const),
                  pl.BlockSpec((D_MODEL, LANES), const),
                  pl.BlockSpec((1, LANES), const)],
        out_specs=[pl.BlockSpec((tm, D_MODEL), row), pl.BlockSpec((tm, D_MODEL), row),
                   pl.BlockSpec((tm, LANES), row), pl.BlockSpec((tm, LANES), row)],
        out_shape=[jax.ShapeDtypeStruct((n, D_MODEL), F32), jax.ShapeDtypeStruct((n, D_MODEL), F32),
                   jax.ShapeDtypeStruct((n, LANES), I32), jax.ShapeDtypeStruct((n, LANES), F32)],
        compiler_params=_cparams("parallel"),
        name="merge",
    )(x, zall, ys, oa, om, bg, ws, wa, wm, wo, g2, wr, br)


def _moe_pos_body(te_ref, pos_ref, cnt_ref, carry_ref):
    i = pl.program_id(0)
    tt = te_ref.shape[0]

    @pl.when(i == 0)
    def _():
        carry_ref[...] = jnp.zeros(carry_ref.shape, F32)

    te = te_ref[...]
    lane = lax.broadcasted_iota(I32, (tt, LANES), 1)
    onehot = [lane == te[:, k:k + 1] for k in range(TOP_K)]
    msum = jnp.zeros((tt, LANES), F32)
    for k in range(TOP_K):
        msum = msum + jnp.where(onehot[k], 1.0, 0.0)
    strict = (lax.broadcasted_iota(I32, (tt, tt), 0) > lax.broadcasted_iota(I32, (tt, tt), 1))
    prefix = jnp.dot(jnp.where(strict, 1.0, 0.0).astype(BF16), msum.astype(BF16), preferred_element_type=F32)
    prefix = prefix + carry_ref[0:1, :]
    pos = jnp.zeros((tt, LANES), F32)
    for k in range(TOP_K):
        pk = jnp.sum(jnp.where(onehot[k], prefix, 0.0), axis=1, keepdims=True)
        pos = jnp.where(lane == k, pk, pos)
    pos_ref[...] = pos
    carry_ref[...] = carry_ref[...] + jnp.sum(msum, axis=0, keepdims=True)
    cnt_ref[...] = carry_ref[...]


def _moe_dest_body(te_ref, pos_ref, cnt_ref, dest_ref, be_ref, nu_ref, *, bm):
    tt = te_ref.shape[0]
    cnt = cnt_ref[...]
    padded = jnp.floor((cnt + (bm - 1)) * (1.0 / bm)) * bm
    upper = (lax.broadcasted_iota(I32, (LANES, LANES), 0) < lax.broadcasted_iota(I32, (LANES, LANES), 1))
    pad_start = jnp.dot(padded, jnp.where(upper, 1.0, 0.0), precision=HIGHEST, preferred_element_type=F32)
    pad_end = pad_start + padded
    te = te_ref[...]
    pos = pos_ref[...]
    lane = lax.broadcasted_iota(I32, (tt, LANES), 1)
    dest = jnp.zeros((tt, LANES), F32)
    for k in range(TOP_K):
        ps = jnp.sum(jnp.where(lane == te[:, k:k + 1], pad_start[0:1, :], 0.0), axis=1, keepdims=True)
        dest = jnp.where(lane == k, ps + pos[:, k:k + 1], dest)
    dest_ref[...] = dest.astype(I32)
    nb = be_ref.shape[0]
    bstart = (lax.broadcasted_iota(I32, (nb, LANES), 0) * bm).astype(F32)
    lane_b = lax.broadcasted_iota(I32, (nb, LANES), 1)
    done = jnp.where((pad_end[0:1, :] <= bstart) & (lane_b < N_EXPERTS), 1.0, 0.0)
    be = jnp.minimum(jnp.sum(done, axis=1, keepdims=True), N_EXPERTS - 1.0)
    be_ref[...] = jnp.broadcast_to(be, (nb, LANES)).astype(I32)
    total = jnp.sum(padded[0:1, :], axis=1, keepdims=True)
    nu_ref[...] = jnp.broadcast_to(total * (1.0 / bm), nu_ref.shape).astype(I32)


def _moe_dispatch_body(dest_ref, ha_ref, hb_ref, xs_in_ref, xs_ref, sem, *, tiles_a):
    del xs_in_ref
    tt = ha_ref.shape[0]

    def scatter_rows(h_ref):
        def copy(r, k):
            d = dest_ref[r * TOP_K + k]
            return pltpu.make_async_copy(h_ref.at[pl.ds(r, 1), :], xs_ref.at[pl.ds(d, 1), :], sem)

        def issue(r, carry):
            for k in range(TOP_K):
                copy(r, k).start()
            return carry
        lax.fori_loop(0, tt, issue, 0)

        def drain(r, carry):
            for k in range(TOP_K):
                copy(r, k).wait()
            return carry
        lax.fori_loop(0, tt, drain, 0)

    @pl.when(pl.program_id(0) < tiles_a)
    def _():
        scatter_rows(ha_ref)

    @pl.when(pl.program_id(0) >= tiles_a)
    def _():
        scatter_rows(hb_ref)


def _moe_expert_body(be_ref, nu_ref, xs_ref, w1_ref, b1_ref, w2_ref, b2_ref, o_ref, w1s_ref, w2s_ref):
    i = pl.program_id(0)
    used = i < nu_ref[0]
    e = be_ref[i]
    prev = be_ref[jnp.maximum(i - 1, 0)]
    half = LANES

    @pl.when(used & ((i == 0) | (e != prev)))
    def _():
        r = lax.broadcasted_iota(I32, (2 * half, 2 * half), 0)
        c = lax.broadcasted_iota(I32, (2 * half, 2 * half), 1)
        src_col = jnp.where(c < half, 2 * c, 2 * (c - half) + 1)
        perm = jnp.where(r == src_col, 1.0, 0.0).astype(BF16)
        for j in range(2 * D_FF // (2 * half)):
            sl = slice(j * 2 * half, (j + 1) * 2 * half)
            w1s_ref[:, sl] = jnp.dot(w1_ref[0, :, sl].astype(BF16), perm, preferred_element_type=F32).astype(BF16)
        w2s_ref[...] = w2_ref[0].astype(BF16)

    @pl.when(used)
    def _():
        u = jnp.dot(xs_ref[...].astype(BF16), w1s_ref[...], preferred_element_type=F32) + b1_ref[0]
        acts = []
        for j in range(D_FF // half):
            glu = jnp.minimum(u[:, 2 * j * half:(2 * j + 1) * half], SWIGLU_LIMIT)
            lin = jnp.clip(u[:, (2 * j + 1) * half:(2 * j + 2) * half], -SWIGLU_LIMIT, SWIGLU_LIMIT)
            acts.append((glu * jax.nn.sigmoid(SWIGLU_ALPHA * glu) * (lin + 1.0)).astype(BF16))
        act = jnp.concatenate(acts, axis=1)
        o_ref[...] = jnp.dot(act, w2s_ref[...], preferred_element_type=F32) + b2_ref[0]

    @pl.when(jnp.logical_not(used))
    def _():
        o_ref[...] = jnp.zeros(o_ref.shape, F32)


def _moe_combine_body(dest_ref, gw_ref, x1_ref, gf_ref, os_ref, y_ref, buf_ref, sem):
    tt = x1_ref.shape[0]

    def copy(r, k):
        d = dest_ref[r * TOP_K + k]
        return pltpu.make_async_copy(os_ref.at[pl.ds(d, 1), :], buf_ref.at[k, pl.ds(r, 1), :], sem)

    def issue(r, carry):
        for k in range(TOP_K):
            copy(r, k).start()
        return carry
    lax.fori_loop(0, tt, issue, 0)

    def drain(r, carry):
        for k in range(TOP_K):
            copy(r, k).wait()
        return carry
    lax.fori_loop(0, tt, drain, 0)

    gw = gw_ref[...]
    y = gw[:, 0:1] * buf_ref[0]
    for k in range(1, TOP_K):
        y = y + gw[:, k:k + 1] * buf_ref[k]
    x2 = x1_ref[...] + y
    out = x2 * lax.rsqrt(jnp.mean(x2 * x2, axis=-1, keepdims=True) + EPS)
    y_ref[...] = out * gf_ref[...]


def moe_and_final_norm(x1a, x1b, h2a, h2b, te, gw, w1, b1p, w2, b2, g_final):
    n = te.shape[0]
    tiles_a = x1a.shape[0] // MOE_T
    row_a = lambda i: (jnp.minimum(i, tiles_a - 1), 0)
    row_b = lambda i: (jnp.maximum(i - tiles_a, 0), 0)
    tt = MOE_T
    bm = MOE_BM
    nb = -(-(n * TOP_K + N_EXPERTS * (bm - 1)) // bm)
    nbp = -(-nb // SUBLANES) * SUBLANES
    row = lambda i: (i, 0)
    const = lambda i: (0, 0)

    pos, cnt = pl.pallas_call(
        _moe_pos_body,
        grid=(n // tt,),
        in_specs=[pl.BlockSpec((tt, LANES), row)],
        out_specs=[pl.BlockSpec((tt, LANES), row), pl.BlockSpec((SUBLANES, LANES), const)],
        out_shape=[jax.ShapeDtypeStruct((n, LANES), F32), jax.ShapeDtypeStruct((SUBLANES, LANES), F32)],
        scratch_shapes=[pltpu.VMEM((SUBLANES, LANES), F32)],
        compiler_params=_cparams("arbitrary"),
        name="moe_positions",
    )(te)

    dest, be, nu = pl.pallas_call(
        functools.partial(_moe_dest_body, bm=bm),
        grid=(n // tt,),
        in_specs=[pl.BlockSpec((tt, LANES), row), pl.BlockSpec((tt, LANES), row),
                  pl.BlockSpec((SUBLANES, LANES), const)],
        out_specs=[pl.BlockSpec((tt, LANES), row), pl.BlockSpec((nbp, LANES), const),
                   pl.BlockSpec((SUBLANES, LANES), const)],
        out_shape=[jax.ShapeDtypeStruct((n, LANES), I32), jax.ShapeDtypeStruct((nbp, LANES), I32),
                   jax.ShapeDtypeStruct((SUBLANES, LANES), I32)],
        compiler_params=_cparams("arbitrary"),
        name="moe_destinations",
    )(te, pos, cnt)
    dest_flat = dest[:, :TOP_K].reshape(n * TOP_K)
    block_e = be[:nb, 0]
    n_used = nu[0, 0:1]

    xs = pl.pallas_call(
        functools.partial(_moe_dispatch_body, tiles_a=tiles_a),
        grid=(n // tt,),
        in_specs=[pl.BlockSpec((tt * TOP_K,), lambda i: (i,), memory_space=pltpu.SMEM),
                  pl.BlockSpec((tt, D_MODEL), row_a),
                  pl.BlockSpec((tt, D_MODEL), row_b),
                  pl.BlockSpec(memory_space=pl.ANY)],
        out_specs=pl.BlockSpec(memory_space=pl.ANY),
        out_shape=jax.ShapeDtypeStruct((nb * bm, D_MODEL), F32),
        scratch_shapes=[pltpu.SemaphoreType.DMA(())],
        input_output_aliases={3: 0},
        compiler_params=_cparams("arbitrary"),
        name="moe_dispatch",
    )(dest_flat, h2a, h2b, jnp.zeros((nb * bm, D_MODEL), F32))

    out_sorted = pl.pallas_call(
        _moe_expert_body,
        grid_spec=pltpu.PrefetchScalarGridSpec(
            num_scalar_prefetch=2,
            grid=(nb,),
            in_specs=[pl.BlockSpec((bm, D_MODEL), lambda i, be_, nu_: (i, 0)),
                      pl.BlockSpec((1, D_MODEL, 2 * D_FF), lambda i, be_, nu_: (be_[i], 0, 0)),
                      pl.BlockSpec((1, 1, 2 * D_FF), lambda i, be_, nu_: (be_[i], 0, 0)),
                      pl.BlockSpec((1, D_FF, D_MODEL), lambda i, be_, nu_: (be_[i], 0, 0)),
                      pl.BlockSpec((1, 1, D_MODEL), lambda i, be_, nu_: (be_[i], 0, 0))],
            out_specs=pl.BlockSpec((bm, D_MODEL), lambda i, be_, nu_: (i, 0)),
            scratch_shapes=[pltpu.VMEM((D_MODEL, 2 * D_FF), BF16), pltpu.VMEM((D_FF, D_MODEL), BF16)]),
        out_shape=jax.ShapeDtypeStruct((nb * bm, D_MODEL), F32),
        compiler_params=_cparams("arbitrary"),
        name="moe_experts",
    )(block_e, n_used, xs, w1, b1p, w2, b2)

    def combine(x1_part, tile0):
        return pl.pallas_call(
            _moe_combine_body,
            grid=(x1_part.shape[0] // tt,),
            in_specs=[pl.BlockSpec((tt * TOP_K,), lambda i: (i + tile0,), memory_space=pltpu.SMEM),
                      pl.BlockSpec((tt, LANES), lambda i: (i + tile0, 0)),
                      pl.BlockSpec((tt, D_MODEL), row),
                      pl.BlockSpec((1, D_MODEL), const),
                      pl.BlockSpec(memory_space=pl.ANY)],
            out_specs=pl.BlockSpec((tt, D_MODEL), row),
            out_shape=jax.ShapeDtypeStruct(x1_part.shape, F32),
            scratch_shapes=[pltpu.VMEM((TOP_K, tt, D_MODEL), F32), pltpu.SemaphoreType.DMA(())],
            compiler_params=_cparams("arbitrary"),
            name="moe_combine",
        )(dest_flat, gw, x1_part, g_final.reshape(1, D_MODEL), out_sorted)

    return combine(x1a, 0), combine(x1b, tiles_a)


def _split_cols(w):
    outs, off = [], 0
    for wd in IN_WIDTHS:
        outs.append(w[:, off:off + wd])
        off += wd
    return outs


def _lane_row(v, off):
    return jnp.zeros((1, LANES), F32).at[0, off:off + v.shape[0]].set(v.astype(F32))


def kernel(x_prompt, x_sample, cache_k, cache_v, cache_idx_k, state_conv, state_ssm, cache_mem_k, cache_mem_v,
           page_table, mem_prompt, g_norm1, w_in, b_gate, conv_w, conv_b, dt_bias, a_log, d_skip, g_ssd_norm,
           g_mem, w_mem_kv, w_ssd_out, w_attn_out, w_mem_out, w_out, g_norm2, w_router, b_router, w_exp1,
           b_exp1, w_exp2, b_exp2, g_final):
    assert w_in.shape[0] == 1, "single-layer trunk"
    bp, lp, _ = x_prompt.shape
    bs, ls, _ = x_sample.shape
    np_, ns = bp * lp, bs * ls

    wz, wxbc, wdt, wq, wk, wv, wqi, wki, wwi, wqm, wgate = _split_cols(w_in[0])
    w_all = jnp.concatenate(
        [wxbc, wgate, wz, wq, wqm, wqi, wk, wv, wki, wdt, wwi,
         jnp.zeros((D_MODEL, W_ALL - OFF_SM - SM_WI - IDX_HEADS), F32)], axis=1).astype(BF16)
    dtb_row = _lane_row(dt_bias[0], SM_DT)
    aneg_row = _lane_row(-jnp.exp(a_log[0].astype(F32)), SM_DT)
    dsk_row = jnp.repeat(d_skip[0].astype(F32), SSD_HEAD_DIM).reshape(1, D_INNER)
    gs_row = g_ssd_norm[0].reshape(1, D_INNER)
    cb_row = conv_b[0].reshape(1, CONV_DIM)
    wr_pad = jnp.zeros((D_MODEL, LANES), F32).at[:, :N_EXPERTS].set(w_router[0])
    br_pad = jnp.full((1, LANES), NEG, F32).at[0, :N_EXPERTS].set(b_router[0])
    b1p = b_exp1[0].reshape(N_EXPERTS, D_FF // LANES, LANES, 2).transpose(0, 1, 3, 2).reshape(N_EXPERTS, 1, 2 * D_FF)
    b2 = b_exp2[0].reshape(N_EXPERTS, 1, D_MODEL)

    xp = x_prompt.reshape(np_, D_MODEL)
    xs = x_sample.reshape(ns, D_MODEL)
    zp = norm_matmul(xp, g_norm1[0], w_all, 1024, IN_PROJ_TN)
    zs = norm_matmul(xs, g_norm1[0], w_all, ns, IN_PROJ_TN)

    kv_p = norm_matmul(mem_prompt.reshape(bp * N_MEM, D_MODEL), g_mem[0], w_mem_kv[0].astype(BF16),
                       min(1024, bp * N_MEM), MEM_WIDTH)
    om_p = mem_attn(zp, bp, lp, kv_p, 0, kv_p, 1, 512)
    om_s = mem_attn(zs, bs, ls, cache_mem_k[0].reshape(bs * N_MEM, MEM_WIDTH), 0,
                    cache_mem_v[0].reshape(bs * N_MEM, MEM_WIDTH), 0, ls)

    conv_prev_p = jnp.zeros((bp, SUBLANES, CONV_DIM), F32)
    conv_prev_s = jnp.concatenate(
        [jnp.zeros((bs, SUBLANES - (CONV_WIDTH - 1), CONV_DIM), F32), state_conv[0]], axis=1)
    ssm0_p = jnp.zeros((bp, D_INNER, D_STATE), F32)
    ssm0_s = state_ssm[0].reshape(bs, D_INNER, D_STATE)
    ys_p, ssm_p = ssd(zp, bp, lp, conv_prev_p, ssm0_p, conv_w[0], cb_row, dtb_row, aneg_row, dsk_row, gs_row)
    ys_s, ssm_s = ssd(zs, bs, ls, conv_prev_s, ssm0_s, conv_w[0], cb_row, dtb_row, aneg_row, dsk_row, gs_row)

    oa_p = dsa_prompt(zp, bp, lp)
    oa_s = dsa_sample(zs, bs, ls, cache_k[0], cache_v[0], cache_idx_k[0], page_table)

    mw = (b_gate[0].reshape(1, -1), w_ssd_out[0].astype(BF16), w_attn_out[0].astype(BF16),
          w_mem_out[0].astype(BF16), w_out[0].astype(BF16), g_norm2[0].reshape(1, D_MODEL), wr_pad, br_pad)
    x1_p, h2_p, te_p, gw_p = merge(xp, zp, ys_p, oa_p, om_p, *mw, 512)
    x1_s, h2_s, te_s, gw_s = merge(xs, zs, ys_s, oa_s, om_s, *mw, ns)

    cat = lambda a, b: jnp.concatenate([a, b], axis=0)
    y_all = moe_and_final_norm(x1_p, x1_s, h2_p, h2_s, cat(te_p, te_s), cat(gw_p, gw_s),
                               w_exp1[0], b1p, w_exp2[0], b2, g_final)
    y_prompt = y_all[0].reshape(bp, lp, D_MODEL)
    y_sample = y_all[1].reshape(bs, ls, D_MODEL)

    def kvi(z, b, l):
        k = z[:, OFF_K:OFF_K + N_KV_HEADS * HEAD_DIM].reshape(1, b, l, N_KV_HEADS, HEAD_DIM)
        v = z[:, OFF_V:OFF_V + N_KV_HEADS * HEAD_DIM].reshape(1, b, l, N_KV_HEADS, HEAD_DIM)
        ki = z[:, OFF_SM + SM_KI:OFF_SM + SM_KI + IDX_DIM].reshape(1, b, l, IDX_DIM)
        conv = z.reshape(b, l, W_ALL)[:, l - (CONV_WIDTH - 1):, OFF_XBC:OFF_XBC + CONV_DIM][None]
        return k, v, ki, conv

    k_p, v_p, ki_p, conv_p = kvi(zp, bp, lp)
    k_s, v_s, ki_s, conv_s = kvi(zs, bs, ls)
    mk_p = kv_p[:, :MEM_WIDTH].reshape(1, bp, N_MEM, MEM_HEADS, MEM_HEAD_DIM)
    mv_p = kv_p[:, MEM_WIDTH:].reshape(1, bp, N_MEM, MEM_HEADS, MEM_HEAD_DIM)
    ssm_shape = (1, -1, SSD_HEADS, SSD_HEAD_DIM, D_STATE)
    return (y_prompt, y_sample, k_p, v_p, ki_p, conv_p, ssm_p.reshape(ssm_shape), mk_p, mv_p,
            k_s, v_s, ki_s, conv_s, ssm_s.reshape(ssm_shape))
```

```python
import functools

import numpy as np
import jax
import jax.numpy as jnp
from jax import lax
from jax.experimental import pallas as pl
from jax.experimental.pallas import tpu as pltpu

F32 = jnp.float32
BF16 = jnp.bfloat16
I32 = jnp.int32
HIGHEST = lax.Precision.HIGHEST

D_MODEL = 1024
D_INNER = 2048
SSD_HEAD_DIM = 64
SSD_HEADS = 32
SSD_GROUPS = 4
D_STATE = 128
CONV_WIDTH = 4
CONV_DIM = D_INNER + 2 * SSD_GROUPS * D_STATE
SSD_CHUNK = 128
N_HEADS = 16
N_KV_HEADS = 4
HEAD_DIM = 64
IDX_HEADS = 8
IDX_DIM = 64
TOPK_MAX = 256
N_MEM = 256
MEM_HEADS = 4
MEM_HEAD_DIM = 256
MEM_WIDTH = MEM_HEADS * MEM_HEAD_DIM
N_EXPERTS = 32
TOP_K = 4
D_FF = D_MODEL
SWIGLU_LIMIT = 7.0
SWIGLU_ALPHA = 1.702
N_BRANCH = 3
EPS = 1e-6
PAGE_SIZE = 128
IN_WIDTHS = (D_INNER, CONV_DIM, SSD_HEADS, N_HEADS * HEAD_DIM, N_KV_HEADS * HEAD_DIM, N_KV_HEADS * HEAD_DIM,
             IDX_HEADS * IDX_DIM, IDX_DIM, IDX_HEADS, MEM_WIDTH, N_BRANCH * D_MODEL)

LANES = 128
SUBLANES = 8
VMEM_LIMIT = 56 * 1024 * 1024

OFF_XBC = 0
OFF_GATE = OFF_XBC + CONV_DIM
OFF_Z = OFF_GATE + N_BRANCH * D_MODEL
OFF_Q = OFF_Z + D_INNER
OFF_QM = OFF_Q + N_HEADS * HEAD_DIM
OFF_QI = OFF_QM + MEM_WIDTH
OFF_K = OFF_QI + IDX_HEADS * IDX_DIM
OFF_V = OFF_K + N_KV_HEADS * HEAD_DIM
OFF_SM = OFF_V + N_KV_HEADS * HEAD_DIM
SM_KI = 0
SM_DT = SM_KI + IDX_DIM
SM_WI = SM_DT + SSD_HEADS
IN_PROJ_TN = 1280
W_ALL = OFF_SM + 2 * LANES

NEG = -1e30
INT_MIN = np.int32(-2 ** 31)
INT_MAX = np.int32(2 ** 31 - 1)

ATT_PASSES = 1
MOE_BM = 512
MOE_T = 256


def _cparams(*sem):
    return pltpu.CompilerParams(dimension_semantics=sem, vmem_limit_bytes=VMEM_LIMIT)


def _nt_dot(a, b):
    return lax.dot_general(a, b, (((1,), (1,)), ((), ())), preferred_element_type=F32)


def _float_key(x):
    bits = lax.bitcast_convert_type(x, I32)
    return jnp.where(bits < 0, bits ^ INT_MAX, bits)


def _norm_matmul_body(x_ref, g_ref, w_ref, o_ref, h_ref):
    @pl.when(pl.program_id(1) == 0)
    def _():
        x = x_ref[...]
        h = x * lax.rsqrt(jnp.mean(x * x, axis=-1, keepdims=True) + EPS)
        h_ref[...] = (h * g_ref[...]).astype(BF16)

    o_ref[...] = jnp.dot(h_ref[...], w_ref[...], preferred_element_type=F32)


def norm_matmul(x, g, w, tm, tn):
    n, d = x.shape
    wn = w.shape[1]
    return pl.pallas_call(
        _norm_matmul_body,
        grid=(n // tm, wn // tn),
        in_specs=[pl.BlockSpec((tm, d), lambda i, j: (i, 0)),
                  pl.BlockSpec((1, d), lambda i, j: (0, 0)),
                  pl.BlockSpec((d, tn), lambda i, j: (0, j))],
        out_specs=pl.BlockSpec((tm, tn), lambda i, j: (i, j)),
        out_shape=jax.ShapeDtypeStruct((n, wn), F32),
        scratch_shapes=[pltpu.VMEM((tm, d), BF16)],
        compiler_params=_cparams("parallel", "arbitrary"),
        name="norm_matmul",
    )(x, g.reshape(1, d), w)


def _softplus(x):
    return jnp.maximum(x, 0.0) + jnp.log1p(jnp.exp(-jnp.abs(x)))


def _silu(x):
    return x * jax.nn.sigmoid(x)


def _expand_heads(v, sel):
    hi = v.astype(BF16)
    r1 = v - hi.astype(F32)
    mid = r1.astype(BF16)
    lo = (r1 - mid.astype(F32)).astype(BF16)
    out = jnp.dot(hi, sel, preferred_element_type=F32)
    out = out + jnp.dot(mid, sel, preferred_element_type=F32)
    return out + jnp.dot(lo, sel, preferred_element_type=F32)


def _ssd_body(xbc_ref, z_ref, sm_ref, convp_ref, init_ref, cw_ref, cb_ref, dtb_ref, aneg_ref, dsk_ref, gs_ref, sel_ref,
              y_ref, st_ref, carry_ref, state_ref, *, rows_in, q):
    c = pl.program_id(1)
    taps = CONV_WIDTH - 1

    @pl.when(c == 0)
    def _():
        prev = convp_ref[0]
        for j in range(taps):
            acc0 = cw_ref[j:j + 1, :] * prev[SUBLANES - 1:SUBLANES, :]
            for m in range(1, j + 1):
                acc0 = acc0 + cw_ref[j - m:j - m + 1, :] * prev[SUBLANES - 1 - m:SUBLANES - m, :]
            carry_ref[j:j + 1, :] = acc0
        state_ref[...] = init_ref[0]

    x = xbc_ref[...]
    if rows_in < q:
        x = jnp.concatenate([x, jnp.zeros((q - rows_in, CONV_DIM), F32)], axis=0)

    first_row = lax.broadcasted_iota(I32, (q, CONV_DIM), 0) == 0
    shifted = None
    for j in range(taps):
        stage = cw_ref[j:j + 1, :] * x
        if shifted is not None:
            stage = stage + shifted
        shifted = jnp.where(first_row, carry_ref[j:j + 1, :], pltpu.roll(stage, 1, 0))
        carry_ref[j:j + 1, :] = stage[q - 1:q, :]
    xc = _silu(cw_ref[taps:taps + 1, :] * x + shifted + cb_ref[...])

    xs = xc[:, :D_INNER]
    gn = SSD_GROUPS * D_STATE
    bm = xc[:, D_INNER:D_INNER + gn].astype(BF16)
    cm = xc[:, D_INNER + gn:].astype(BF16)

    sm = sm_ref[...]
    zz = z_ref[...]
    if rows_in < q:
        sm = jnp.concatenate([sm, jnp.zeros((q - rows_in, LANES), F32)], axis=0)
        zz = jnp.concatenate([zz, jnp.zeros((q - rows_in, D_INNER), F32)], axis=0)
    row = lax.broadcasted_iota(I32, (q, LANES), 0)
    dt = _softplus(sm + dtb_ref[...])
    if rows_in < q:
        dt = jnp.where(row < rows_in, dt, 0.0)
    a = dt * aneg_ref[...]
    tri = (lax.broadcasted_iota(I32, (q, q), 0) >= lax.broadcasted_iota(I32, (q, q), 1)).astype(F32)
    a_cs = jnp.dot(tri, a, precision=HIGHEST, preferred_element_type=F32)
    a_t = a_cs.T
    a_last = a_cs[q - 1:q, :]
    dte = jnp.exp(a_last - a_cs)
    e_in = jnp.exp(a_cs)

    sel = sel_ref[...]
    xdt = xs * _expand_heads(dt, sel)
    xdt_bf = xdt.astype(BF16)
    xw_bf = (xs * _expand_heads(dt * dte, sel)).astype(BF16)
    ein_x = _expand_heads(e_in, sel)

    causal = lax.broadcasted_iota(I32, (q, q), 0) >= lax.broadcasted_iota(I32, (q, q), 1)
    lane = lax.broadcasted_iota(I32, (q, LANES), 1)
    hpg = SSD_HEADS // SSD_GROUPS
    gw = hpg * SSD_HEAD_DIM
    y_parts = []
    for g in range(SSD_GROUPS):
        bg = bm[:, g * D_STATE:(g + 1) * D_STATE]
        cg = cm[:, g * D_STATE:(g + 1) * D_STATE]
        cb = _nt_dot(cg, bg)
        m_h = []
        for e in range(hpg):
            h = g * hpg + e
            col = a_cs[:, SM_DT + h:SM_DT + h + 1]
            rw = a_t[SM_DT + h:SM_DT + h + 1, :]
            decay = jnp.exp(jnp.where(causal, col - rw, -jnp.inf))
            m_h.append((cb * decay).astype(BF16))
        yd = []
        for t in range(hpg // 2):
            pair = g * (hpg // 2) + t
            slab = xdt_bf[:, pair * LANES:(pair + 1) * LANES]
            ya = jnp.dot(m_h[2 * t], slab, preferred_element_type=F32)
            yb = jnp.dot(m_h[2 * t + 1], slab, preferred_element_type=F32)
            yd.append(jnp.where(lane < SSD_HEAD_DIM, ya, yb))
        s_old = state_ref[g * gw:(g + 1) * gw, :]
        y_off = _nt_dot(cg, s_old.astype(BF16)) * ein_x[:, g * gw:(g + 1) * gw]
        y_parts.append(jnp.concatenate(yd, axis=1) + y_off)
        new = lax.dot_general(xw_bf[:, g * gw:(g + 1) * gw], bg, (((0,), (0,)), ((), ())),
                              preferred_element_type=F32)
        for e in range(hpg):
            h = g * hpg + e
            dec = jnp.exp(a_t[SM_DT + h:SM_DT + h + 1, q - 1:q])
            lo = e * SSD_HEAD_DIM
            state_ref[h * SSD_HEAD_DIM:(h + 1) * SSD_HEAD_DIM, :] = (
                s_old[lo:lo + SSD_HEAD_DIM, :] * dec + new[lo:lo + SSD_HEAD_DIM, :])

    y = jnp.concatenate(y_parts, axis=1) + dsk_ref[...] * xs
    y = y * _silu(zz)
    outs = []
    for g in range(SSD_GROUPS):
        yg = y[:, g * gw:(g + 1) * gw]
        outs.append(yg * lax.rsqrt(jnp.mean(yg * yg, axis=-1, keepdims=True) + EPS))
    y = jnp.concatenate(outs, axis=1) * gs_ref[...]
    y_ref[...] = y[:rows_in].astype(BF16)

    @pl.when(c == pl.num_programs(1) - 1)
    def _():
        st_ref[0] = state_ref[...]


def ssd(zall, batch, seq, conv_prev8, ssm_init, conv_w, conv_b, dtb_row, aneg_row, dsk_row, gs_row):
    q = SSD_CHUNK
    rows_in = min(seq, q)
    nch = seq // rows_in
    sel = (jnp.arange(LANES)[:, None] - SM_DT == jnp.arange(D_INNER)[None, :] // SSD_HEAD_DIM).astype(BF16)
    row_map = lambda b, c: (b * nch + c)
    const2 = lambda b, c: (0, 0)
    body = functools.partial(_ssd_body, rows_in=rows_in, q=q)
    return pl.pallas_call(
        body,
        grid=(batch, nch),
        in_specs=[pl.BlockSpec((rows_in, CONV_DIM), lambda b, c: (row_map(b, c), OFF_XBC // CONV_DIM)),
                  pl.BlockSpec((rows_in, D_INNER), lambda b, c: (row_map(b, c), OFF_Z // D_INNER)),
                  pl.BlockSpec((rows_in, LANES), lambda b, c: (row_map(b, c), OFF_SM // LANES)),
                  pl.BlockSpec((1, SUBLANES, CONV_DIM), lambda b, c: (b, 0, 0)),
                  pl.BlockSpec((1, D_INNER, D_STATE), lambda b, c: (b, 0, 0)),
                  pl.BlockSpec((CONV_WIDTH, CONV_DIM), const2),
                  pl.BlockSpec((1, CONV_DIM), const2),
                  pl.BlockSpec((1, LANES), const2),
                  pl.BlockSpec((1, LANES), const2),
                  pl.BlockSpec((1, D_INNER), const2),
                  pl.BlockSpec((1, D_INNER), const2),
                  pl.BlockSpec((LANES, D_INNER), const2)],
        out_specs=[pl.BlockSpec((rows_in, D_INNER), lambda b, c: (row_map(b, c), 0)),
                   pl.BlockSpec((1, D_INNER, D_STATE), lambda b, c: (b, 0, 0))],
        out_shape=[jax.ShapeDtypeStruct((batch * seq, D_INNER), BF16),
                   jax.ShapeDtypeStruct((batch, D_INNER, D_STATE), F32)],
        scratch_shapes=[pltpu.VMEM((SUBLANES, CONV_DIM), F32),
                        pltpu.VMEM((D_INNER, D_STATE), F32)],
        compiler_params=_cparams("parallel", "arbitrary"),
        name="ssd",
    )(zall, zall, zall, conv_prev8, ssm_init, conv_w, conv_b, dtb_row, aneg_row, dsk_row, gs_row, sel)


def _mem_attn_body(q_ref, k_ref, v_ref, o_ref):
    for h in range(MEM_HEADS):
        sl = slice(h * MEM_HEAD_DIM, (h + 1) * MEM_HEAD_DIM)
        s = _nt_dot(q_ref[:, sl].astype(BF16), k_ref[:, sl].astype(BF16)) * (MEM_HEAD_DIM ** -0.5)
        m = jnp.max(s, axis=-1, keepdims=True)
        p = jnp.exp(s - m)
        p = p / jnp.sum(p, axis=-1, keepdims=True)
        o = jnp.dot(p.astype(BF16), v_ref[:, sl].astype(BF16), preferred_element_type=F32)
        o_ref[:, sl] = o.astype(BF16)


def mem_attn(zall, batch, seq, k_arr, k_col, v_arr, v_col, tm):
    nt = seq // tm
    return pl.pallas_call(
        _mem_attn_body,
        grid=(batch, nt),
        in_specs=[pl.BlockSpec((tm, MEM_WIDTH), lambda b, i: (b * nt + i, OFF_QM // MEM_WIDTH)),
                  pl.BlockSpec((N_MEM, MEM_WIDTH), lambda b, i: (b, k_col)),
                  pl.BlockSpec((N_MEM, MEM_WIDTH), lambda b, i: (b, v_col))],
        out_specs=pl.BlockSpec((tm, MEM_WIDTH), lambda b, i: (b * nt + i, 0)),
        out_shape=jax.ShapeDtypeStruct((batch * seq, MEM_WIDTH), BF16),
        compiler_params=_cparams("parallel", "arbitrary"),
        name="mem_attn",
    )(zall, k_arr, v_arr)


def _kth_largest_key(count_ge, shape, n_sel):
    def bit_body(t, ans):
        cand = ans | jnp.left_shift(jnp.int32(1), 31 - t)
        cnt = count_ge(cand ^ INT_MIN)
        return jnp.where(cnt >= n_sel, cand, ans)

    ans = lax.fori_loop(0, 32, bit_body, jnp.zeros(shape, I32))
    return ans ^ INT_MIN


def _tie_cut(count_eq_below, need, shape, nbits):
    def bit_body(t, lo):
        cand = lo | jnp.left_shift(jnp.int32(1), nbits - 1 - t)
        cnt = count_eq_below(cand)
        return jnp.where(cnt < need, cand, lo)

    return lax.fori_loop(0, nbits, bit_body, jnp.zeros(shape, I32))


def _select_bias(key, kpos, thr, cut, visible):
    sel = (key > thr) | ((key == thr) & (kpos <= cut))
    return jnp.where(sel & visible, 0.0, NEG)


def _dsa_prompt_body(q_ref, qi_ref, smq_ref, k_ref, v_ref, smk_ref, o_ref,
                     kh_ref, vt_ref, kis_ref, qt2_ref, qit_ref, keys_ref, bias_ref, ot_ref, *, tq, kc, seq, n_sel):
    i = pl.program_id(1)

    @pl.when(i == 0)
    def _():
        def cast_rows(r, carry):
            rs = pl.ds(pl.multiple_of(r * kc, kc), kc)
            kk = k_ref[rs, :]
            for h in range(N_KV_HEADS):
                kh_ref[h, rs, :] = kk[:, h * HEAD_DIM:(h + 1) * HEAD_DIM].astype(BF16)
            vt_ref[r] = v_ref[rs, :].T.astype(BF16)
            kis_ref[rs, :] = smk_ref[rs, SM_KI:SM_KI + IDX_DIM].astype(BF16)
            return carry
        lax.fori_loop(0, seq // kc, cast_rows, 0)

    nkc = (i * tq + tq - 1) // kc + 1
    qpos = i * tq + lax.broadcasted_iota(I32, (kc, tq), 1)
    krow = lax.broadcasted_iota(I32, (kc, tq), 0)

    qt = (q_ref[...] * (HEAD_DIM ** -0.5)).T.astype(BF16)
    for h in range(N_HEADS):
        qt2_ref[:, h * tq:(h + 1) * tq] = qt[h * HEAD_DIM:(h + 1) * HEAD_DIM, :]
    qit = (qi_ref[...] * (IDX_DIM ** -0.5)).T.astype(BF16)
    for h in range(IDX_HEADS):
        qit_ref[:, h * tq:(h + 1) * tq] = qit[h * IDX_DIM:(h + 1) * IDX_DIM, :]
    wt = smq_ref[...].T[SM_WI:SM_WI + IDX_HEADS, :] * (IDX_HEADS ** -0.5)

    def score_chunk(c, carry):
        ks = pl.ds(pl.multiple_of(c * kc, kc), kc)
        d = jnp.dot(kis_ref[ks, :], qit_ref[...], preferred_element_type=F32)
        sc = jnp.zeros((kc, tq), F32)
        for h in range(IDX_HEADS):
            sc = sc + jnp.maximum(d[:, h * tq:(h + 1) * tq], 0.0) * wt[h:h + 1, :]
        key = _float_key(sc + 0.0)
        keys_ref[c] = jnp.where(c * kc + krow <= qpos, key, INT_MIN)
        return carry
    lax.fori_loop(0, nkc, score_chunk, 0)

    def count(pred):
        def body(c, acc):
            return acc + jnp.where(pred(keys_ref[c], c * kc + krow), 1.0, 0.0)
        acc = lax.fori_loop(0, nkc, body, jnp.zeros((kc, tq), F32))
        return jnp.sum(acc, axis=0, keepdims=True)

    vec = (1, tq)
    thr = _kth_largest_key(lambda cand: count(lambda key, kpos: key >= cand), vec, n_sel)
    n_gt = count(lambda key, kpos: key > thr)
    n_eq = count(lambda key, kpos: key == thr)
    need = n_sel - n_gt
    excess = jnp.max(jnp.where((n_eq > need) & (thr != INT_MIN), 1.0, 0.0))
    nbits = max(1, int(seq - 1).bit_length())
    cut = lax.cond(
        excess > 0.0,
        lambda: _tie_cut(lambda cand: count(lambda key, kpos: (key == thr) & (kpos < cand)), need, vec, nbits),
        lambda: jnp.full(vec, INT_MAX, I32))

    def bias_chunk(c, carry):
        kpos = c * kc + krow
        bias_ref[c] = _select_bias(keys_ref[c], kpos, thr, cut, kpos <= qpos)
        return carry
    lax.fori_loop(0, nkc, bias_chunk, 0)

    grp = N_HEADS // N_KV_HEADS
    kv_per_pass = N_KV_HEADS // ATT_PASSES
    hpp = kv_per_pass * grp
    for ps_i in range(ATT_PASSES):
        kv0 = ps_i * kv_per_pass

        def att_chunk(c, carry, kv0=kv0):
            ms, ls, accs = carry
            ks = pl.ds(pl.multiple_of(c * kc, kc), kc)
            bias = bias_ref[c]
            s4 = [jnp.dot(kh_ref[kv0 + j, ks, :], qt2_ref[:, (kv0 + j) * grp * tq:(kv0 + j + 1) * grp * tq],
                          preferred_element_type=F32) for j in range(kv_per_pass)]
            ms_n, ls_n, accs_n = [], [], []
            for h in range(hpp):
                s = s4[h // grp][:, (h % grp) * tq:(h % grp + 1) * tq] + bias
                m_new = jnp.maximum(ms[h], jnp.max(s, axis=0, keepdims=True))
                p = jnp.exp(s - m_new)
                alpha = jnp.exp(ms[h] - m_new)
                ms_n.append(m_new)
                ls_n.append(alpha * ls[h] + jnp.sum(p, axis=0, keepdims=True))
                kh = kv0 + h // grp
                vtc = vt_ref[c, kh * HEAD_DIM:(kh + 1) * HEAD_DIM, :]
                accs_n.append(alpha * accs[h] + jnp.dot(vtc, p.astype(BF16), preferred_element_type=F32))
            return tuple(ms_n), tuple(ls_n), tuple(accs_n)

        init = (tuple(jnp.full(vec, NEG, F32) for _ in range(hpp)),
                tuple(jnp.zeros(vec, F32) for _ in range(hpp)),
                tuple(jnp.zeros((HEAD_DIM, tq), F32) for _ in range(hpp)))
        _, ls, accs = lax.fori_loop(0, nkc, att_chunk, init)
        for h in range(hpp):
            hh = kv0 * grp + h
            ot_ref[hh * HEAD_DIM:(hh + 1) * HEAD_DIM, :] = accs[h] / ls[h]
    o_ref[...] = ot_ref[...].T.astype(BF16)


def dsa_prompt(zall, batch, seq):
    tq = 128
    kc = 256
    nq = seq // tq
    n_sel = min(TOPK_MAX, seq // 4)
    kvw = N_KV_HEADS * HEAD_DIM
    qw = N_HEADS * HEAD_DIM
    qiw = IDX_HEADS * IDX_DIM
    body = functools.partial(_dsa_prompt_body, tq=tq, kc=kc, seq=seq, n_sel=n_sel)
    return pl.pallas_call(
        body,
        grid=(batch, nq),
        in_specs=[pl.BlockSpec((tq, qw), lambda b, i: (b * nq + i, OFF_Q // qw)),
                  pl.BlockSpec((tq, qiw), lambda b, i: (b * nq + i, OFF_QI // qiw)),
                  pl.BlockSpec((tq, LANES), lambda b, i: (b * nq + i, OFF_SM // LANES)),
                  pl.BlockSpec((seq, kvw), lambda b, i: (b, OFF_K // kvw)),
                  pl.BlockSpec((seq, kvw), lambda b, i: (b, OFF_V // kvw)),
                  pl.BlockSpec((seq, LANES), lambda b, i: (b, OFF_SM // LANES))],
        out_specs=pl.BlockSpec((tq, qw), lambda b, i: (b * nq + i, 0)),
        out_shape=jax.ShapeDtypeStruct((batch * seq, qw), BF16),
        scratch_shapes=[pltpu.VMEM((N_KV_HEADS, seq, HEAD_DIM), BF16),
                        pltpu.VMEM((seq // kc, kvw, kc), BF16),
                        pltpu.VMEM((seq, IDX_DIM), BF16),
                        pltpu.VMEM((HEAD_DIM, N_HEADS * tq), BF16),
                        pltpu.VMEM((IDX_DIM, IDX_HEADS * tq), BF16),
                        pltpu.VMEM((seq // kc, kc, tq), I32),
                        pltpu.VMEM((seq // kc, kc, tq), F32),
                        pltpu.VMEM((qw, tq), F32)],
        compiler_params=_cparams("parallel", "arbitrary"),
        name="dsa_prompt",
    )(zall, zall, zall, zall, zall, zall)


def _dsa_s_score_body(pt_ref, qs_ref, w_ref, *refs, pg):
    ki_refs, o_ref = refs[:pg], refs[pg]
    qs = qs_ref[0]
    wcol = w_ref[0] * (IDX_HEADS ** -0.5)
    t = qs.shape[0] // IDX_HEADS
    for p in range(pg):
        d = jnp.dot(qs, ki_refs[p][0].astype(BF16), preferred_element_type=F32)
        r = jnp.maximum(d * (IDX_DIM ** -0.5), 0.0) * wcol
        sc = r[0:t, :]
        for h in range(1, IDX_HEADS):
            sc = sc + r[h * t:(h + 1) * t, :]
        o_ref[0, :, p * PAGE_SIZE:(p + 1) * PAGE_SIZE] = sc + 0.0


def _dsa_s_select_body(sc_ref, qs_ref, w_ref, smn_ref, o_ref, *, t, past, n_sel, nbat):
    rows = nbat * t
    sc_new = []
    for b in range(nbat):
        wcol = w_ref[b] * (IDX_HEADS ** -0.5)
        ki_new = smn_ref[b * t:(b + 1) * t, SM_KI:SM_KI + IDX_DIM].astype(BF16)
        ki_new = jnp.concatenate([ki_new, jnp.zeros((LANES - t, IDX_DIM), BF16)], axis=0)
        d = _nt_dot(qs_ref[b], ki_new)
        r = jnp.maximum(d * (IDX_DIM ** -0.5), 0.0) * wcol
        sc = r[0:t, :]
        for h in range(1, IDX_HEADS):
            sc = sc + r[h * t:(h + 1) * t, :]
        sc_new.append(sc)
    sc_new = jnp.concatenate(sc_new, axis=0)
    lane_t = lax.broadcasted_iota(I32, (t, LANES), 1)
    vis_t = lane_t <= lax.broadcasted_iota(I32, (t, LANES), 0)
    vis_n = jnp.concatenate([vis_t] * nbat, axis=0)
    lane_n = lax.broadcasted_iota(I32, (rows, LANES), 1)
    key_n = jnp.where(vis_n, _float_key(sc_new + 0.0), INT_MIN)
    key_p = _float_key(sc_ref[...].reshape(rows, past))
    pos_p = lax.broadcasted_iota(I32, (rows, past), 1)
    pos_n = past + lane_n

    def count(pred):
        return (jnp.sum(jnp.where(pred(key_p, pos_p), 1.0, 0.0), axis=1, keepdims=True)
                + jnp.sum(jnp.where(pred(key_n, pos_n), 1.0, 0.0), axis=1, keepdims=True))

    vec = (rows, 1)
    thr = _kth_largest_key(lambda cand: count(lambda key, kpos: key >= cand), vec, n_sel)
    need = n_sel - count(lambda key, kpos: key > thr)
    n_eq = count(lambda key, kpos: key == thr)
    excess = jnp.max(jnp.where((n_eq > need) & (thr != INT_MIN), 1.0, 0.0))
    nbits = max(1, int(past + t - 1).bit_length())
    cut = lax.cond(
        excess > 0.0,
        lambda: _tie_cut(lambda cand: count(lambda key, kpos: (key == thr) & (kpos < cand)), need, vec, nbits),
        lambda: jnp.full(vec, INT_MAX, I32))
    o_ref[:, :, 0:past] = _select_bias(key_p, pos_p, thr, cut, pos_p >= 0).reshape(nbat, t, past)
    o_ref[:, :, past:past + LANES] = _select_bias(key_n, pos_n, thr, cut, vis_n).reshape(nbat, t, LANES)


def _dsa_s_attn_body(pt_ref, qbd_ref, bias_ref, biasn_ref, kn_ref, vn_ref, *refs, pg, t):
    k_refs, v_refs = refs[:pg], refs[pg:2 * pg]
    o_ref, m_ref, l_ref, acc_ref = refs[2 * pg:]
    j = pl.program_id(1)
    rows = qbd_ref.shape[1]
    rep = rows // t

    @pl.when(j == 0)
    def _():
        m_ref[...] = jnp.full(m_ref.shape, NEG, F32)
        l_ref[...] = jnp.zeros(l_ref.shape, F32)
        acc_ref[...] = jnp.zeros(acc_ref.shape, F32)

    qbd = qbd_ref[0]

    def update(kt, vt, bias):
        s = jnp.dot(qbd, kt, preferred_element_type=F32) * (HEAD_DIM ** -0.5) + jnp.concatenate([bias] * rep, axis=0)
        m = m_ref[...]
        m_new = jnp.maximum(m, jnp.max(s, axis=1, keepdims=True))
        p = jnp.exp(s - m_new)
        alpha = jnp.exp(m - m_new)
        l_ref[...] = alpha * l_ref[...] + jnp.sum(p, axis=1, keepdims=True)
        acc_ref[...] = alpha * acc_ref[...] + _nt_dot(p.astype(BF16), vt)
        m_ref[...] = m_new

    kt = jnp.concatenate([r[0] for r in k_refs], axis=1).astype(BF16)
    vt = jnp.concatenate([r[0] for r in v_refs], axis=1).astype(BF16)
    update(kt, vt, bias_ref[0])

    @pl.when(j == pl.num_programs(1) - 1)
    def _():
        kvw = N_KV_HEADS * HEAD_DIM
        zpad = jnp.zeros((LANES - t, kvw), F32)
        update(jnp.concatenate([kn_ref[...], zpad], axis=0).T.astype(BF16),
               jnp.concatenate([vn_ref[...], zpad], axis=0).T.astype(BF16), biasn_ref[0])
        o_ref[0] = acc_ref[...] / l_ref[...]


def dsa_sample(zs, batch, t, cache_k, cache_v, cache_ki, page_table):
    n_pages = page_table.shape[1]
    past = n_pages * PAGE_SIZE
    n_sel = min(TOPK_MAX, (past + t) // 4)
    pg = 16 if n_pages % 16 == 0 else 8
    nj = n_pages // pg
    n_pool = cache_k.shape[0]
    kvw = N_KV_HEADS * HEAD_DIM
    grp = N_HEADS // N_KV_HEADS

    qi = zs[:, OFF_QI:OFF_QI + IDX_HEADS * IDX_DIM].reshape(batch, t, IDX_HEADS, IDX_DIM)
    qs = jnp.transpose(qi, (0, 2, 1, 3)).reshape(batch, IDX_HEADS * t, IDX_DIM).astype(BF16)
    wi = zs[:, OFF_SM + SM_WI:OFF_SM + SM_WI + IDX_HEADS].reshape(batch, t, IDX_HEADS)
    wcol = jnp.transpose(wi, (0, 2, 1)).reshape(batch, IDX_HEADS * t, 1)

    def page_spec(shape, p):
        return pl.BlockSpec(shape, lambda b, j, pt: (pt[b, j * pg + p],) + (0,) * (len(shape) - 1))

    scores = pl.pallas_call(
        functools.partial(_dsa_s_score_body, pg=pg),
        grid_spec=pltpu.PrefetchScalarGridSpec(
            num_scalar_prefetch=1,
            grid=(batch, nj),
            in_specs=[pl.BlockSpec((1, IDX_HEADS * t, IDX_DIM), lambda b, j, pt: (b, 0, 0)),
                      pl.BlockSpec((1, IDX_HEADS * t, 1), lambda b, j, pt: (b, 0, 0))]
                     + [page_spec((1, IDX_DIM, PAGE_SIZE), p) for p in range(pg)],
            out_specs=pl.BlockSpec((1, t, pg * PAGE_SIZE), lambda b, j, pt: (b, 0, j))),
        out_shape=jax.ShapeDtypeStruct((batch, t, past), F32),
        compiler_params=_cparams("parallel", "arbitrary"),
        name="dsa_sample_scores",
    )(page_table, qs, wcol, *([jnp.swapaxes(cache_ki, 1, 2)] * pg))

    nbat = 4 if (batch % 4 == 0 and t % SUBLANES == 0) else 1
    bias = pl.pallas_call(
        functools.partial(_dsa_s_select_body, t=t, past=past, n_sel=n_sel, nbat=nbat),
        grid=(batch // nbat,),
        in_specs=[pl.BlockSpec((nbat, t, past), lambda b: (b, 0, 0)),
                  pl.BlockSpec((nbat, IDX_HEADS * t, IDX_DIM), lambda b: (b, 0, 0)),
                  pl.BlockSpec((nbat, IDX_HEADS * t, 1), lambda b: (b, 0, 0)),
                  pl.BlockSpec((nbat * t, LANES), lambda b: (b, OFF_SM // LANES))],
        out_specs=pl.BlockSpec((nbat, t, past + LANES), lambda b: (b, 0, 0)),
        out_shape=jax.ShapeDtypeStruct((batch, t, past + LANES), F32),
        compiler_params=_cparams("parallel"),
        name="dsa_sample_select",
    )(scores, qs, wcol, zs)

    q = zs[:, OFF_Q:OFF_Q + N_HEADS * HEAD_DIM].reshape(batch, t, N_KV_HEADS, grp, HEAD_DIM)
    q = jnp.transpose(q, (0, 2, 3, 1, 4))
    eye = jnp.eye(N_KV_HEADS, dtype=F32)
    qbd = (q[:, :, :, :, None, :] * eye[None, :, None, None, :, None]).reshape(batch, N_HEADS * t, kvw).astype(BF16)

    ck = jnp.transpose(cache_k, (0, 2, 3, 1)).reshape(n_pool, kvw, PAGE_SIZE)
    cv = jnp.transpose(cache_v, (0, 2, 3, 1)).reshape(n_pool, kvw, PAGE_SIZE)
    rows = N_HEADS * t
    out = pl.pallas_call(
        functools.partial(_dsa_s_attn_body, pg=pg, t=t),
        grid_spec=pltpu.PrefetchScalarGridSpec(
            num_scalar_prefetch=1,
            grid=(batch, nj),
            in_specs=[pl.BlockSpec((1, rows, kvw), lambda b, j, pt: (b, 0, 0)),
                      pl.BlockSpec((1, t, pg * PAGE_SIZE), lambda b, j, pt: (b, 0, j)),
                      pl.BlockSpec((1, t, LANES), lambda b, j, pt: (b, 0, past // LANES)),
                      pl.BlockSpec((t, kvw), lambda b, j, pt: (b, OFF_K // kvw)),
                      pl.BlockSpec((t, kvw), lambda b, j, pt: (b, OFF_V // kvw))]
                     + [page_spec((1, kvw, PAGE_SIZE), p) for p in range(pg)]
                     + [page_spec((1, kvw, PAGE_SIZE), p) for p in range(pg)],
            out_specs=pl.BlockSpec((1, rows, kvw), lambda b, j, pt: (b, 0, 0)),
            scratch_shapes=[pltpu.VMEM((rows, 1), F32), pltpu.VMEM((rows, 1), F32), pltpu.VMEM((rows, kvw), F32)]),
        out_shape=jax.ShapeDtypeStruct((batch, rows, kvw), F32),
        compiler_params=_cparams("parallel", "arbitrary"),
        name="dsa_sample_attn",
    )(page_table, qbd, bias, bias, zs, zs, *([ck] * pg), *([cv] * pg))

    o = out.reshape(batch, N_KV_HEADS, grp, t, N_KV_HEADS, HEAD_DIM)
    o = jnp.stack([o[:, kh, :, :, kh, :] for kh in range(N_KV_HEADS)], axis=1)
    return jnp.transpose(o, (0, 3, 1, 2, 4)).reshape(batch * t, N_HEADS * HEAD_DIM).astype(BF16)


def _merge_body(x_ref, gate_ref, ys_ref, oa_ref, om_ref, bg_ref, ws_ref, wa_ref, wm_ref, wo_ref, g2_ref, wr_ref, br_ref,
                x1_ref, h2_ref, te_ref, gw_ref):
    gates = jax.nn.sigmoid(gate_ref[...] + bg_ref[...])
    merged = (gates[:, 0:D_MODEL] * jnp.dot(ys_ref[...], ws_ref[...], preferred_element_type=F32)
              + gates[:, D_MODEL:2 * D_MODEL] * jnp.dot(oa_ref[...], wa_ref[...], preferred_element_type=F32)
              + gates[:, 2 * D_MODEL:] * jnp.dot(om_ref[...], wm_ref[...], preferred_element_type=F32))
    x1 = x_ref[...] + jnp.dot(merged.astype(BF16), wo_ref[...], preferred_element_type=F32)
    x1_ref[...] = x1
    h2 = x1 * lax.rsqrt(jnp.mean(x1 * x1, axis=-1, keepdims=True) + EPS)
    h2 = h2 * g2_ref[...]
    h2_ref[...] = h2
    logits = jnp.dot(h2, wr_ref[...], precision=HIGHEST, preferred_element_type=F32) + br_ref[...]
    lane = lax.broadcasted_iota(I32, logits.shape, 1)
    te = jnp.zeros(logits.shape, I32)
    tv = []
    for k in range(TOP_K):
        m = jnp.max(logits, axis=1, keepdims=True)
        idx = jnp.min(jnp.where(logits == m, lane, LANES), axis=1, keepdims=True)
        te = jnp.where(lane == k, idx, te)
        tv.append(m)
        logits = jnp.where(lane == idx, -jnp.inf, logits)
    ex = [jnp.exp(v - tv[0]) for v in tv]
    den = ex[0] + ex[1] + ex[2] + ex[3]
    gw = jnp.zeros(logits.shape, F32)
    for k in range(TOP_K):
        gw = jnp.where(lane == k, ex[k] / den, gw)
    te_ref[...] = te
    gw_ref[...] = gw


def merge(x, zall, ys, oa, om, bg, ws, wa, wm, wo, g2, wr, br, tm):
    n = x.shape[0]
    gw3 = N_BRANCH * D_MODEL
    row = lambda i: (i, 0)
    const = lambda i: (0, 0)
    return pl.pallas_call(
        _merge_body,
        grid=(n // tm,),
        in_specs=[pl.BlockSpec((tm, D_MODEL), row),
                  pl.BlockSpec((tm, gw3), lambda i: (i, OFF_GATE // gw3)),
                  pl.BlockSpec((tm, D_INNER), row),
                  pl.BlockSpec((tm, N_HEADS * HEAD_DIM), row),
                  pl.BlockSpec((tm, MEM_WIDTH), row),
                  pl.BlockSpec((1, gw3), const),
                  pl.BlockSpec((D_INNER, D_MODEL), const),
                  pl.BlockSpec((N_HEADS * HEAD_DIM, D_MODEL), const),
                  pl.BlockSpec((MEM_WIDTH, D_MODEL), const),
                  pl.BlockSpec((D_MODEL, D_MODEL), const),
                  pl.BlockSpec((1, D_MODEL), const),
                  pl.BlockSpec((D_MODEL, LANES), const),
                  pl.BlockSpec((1, LANES), const)],
        out_specs=[pl.BlockSpec((tm, D_MODEL), row), pl.BlockSpec((tm, D_MODEL), row),
                   pl.BlockSpec((tm, LANES), row), pl.BlockSpec((tm, LANES), row)],
        out_shape=[jax.ShapeDtypeStruct((n, D_MODEL), F32), jax.ShapeDtypeStruct((n, D_MODEL), F32),
                   jax.ShapeDtypeStruct((n, LANES), I32), jax.ShapeDtypeStruct((n, LANES), F32)],
        compiler_params=_cparams("parallel"),
        name="merge",
    )(x, zall, ys, oa, om, bg, ws, wa, wm, wo, g2, wr, br)


def _moe_pos_body(te_ref, pos_ref, cnt_ref, carry_ref):
    i = pl.program_id(0)
    tt = te_ref.shape[0]

    @pl.when(i == 0)
    def _():
        carry_ref[...] = jnp.zeros(carry_ref.shape, F32)

    te = te_ref[...]
    lane = lax.broadcasted_iota(I32, (tt, LANES), 1)
    onehot = [lane == te[:, k:k + 1] for k in range(TOP_K)]
    msum = jnp.zeros((tt, LANES), F32)
    for k in range(TOP_K):
        msum = msum + jnp.where(onehot[k], 1.0, 0.0)
    strict = (lax.broadcasted_iota(I32, (tt, tt), 0) > lax.broadcasted_iota(I32, (tt, tt), 1))
    prefix = jnp.dot(jnp.where(strict, 1.0, 0.0).astype(BF16), msum.astype(BF16), preferred_element_type=F32)
    prefix = prefix + carry_ref[0:1, :]
    pos = jnp.zeros((tt, LANES), F32)
    for k in range(TOP_K):
        pk = jnp.sum(jnp.where(onehot[k], prefix, 0.0), axis=1, keepdims=True)
        pos = jnp.where(lane == k, pk, pos)
    pos_ref[...] = pos
    carry_ref[...] = carry_ref[...] + jnp.sum(msum, axis=0, keepdims=True)
    cnt_ref[...] = carry_ref[...]


def _moe_dest_body(te_ref, pos_ref, cnt_ref, dest_ref, be_ref, nu_ref, *, bm):
    tt = te_ref.shape[0]
    cnt = cnt_ref[...]
    padded = jnp.floor((cnt + (bm - 1)) * (1.0 / bm)) * bm
    upper = (lax.broadcasted_iota(I32, (LANES, LANES), 0) < lax.broadcasted_iota(I32, (LANES, LANES), 1))
    pad_start = jnp.dot(padded, jnp.where(upper, 1.0, 0.0), precision=HIGHEST, preferred_element_type=F32)
    pad_end = pad_start + padded
    te = te_ref[...]
    pos = pos_ref[...]
    lane = lax.broadcasted_iota(I32, (tt, LANES), 1)
    dest = jnp.zeros((tt, LANES), F32)
    for k in range(TOP_K):
        ps = jnp.sum(jnp.where(lane == te[:, k:k + 1], pad_start[0:1, :], 0.0), axis=1, keepdims=True)
        dest = jnp.where(lane == k, ps + pos[:, k:k + 1], dest)
    dest_ref[...] = dest.astype(I32)
    nb = be_ref.shape[0]
    bstart = (lax.broadcasted_iota(I32, (nb, LANES), 0) * bm).astype(F32)
    lane_b = lax.broadcasted_iota(I32, (nb, LANES), 1)
    done = jnp.where((pad_end[0:1, :] <= bstart) & (lane_b < N_EXPERTS), 1.0, 0.0)
    be = jnp.minimum(jnp.sum(done, axis=1, keepdims=True), N_EXPERTS - 1.0)
    be_ref[...] = jnp.broadcast_to(be, (nb, LANES)).astype(I32)
    total = jnp.sum(padded[0:1, :], axis=1, keepdims=True)
    nu_ref[...] = jnp.broadcast_to(total * (1.0 / bm), nu_ref.shape).astype(I32)


def _moe_dispatch_body(dest_ref, ha_ref, hb_ref, xs_in_ref, xs_ref, sem, *, tiles_a):
    del xs_in_ref
    tt = ha_ref.shape[0]

    def scatter_rows(h_ref):
        def copy(r, k):
            d = dest_ref[r * TOP_K + k]
            return pltpu.make_async_copy(h_ref.at[pl.ds(r, 1), :], xs_ref.at[pl.ds(d, 1), :], sem)

        def issue(r, carry):
            for k in range(TOP_K):
                copy(r, k).start()
            return carry
        lax.fori_loop(0, tt, issue, 0)

        def drain(r, carry):
            for k in range(TOP_K):
                copy(r, k).wait()
            return carry
        lax.fori_loop(0, tt, drain, 0)

    @pl.when(pl.program_id(0) < tiles_a)
    def _():
        scatter_rows(ha_ref)

    @pl.when(pl.program_id(0) >= tiles_a)
    def _():
        scatter_rows(hb_ref)


def _moe_expert_body(be_ref, nu_ref, xs_ref, w1_ref, b1_ref, w2_ref, b2_ref, o_ref, w1s_ref, w2s_ref):
    i = pl.program_id(0)
    used = i < nu_ref[0]
    e = be_ref[i]
    prev = be_ref[jnp.maximum(i - 1, 0)]
    half = LANES

    @pl.when(used & ((i == 0) | (e != prev)))
    def _():
        r = lax.broadcasted_iota(I32, (2 * half, 2 * half), 0)
        c = lax.broadcasted_iota(I32, (2 * half, 2 * half), 1)
        src_col = jnp.where(c < half, 2 * c, 2 * (c - half) + 1)
        perm = jnp.where(r == src_col, 1.0, 0.0).astype(BF16)
        for j in range(2 * D_FF // (2 * half)):
            sl = slice(j * 2 * half, (j + 1) * 2 * half)
            w1s_ref[:, sl] = jnp.dot(w1_ref[0, :, sl].astype(BF16), perm, preferred_element_type=F32).astype(BF16)
        w2s_ref[...] = w2_ref[0].astype(BF16)

    @pl.when(used)
    def _():
        u = jnp.dot(xs_ref[...].astype(BF16), w1s_ref[...], preferred_element_type=F32) + b1_ref[0]
        acts = []
        for j in range(D_FF // half):
            glu = jnp.minimum(u[:, 2 * j * half:(2 * j + 1) * half], SWIGLU_LIMIT)
            lin = jnp.clip(u[:, (2 * j + 1) * half:(2 * j + 2) * half], -SWIGLU_LIMIT, SWIGLU_LIMIT)
            acts.append((glu * jax.nn.sigmoid(SWIGLU_ALPHA * glu) * (lin + 1.0)).astype(BF16))
        act = jnp.concatenate(acts, axis=1)
        o_ref[...] = jnp.dot(act, w2s_ref[...], preferred_element_type=F32) + b2_ref[0]

    @pl.when(jnp.logical_not(used))
    def _():
        o_ref[...] = jnp.zeros(o_ref.shape, F32)


def _moe_combine_body(dest_ref, gw_ref, x1_ref, gf_ref, os_ref, y_ref, buf_ref, sem):
    tt = x1_ref.shape[0]

    def copy(r, k):
        d = dest_ref[r * TOP_K + k]
        return pltpu.make_async_copy(os_ref.at[pl.ds(d, 1), :], buf_ref.at[k, pl.ds(r, 1), :], sem)

    def issue(r, carry):
        for k in range(TOP_K):
            copy(r, k).start()
        return carry
    lax.fori_loop(0, tt, issue, 0)

    def drain(r, carry):
        for k in range(TOP_K):
            copy(r, k).wait()
        return carry
    lax.fori_loop(0, tt, drain, 0)

    gw = gw_ref[...]
    y = gw[:, 0:1] * buf_ref[0]
    for k in range(1, TOP_K):
        y = y + gw[:, k:k + 1] * buf_ref[k]
    x2 = x1_ref[...] + y
    out = x2 * lax.rsqrt(jnp.mean(x2 * x2, axis=-1, keepdims=True) + EPS)
    y_ref[...] = out * gf_ref[...]


def moe_and_final_norm(x1a, x1b, h2a, h2b, te, gw, w1, b1p, w2, b2, g_final):
    n = te.shape[0]
    tiles_a = x1a.shape[0] // MOE_T
    row_a = lambda i: (jnp.minimum(i, tiles_a - 1), 0)
    row_b = lambda i: (jnp.maximum(i - tiles_a, 0), 0)
    tt = MOE_T
    bm = MOE_BM
    nb = -(-(n * TOP_K + N_EXPERTS * (bm - 1)) // bm)
    nbp = -(-nb // SUBLANES) * SUBLANES
    row = lambda i: (i, 0)
    const = lambda i: (0, 0)

    pos, cnt = pl.pallas_call(
        _moe_pos_body,
        grid=(n // tt,),
        in_specs=[pl.BlockSpec((tt, LANES), row)],
        out_specs=[pl.BlockSpec((tt, LANES), row), pl.BlockSpec((SUBLANES, LANES), const)],
        out_shape=[jax.ShapeDtypeStruct((n, LANES), F32), jax.ShapeDtypeStruct((SUBLANES, LANES), F32)],
        scratch_shapes=[pltpu.VMEM((SUBLANES, LANES), F32)],
        compiler_params=_cparams("arbitrary"),
        name="moe_positions",
    )(te)

    dest, be, nu = pl.pallas_call(
        functools.partial(_moe_dest_body, bm=bm),
        grid=(n // tt,),
        in_specs=[pl.BlockSpec((tt, LANES), row), pl.BlockSpec((tt, LANES), row),
                  pl.BlockSpec((SUBLANES, LANES), const)],
        out_specs=[pl.BlockSpec((tt, LANES), row), pl.BlockSpec((nbp, LANES), const),
                   pl.BlockSpec((SUBLANES, LANES), const)],
        out_shape=[jax.ShapeDtypeStruct((n, LANES), I32), jax.ShapeDtypeStruct((nbp, LANES), I32),
                   jax.ShapeDtypeStruct((SUBLANES, LANES), I32)],
        compiler_params=_cparams("arbitrary"),
        name="moe_destinations",
    )(te, pos, cnt)
    dest_flat = dest[:, :TOP_K].reshape(n * TOP_K)
    block_e = be[:nb, 0]
    n_used = nu[0, 0:1]

    xs = pl.pallas_call(
        functools.partial(_moe_dispatch_body, tiles_a=tiles_a),
        grid=(n // tt,),
        in_specs=[pl.BlockSpec((tt * TOP_K,), lambda i: (i,), memory_space=pltpu.SMEM),
                  pl.BlockSpec((tt, D_MODEL), row_a),
                  pl.BlockSpec((tt, D_MODEL), row_b),
                  pl.BlockSpec(memory_space=pl.ANY)],
        out_specs=pl.BlockSpec(memory_space=pl.ANY),
        out_shape=jax.ShapeDtypeStruct((nb * bm, D_MODEL), F32),
        scratch_shapes=[pltpu.SemaphoreType.DMA(())],
        input_output_aliases={3: 0},
        compiler_params=_cparams("arbitrary"),
        name="moe_dispatch",
    )(dest_flat, h2a, h2b, jnp.zeros((nb * bm, D_MODEL), F32))

    out_sorted = pl.pallas_call(
        _moe_expert_body,
        grid_spec=pltpu.PrefetchScalarGridSpec(
            num_scalar_prefetch=2,
            grid=(nb,),
            in_specs=[pl.BlockSpec((bm, D_MODEL), lambda i, be_, nu_: (i, 0)),
                      pl.BlockSpec((1, D_MODEL, 2 * D_FF), lambda i, be_, nu_: (be_[i], 0, 0)),
                      pl.BlockSpec((1, 1, 2 * D_FF), lambda i, be_, nu_: (be_[i], 0, 0)),
                      pl.BlockSpec((1, D_FF, D_MODEL), lambda i, be_, nu_: (be_[i], 0, 0)),
                      pl.BlockSpec((1, 1, D_MODEL), lambda i, be_, nu_: (be_[i], 0, 0))],
            out_specs=pl.BlockSpec((bm, D_MODEL), lambda i, be_, nu_: (i, 0)),
            scratch_shapes=[pltpu.VMEM((D_MODEL, 2 * D_FF), BF16), pltpu.VMEM((D_FF, D_MODEL), BF16)]),
        out_shape=jax.ShapeDtypeStruct((nb * bm, D_MODEL), F32),
        compiler_params=_cparams("arbitrary"),
        name="moe_experts",
    )(block_e, n_used, xs, w1, b1p, w2, b2)

    def combine(x1_part, tile0):
        return pl.pallas_call(
            _moe_combine_body,
            grid=(x1_part.shape[0] // tt,),
            in_specs=[pl.BlockSpec((tt * TOP_K,), lambda i: (i + tile0,), memory_space=pltpu.SMEM),
                      pl.BlockSpec((tt, LANES), lambda i: (i + tile0, 0)),
                      pl.BlockSpec((tt, D_MODEL), row),
                      pl.BlockSpec((1, D_MODEL), const),
                      pl.BlockSpec(memory_space=pl.ANY)],
            out_specs=pl.BlockSpec((tt, D_MODEL), row),
            out_shape=jax.ShapeDtypeStruct(x1_part.shape, F32),
            scratch_shapes=[pltpu.VMEM((TOP_K, tt, D_MODEL), F32), pltpu.SemaphoreType.DMA(())],
            compiler_params=_cparams("arbitrary"),
            name="moe_combine",
        )(dest_flat, gw, x1_part, g_final.reshape(1, D_MODEL), out_sorted)

    return combine(x1a, 0), combine(x1b, tiles_a)


def _split_cols(w):
    outs, off = [], 0
    for wd in IN_WIDTHS:
        outs.append(w[:, off:off + wd])
        off += wd
    return outs


def _lane_row(v, off):
    return jnp.zeros((1, LANES), F32).at[0, off:off + v.shape[0]].set(v.astype(F32))


def kernel(x_prompt, x_sample, cache_k, cache_v, cache_idx_k, state_conv, state_ssm, cache_mem_k, cache_mem_v,
           page_table, mem_prompt, g_norm1, w_in, b_gate, conv_w, conv_b, dt_bias, a_log, d_skip, g_ssd_norm,
           g_mem, w_mem_kv, w_ssd_out, w_attn_out, w_mem_out, w_out, g_norm2, w_router, b_router, w_exp1,
           b_exp1, w_exp2, b_exp2, g_final):
    assert w_in.shape[0] == 1, "single-layer trunk"
    bp, lp, _ = x_prompt.shape
    bs, ls, _ = x_sample.shape
    np_, ns = bp * lp, bs * ls

    wz, wxbc, wdt, wq, wk, wv, wqi, wki, wwi, wqm, wgate = _split_cols(w_in[0])
    w_all = jnp.concatenate(
        [wxbc, wgate, wz, wq, wqm, wqi, wk, wv, wki, wdt, wwi,
         jnp.zeros((D_MODEL, W_ALL - OFF_SM - SM_WI - IDX_HEADS), F32)], axis=1).astype(BF16)
    dtb_row = _lane_row(dt_bias[0], SM_DT)
    aneg_row = _lane_row(-jnp.exp(a_log[0].astype(F32)), SM_DT)
    dsk_row = jnp.repeat(d_skip[0].astype(F32), SSD_HEAD_DIM).reshape(1, D_INNER)
    gs_row = g_ssd_norm[0].reshape(1, D_INNER)
    cb_row = conv_b[0].reshape(1, CONV_DIM)
    wr_pad = jnp.zeros((D_MODEL, LANES), F32).at[:, :N_EXPERTS].set(w_router[0])
    br_pad = jnp.full((1, LANES), NEG, F32).at[0, :N_EXPERTS].set(b_router[0])
    b1p = b_exp1[0].reshape(N_EXPERTS, D_FF // LANES, LANES, 2).transpose(0, 1, 3, 2).reshape(N_EXPERTS, 1, 2 * D_FF)
    b2 = b_exp2[0].reshape(N_EXPERTS, 1, D_MODEL)

    xp = x_prompt.reshape(np_, D_MODEL)
    xs = x_sample.reshape(ns, D_MODEL)
    zp = norm_matmul(xp, g_norm1[0], w_all, 1024, IN_PROJ_TN)
    zs = norm_matmul(xs, g_norm1[0], w_all, ns, IN_PROJ_TN)

    kv_p = norm_matmul(mem_prompt.reshape(bp * N_MEM, D_MODEL), g_mem[0], w_mem_kv[0].astype(BF16),
                       min(1024, bp * N_MEM), MEM_WIDTH)
    om_p = mem_attn(zp, bp, lp, kv_p, 0, kv_p, 1, 512)
    om_s = mem_attn(zs, bs, ls, cache_mem_k[0].reshape(bs * N_MEM, MEM_WIDTH), 0,
                    cache_mem_v[0].reshape(bs * N_MEM, MEM_WIDTH), 0, ls)

    conv_prev_p = jnp.zeros((bp, SUBLANES, CONV_DIM), F32)
    conv_prev_s = jnp.concatenate(
        [jnp.zeros((bs, SUBLANES - (CONV_WIDTH - 1), CONV_DIM), F32), state_conv[0]], axis=1)
    ssm0_p = jnp.zeros((bp, D_INNER, D_STATE), F32)
    ssm0_s = state_ssm[0].reshape(bs, D_INNER, D_STATE)
    ys_p, ssm_p = ssd(zp, bp, lp, conv_prev_p, ssm0_p, conv_w[0], cb_row, dtb_row, aneg_row, dsk_row, gs_row)
    ys_s, ssm_s = ssd(zs, bs, ls, conv_prev_s, ssm0_s, conv_w[0], cb_row, dtb_row, aneg_row, dsk_row, gs_row)

    oa_p = dsa_prompt(zp, bp, lp)
    oa_s = dsa_sample(zs, bs, ls, cache_k[0], cache_v[0], cache_idx_k[0], page_table)

    mw = (b_gate[0].reshape(1, -1), w_ssd_out[0].astype(BF16), w_attn_out[0].astype(BF16),
          w_mem_out[0].astype(BF16), w_out[0].astype(BF16), g_norm2[0].reshape(1, D_MODEL), wr_pad, br_pad)
    x1_p, h2_p, te_p, gw_p = merge(xp, zp, ys_p, oa_p, om_p, *mw, 512)
    x1_s, h2_s, te_s, gw_s = merge(xs, zs, ys_s, oa_s, om_s, *mw, ns)

    cat = lambda a, b: jnp.concatenate([a, b], axis=0)
    y_all = moe_and_final_norm(x1_p, x1_s, h2_p, h2_s, cat(te_p, te_s), cat(gw_p, gw_s),
                               w_exp1[0], b1p, w_exp2[0], b2, g_final)
    y_prompt = y_all[0].reshape(bp, lp, D_MODEL)
    y_sample = y_all[1].reshape(bs, ls, D_MODEL)

    def kvi(z, b, l):
        k = z[:, OFF_K:OFF_K + N_KV_HEADS * HEAD_DIM].reshape(1, b, l, N_KV_HEADS, HEAD_DIM)
        v = z[:, OFF_V:OFF_V + N_KV_HEADS * HEAD_DIM].reshape(1, b, l, N_KV_HEADS, HEAD_DIM)
        ki = z[:, OFF_SM + SM_KI:OFF_SM + SM_KI + IDX_DIM].reshape(1, b, l, IDX_DIM)
        conv = z.reshape(b, l, W_ALL)[:, l - (CONV_WIDTH - 1):, OFF_XBC:OFF_XBC + CONV_DIM][None]
        return k, v, ki, conv

    k_p, v_p, ki_p, conv_p = kvi(zp, bp, lp)
    k_s, v_s, ki_s, conv_s = kvi(zs, bs, ls)
    mk_p = kv_p[:, :MEM_WIDTH].reshape(1, bp, N_MEM, MEM_HEADS, MEM_HEAD_DIM)
    mv_p = kv_p[:, MEM_WIDTH:].reshape(1, bp, N_MEM, MEM_HEADS, MEM_HEAD_DIM)
    ssm_shape = (1, -1, SSD_HEADS, SSD_HEAD_DIM, D_STATE)
    return (y_prompt, y_sample, k_p, v_p, ki_p, conv_p, ssm_p.reshape(ssm_shape), mk_p, mv_p,
            k_s, v_s, ki_s, conv_s, ssm_s.reshape(ssm_shape))
```

```python
import functools

import numpy as np
import jax
import jax.numpy as jnp
from jax import lax
from jax.experimental import pallas as pl
from jax.experimental.pallas import tpu as pltpu

F32 = jnp.float32
BF16 = jnp.bfloat16
I32 = jnp.int32
HIGHEST = lax.Precision.HIGHEST

D_MODEL = 1024
D_INNER = 2048
SSD_HEAD_DIM = 64
SSD_HEADS = 32
SSD_GROUPS = 4
D_STATE = 128
CONV_WIDTH = 4
CONV_DIM = D_INNER + 2 * SSD_GROUPS * D_STATE
SSD_CHUNK = 128
N_HEADS = 16
N_KV_HEADS = 4
HEAD_DIM = 64
IDX_HEADS = 8
IDX_DIM = 64
TOPK_MAX = 256
N_MEM = 256
MEM_HEADS = 4
MEM_HEAD_DIM = 256
MEM_WIDTH = MEM_HEADS * MEM_HEAD_DIM
N_EXPERTS = 32
TOP_K = 4
D_FF = D_MODEL
SWIGLU_LIMIT = 7.0
SWIGLU_ALPHA = 1.702
N_BRANCH = 3
EPS = 1e-6
PAGE_SIZE = 128
IN_WIDTHS = (D_INNER, CONV_DIM, SSD_HEADS, N_HEADS * HEAD_DIM, N_KV_HEADS * HEAD_DIM, N_KV_HEADS * HEAD_DIM,
             IDX_HEADS * IDX_DIM, IDX_DIM, IDX_HEADS, MEM_WIDTH, N_BRANCH * D_MODEL)

LANES = 128
SUBLANES = 8
VMEM_LIMIT = 56 * 1024 * 1024

OFF_XBC = 0
OFF_GATE = OFF_XBC + CONV_DIM
OFF_Z = OFF_GATE + N_BRANCH * D_MODEL
OFF_Q = OFF_Z + D_INNER
OFF_QM = OFF_Q + N_HEADS * HEAD_DIM
OFF_QI = OFF_QM + MEM_WIDTH
OFF_K = OFF_QI + IDX_HEADS * IDX_DIM
OFF_V = OFF_K + N_KV_HEADS * HEAD_DIM
OFF_SM = OFF_V + N_KV_HEADS * HEAD_DIM
SM_KI = 0
SM_DT = SM_KI + IDX_DIM
SM_WI = SM_DT + SSD_HEADS
IN_PROJ_TN = 1280
W_ALL = OFF_SM + 2 * LANES

NEG = -1e30
INT_MIN = np.int32(-2 ** 31)
INT_MAX = np.int32(2 ** 31 - 1)

ATT_PASSES = 1
MOE_BM = 512
MOE_T = 256


def _cparams(*sem):
    return pltpu.CompilerParams(dimension_semantics=sem, vmem_limit_bytes=VMEM_LIMIT)


def _nt_dot(a, b):
    return lax.dot_general(a, b, (((1,), (1,)), ((), ())), preferred_element_type=F32)


def _float_key(x):
    bits = lax.bitcast_convert_type(x, I32)
    return jnp.where(bits < 0, bits ^ INT_MAX, bits)


def _norm_matmul_body(x_ref, g_ref, w_ref, o_ref, h_ref):
    @pl.when(pl.program_id(1) == 0)
    def _():
        x = x_ref[...]
        h = x * lax.rsqrt(jnp.mean(x * x, axis=-1, keepdims=True) + EPS)
        h_ref[...] = (h * g_ref[...]).astype(BF16)

    o_ref[...] = jnp.dot(h_ref[...], w_ref[...], preferred_element_type=F32)


def norm_matmul(x, g, w, tm, tn):
    n, d = x.shape
    wn = w.shape[1]
    return pl.pallas_call(
        _norm_matmul_body,
        grid=(n // tm, wn // tn),
        in_specs=[pl.BlockSpec((tm, d), lambda i, j: (i, 0)),
                  pl.BlockSpec((1, d), lambda i, j: (0, 0)),
                  pl.BlockSpec((d, tn), lambda i, j: (0, j))],
        out_specs=pl.BlockSpec((tm, tn), lambda i, j: (i, j)),
        out_shape=jax.ShapeDtypeStruct((n, wn), F32),
        scratch_shapes=[pltpu.VMEM((tm, d), BF16)],
        compiler_params=_cparams("parallel", "arbitrary"),
        name="norm_matmul",
    )(x, g.reshape(1, d), w)


def _softplus(x):
    return jnp.maximum(x, 0.0) + jnp.log1p(jnp.exp(-jnp.abs(x)))


def _silu(x):
    return x * jax.nn.sigmoid(x)


def _expand_heads(v, sel):
    hi = v.astype(BF16)
    r1 = v - hi.astype(F32)
    mid = r1.astype(BF16)
    lo = (r1 - mid.astype(F32)).astype(BF16)
    out = jnp.dot(hi, sel, preferred_element_type=F32)
    out = out + jnp.dot(mid, sel, preferred_element_type=F32)
    return out + jnp.dot(lo, sel, preferred_element_type=F32)


def _ssd_body(xbc_ref, z_ref, sm_ref, convp_ref, init_ref, cw_ref, cb_ref, dtb_ref, aneg_ref, dsk_ref, gs_ref, sel_ref,
              y_ref, st_ref, carry_ref, state_ref, *, rows_in, q):
    c = pl.program_id(1)
    taps = CONV_WIDTH - 1

    @pl.when(c == 0)
    def _():
        prev = convp_ref[0]
        for j in range(taps):
            acc0 = cw_ref[j:j + 1, :] * prev[SUBLANES - 1:SUBLANES, :]
            for m in range(1, j + 1):
                acc0 = acc0 + cw_ref[j - m:j - m + 1, :] * prev[SUBLANES - 1 - m:SUBLANES - m, :]
            carry_ref[j:j + 1, :] = acc0
        state_ref[...] = init_ref[0]

    x = xbc_ref[...]
    if rows_in < q:
        x = jnp.concatenate([x, jnp.zeros((q - rows_in, CONV_DIM), F32)], axis=0)

    first_row = lax.broadcasted_iota(I32, (q, CONV_DIM), 0) == 0
    shifted = None
    for j in range(taps):
        stage = cw_ref[j:j + 1, :] * x
        if shifted is not None:
            stage = stage + shifted
        shifted = jnp.where(first_row, carry_ref[j:j + 1, :], pltpu.roll(stage, 1, 0))
        carry_ref[j:j + 1, :] = stage[q - 1:q, :]
    xc = _silu(cw_ref[taps:taps + 1, :] * x + shifted + cb_ref[...])

    xs = xc[:, :D_INNER]
    gn = SSD_GROUPS * D_STATE
    bm = xc[:, D_INNER:D_INNER + gn].astype(BF16)
    cm = xc[:, D_INNER + gn:].astype(BF16)

    sm = sm_ref[...]
    zz = z_ref[...]
    if rows_in < q:
        sm = jnp.concatenate([sm, jnp.zeros((q - rows_in, LANES), F32)], axis=0)
        zz = jnp.concatenate([zz, jnp.zeros((q - rows_in, D_INNER), F32)], axis=0)
    row = lax.broadcasted_iota(I32, (q, LANES), 0)
    dt = _softplus(sm + dtb_ref[...])
    if rows_in < q:
        dt = jnp.where(row < rows_in, dt, 0.0)
    a = dt * aneg_ref[...]
    tri = (lax.broadcasted_iota(I32, (q, q), 0) >= lax.broadcasted_iota(I32, (q, q), 1)).astype(F32)
    a_cs = jnp.dot(tri, a, precision=HIGHEST, preferred_element_type=F32)
    a_t = a_cs.T
    a_last = a_cs[q - 1:q, :]
    dte = jnp.exp(a_last - a_cs)
    e_in = jnp.exp(a_cs)

    sel = sel_ref[...]
    xdt = xs * _expand_heads(dt, sel)
    xdt_bf = xdt.astype(BF16)
    xw_bf = (xs * _expand_heads(dt * dte, sel)).astype(BF16)
    ein_x = _expand_heads(e_in, sel)

    causal = lax.broadcasted_iota(I32, (q, q), 0) >= lax.broadcasted_iota(I32, (q, q), 1)
    lane = lax.broadcasted_iota(I32, (q, LANES), 1)
    hpg = SSD_HEADS // SSD_GROUPS
    gw = hpg * SSD_HEAD_DIM
    y_parts = []
    for g in range(SSD_GROUPS):
        bg = bm[:, g * D_STATE:(g + 1) * D_STATE]
        cg = cm[:, g * D_STATE:(g + 1) * D_STATE]
        cb = _nt_dot(cg, bg)
        m_h = []
        for e in range(hpg):
            h = g * hpg + e
            col = a_cs[:, SM_DT + h:SM_DT + h + 1]
            rw = a_t[SM_DT + h:SM_DT + h + 1, :]
            decay = jnp.exp(jnp.where(causal, col - rw, -jnp.inf))
            m_h.append((cb * decay).astype(BF16))
        yd = []
        for t in range(hpg // 2):
            pair = g * (hpg // 2) + t
            slab = xdt_bf[:, pair * LANES:(pair + 1) * LANES]
            ya = jnp.dot(m_h[2 * t], slab, preferred_element_type=F32)
            yb = jnp.dot(m_h[2 * t + 1], slab, preferred_element_type=F32)
            yd.append(jnp.where(lane < SSD_HEAD_DIM, ya, yb))
        s_old = state_ref[g * gw:(g + 1) * gw, :]
        y_off = _nt_dot(cg, s_old.astype(BF16)) * ein_x[:, g * gw:(g + 1) * gw]
        y_parts.append(jnp.concatenate(yd, axis=1) + y_off)
        new = lax.dot_general(xw_bf[:, g * gw:(g + 1) * gw], bg, (((0,), (0,)), ((), ())),
                              preferred_element_type=F32)
        for e in range(hpg):
            h = g * hpg + e
            dec = jnp.exp(a_t[SM_DT + h:SM_DT + h + 1, q - 1:q])
            lo = e * SSD_HEAD_DIM
            state_ref[h * SSD_HEAD_DIM:(h + 1) * SSD_HEAD_DIM, :] = (
                s_old[lo:lo + SSD_HEAD_DIM, :] * dec + new[lo:lo + SSD_HEAD_DIM, :])

    y = jnp.concatenate(y_parts, axis=1) + dsk_ref[...] * xs
    y = y * _silu(zz)
    outs = []
    for g in range(SSD_GROUPS):
        yg = y[:, g * gw:(g + 1) * gw]
        outs.append(yg * lax.rsqrt(jnp.mean(yg * yg, axis=-1, keepdims=True) + EPS))
    y = jnp.concatenate(outs, axis=1) * gs_ref[...]
    y_ref[...] = y[:rows_in].astype(BF16)

    @pl.when(c == pl.num_programs(1) - 1)
    def _():
        st_ref[0] = state_ref[...]


def ssd(zall, batch, seq, conv_prev8, ssm_init, conv_w, conv_b, dtb_row, aneg_row, dsk_row, gs_row):
    q = SSD_CHUNK
    rows_in = min(seq, q)
    nch = seq // rows_in
    sel = (jnp.arange(LANES)[:, None] - SM_DT == jnp.arange(D_INNER)[None, :] // SSD_HEAD_DIM).astype(BF16)
    row_map = lambda b, c: (b * nch + c)
    const2 = lambda b, c: (0, 0)
    body = functools.partial(_ssd_body, rows_in=rows_in, q=q)
    return pl.pallas_call(
        body,
        grid=(batch, nch),
        in_specs=[pl.BlockSpec((rows_in, CONV_DIM), lambda b, c: (row_map(b, c), OFF_XBC // CONV_DIM)),
                  pl.BlockSpec((rows_in, D_INNER), lambda b, c: (row_map(b, c), OFF_Z // D_INNER)),
                  pl.BlockSpec((rows_in, LANES), lambda b, c: (row_map(b, c), OFF_SM // LANES)),
                  pl.BlockSpec((1, SUBLANES, CONV_DIM), lambda b, c: (b, 0, 0)),
                  pl.BlockSpec((1, D_INNER, D_STATE), lambda b, c: (b, 0, 0)),
                  pl.BlockSpec((CONV_WIDTH, CONV_DIM), const2),
                  pl.BlockSpec((1, CONV_DIM), const2),
                  pl.BlockSpec((1, LANES), const2),
                  pl.BlockSpec((1, LANES), const2),
                  pl.BlockSpec((1, D_INNER), const2),
                  pl.BlockSpec((1, D_INNER), const2),
                  pl.BlockSpec((LANES, D_INNER), const2)],
        out_specs=[pl.BlockSpec((rows_in, D_INNER), lambda b, c: (row_map(b, c), 0)),
                   pl.BlockSpec((1, D_INNER, D_STATE), lambda b, c: (b, 0, 0))],
        out_shape=[jax.ShapeDtypeStruct((batch * seq, D_INNER), BF16),
                   jax.ShapeDtypeStruct((batch, D_INNER, D_STATE), F32)],
        scratch_shapes=[pltpu.VMEM((SUBLANES, CONV_DIM), F32),
                        pltpu.VMEM((D_INNER, D_STATE), F32)],
        compiler_params=_cparams("parallel", "arbitrary"),
        name="ssd",
    )(zall, zall, zall, conv_prev8, ssm_init, conv_w, conv_b, dtb_row, aneg_row, dsk_row, gs_row, sel)


def _mem_attn_body(q_ref, k_ref, v_ref, o_ref):
    for h in range(MEM_HEADS):
        sl = slice(h * MEM_HEAD_DIM, (h + 1) * MEM_HEAD_DIM)
        s = _nt_dot(q_ref[:, sl].astype(BF16), k_ref[:, sl].astype(BF16)) * (MEM_HEAD_DIM ** -0.5)
        m = jnp.max(s, axis=-1, keepdims=True)
        p = jnp.exp(s - m)
        p = p / jnp.sum(p, axis=-1, keepdims=True)
        o = jnp.dot(p.astype(BF16), v_ref[:, sl].astype(BF16), preferred_element_type=F32)
        o_ref[:, sl] = o.astype(BF16)


def mem_attn(zall, batch, seq, k_arr, k_col, v_arr, v_col, tm):
    nt = seq // tm
    return pl.pallas_call(
        _mem_attn_body,
        grid=(batch, nt),
        in_specs=[pl.BlockSpec((tm, MEM_WIDTH), lambda b, i: (b * nt + i, OFF_QM // MEM_WIDTH)),
                  pl.BlockSpec((N_MEM, MEM_WIDTH), lambda b, i: (b, k_col)),
                  pl.BlockSpec((N_MEM, MEM_WIDTH), lambda b, i: (b, v_col))],
        out_specs=pl.BlockSpec((tm, MEM_WIDTH), lambda b, i: (b * nt + i, 0)),
        out_shape=jax.ShapeDtypeStruct((batch * seq, MEM_WIDTH), BF16),
        compiler_params=_cparams("parallel", "arbitrary"),
        name="mem_attn",
    )(zall, k_arr, v_arr)


def _kth_largest_key(count_ge, shape, n_sel):
    def bit_body(t, ans):
        cand = ans | jnp.left_shift(jnp.int32(1), 31 - t)
        cnt = count_ge(cand ^ INT_MIN)
        return jnp.where(cnt >= n_sel, cand, ans)

    ans = lax.fori_loop(0, 32, bit_body, jnp.zeros(shape, I32))
    return ans ^ INT_MIN


def _tie_cut(count_eq_below, need, shape, nbits):
    def bit_body(t, lo):
        cand = lo | jnp.left_shift(jnp.int32(1), nbits - 1 - t)
        cnt = count_eq_below(cand)
        return jnp.where(cnt < need, cand, lo)

    return lax.fori_loop(0, nbits, bit_body, jnp.zeros(shape, I32))


def _select_bias(key, kpos, thr, cut, visible):
    sel = (key > thr) | ((key == thr) & (kpos <= cut))
    return jnp.where(sel & visible, 0.0, NEG)


def _dsa_prompt_body(q_ref, qi_ref, smq_ref, k_ref, v_ref, smk_ref, o_ref,
                     kh_ref, vt_ref, kis_ref, qt2_ref, qit_ref, keys_ref, bias_ref, ot_ref, *, tq, kc, seq, n_sel):
    i = pl.program_id(1)

    @pl.when(i == 0)
    def _():
        def cast_rows(r, carry):
            rs = pl.ds(pl.multiple_of(r * kc, kc), kc)
            kk = k_ref[rs, :]
            for h in range(N_KV_HEADS):
                kh_ref[h, rs, :] = kk[:, h * HEAD_DIM:(h + 1) * HEAD_DIM].astype(BF16)
            vt_ref[r] = v_ref[rs, :].T.astype(BF16)
            kis_ref[rs, :] = smk_ref[rs, SM_KI:SM_KI + IDX_DIM].astype(BF16)
            return carry
        lax.fori_loop(0, seq // kc, cast_rows, 0)

    nkc = (i * tq + tq - 1) // kc + 1
    qpos = i * tq + lax.broadcasted_iota(I32, (kc, tq), 1)
    krow = lax.broadcasted_iota(I32, (kc, tq), 0)

    qt = (q_ref[...] * (HEAD_DIM ** -0.5)).T.astype(BF16)
    for h in range(N_HEADS):
        qt2_ref[:, h * tq:(h + 1) * tq] = qt[h * HEAD_DIM:(h + 1) * HEAD_DIM, :]
    qit = (qi_ref[...] * (IDX_DIM ** -0.5)).T.astype(BF16)
    for h in range(IDX_HEADS):
        qit_ref[:, h * tq:(h + 1) * tq] = qit[h * IDX_DIM:(h + 1) * IDX_DIM, :]
    wt = smq_ref[...].T[SM_WI:SM_WI + IDX_HEADS, :] * (IDX_HEADS ** -0.5)

    def score_chunk(c, carry):
        ks = pl.ds(pl.multiple_of(c * kc, kc), kc)
        d = jnp.dot(kis_ref[ks, :], qit_ref[...], preferred_element_type=F32)
        sc = jnp.zeros((kc, tq), F32)
        for h in range(IDX_HEADS):
            sc = sc + jnp.maximum(d[:, h * tq:(h + 1) * tq], 0.0) * wt[h:h + 1, :]
        key = _float_key(sc + 0.0)
        keys_ref[c] = jnp.where(c * kc + krow <= qpos, key, INT_MIN)
        return carry
    lax.fori_loop(0, nkc, score_chunk, 0)

    def count(pred):
        def body(c, acc):
            return acc + jnp.where(pred(keys_ref[c], c * kc + krow), 1.0, 0.0)
        acc = lax.fori_loop(0, nkc, body, jnp.zeros((kc, tq), F32))
        return jnp.sum(acc, axis=0, keepdims=True)

    vec = (1, tq)
    thr = _kth_largest_key(lambda cand: count(lambda key, kpos: key >= cand), vec, n_sel)
    n_gt = count(lambda key, kpos: key > thr)
    n_eq = count(lambda key, kpos: key == thr)
    need = n_sel - n_gt
    excess = jnp.max(jnp.where((n_eq > need) & (thr != INT_MIN), 1.0, 0.0))
    nbits = max(1, int(seq - 1).bit_length())
    cut = lax.cond(
        excess > 0.0,
        lambda: _tie_cut(lambda cand: count(lambda key, kpos: (key == thr) & (kpos < cand)), need, vec, nbits),
        lambda: jnp.full(vec, INT_MAX, I32))

    def bias_chunk(c, carry):
        kpos = c * kc + krow
        bias_ref[c] = _select_bias(keys_ref[c], kpos, thr, cut, kpos <= qpos)
        return carry
    lax.fori_loop(0, nkc, bias_chunk, 0)

    grp = N_HEADS // N_KV_HEADS
    kv_per_pass = N_KV_HEADS // ATT_PASSES
    hpp = kv_per_pass * grp
    for ps_i in range(ATT_PASSES):
        kv0 = ps_i * kv_per_pass

        def att_chunk(c, carry, kv0=kv0):
            ms, ls, accs = carry
            ks = pl.ds(pl.multiple_of(c * kc, kc), kc)
            bias = bias_ref[c]
            s4 = [jnp.dot(kh_ref[kv0 + j, ks, :], qt2_ref[:, (kv0 + j) * grp * tq:(kv0 + j + 1) * grp * tq],
                          preferred_element_type=F32) for j in range(kv_per_pass)]
            ms_n, ls_n, accs_n = [], [], []
            for h in range(hpp):
                s = s4[h // grp][:, (h % grp) * tq:(h % grp + 1) * tq] + bias
                m_new = jnp.maximum(ms[h], jnp.max(s, axis=0, keepdims=True))
                p = jnp.exp(s - m_new)
                alpha = jnp.exp(ms[h] - m_new)
                ms_n.append(m_new)
                ls_n.append(alpha * ls[h] + jnp.sum(p, axis=0, keepdims=True))
                kh = kv0 + h // grp
                vtc = vt_ref[c, kh * HEAD_DIM:(kh + 1) * HEAD_DIM, :]
                accs_n.append(alpha * accs[h] + jnp.dot(vtc, p.astype(BF16), preferred_element_type=F32))
            return tuple(ms_n), tuple(ls_n), tuple(accs_n)

        init = (tuple(jnp.full(vec, NEG, F32) for _ in range(hpp)),
                tuple(jnp.zeros(vec, F32) for _ in range(hpp)),
                tuple(jnp.zeros((HEAD_DIM, tq), F32) for _ in range(hpp)))
        _, ls, accs = lax.fori_loop(0, nkc, att_chunk, init)
        for h in range(hpp):
            hh = kv0 * grp + h
            ot_ref[hh * HEAD_DIM:(hh + 1) * HEAD_DIM, :] = accs[h] / ls[h]
    o_ref[...] = ot_ref[...].T.astype(BF16)


def dsa_prompt(zall, batch, seq):
    tq = 128
    kc = 256
    nq = seq // tq
    n_sel = min(TOPK_MAX, seq // 4)
    kvw = N_KV_HEADS * HEAD_DIM
    qw = N_HEADS * HEAD_DIM
    qiw = IDX_HEADS * IDX_DIM
    body = functools.partial(_dsa_prompt_body, tq=tq, kc=kc, seq=seq, n_sel=n_sel)
    return pl.pallas_call(
        body,
        grid=(batch, nq),
        in_specs=[pl.BlockSpec((tq, qw), lambda b, i: (b * nq + i, OFF_Q // qw)),
                  pl.BlockSpec((tq, qiw), lambda b, i: (b * nq + i, OFF_QI // qiw)),
                  pl.BlockSpec((tq, LANES), lambda b, i: (b * nq + i, OFF_SM // LANES)),
                  pl.BlockSpec((seq, kvw), lambda b, i: (b, OFF_K // kvw)),
                  pl.BlockSpec((seq, kvw), lambda b, i: (b, OFF_V // kvw)),
                  pl.BlockSpec((seq, LANES), lambda b, i: (b, OFF_SM // LANES))],
        out_specs=pl.BlockSpec((tq, qw), lambda b, i: (b * nq + i, 0)),
        out_shape=jax.ShapeDtypeStruct((batch * seq, qw), BF16),
        scratch_shapes=[pltpu.VMEM((N_KV_HEADS, seq, HEAD_DIM), BF16),
                        pltpu.VMEM((seq // kc, kvw, kc), BF16),
                        pltpu.VMEM((seq, IDX_DIM), BF16),
                        pltpu.VMEM((HEAD_DIM, N_HEADS * tq), BF16),
                        pltpu.VMEM((IDX_DIM, IDX_HEADS * tq), BF16),
                        pltpu.VMEM((seq // kc, kc, tq), I32),
                        pltpu.VMEM((seq // kc, kc, tq), F32),
                        pltpu.VMEM((qw, tq), F32)],
        compiler_params=_cparams("parallel", "arbitrary"),
        name="dsa_prompt",
    )(zall, zall, zall, zall, zall, zall)


def _dsa_s_score_body(pt_ref, qs_ref, w_ref, *refs, pg):
    ki_refs, o_ref = refs[:pg], refs[pg]
    qs = qs_ref[0]
    wcol = w_ref[0] * (IDX_HEADS ** -0.5)
    t = qs.shape[0] // IDX_HEADS
    for p in range(pg):
        d = jnp.dot(qs, ki_refs[p][0].astype(BF16), preferred_element_type=F32)
        r = jnp.maximum(d * (IDX_DIM ** -0.5), 0.0) * wcol
        sc = r[0:t, :]
        for h in range(1, IDX_HEADS):
            sc = sc + r[h * t:(h + 1) * t, :]
        o_ref[0, :, p * PAGE_SIZE:(p + 1) * PAGE_SIZE] = sc + 0.0


def _dsa_s_select_body(sc_ref, qs_ref, w_ref, smn_ref, o_ref, *, t, past, n_sel, nbat):
    rows = nbat * t
    sc_new = []
    for b in range(nbat):
        wcol = w_ref[b] * (IDX_HEADS ** -0.5)
        ki_new = smn_ref[b * t:(b + 1) * t, SM_KI:SM_KI + IDX_DIM].astype(BF16)
        ki_new = jnp.concatenate([ki_new, jnp.zeros((LANES - t, IDX_DIM), BF16)], axis=0)
        d = _nt_dot(qs_ref[b], ki_new)
        r = jnp.maximum(d * (IDX_DIM ** -0.5), 0.0) * wcol
        sc = r[0:t, :]
        for h in range(1, IDX_HEADS):
            sc = sc + r[h * t:(h + 1) * t, :]
        sc_new.append(sc)
    sc_new = jnp.concatenate(sc_new, axis=0)
    lane_t = lax.broadcasted_iota(I32, (t, LANES), 1)
    vis_t = lane_t <= lax.broadcasted_iota(I32, (t, LANES), 0)
    vis_n = jnp.concatenate([vis_t] * nbat, axis=0)
    lane_n = lax.broadcasted_iota(I32, (rows, LANES), 1)
    key_n = jnp.where(vis_n, _float_key(sc_new + 0.0), INT_MIN)
    key_p = _float_key(sc_ref[...].reshape(rows, past))
    pos_p = lax.broadcasted_iota(I32, (rows, past), 1)
    pos_n = past + lane_n

    def count(pred):
        return (jnp.sum(jnp.where(pred(key_p, pos_p), 1.0, 0.0), axis=1, keepdims=True)
                + jnp.sum(jnp.where(pred(key_n, pos_n), 1.0, 0.0), axis=1, keepdims=True))

    vec = (rows, 1)
    thr = _kth_largest_key(lambda cand: count(lambda key, kpos: key >= cand), vec, n_sel)
    need = n_sel - count(lambda key, kpos: key > thr)
    n_eq = count(lambda key, kpos: key == thr)
    excess = jnp.max(jnp.where((n_eq > need) & (thr != INT_MIN), 1.0, 0.0))
    nbits = max(1, int(past + t - 1).bit_length())
    cut = lax.cond(
        excess > 0.0,
        lambda: _tie_cut(lambda cand: count(lambda key, kpos: (key == thr) & (kpos < cand)), need, vec, nbits),
        lambda: jnp.full(vec, INT_MAX, I32))
    o_ref[:, :, 0:past] = _select_bias(key_p, pos_p, thr, cut, pos_p >= 0).reshape(nbat, t, past)
    o_ref[:, :, past:past + LANES] = _select_bias(key_n, pos_n, thr, cut, vis_n).reshape(nbat, t, LANES)


def _dsa_s_attn_body(pt_ref, qbd_ref, bias_ref, biasn_ref, kn_ref, vn_ref, *refs, pg, t):
    k_refs, v_refs = refs[:pg], refs[pg:2 * pg]
    o_ref, m_ref, l_ref, acc_ref = refs[2 * pg:]
    j = pl.program_id(1)
    rows = qbd_ref.shape[1]
    rep = rows // t

    @pl.when(j == 0)
    def _():
        m_ref[...] = jnp.full(m_ref.shape, NEG, F32)
        l_ref[...] = jnp.zeros(l_ref.shape, F32)
        acc_ref[...] = jnp.zeros(acc_ref.shape, F32)

    qbd = qbd_ref[0]

    def update(kt, vt, bias):
        s = jnp.dot(qbd, kt, preferred_element_type=F32) * (HEAD_DIM ** -0.5) + jnp.concatenate([bias] * rep, axis=0)
        m = m_ref[...]
        m_new = jnp.maximum(m, jnp.max(s, axis=1, keepdims=True))
        p = jnp.exp(s - m_new)
        alpha = jnp.exp(m - m_new)
        l_ref[...] = alpha * l_ref[...] + jnp.sum(p, axis=1, keepdims=True)
        acc_ref[...] = alpha * acc_ref[...] + _nt_dot(p.astype(BF16), vt)
        m_ref[...] = m_new

    kt = jnp.concatenate([r[0] for r in k_refs], axis=1).astype(BF16)
    vt = jnp.concatenate([r[0] for r in v_refs], axis=1).astype(BF16)
    update(kt, vt, bias_ref[0])

    @pl.when(j == pl.num_programs(1) - 1)
    def _():
        kvw = N_KV_HEADS * HEAD_DIM
        zpad = jnp.zeros((LANES - t, kvw), F32)
        update(jnp.concatenate([kn_ref[...], zpad], axis=0).T.astype(BF16),
               jnp.concatenate([vn_ref[...], zpad], axis=0).T.astype(BF16), biasn_ref[0])
        o_ref[0] = acc_ref[...] / l_ref[...]


def dsa_sample(zs, batch, t, cache_k, cache_v, cache_ki, page_table):
    n_pages = page_table.shape[1]
    past = n_pages * PAGE_SIZE
    n_sel = min(TOPK_MAX, (past + t) // 4)
    pg = 16 if n_pages % 16 == 0 else 8
    nj = n_pages // pg
    n_pool = cache_k.shape[0]
    kvw = N_KV_HEADS * HEAD_DIM
    grp = N_HEADS // N_KV_HEADS

    qi = zs[:, OFF_QI:OFF_QI + IDX_HEADS * IDX_DIM].reshape(batch, t, IDX_HEADS, IDX_DIM)
    qs = jnp.transpose(qi, (0, 2, 1, 3)).reshape(batch, IDX_HEADS * t, IDX_DIM).astype(BF16)
    wi = zs[:, OFF_SM + SM_WI:OFF_SM + SM_WI + IDX_HEADS].reshape(batch, t, IDX_HEADS)
    wcol = jnp.transpose(wi, (0, 2, 1)).reshape(batch, IDX_HEADS * t, 1)

    def page_spec(shape, p):
        return pl.BlockSpec(shape, lambda b, j, pt: (pt[b, j * pg + p],) + (0,) * (len(shape) - 1))

    scores = pl.pallas_call(
        functools.partial(_dsa_s_score_body, pg=pg),
        grid_spec=pltpu.PrefetchScalarGridSpec(
            num_scalar_prefetch=1,
            grid=(batch, nj),
            in_specs=[pl.BlockSpec((1, IDX_HEADS * t, IDX_DIM), lambda b, j, pt: (b, 0, 0)),
                      pl.BlockSpec((1, IDX_HEADS * t, 1), lambda b, j, pt: (b, 0, 0))]
                     + [page_spec((1, IDX_DIM, PAGE_SIZE), p) for p in range(pg)],
            out_specs=pl.BlockSpec((1, t, pg * PAGE_SIZE), lambda b, j, pt: (b, 0, j))),
        out_shape=jax.ShapeDtypeStruct((batch, t, past), F32),
        compiler_params=_cparams("parallel", "arbitrary"),
        name="dsa_sample_scores",
    )(page_table, qs, wcol, *([jnp.swapaxes(cache_ki, 1, 2)] * pg))

    nbat = 4 if (batch % 4 == 0 and t % SUBLANES == 0) else 1
    bias = pl.pallas_call(
        functools.partial(_dsa_s_select_body, t=t, past=past, n_sel=n_sel, nbat=nbat),
        grid=(batch // nbat,),
        in_specs=[pl.BlockSpec((nbat, t, past), lambda b: (b, 0, 0)),
                  pl.BlockSpec((nbat, IDX_HEADS * t, IDX_DIM), lambda b: (b, 0, 0)),
                  pl.BlockSpec((nbat, IDX_HEADS * t, 1), lambda b: (b, 0, 0)),
                  pl.BlockSpec((nbat * t, LANES), lambda b: (b, OFF_SM // LANES))],
        out_specs=pl.BlockSpec((nbat, t, past + LANES), lambda b: (b, 0, 0)),
        out_shape=jax.ShapeDtypeStruct((batch, t, past + LANES), F32),
        compiler_params=_cparams("parallel"),
        name="dsa_sample_select",
    )(scores, qs, wcol, zs)

    q = zs[:, OFF_Q:OFF_Q + N_HEADS * HEAD_DIM].reshape(batch, t, N_KV_HEADS, grp, HEAD_DIM)
    q = jnp.transpose(q, (0, 2, 3, 1, 4))
    eye = jnp.eye(N_KV_HEADS, dtype=F32)
    qbd = (q[:, :, :, :, None, :] * eye[None, :, None, None, :, None]).reshape(batch, N_HEADS * t, kvw).astype(BF16)

    ck = jnp.transpose(cache_k, (0, 2, 3, 1)).reshape(n_pool, kvw, PAGE_SIZE)
    cv = jnp.transpose(cache_v, (0, 2, 3, 1)).reshape(n_pool, kvw, PAGE_SIZE)
    rows = N_HEADS * t
    out = pl.pallas_call(
        functools.partial(_dsa_s_attn_body, pg=pg, t=t),
        grid_spec=pltpu.PrefetchScalarGridSpec(
            num_scalar_prefetch=1,
            grid=(batch, nj),
            in_specs=[pl.BlockSpec((1, rows, kvw), lambda b, j, pt: (b, 0, 0)),
                      pl.BlockSpec((1, t, pg * PAGE_SIZE), lambda b, j, pt: (b, 0, j)),
                      pl.BlockSpec((1, t, LANES), lambda b, j, pt: (b, 0, past // LANES)),
                      pl.BlockSpec((t, kvw), lambda b, j, pt: (b, OFF_K // kvw)),
                      pl.BlockSpec((t, kvw), lambda b, j, pt: (b, OFF_V // kvw))]
                     + [page_spec((1, kvw, PAGE_SIZE), p) for p in range(pg)]
                     + [page_spec((1, kvw, PAGE_SIZE), p) for p in range(pg)],
            out_specs=pl.BlockSpec((1, rows, kvw), lambda b, j, pt: (b, 0, 0)),
            scratch_shapes=[pltpu.VMEM((rows, 1), F32), pltpu.VMEM((rows, 1), F32), pltpu.VMEM((rows, kvw), F32)]),
        out_shape=jax.ShapeDtypeStruct((batch, rows, kvw), F32),
        compiler_params=_cparams("parallel", "arbitrary"),
        name="dsa_sample_attn",
    )(page_table, qbd, bias, bias, zs, zs, *([ck] * pg), *([cv] * pg))

    o = out.reshape(batch, N_KV_HEADS, grp, t, N_KV_HEADS, HEAD_DIM)
    o = jnp.stack([o[:, kh, :, :, kh, :] for kh in range(N_KV_HEADS)], axis=1)
    return jnp.transpose(o, (0, 3, 1, 2, 4)).reshape(batch * t, N_HEADS * HEAD_DIM).astype(BF16)


def _merge_body(x_ref, gate_ref, ys_ref, oa_ref, om_ref, bg_ref, ws_ref, wa_ref, wm_ref, wo_ref, g2_ref, wr_ref, br_ref,
                x1_ref, h2_ref, te_ref, gw_ref):
    gates = jax.nn.sigmoid(gate_ref[...] + bg_ref[...])
    merged = (gates[:, 0:D_MODEL] * jnp.dot(ys_ref[...], ws_ref[...], preferred_element_type=F32)
              + gates[:, D_MODEL:2 * D_MODEL] * jnp.dot(oa_ref[...], wa_ref[...], preferred_element_type=F32)
              + gates[:, 2 * D_MODEL:] * jnp.dot(om_ref[...], wm_ref[...], preferred_element_type=F32))
    x1 = x_ref[...] + jnp.dot(merged.astype(BF16), wo_ref[...], preferred_element_type=F32)
    x1_ref[...] = x1
    h2 = x1 * lax.rsqrt(jnp.mean(x1 * x1, axis=-1, keepdims=True) + EPS)
    h2 = h2 * g2_ref[...]
    h2_ref[...] = h2
    h_hi = h2.astype(BF16)
    h_mid = (h2 - h_hi.astype(F32)).astype(BF16)
    wr_hi = wr_ref[0]
    logits = (jnp.dot(h_hi, wr_hi, preferred_element_type=F32) + jnp.dot(h_hi, wr_ref[1], preferred_element_type=F32)
              + jnp.dot(h_mid, wr_hi, preferred_element_type=F32)) + br_ref[...]
    lane = lax.broadcasted_iota(I32, logits.shape, 1)
    te = jnp.zeros(logits.shape, I32)
    tv = []
    for k in range(TOP_K):
        m = jnp.max(logits, axis=1, keepdims=True)
        idx = jnp.min(jnp.where(logits == m, lane, LANES), axis=1, keepdims=True)
        te = jnp.where(lane == k, idx, te)
        tv.append(m)
        logits = jnp.where(lane == idx, -jnp.inf, logits)
    ex = [jnp.exp(v - tv[0]) for v in tv]
    den = ex[0] + ex[1] + ex[2] + ex[3]
    gw = jnp.zeros(logits.shape, F32)
    for k in range(TOP_K):
        gw = jnp.where(lane == k, ex[k] / den, gw)
    te_ref[...] = te
    gw_ref[...] = gw


def merge(x, zall, ys, oa, om, bg, ws, wa, wm, wo, g2, wr, br, tm):
    n = x.shape[0]
    gw3 = N_BRANCH * D_MODEL
    row = lambda i: (i, 0)
    const = lambda i: (0, 0)
    return pl.pallas_call(
        _merge_body,
        grid=(n // tm,),
        in_specs=[pl.BlockSpec((tm, D_MODEL), row),
                  pl.BlockSpec((tm, gw3), lambda i: (i, OFF_GATE // gw3)),
                  pl.BlockSpec((tm, D_INNER), row),
                  pl.BlockSpec((tm, N_HEADS * HEAD_DIM), row),
                  pl.BlockSpec((tm, MEM_WIDTH), row),
                  pl.BlockSpec((1, gw3), const),
                  pl.BlockSpec((D_INNER, D_MODEL), const),
                  pl.BlockSpec((N_HEADS * HEAD_DIM, D_MODEL), const),
                  pl.BlockSpec((MEM_WIDTH, D_MODEL), const),
                  pl.BlockSpec((D_MODEL, D_MODEL), const),
                  pl.BlockSpec((1, D_MODEL), const),
                  pl.BlockSpec((2, D_MODEL, LANES), lambda i: (0, 0, 0)),
                  pl.BlockSpec((1, LANES), const)],
        out_specs=[pl.BlockSpec((tm, D_MODEL), row), pl.BlockSpec((tm, D_MODEL), row),
                   pl.BlockSpec((tm, LANES), row), pl.BlockSpec((tm, LANES), row)],
        out_shape=[jax.ShapeDtypeStruct((n, D_MODEL), F32), jax.ShapeDtypeStruct((n, D_MODEL), F32),
                   jax.ShapeDtypeStruct((n, LANES), I32), jax.ShapeDtypeStruct((n, LANES), F32)],
        compiler_params=_cparams("parallel"),
        name="merge",
    )(x, zall, ys, oa, om, bg, ws, wa, wm, wo, g2, wr, br)


def _moe_pos_body(te_ref, pos_ref, cnt_ref, carry_ref):
    i = pl.program_id(0)
    tt = te_ref.shape[0]

    @pl.when(i == 0)
    def _():
        carry_ref[...] = jnp.zeros(carry_ref.shape, F32)

    te = te_ref[...]
    lane = lax.broadcasted_iota(I32, (tt, LANES), 1)
    onehot = [lane == te[:, k:k + 1] for k in range(TOP_K)]
    msum = jnp.zeros((tt, LANES), F32)
    for k in range(TOP_K):
        msum = msum + jnp.where(onehot[k], 1.0, 0.0)
    strict = (lax.broadcasted_iota(I32, (tt, tt), 0) > lax.broadcasted_iota(I32, (tt, tt), 1))
    prefix = jnp.dot(jnp.where(strict, 1.0, 0.0).astype(BF16), msum.astype(BF16), preferred_element_type=F32)
    prefix = prefix + carry_ref[0:1, :]
    pos = jnp.zeros((tt, LANES), F32)
    for k in range(TOP_K):
        pk = jnp.sum(jnp.where(onehot[k], prefix, 0.0), axis=1, keepdims=True)
        pos = jnp.where(lane == k, pk, pos)
    pos_ref[...] = pos
    carry_ref[...] = carry_ref[...] + jnp.sum(msum, axis=0, keepdims=True)
    cnt_ref[...] = carry_ref[...]


def _moe_dest_body(te_ref, pos_ref, cnt_ref, dest_ref, be_ref, nu_ref, start_ref, *, bm):
    tt = te_ref.shape[0]

    @pl.when(pl.program_id(0) == 0)
    def _():
        cnt = cnt_ref[...]
        padded = jnp.floor((cnt + (bm - 1)) * (1.0 / bm)) * bm
        upper = (lax.broadcasted_iota(I32, (LANES, LANES), 0) < lax.broadcasted_iota(I32, (LANES, LANES), 1))
        pad_start = jnp.dot(padded, jnp.where(upper, 1.0, 0.0), precision=HIGHEST, preferred_element_type=F32)
        start_ref[...] = pad_start
        pad_end = pad_start + padded
        nb = be_ref.shape[0]
        bstart = (lax.broadcasted_iota(I32, (nb, LANES), 0) * bm).astype(F32)
        lane_b = lax.broadcasted_iota(I32, (nb, LANES), 1)
        done = jnp.where((pad_end[0:1, :] <= bstart) & (lane_b < N_EXPERTS), 1.0, 0.0)
        be = jnp.minimum(jnp.sum(done, axis=1, keepdims=True), N_EXPERTS - 1.0)
        be_ref[...] = jnp.broadcast_to(be, (nb, LANES)).astype(I32)
        total = jnp.sum(padded[0:1, :], axis=1, keepdims=True)
        nu_ref[...] = jnp.broadcast_to(total * (1.0 / bm), nu_ref.shape).astype(I32)

    pad_start = start_ref[0:1, :]
    te = te_ref[...]
    pos = pos_ref[...]
    lane = lax.broadcasted_iota(I32, (tt, LANES), 1)
    dest = jnp.zeros((tt, LANES), F32)
    for k in range(TOP_K):
        ps = jnp.sum(jnp.where(lane == te[:, k:k + 1], pad_start, 0.0), axis=1, keepdims=True)
        dest = jnp.where(lane == k, ps + pos[:, k:k + 1], dest)
    dest_ref[...] = dest.astype(I32)


def _moe_dispatch_body(dest_ref, ha_ref, hb_ref, xs_in_ref, xs_ref, sem, *, tiles_a):
    del xs_in_ref
    tt = ha_ref.shape[0]

    def scatter_rows(h_ref):
        def copy(r, k):
            d = dest_ref[r * TOP_K + k]
            return pltpu.make_async_copy(h_ref.at[pl.ds(r, 1), :], xs_ref.at[pl.ds(d, 1), :], sem)

        def issue(r, carry):
            for k in range(TOP_K):
                copy(r, k).start()
            return carry
        lax.fori_loop(0, tt, issue, 0)

        def drain(r, carry):
            for k in range(TOP_K):
                copy(r, k).wait()
            return carry
        lax.fori_loop(0, tt, drain, 0)

    @pl.when(pl.program_id(0) < tiles_a)
    def _():
        scatter_rows(ha_ref)

    @pl.when(pl.program_id(0) >= tiles_a)
    def _():
        scatter_rows(hb_ref)


def _moe_expert_body(be_ref, nu_ref, xs_ref, w1_ref, b1_ref, w2_ref, b2_ref, o_ref, w1s_ref, w2s_ref):
    i = pl.program_id(0)
    used = i < nu_ref[0]
    e = be_ref[i]
    prev = be_ref[jnp.maximum(i - 1, 0)]
    half = LANES

    @pl.when(used & ((i == 0) | (e != prev)))
    def _():
        r = lax.broadcasted_iota(I32, (2 * half, 2 * half), 0)
        c = lax.broadcasted_iota(I32, (2 * half, 2 * half), 1)
        src_col = jnp.where(c < half, 2 * c, 2 * (c - half) + 1)
        perm = jnp.where(r == src_col, 1.0, 0.0).astype(BF16)
        for j in range(2 * D_FF // (2 * half)):
            sl = slice(j * 2 * half, (j + 1) * 2 * half)
            w1s_ref[:, sl] = jnp.dot(w1_ref[0, :, sl].astype(BF16), perm, preferred_element_type=F32).astype(BF16)
        w2s_ref[...] = w2_ref[0].astype(BF16)

    @pl.when(used)
    def _():
        u = jnp.dot(xs_ref[...].astype(BF16), w1s_ref[...], preferred_element_type=F32) + b1_ref[0]
        acts = []
        for j in range(D_FF // half):
            glu = jnp.minimum(u[:, 2 * j * half:(2 * j + 1) * half], SWIGLU_LIMIT)
            lin = jnp.clip(u[:, (2 * j + 1) * half:(2 * j + 2) * half], -SWIGLU_LIMIT, SWIGLU_LIMIT)
            acts.append((glu * jax.nn.sigmoid(SWIGLU_ALPHA * glu) * (lin + 1.0)).astype(BF16))
        act = jnp.concatenate(acts, axis=1)
        o_ref[...] = jnp.dot(act, w2s_ref[...], preferred_element_type=F32) + b2_ref[0]

    @pl.when(jnp.logical_not(used))
    def _():
        o_ref[...] = jnp.zeros(o_ref.shape, F32)


def _moe_combine_body(dest_ref, gw_ref, x1_ref, gf_ref, os_ref, y_ref, buf_ref, sem):
    tt = x1_ref.shape[0]

    def copy(r, k):
        d = dest_ref[r * TOP_K + k]
        return pltpu.make_async_copy(os_ref.at[pl.ds(d, 1), :], buf_ref.at[k, pl.ds(r, 1), :], sem)

    def issue(r, carry):
        for k in range(TOP_K):
            copy(r, k).start()
        return carry
    lax.fori_loop(0, tt, issue, 0)

    def drain(r, carry):
        for k in range(TOP_K):
            copy(r, k).wait()
        return carry
    lax.fori_loop(0, tt, drain, 0)

    gw = gw_ref[...]
    y = gw[:, 0:1] * buf_ref[0]
    for k in range(1, TOP_K):
        y = y + gw[:, k:k + 1] * buf_ref[k]
    x2 = x1_ref[...] + y
    out = x2 * lax.rsqrt(jnp.mean(x2 * x2, axis=-1, keepdims=True) + EPS)
    y_ref[...] = out * gf_ref[...]


def moe_and_final_norm(x1a, x1b, h2a, h2b, te, gw, w1, b1p, w2, b2, g_final):
    n = te.shape[0]
    tiles_a = x1a.shape[0] // MOE_T
    row_a = lambda i: (jnp.minimum(i, tiles_a - 1), 0)
    row_b = lambda i: (jnp.maximum(i - tiles_a, 0), 0)
    tt = MOE_T
    bm = MOE_BM
    nb = -(-(n * TOP_K + N_EXPERTS * (bm - 1)) // bm)
    nbp = -(-nb // SUBLANES) * SUBLANES
    row = lambda i: (i, 0)
    const = lambda i: (0, 0)

    pos, cnt = pl.pallas_call(
        _moe_pos_body,
        grid=(n // tt,),
        in_specs=[pl.BlockSpec((tt, LANES), row)],
        out_specs=[pl.BlockSpec((tt, LANES), row), pl.BlockSpec((SUBLANES, LANES), const)],
        out_shape=[jax.ShapeDtypeStruct((n, LANES), F32), jax.ShapeDtypeStruct((SUBLANES, LANES), F32)],
        scratch_shapes=[pltpu.VMEM((SUBLANES, LANES), F32)],
        compiler_params=_cparams("arbitrary"),
        name="moe_positions",
    )(te)

    dest, be, nu = pl.pallas_call(
        functools.partial(_moe_dest_body, bm=bm),
        grid=(n // tt,),
        in_specs=[pl.BlockSpec((tt, LANES), row), pl.BlockSpec((tt, LANES), row),
                  pl.BlockSpec((SUBLANES, LANES), const)],
        out_specs=[pl.BlockSpec((tt, LANES), row), pl.BlockSpec((nbp, LANES), const),
                   pl.BlockSpec((SUBLANES, LANES), const)],
        out_shape=[jax.ShapeDtypeStruct((n, LANES), I32), jax.ShapeDtypeStruct((nbp, LANES), I32),
                   jax.ShapeDtypeStruct((SUBLANES, LANES), I32)],
        scratch_shapes=[pltpu.VMEM((SUBLANES, LANES), F32)],
        compiler_params=_cparams("arbitrary"),
        name="moe_destinations",
    )(te, pos, cnt)
    dest_flat = dest[:, :TOP_K].reshape(n * TOP_K)
    block_e = be[:nb, 0]
    n_used = nu[0, 0:1]

    xs = pl.pallas_call(
        functools.partial(_moe_dispatch_body, tiles_a=tiles_a),
        grid=(n // tt,),
        in_specs=[pl.BlockSpec((tt * TOP_K,), lambda i: (i,), memory_space=pltpu.SMEM),
                  pl.BlockSpec((tt, D_MODEL), row_a),
                  pl.BlockSpec((tt, D_MODEL), row_b),
                  pl.BlockSpec(memory_space=pl.ANY)],
        out_specs=pl.BlockSpec(memory_space=pl.ANY),
        out_shape=jax.ShapeDtypeStruct((nb * bm, D_MODEL), F32),
        scratch_shapes=[pltpu.SemaphoreType.DMA(())],
        input_output_aliases={3: 0},
        compiler_params=_cparams("arbitrary"),
        name="moe_dispatch",
    )(dest_flat, h2a, h2b, jnp.zeros((nb * bm, D_MODEL), F32))

    out_sorted = pl.pallas_call(
        _moe_expert_body,
        grid_spec=pltpu.PrefetchScalarGridSpec(
            num_scalar_prefetch=2,
            grid=(nb,),
            in_specs=[pl.BlockSpec((bm, D_MODEL), lambda i, be_, nu_: (i, 0)),
                      pl.BlockSpec((1, D_MODEL, 2 * D_FF), lambda i, be_, nu_: (be_[i], 0, 0)),
                      pl.BlockSpec((1, 1, 2 * D_FF), lambda i, be_, nu_: (be_[i], 0, 0)),
                      pl.BlockSpec((1, D_FF, D_MODEL), lambda i, be_, nu_: (be_[i], 0, 0)),
                      pl.BlockSpec((1, 1, D_MODEL), lambda i, be_, nu_: (be_[i], 0, 0))],
            out_specs=pl.BlockSpec((bm, D_MODEL), lambda i, be_, nu_: (i, 0)),
            scratch_shapes=[pltpu.VMEM((D_MODEL, 2 * D_FF), BF16), pltpu.VMEM((D_FF, D_MODEL), BF16)]),
        out_shape=jax.ShapeDtypeStruct((nb * bm, D_MODEL), F32),
        compiler_params=_cparams("arbitrary"),
        name="moe_experts",
    )(block_e, n_used, xs, w1, b1p, w2, b2)

    def combine(x1_part, tile0):
        return pl.pallas_call(
            _moe_combine_body,
            grid=(x1_part.shape[0] // tt,),
            in_specs=[pl.BlockSpec((tt * TOP_K,), lambda i: (i + tile0,), memory_space=pltpu.SMEM),
                      pl.BlockSpec((tt, LANES), lambda i: (i + tile0, 0)),
                      pl.BlockSpec((tt, D_MODEL), row),
                      pl.BlockSpec((1, D_MODEL), const),
                      pl.BlockSpec(memory_space=pl.ANY)],
            out_specs=pl.BlockSpec((tt, D_MODEL), row),
            out_shape=jax.ShapeDtypeStruct(x1_part.shape, F32),
            scratch_shapes=[pltpu.VMEM((TOP_K, tt, D_MODEL), F32), pltpu.SemaphoreType.DMA(())],
            compiler_params=_cparams("arbitrary"),
            name="moe_combine",
        )(dest_flat, gw, x1_part, g_final.reshape(1, D_MODEL), out_sorted)

    return combine(x1a, 0), combine(x1b, tiles_a)


def _split_cols(w):
    outs, off = [], 0
    for wd in IN_WIDTHS:
        outs.append(w[:, off:off + wd])
        off += wd
    return outs


def _lane_row(v, off):
    return jnp.zeros((1, LANES), F32).at[0, off:off + v.shape[0]].set(v.astype(F32))


def kernel(x_prompt, x_sample, cache_k, cache_v, cache_idx_k, state_conv, state_ssm, cache_mem_k, cache_mem_v,
           page_table, mem_prompt, g_norm1, w_in, b_gate, conv_w, conv_b, dt_bias, a_log, d_skip, g_ssd_norm,
           g_mem, w_mem_kv, w_ssd_out, w_attn_out, w_mem_out, w_out, g_norm2, w_router, b_router, w_exp1,
           b_exp1, w_exp2, b_exp2, g_final):
    assert w_in.shape[0] == 1, "single-layer trunk"
    bp, lp, _ = x_prompt.shape
    bs, ls, _ = x_sample.shape
    np_, ns = bp * lp, bs * ls

    wz, wxbc, wdt, wq, wk, wv, wqi, wki, wwi, wqm, wgate = _split_cols(w_in[0])
    w_all = jnp.concatenate(
        [wxbc, wgate, wz, wq, wqm, wqi, wk, wv, wki, wdt, wwi,
         jnp.zeros((D_MODEL, W_ALL - OFF_SM - SM_WI - IDX_HEADS), F32)], axis=1).astype(BF16)
    dtb_row = _lane_row(dt_bias[0], SM_DT)
    aneg_row = _lane_row(-jnp.exp(a_log[0].astype(F32)), SM_DT)
    dsk_row = jnp.repeat(d_skip[0].astype(F32), SSD_HEAD_DIM).reshape(1, D_INNER)
    gs_row = g_ssd_norm[0].reshape(1, D_INNER)
    cb_row = conv_b[0].reshape(1, CONV_DIM)
    wr_f32 = jnp.zeros((D_MODEL, LANES), F32).at[:, :N_EXPERTS].set(w_router[0])
    wr_hi = wr_f32.astype(BF16)
    wr_pad = jnp.stack([wr_hi, (wr_f32 - wr_hi.astype(F32)).astype(BF16)])
    br_pad = jnp.full((1, LANES), NEG, F32).at[0, :N_EXPERTS].set(b_router[0])
    b1p = b_exp1[0].reshape(N_EXPERTS, D_FF // LANES, LANES, 2).transpose(0, 1, 3, 2).reshape(N_EXPERTS, 1, 2 * D_FF)
    b2 = b_exp2[0].reshape(N_EXPERTS, 1, D_MODEL)

    xp = x_prompt.reshape(np_, D_MODEL)
    xs = x_sample.reshape(ns, D_MODEL)
    zp = norm_matmul(xp, g_norm1[0], w_all, 1024, IN_PROJ_TN)
    zs = norm_matmul(xs, g_norm1[0], w_all, ns, IN_PROJ_TN)

    kv_p = norm_matmul(mem_prompt.reshape(bp * N_MEM, D_MODEL), g_mem[0], w_mem_kv[0].astype(BF16),
                       min(1024, bp * N_MEM), MEM_WIDTH)
    om_p = mem_attn(zp, bp, lp, kv_p, 0, kv_p, 1, 512)
    om_s = mem_attn(zs, bs, ls, cache_mem_k[0].reshape(bs * N_MEM, MEM_WIDTH), 0,
                    cache_mem_v[0].reshape(bs * N_MEM, MEM_WIDTH), 0, ls)

    conv_prev_p = jnp.zeros((bp, SUBLANES, CONV_DIM), F32)
    conv_prev_s = jnp.concatenate(
        [jnp.zeros((bs, SUBLANES - (CONV_WIDTH - 1), CONV_DIM), F32), state_conv[0]], axis=1)
    ssm0_p = jnp.zeros((bp, D_INNER, D_STATE), F32)
    ssm0_s = state_ssm[0].reshape(bs, D_INNER, D_STATE)
    ys_p, ssm_p = ssd(zp, bp, lp, conv_prev_p, ssm0_p, conv_w[0], cb_row, dtb_row, aneg_row, dsk_row, gs_row)
    ys_s, ssm_s = ssd(zs, bs, ls, conv_prev_s, ssm0_s, conv_w[0], cb_row, dtb_row, aneg_row, dsk_row, gs_row)

    oa_p = dsa_prompt(zp, bp, lp)
    oa_s = dsa_sample(zs, bs, ls, cache_k[0], cache_v[0], cache_idx_k[0], page_table)

    mw = (b_gate[0].reshape(1, -1), w_ssd_out[0].astype(BF16), w_attn_out[0].astype(BF16),
          w_mem_out[0].astype(BF16), w_out[0].astype(BF16), g_norm2[0].reshape(1, D_MODEL), wr_pad, br_pad)
    x1_p, h2_p, te_p, gw_p = merge(xp, zp, ys_p, oa_p, om_p, *mw, 512)
    x1_s, h2_s, te_s, gw_s = merge(xs, zs, ys_s, oa_s, om_s, *mw, ns)

    cat = lambda a, b: jnp.concatenate([a, b], axis=0)
    y_all = moe_and_final_norm(x1_p, x1_s, h2_p, h2_s, cat(te_p, te_s), cat(gw_p, gw_s),
                               w_exp1[0], b1p, w_exp2[0], b2, g_final)
    y_prompt = y_all[0].reshape(bp, lp, D_MODEL)
    y_sample = y_all[1].reshape(bs, ls, D_MODEL)

    def kvi(z, b, l):
        k = z[:, OFF_K:OFF_K + N_KV_HEADS * HEAD_DIM].reshape(1, b, l, N_KV_HEADS, HEAD_DIM)
        v = z[:, OFF_V:OFF_V + N_KV_HEADS * HEAD_DIM].reshape(1, b, l, N_KV_HEADS, HEAD_DIM)
        ki = z[:, OFF_SM + SM_KI:OFF_SM + SM_KI + IDX_DIM].reshape(1, b, l, IDX_DIM)
        conv = z.reshape(b, l, W_ALL)[:, l - (CONV_WIDTH - 1):, OFF_XBC:OFF_XBC + CONV_DIM][None]
        return k, v, ki, conv

    k_p, v_p, ki_p, conv_p = kvi(zp, bp, lp)
    k_s, v_s, ki_s, conv_s = kvi(zs, bs, ls)
    mk_p = kv_p[:, :MEM_WIDTH].reshape(1, bp, N_MEM, MEM_HEADS, MEM_HEAD_DIM)
    mv_p = kv_p[:, MEM_WIDTH:].reshape(1, bp, N_MEM, MEM_HEADS, MEM_HEAD_DIM)
    ssm_shape = (1, -1, SSD_HEADS, SSD_HEAD_DIM, D_STATE)
    return (y_prompt, y_sample, k_p, v_p, ki_p, conv_p, ssm_p.reshape(ssm_shape), mk_p, mv_p,
            k_s, v_s, ki_s, conv_s, ssm_s.reshape(ssm_shape))
```

```python
import functools

import numpy as np
import jax
import jax.numpy as jnp
from jax import lax
from jax.experimental import pallas as pl
from jax.experimental.pallas import tpu as pltpu

F32 = jnp.float32
BF16 = jnp.bfloat16
I32 = jnp.int32
HIGHEST = lax.Precision.HIGHEST

D_MODEL = 1024
D_INNER = 2048
SSD_HEAD_DIM = 64
SSD_HEADS = 32
SSD_GROUPS = 4
D_STATE = 128
CONV_WIDTH = 4
CONV_DIM = D_INNER + 2 * SSD_GROUPS * D_STATE
SSD_CHUNK = 128
N_HEADS = 16
N_KV_HEADS = 4
HEAD_DIM = 64
IDX_HEADS = 8
IDX_DIM = 64
TOPK_MAX = 256
N_MEM = 256
MEM_HEADS = 4
MEM_HEAD_DIM = 256
MEM_WIDTH = MEM_HEADS * MEM_HEAD_DIM
N_EXPERTS = 32
TOP_K = 4
D_FF = D_MODEL
SWIGLU_LIMIT = 7.0
SWIGLU_ALPHA = 1.702
N_BRANCH = 3
EPS = 1e-6
PAGE_SIZE = 128
IN_WIDTHS = (D_INNER, CONV_DIM, SSD_HEADS, N_HEADS * HEAD_DIM, N_KV_HEADS * HEAD_DIM, N_KV_HEADS * HEAD_DIM,
             IDX_HEADS * IDX_DIM, IDX_DIM, IDX_HEADS, MEM_WIDTH, N_BRANCH * D_MODEL)

LANES = 128
SUBLANES = 8
VMEM_LIMIT = 56 * 1024 * 1024

OFF_XBC = 0
OFF_GATE = OFF_XBC + CONV_DIM
OFF_Z = OFF_GATE + N_BRANCH * D_MODEL
OFF_Q = OFF_Z + D_INNER
OFF_QM = OFF_Q + N_HEADS * HEAD_DIM
OFF_QI = OFF_QM + MEM_WIDTH
OFF_K = OFF_QI + IDX_HEADS * IDX_DIM
OFF_V = OFF_K + N_KV_HEADS * HEAD_DIM
OFF_SM = OFF_V + N_KV_HEADS * HEAD_DIM
SM_KI = 0
SM_DT = SM_KI + IDX_DIM
SM_WI = SM_DT + SSD_HEADS
IN_PROJ_TN = 1280
W_ALL = OFF_SM + 2 * LANES

NEG = -1e30
INT_MIN = np.int32(-2 ** 31)
INT_MAX = np.int32(2 ** 31 - 1)

ATT_PASSES = 1
MOE_BM = 512
MOE_T = 256
MOE_MOVE_T = 512


def _cparams(*sem):
    return pltpu.CompilerParams(dimension_semantics=sem, vmem_limit_bytes=VMEM_LIMIT)


def _nt_dot(a, b):
    return lax.dot_general(a, b, (((1,), (1,)), ((), ())), preferred_element_type=F32)


def _float_key(x):
    bits = lax.bitcast_convert_type(x, I32)
    return jnp.where(bits < 0, bits ^ INT_MAX, bits)


def _norm_matmul_body(x_ref, g_ref, w_ref, o_ref, h_ref):
    @pl.when(pl.program_id(1) == 0)
    def _():
        x = x_ref[...]
        h = x * lax.rsqrt(jnp.mean(x * x, axis=-1, keepdims=True) + EPS)
        h_ref[...] = (h * g_ref[...]).astype(BF16)

    o_ref[...] = jnp.dot(h_ref[...], w_ref[...], preferred_element_type=F32)


def norm_matmul(x, g, w, tm, tn):
    n, d = x.shape
    wn = w.shape[1]
    return pl.pallas_call(
        _norm_matmul_body,
        grid=(n // tm, wn // tn),
        in_specs=[pl.BlockSpec((tm, d), lambda i, j: (i, 0)),
                  pl.BlockSpec((1, d), lambda i, j: (0, 0)),
                  pl.BlockSpec((d, tn), lambda i, j: (0, j))],
        out_specs=pl.BlockSpec((tm, tn), lambda i, j: (i, j)),
        out_shape=jax.ShapeDtypeStruct((n, wn), F32),
        scratch_shapes=[pltpu.VMEM((tm, d), BF16)],
        compiler_params=_cparams("parallel", "arbitrary"),
        name="norm_matmul",
    )(x, g.reshape(1, d), w)


def _softplus(x):
    return jnp.maximum(x, 0.0) + jnp.log1p(jnp.exp(-jnp.abs(x)))


def _silu(x):
    return x * jax.nn.sigmoid(x)


def _expand_heads(v, sel):
    hi = v.astype(BF16)
    r1 = v - hi.astype(F32)
    mid = r1.astype(BF16)
    lo = (r1 - mid.astype(F32)).astype(BF16)
    out = jnp.dot(hi, sel, preferred_element_type=F32)
    out = out + jnp.dot(mid, sel, preferred_element_type=F32)
    return out + jnp.dot(lo, sel, preferred_element_type=F32)


def _ssd_body(xbc_ref, z_ref, sm_ref, convp_ref, init_ref, cw_ref, cb_ref, dtb_ref, aneg_ref, dsk_ref, gs_ref, sel_ref,
              y_ref, st_ref, carry_ref, state_ref, *, rows_in, q):
    c = pl.program_id(1)
    taps = CONV_WIDTH - 1

    @pl.when(c == 0)
    def _():
        prev = convp_ref[0]
        for j in range(taps):
            acc0 = cw_ref[j:j + 1, :] * prev[SUBLANES - 1:SUBLANES, :]
            for m in range(1, j + 1):
                acc0 = acc0 + cw_ref[j - m:j - m + 1, :] * prev[SUBLANES - 1 - m:SUBLANES - m, :]
            carry_ref[j:j + 1, :] = acc0
        state_ref[...] = init_ref[0]

    x = xbc_ref[...]
    if rows_in < q:
        x = jnp.concatenate([x, jnp.zeros((q - rows_in, CONV_DIM), F32)], axis=0)

    first_row = lax.broadcasted_iota(I32, (q, CONV_DIM), 0) == 0
    shifted = None
    for j in range(taps):
        stage = cw_ref[j:j + 1, :] * x
        if shifted is not None:
            stage = stage + shifted
        shifted = jnp.where(first_row, carry_ref[j:j + 1, :], pltpu.roll(stage, 1, 0))
        carry_ref[j:j + 1, :] = stage[q - 1:q, :]
    xc = _silu(cw_ref[taps:taps + 1, :] * x + shifted + cb_ref[...])

    xs = xc[:, :D_INNER]
    gn = SSD_GROUPS * D_STATE
    bm = xc[:, D_INNER:D_INNER + gn].astype(BF16)
    cm = xc[:, D_INNER + gn:].astype(BF16)

    sm = sm_ref[...]
    zz = z_ref[...]
    if rows_in < q:
        sm = jnp.concatenate([sm, jnp.zeros((q - rows_in, LANES), F32)], axis=0)
        zz = jnp.concatenate([zz, jnp.zeros((q - rows_in, D_INNER), F32)], axis=0)
    row = lax.broadcasted_iota(I32, (q, LANES), 0)
    dt = _softplus(sm + dtb_ref[...])
    if rows_in < q:
        dt = jnp.where(row < rows_in, dt, 0.0)
    a = dt * aneg_ref[...]
    tri = (lax.broadcasted_iota(I32, (q, q), 0) >= lax.broadcasted_iota(I32, (q, q), 1)).astype(F32)
    a_cs = jnp.dot(tri, a, precision=HIGHEST, preferred_element_type=F32)
    a_t = a_cs.T
    a_last = a_cs[q - 1:q, :]
    dte = jnp.exp(a_last - a_cs)
    e_in = jnp.exp(a_cs)

    sel = sel_ref[...]
    xdt = xs * _expand_heads(dt, sel)
    xdt_bf = xdt.astype(BF16)
    xw_bf = (xs * _expand_heads(dt * dte, sel)).astype(BF16)
    ein_x = _expand_heads(e_in, sel)

    causal = lax.broadcasted_iota(I32, (q, q), 0) >= lax.broadcasted_iota(I32, (q, q), 1)
    lane = lax.broadcasted_iota(I32, (q, LANES), 1)
    hpg = SSD_HEADS // SSD_GROUPS
    gw = hpg * SSD_HEAD_DIM
    y_parts = []
    for g in range(SSD_GROUPS):
        bg = bm[:, g * D_STATE:(g + 1) * D_STATE]
        cg = cm[:, g * D_STATE:(g + 1) * D_STATE]
        cb = _nt_dot(cg, bg)
        m_h = []
        for e in range(hpg):
            h = g * hpg + e
            col = a_cs[:, SM_DT + h:SM_DT + h + 1]
            rw = a_t[SM_DT + h:SM_DT + h + 1, :]
            decay = jnp.exp(jnp.where(causal, col - rw, -jnp.inf))
            m_h.append((cb * decay).astype(BF16))
        yd = []
        for t in range(hpg // 2):
            pair = g * (hpg // 2) + t
            slab = xdt_bf[:, pair * LANES:(pair + 1) * LANES]
            ya = jnp.dot(m_h[2 * t], slab, preferred_element_type=F32)
            yb = jnp.dot(m_h[2 * t + 1], slab, preferred_element_type=F32)
            yd.append(jnp.where(lane < SSD_HEAD_DIM, ya, yb))
        s_old = state_ref[g * gw:(g + 1) * gw, :]
        y_off = _nt_dot(cg, s_old.astype(BF16)) * ein_x[:, g * gw:(g + 1) * gw]
        y_parts.append(jnp.concatenate(yd, axis=1) + y_off)
        new = lax.dot_general(xw_bf[:, g * gw:(g + 1) * gw], bg, (((0,), (0,)), ((), ())),
                              preferred_element_type=F32)
        for e in range(hpg):
            h = g * hpg + e
            dec = jnp.exp(a_t[SM_DT + h:SM_DT + h + 1, q - 1:q])
            lo = e * SSD_HEAD_DIM
            state_ref[h * SSD_HEAD_DIM:(h + 1) * SSD_HEAD_DIM, :] = (
                s_old[lo:lo + SSD_HEAD_DIM, :] * dec + new[lo:lo + SSD_HEAD_DIM, :])

    y = jnp.concatenate(y_parts, axis=1) + dsk_ref[...] * xs
    y = y * _silu(zz)
    outs = []
    for g in range(SSD_GROUPS):
        yg = y[:, g * gw:(g + 1) * gw]
        outs.append(yg * lax.rsqrt(jnp.mean(yg * yg, axis=-1, keepdims=True) + EPS))
    y = jnp.concatenate(outs, axis=1) * gs_ref[...]
    y_ref[...] = y[:rows_in].astype(BF16)

    @pl.when(c == pl.num_programs(1) - 1)
    def _():
        st_ref[0] = state_ref[...]


def ssd(zall, batch, seq, conv_prev8, ssm_init, conv_w, conv_b, dtb_row, aneg_row, dsk_row, gs_row):
    q = SSD_CHUNK
    rows_in = min(seq, q)
    nch = seq // rows_in
    sel = (jnp.arange(LANES)[:, None] - SM_DT == jnp.arange(D_INNER)[None, :] // SSD_HEAD_DIM).astype(BF16)
    row_map = lambda b, c: (b * nch + c)
    const2 = lambda b, c: (0, 0)
    body = functools.partial(_ssd_body, rows_in=rows_in, q=q)
    return pl.pallas_call(
        body,
        grid=(batch, nch),
        in_specs=[pl.BlockSpec((rows_in, CONV_DIM), lambda b, c: (row_map(b, c), OFF_XBC // CONV_DIM)),
                  pl.BlockSpec((rows_in, D_INNER), lambda b, c: (row_map(b, c), OFF_Z // D_INNER)),
                  pl.BlockSpec((rows_in, LANES), lambda b, c: (row_map(b, c), OFF_SM // LANES)),
                  pl.BlockSpec((1, SUBLANES, CONV_DIM), lambda b, c: (b, 0, 0)),
                  pl.BlockSpec((1, D_INNER, D_STATE), lambda b, c: (b, 0, 0)),
                  pl.BlockSpec((CONV_WIDTH, CONV_DIM), const2),
                  pl.BlockSpec((1, CONV_DIM), const2),
                  pl.BlockSpec((1, LANES), const2),
                  pl.BlockSpec((1, LANES), const2),
                  pl.BlockSpec((1, D_INNER), const2),
                  pl.BlockSpec((1, D_INNER), const2),
                  pl.BlockSpec((LANES, D_INNER), const2)],
        out_specs=[pl.BlockSpec((rows_in, D_INNER), lambda b, c: (row_map(b, c), 0)),
                   pl.BlockSpec((1, D_INNER, D_STATE), lambda b, c: (b, 0, 0))],
        out_shape=[jax.ShapeDtypeStruct((batch * seq, D_INNER), BF16),
                   jax.ShapeDtypeStruct((batch, D_INNER, D_STATE), F32)],
        scratch_shapes=[pltpu.VMEM((SUBLANES, CONV_DIM), F32),
                        pltpu.VMEM((D_INNER, D_STATE), F32)],
        compiler_params=_cparams("parallel", "arbitrary"),
        name="ssd",
    )(zall, zall, zall, conv_prev8, ssm_init, conv_w, conv_b, dtb_row, aneg_row, dsk_row, gs_row, sel)


def _mem_attn_body(q_ref, k_ref, v_ref, o_ref):
    for h in range(MEM_HEADS):
        sl = slice(h * MEM_HEAD_DIM, (h + 1) * MEM_HEAD_DIM)
        s = _nt_dot(q_ref[:, sl].astype(BF16), k_ref[:, sl].astype(BF16)) * (MEM_HEAD_DIM ** -0.5)
        m = jnp.max(s, axis=-1, keepdims=True)
        p = jnp.exp(s - m)
        p = p / jnp.sum(p, axis=-1, keepdims=True)
        o = jnp.dot(p.astype(BF16), v_ref[:, sl].astype(BF16), preferred_element_type=F32)
        o_ref[:, sl] = o.astype(BF16)


def mem_attn(zall, batch, seq, k_arr, k_col, v_arr, v_col, tm):
    nt = seq // tm
    return pl.pallas_call(
        _mem_attn_body,
        grid=(batch, nt),
        in_specs=[pl.BlockSpec((tm, MEM_WIDTH), lambda b, i: (b * nt + i, OFF_QM // MEM_WIDTH)),
                  pl.BlockSpec((N_MEM, MEM_WIDTH), lambda b, i: (b, k_col)),
                  pl.BlockSpec((N_MEM, MEM_WIDTH), lambda b, i: (b, v_col))],
        out_specs=pl.BlockSpec((tm, MEM_WIDTH), lambda b, i: (b * nt + i, 0)),
        out_shape=jax.ShapeDtypeStruct((batch * seq, MEM_WIDTH), BF16),
        compiler_params=_cparams("parallel", "arbitrary"),
        name="mem_attn",
    )(zall, k_arr, v_arr)


def _kth_largest_key(count_ge, shape, n_sel):
    def bit_body(t, ans):
        cand = ans | jnp.left_shift(jnp.int32(1), 31 - t)
        cnt = count_ge(cand ^ INT_MIN)
        return jnp.where(cnt >= n_sel, cand, ans)

    ans = lax.fori_loop(0, 32, bit_body, jnp.zeros(shape, I32))
    return ans ^ INT_MIN


def _tie_cut(count_eq_below, need, shape, nbits):
    def bit_body(t, lo):
        cand = lo | jnp.left_shift(jnp.int32(1), nbits - 1 - t)
        cnt = count_eq_below(cand)
        return jnp.where(cnt < need, cand, lo)

    return lax.fori_loop(0, nbits, bit_body, jnp.zeros(shape, I32))


def _select_bias(key, kpos, thr, cut, visible):
    sel = (key > thr) | ((key == thr) & (kpos <= cut))
    return jnp.where(sel & visible, 0.0, NEG)


def _dsa_prompt_body(q_ref, qi_ref, smq_ref, k_ref, v_ref, smk_ref, o_ref,
                     kh_ref, vt_ref, kis_ref, qt2_ref, qit_ref, keys_ref, bias_ref, ot_ref, *, tq, kc, seq, n_sel):
    i = pl.program_id(1)

    @pl.when(i == 0)
    def _():
        def cast_rows(r, carry):
            rs = pl.ds(pl.multiple_of(r * kc, kc), kc)
            kk = k_ref[rs, :]
            for h in range(N_KV_HEADS):
                kh_ref[h, rs, :] = kk[:, h * HEAD_DIM:(h + 1) * HEAD_DIM].astype(BF16)
            vt_ref[r] = v_ref[rs, :].T.astype(BF16)
            kis_ref[rs, :] = smk_ref[rs, SM_KI:SM_KI + IDX_DIM].astype(BF16)
            return carry
        lax.fori_loop(0, seq // kc, cast_rows, 0)

    nkc = (i * tq + tq - 1) // kc + 1
    qpos = i * tq + lax.broadcasted_iota(I32, (kc, tq), 1)
    krow = lax.broadcasted_iota(I32, (kc, tq), 0)

    qt = (q_ref[...] * (HEAD_DIM ** -0.5)).T.astype(BF16)
    for h in range(N_HEADS):
        qt2_ref[:, h * tq:(h + 1) * tq] = qt[h * HEAD_DIM:(h + 1) * HEAD_DIM, :]
    qit = (qi_ref[...] * (IDX_DIM ** -0.5)).T.astype(BF16)
    for h in range(IDX_HEADS):
        qit_ref[:, h * tq:(h + 1) * tq] = qit[h * IDX_DIM:(h + 1) * IDX_DIM, :]
    wt = smq_ref[...].T[SM_WI:SM_WI + IDX_HEADS, :] * (IDX_HEADS ** -0.5)

    def score_chunk(c, carry):
        ks = pl.ds(pl.multiple_of(c * kc, kc), kc)
        d = jnp.dot(kis_ref[ks, :], qit_ref[...], preferred_element_type=F32)
        sc = jnp.zeros((kc, tq), F32)
        for h in range(IDX_HEADS):
            sc = sc + jnp.maximum(d[:, h * tq:(h + 1) * tq], 0.0) * wt[h:h + 1, :]
        key = _float_key(sc + 0.0)
        keys_ref[c] = jnp.where(c * kc + krow <= qpos, key, INT_MIN)
        return carry
    lax.fori_loop(0, nkc, score_chunk, 0)

    def count(pred):
        def body(c, acc):
            return acc + jnp.where(pred(keys_ref[c], c * kc + krow), 1.0, 0.0)
        acc = lax.fori_loop(0, nkc, body, jnp.zeros((kc, tq), F32))
        return jnp.sum(acc, axis=0, keepdims=True)

    vec = (1, tq)
    thr = _kth_largest_key(lambda cand: count(lambda key, kpos: key >= cand), vec, n_sel)
    n_gt = count(lambda key, kpos: key > thr)
    n_eq = count(lambda key, kpos: key == thr)
    need = n_sel - n_gt
    excess = jnp.max(jnp.where((n_eq > need) & (thr != INT_MIN), 1.0, 0.0))
    nbits = max(1, int(seq - 1).bit_length())
    cut = lax.cond(
        excess > 0.0,
        lambda: _tie_cut(lambda cand: count(lambda key, kpos: (key == thr) & (kpos < cand)), need, vec, nbits),
        lambda: jnp.full(vec, INT_MAX, I32))

    def bias_chunk(c, carry):
        kpos = c * kc + krow
        bias_ref[c] = _select_bias(keys_ref[c], kpos, thr, cut, kpos <= qpos)
        return carry
    lax.fori_loop(0, nkc, bias_chunk, 0)

    grp = N_HEADS // N_KV_HEADS
    kv_per_pass = N_KV_HEADS // ATT_PASSES
    hpp = kv_per_pass * grp
    for ps_i in range(ATT_PASSES):
        kv0 = ps_i * kv_per_pass

        def att_chunk(c, carry, kv0=kv0):
            ms, ls, accs = carry
            ks = pl.ds(pl.multiple_of(c * kc, kc), kc)
            bias = bias_ref[c]
            s4 = [jnp.dot(kh_ref[kv0 + j, ks, :], qt2_ref[:, (kv0 + j) * grp * tq:(kv0 + j + 1) * grp * tq],
                          preferred_element_type=F32) for j in range(kv_per_pass)]
            ms_n, ls_n, accs_n = [], [], []
            for h in range(hpp):
                s = s4[h // grp][:, (h % grp) * tq:(h % grp + 1) * tq] + bias
                m_new = jnp.maximum(ms[h], jnp.max(s, axis=0, keepdims=True))
                p = jnp.exp(s - m_new)
                alpha = jnp.exp(ms[h] - m_new)
                ms_n.append(m_new)
                ls_n.append(alpha * ls[h] + jnp.sum(p, axis=0, keepdims=True))
                kh = kv0 + h // grp
                vtc = vt_ref[c, kh * HEAD_DIM:(kh + 1) * HEAD_DIM, :]
                accs_n.append(alpha * accs[h] + jnp.dot(vtc, p.astype(BF16), preferred_element_type=F32))
            return tuple(ms_n), tuple(ls_n), tuple(accs_n)

        init = (tuple(jnp.full(vec, NEG, F32) for _ in range(hpp)),
                tuple(jnp.zeros(vec, F32) for _ in range(hpp)),
                tuple(jnp.zeros((HEAD_DIM, tq), F32) for _ in range(hpp)))
        _, ls, accs = lax.fori_loop(0, nkc, att_chunk, init)
        for h in range(hpp):
            hh = kv0 * grp + h
            ot_ref[hh * HEAD_DIM:(hh + 1) * HEAD_DIM, :] = accs[h] / ls[h]
    o_ref[...] = ot_ref[...].T.astype(BF16)


def dsa_prompt(zall, batch, seq):
    tq = 128
    kc = 256
    nq = seq // tq
    n_sel = min(TOPK_MAX, seq // 4)
    kvw = N_KV_HEADS * HEAD_DIM
    qw = N_HEADS * HEAD_DIM
    qiw = IDX_HEADS * IDX_DIM
    body = functools.partial(_dsa_prompt_body, tq=tq, kc=kc, seq=seq, n_sel=n_sel)
    return pl.pallas_call(
        body,
        grid=(batch, nq),
        in_specs=[pl.BlockSpec((tq, qw), lambda b, i: (b * nq + i, OFF_Q // qw)),
                  pl.BlockSpec((tq, qiw), lambda b, i: (b * nq + i, OFF_QI // qiw)),
                  pl.BlockSpec((tq, LANES), lambda b, i: (b * nq + i, OFF_SM // LANES)),
                  pl.BlockSpec((seq, kvw), lambda b, i: (b, OFF_K // kvw)),
                  pl.BlockSpec((seq, kvw), lambda b, i: (b, OFF_V // kvw)),
                  pl.BlockSpec((seq, LANES), lambda b, i: (b, OFF_SM // LANES))],
        out_specs=pl.BlockSpec((tq, qw), lambda b, i: (b * nq + i, 0)),
        out_shape=jax.ShapeDtypeStruct((batch * seq, qw), BF16),
        scratch_shapes=[pltpu.VMEM((N_KV_HEADS, seq, HEAD_DIM), BF16),
                        pltpu.VMEM((seq // kc, kvw, kc), BF16),
                        pltpu.VMEM((seq, IDX_DIM), BF16),
                        pltpu.VMEM((HEAD_DIM, N_HEADS * tq), BF16),
                        pltpu.VMEM((IDX_DIM, IDX_HEADS * tq), BF16),
                        pltpu.VMEM((seq // kc, kc, tq), I32),
                        pltpu.VMEM((seq // kc, kc, tq), F32),
                        pltpu.VMEM((qw, tq), F32)],
        compiler_params=_cparams("parallel", "arbitrary"),
        name="dsa_prompt",
    )(zall, zall, zall, zall, zall, zall)


def _dsa_s_score_body(pt_ref, qs_ref, w_ref, *refs, pg):
    ki_refs, o_ref = refs[:pg], refs[pg]
    qs = qs_ref[0]
    wcol = w_ref[0] * (IDX_HEADS ** -0.5)
    t = qs.shape[0] // IDX_HEADS
    for p in range(pg):
        d = jnp.dot(qs, ki_refs[p][0].astype(BF16), preferred_element_type=F32)
        r = jnp.maximum(d * (IDX_DIM ** -0.5), 0.0) * wcol
        sc = r[0:t, :]
        for h in range(1, IDX_HEADS):
            sc = sc + r[h * t:(h + 1) * t, :]
        o_ref[0, :, p * PAGE_SIZE:(p + 1) * PAGE_SIZE] = sc + 0.0


def _dsa_s_select_body(sc_ref, qs_ref, w_ref, smn_ref, o_ref, *, t, past, n_sel, nbat):
    rows = nbat * t
    sc_new = []
    for b in range(nbat):
        wcol = w_ref[b] * (IDX_HEADS ** -0.5)
        ki_new = smn_ref[b * t:(b + 1) * t, SM_KI:SM_KI + IDX_DIM].astype(BF16)
        ki_new = jnp.concatenate([ki_new, jnp.zeros((LANES - t, IDX_DIM), BF16)], axis=0)
        d = _nt_dot(qs_ref[b], ki_new)
        r = jnp.maximum(d * (IDX_DIM ** -0.5), 0.0) * wcol
        sc = r[0:t, :]
        for h in range(1, IDX_HEADS):
            sc = sc + r[h * t:(h + 1) * t, :]
        sc_new.append(sc)
    sc_new = jnp.concatenate(sc_new, axis=0)
    lane_t = lax.broadcasted_iota(I32, (t, LANES), 1)
    vis_t = lane_t <= lax.broadcasted_iota(I32, (t, LANES), 0)
    vis_n = jnp.concatenate([vis_t] * nbat, axis=0)
    lane_n = lax.broadcasted_iota(I32, (rows, LANES), 1)
    key_n = jnp.where(vis_n, _float_key(sc_new + 0.0), INT_MIN)
    key_p = _float_key(sc_ref[...].reshape(rows, past))
    pos_p = lax.broadcasted_iota(I32, (rows, past), 1)
    pos_n = past + lane_n

    def count(pred):
        return (jnp.sum(jnp.where(pred(key_p, pos_p), 1.0, 0.0), axis=1, keepdims=True)
                + jnp.sum(jnp.where(pred(key_n, pos_n), 1.0, 0.0), axis=1, keepdims=True))

    vec = (rows, 1)
    thr = _kth_largest_key(lambda cand: count(lambda key, kpos: key >= cand), vec, n_sel)
    need = n_sel - count(lambda key, kpos: key > thr)
    n_eq = count(lambda key, kpos: key == thr)
    excess = jnp.max(jnp.where((n_eq > need) & (thr != INT_MIN), 1.0, 0.0))
    nbits = max(1, int(past + t - 1).bit_length())
    cut = lax.cond(
        excess > 0.0,
        lambda: _tie_cut(lambda cand: count(lambda key, kpos: (key == thr) & (kpos < cand)), need, vec, nbits),
        lambda: jnp.full(vec, INT_MAX, I32))
    o_ref[:, :, 0:past] = _select_bias(key_p, pos_p, thr, cut, pos_p >= 0).reshape(nbat, t, past)
    o_ref[:, :, past:past + LANES] = _select_bias(key_n, pos_n, thr, cut, vis_n).reshape(nbat, t, LANES)


def _dsa_s_attn_body(pt_ref, qbd_ref, bias_ref, biasn_ref, kn_ref, vn_ref, *refs, pg, t):
    k_refs, v_refs = refs[:pg], refs[pg:2 * pg]
    o_ref, m_ref, l_ref, acc_ref = refs[2 * pg:]
    j = pl.program_id(1)
    rows = qbd_ref.shape[1]
    rep = rows // t

    @pl.when(j == 0)
    def _():
        m_ref[...] = jnp.full(m_ref.shape, NEG, F32)
        l_ref[...] = jnp.zeros(l_ref.shape, F32)
        acc_ref[...] = jnp.zeros(acc_ref.shape, F32)

    qbd = qbd_ref[0]

    def update(kt, vt, bias):
        s = jnp.dot(qbd, kt, preferred_element_type=F32) * (HEAD_DIM ** -0.5) + jnp.concatenate([bias] * rep, axis=0)
        m = m_ref[...]
        m_new = jnp.maximum(m, jnp.max(s, axis=1, keepdims=True))
        p = jnp.exp(s - m_new)
        alpha = jnp.exp(m - m_new)
        l_ref[...] = alpha * l_ref[...] + jnp.sum(p, axis=1, keepdims=True)
        acc_ref[...] = alpha * acc_ref[...] + _nt_dot(p.astype(BF16), vt)
        m_ref[...] = m_new

    kt = jnp.concatenate([r[0] for r in k_refs], axis=1).astype(BF16)
    vt = jnp.concatenate([r[0] for r in v_refs], axis=1).astype(BF16)
    update(kt, vt, bias_ref[0])

    @pl.when(j == pl.num_programs(1) - 1)
    def _():
        kvw = N_KV_HEADS * HEAD_DIM
        zpad = jnp.zeros((LANES - t, kvw), F32)
        update(jnp.concatenate([kn_ref[...], zpad], axis=0).T.astype(BF16),
               jnp.concatenate([vn_ref[...], zpad], axis=0).T.astype(BF16), biasn_ref[0])
        o_ref[0] = acc_ref[...] / l_ref[...]


def dsa_sample(zs, batch, t, cache_k, cache_v, cache_ki, page_table):
    n_pages = page_table.shape[1]
    past = n_pages * PAGE_SIZE
    n_sel = min(TOPK_MAX, (past + t) // 4)
    pg = 16 if n_pages % 16 == 0 else 8
    nj = n_pages // pg
    n_pool = cache_k.shape[0]
    kvw = N_KV_HEADS * HEAD_DIM
    grp = N_HEADS // N_KV_HEADS

    qi = zs[:, OFF_QI:OFF_QI + IDX_HEADS * IDX_DIM].reshape(batch, t, IDX_HEADS, IDX_DIM)
    qs = jnp.transpose(qi, (0, 2, 1, 3)).reshape(batch, IDX_HEADS * t, IDX_DIM).astype(BF16)
    wi = zs[:, OFF_SM + SM_WI:OFF_SM + SM_WI + IDX_HEADS].reshape(batch, t, IDX_HEADS)
    wcol = jnp.transpose(wi, (0, 2, 1)).reshape(batch, IDX_HEADS * t, 1)

    def page_spec(shape, p):
        return pl.BlockSpec(shape, lambda b, j, pt: (pt[b, j * pg + p],) + (0,) * (len(shape) - 1))

    scores = pl.pallas_call(
        functools.partial(_dsa_s_score_body, pg=pg),
        grid_spec=pltpu.PrefetchScalarGridSpec(
            num_scalar_prefetch=1,
            grid=(batch, nj),
            in_specs=[pl.BlockSpec((1, IDX_HEADS * t, IDX_DIM), lambda b, j, pt: (b, 0, 0)),
                      pl.BlockSpec((1, IDX_HEADS * t, 1), lambda b, j, pt: (b, 0, 0))]
                     + [page_spec((1, IDX_DIM, PAGE_SIZE), p) for p in range(pg)],
            out_specs=pl.BlockSpec((1, t, pg * PAGE_SIZE), lambda b, j, pt: (b, 0, j))),
        out_shape=jax.ShapeDtypeStruct((batch, t, past), F32),
        compiler_params=_cparams("parallel", "arbitrary"),
        name="dsa_sample_scores",
    )(page_table, qs, wcol, *([jnp.swapaxes(cache_ki, 1, 2)] * pg))

    nbat = 4 if (batch % 4 == 0 and t % SUBLANES == 0) else 1
    bias = pl.pallas_call(
        functools.partial(_dsa_s_select_body, t=t, past=past, n_sel=n_sel, nbat=nbat),
        grid=(batch // nbat,),
        in_specs=[pl.BlockSpec((nbat, t, past), lambda b: (b, 0, 0)),
                  pl.BlockSpec((nbat, IDX_HEADS * t, IDX_DIM), lambda b: (b, 0, 0)),
                  pl.BlockSpec((nbat, IDX_HEADS * t, 1), lambda b: (b, 0, 0)),
                  pl.BlockSpec((nbat * t, LANES), lambda b: (b, OFF_SM // LANES))],
        out_specs=pl.BlockSpec((nbat, t, past + LANES), lambda b: (b, 0, 0)),
        out_shape=jax.ShapeDtypeStruct((batch, t, past + LANES), F32),
        compiler_params=_cparams("parallel"),
        name="dsa_sample_select",
    )(scores, qs, wcol, zs)

    q = zs[:, OFF_Q:OFF_Q + N_HEADS * HEAD_DIM].reshape(batch, t, N_KV_HEADS, grp, HEAD_DIM)
    q = jnp.transpose(q, (0, 2, 3, 1, 4))
    eye = jnp.eye(N_KV_HEADS, dtype=F32)
    qbd = (q[:, :, :, :, None, :] * eye[None, :, None, None, :, None]).reshape(batch, N_HEADS * t, kvw).astype(BF16)

    ck = jnp.transpose(cache_k, (0, 2, 3, 1)).reshape(n_pool, kvw, PAGE_SIZE)
    cv = jnp.transpose(cache_v, (0, 2, 3, 1)).reshape(n_pool, kvw, PAGE_SIZE)
    rows = N_HEADS * t
    out = pl.pallas_call(
        functools.partial(_dsa_s_attn_body, pg=pg, t=t),
        grid_spec=pltpu.PrefetchScalarGridSpec(
            num_scalar_prefetch=1,
            grid=(batch, nj),
            in_specs=[pl.BlockSpec((1, rows, kvw), lambda b, j, pt: (b, 0, 0)),
                      pl.BlockSpec((1, t, pg * PAGE_SIZE), lambda b, j, pt: (b, 0, j)),
                      pl.BlockSpec((1, t, LANES), lambda b, j, pt: (b, 0, past // LANES)),
                      pl.BlockSpec((t, kvw), lambda b, j, pt: (b, OFF_K // kvw)),
                      pl.BlockSpec((t, kvw), lambda b, j, pt: (b, OFF_V // kvw))]
                     + [page_spec((1, kvw, PAGE_SIZE), p) for p in range(pg)]
                     + [page_spec((1, kvw, PAGE_SIZE), p) for p in range(pg)],
            out_specs=pl.BlockSpec((1, rows, kvw), lambda b, j, pt: (b, 0, 0)),
            scratch_shapes=[pltpu.VMEM((rows, 1), F32), pltpu.VMEM((rows, 1), F32), pltpu.VMEM((rows, kvw), F32)]),
        out_shape=jax.ShapeDtypeStruct((batch, rows, kvw), F32),
        compiler_params=_cparams("parallel", "arbitrary"),
        name="dsa_sample_attn",
    )(page_table, qbd, bias, bias, zs, zs, *([ck] * pg), *([cv] * pg))

    o = out.reshape(batch, N_KV_HEADS, grp, t, N_KV_HEADS, HEAD_DIM)
    o = jnp.stack([o[:, kh, :, :, kh, :] for kh in range(N_KV_HEADS)], axis=1)
    return jnp.transpose(o, (0, 3, 1, 2, 4)).reshape(batch * t, N_HEADS * HEAD_DIM).astype(BF16)


def _merge_body(x_ref, gate_ref, ys_ref, oa_ref, om_ref, bg_ref, ws_ref, wa_ref, wm_ref, wo_ref, g2_ref, wr_ref, br_ref,
                x1_ref, h2_ref, te_ref, gw_ref):
    gates = jax.nn.sigmoid(gate_ref[...] + bg_ref[...])
    merged = (gates[:, 0:D_MODEL] * jnp.dot(ys_ref[...], ws_ref[...], preferred_element_type=F32)
              + gates[:, D_MODEL:2 * D_MODEL] * jnp.dot(oa_ref[...], wa_ref[...], preferred_element_type=F32)
              + gates[:, 2 * D_MODEL:] * jnp.dot(om_ref[...], wm_ref[...], preferred_element_type=F32))
    x1 = x_ref[...] + jnp.dot(merged.astype(BF16), wo_ref[...], preferred_element_type=F32)
    x1_ref[...] = x1
    h2 = x1 * lax.rsqrt(jnp.mean(x1 * x1, axis=-1, keepdims=True) + EPS)
    h2 = h2 * g2_ref[...]
    h2_ref[...] = h2
    h_hi = h2.astype(BF16)
    h_mid = (h2 - h_hi.astype(F32)).astype(BF16)
    wr_hi = wr_ref[0]
    logits = (jnp.dot(h_hi, wr_hi, preferred_element_type=F32) + jnp.dot(h_hi, wr_ref[1], preferred_element_type=F32)
              + jnp.dot(h_mid, wr_hi, preferred_element_type=F32)) + br_ref[...]
    lane = lax.broadcasted_iota(I32, logits.shape, 1)
    te = jnp.zeros(logits.shape, I32)
    tv = []
    for k in range(TOP_K):
        m = jnp.max(logits, axis=1, keepdims=True)
        idx = jnp.min(jnp.where(logits == m, lane, LANES), axis=1, keepdims=True)
        te = jnp.where(lane == k, idx, te)
        tv.append(m)
        logits = jnp.where(lane == idx, -jnp.inf, logits)
    ex = [jnp.exp(v - tv[0]) for v in tv]
    den = ex[0] + ex[1] + ex[2] + ex[3]
    gw = jnp.zeros(logits.shape, F32)
    for k in range(TOP_K):
        gw = jnp.where(lane == k, ex[k] / den, gw)
    te_ref[...] = te
    gw_ref[...] = gw


def merge(x, zall, ys, oa, om, bg, ws, wa, wm, wo, g2, wr, br, tm):
    n = x.shape[0]
    gw3 = N_BRANCH * D_MODEL
    row = lambda i: (i, 0)
    const = lambda i: (0, 0)
    return pl.pallas_call(
        _merge_body,
        grid=(n // tm,),
        in_specs=[pl.BlockSpec((tm, D_MODEL), row),
                  pl.BlockSpec((tm, gw3), lambda i: (i, OFF_GATE // gw3)),
                  pl.BlockSpec((tm, D_INNER), row),
                  pl.BlockSpec((tm, N_HEADS * HEAD_DIM), row),
                  pl.BlockSpec((tm, MEM_WIDTH), row),
                  pl.BlockSpec((1, gw3), const),
                  pl.BlockSpec((D_INNER, D_MODEL), const),
                  pl.BlockSpec((N_HEADS * HEAD_DIM, D_MODEL), const),
                  pl.BlockSpec((MEM_WIDTH, D_MODEL), const),
                  pl.BlockSpec((D_MODEL, D_MODEL), const),
                  pl.BlockSpec((1, D_MODEL), const),
                  pl.BlockSpec((2, D_MODEL, LANES), lambda i: (0, 0, 0)),
                  pl.BlockSpec((1, LANES), const)],
        out_specs=[pl.BlockSpec((tm, D_MODEL), row), pl.BlockSpec((tm, D_MODEL), row),
                   pl.BlockSpec((tm, LANES), row), pl.BlockSpec((tm, LANES), row)],
        out_shape=[jax.ShapeDtypeStruct((n, D_MODEL), F32), jax.ShapeDtypeStruct((n, D_MODEL), F32),
                   jax.ShapeDtypeStruct((n, LANES), I32), jax.ShapeDtypeStruct((n, LANES), F32)],
        compiler_params=_cparams("parallel"),
        name="merge",
    )(x, zall, ys, oa, om, bg, ws, wa, wm, wo, g2, wr, br)


def _moe_pos_body(te_ref, pos_ref, cnt_ref, carry_ref):
    i = pl.program_id(0)
    tt = te_ref.shape[0]

    @pl.when(i == 0)
    def _():
        carry_ref[...] = jnp.zeros(carry_ref.shape, F32)

    te = te_ref[...]
    lane = lax.broadcasted_iota(I32, (tt, LANES), 1)
    onehot = [lane == te[:, k:k + 1] for k in range(TOP_K)]
    msum = jnp.zeros((tt, LANES), F32)
    for k in range(TOP_K):
        msum = msum + jnp.where(onehot[k], 1.0, 0.0)
    strict = (lax.broadcasted_iota(I32, (tt, tt), 0) > lax.broadcasted_iota(I32, (tt, tt), 1))
    prefix = jnp.dot(jnp.where(strict, 1.0, 0.0).astype(BF16), msum.astype(BF16), preferred_element_type=F32)
    prefix = prefix + carry_ref[0:1, :]
    pos = jnp.zeros((tt, LANES), F32)
    for k in range(TOP_K):
        pk = jnp.sum(jnp.where(onehot[k], prefix, 0.0), axis=1, keepdims=True)
        pos = jnp.where(lane == k, pk, pos)
    pos_ref[...] = pos
    carry_ref[...] = carry_ref[...] + jnp.sum(msum, axis=0, keepdims=True)
    cnt_ref[...] = carry_ref[...]


def _moe_dest_body(te_ref, pos_ref, cnt_ref, dest_ref, be_ref, nu_ref, start_ref, *, bm):
    tt = te_ref.shape[0]

    @pl.when(pl.program_id(0) == 0)
    def _():
        cnt = cnt_ref[...]
        padded = jnp.floor((cnt + (bm - 1)) * (1.0 / bm)) * bm
        upper = (lax.broadcasted_iota(I32, (LANES, LANES), 0) < lax.broadcasted_iota(I32, (LANES, LANES), 1))
        pad_start = jnp.dot(padded, jnp.where(upper, 1.0, 0.0), precision=HIGHEST, preferred_element_type=F32)
        start_ref[...] = pad_start
        pad_end = pad_start + padded
        nb = be_ref.shape[0]
        bstart = (lax.broadcasted_iota(I32, (nb, LANES), 0) * bm).astype(F32)
        lane_b = lax.broadcasted_iota(I32, (nb, LANES), 1)
        done = jnp.where((pad_end[0:1, :] <= bstart) & (lane_b < N_EXPERTS), 1.0, 0.0)
        be = jnp.minimum(jnp.sum(done, axis=1, keepdims=True), N_EXPERTS - 1.0)
        be_ref[...] = jnp.broadcast_to(be, (nb, LANES)).astype(I32)
        total = jnp.sum(padded[0:1, :], axis=1, keepdims=True)
        nu_ref[...] = jnp.broadcast_to(total * (1.0 / bm), nu_ref.shape).astype(I32)

    pad_start = start_ref[0:1, :]
    te = te_ref[...]
    pos = pos_ref[...]
    lane = lax.broadcasted_iota(I32, (tt, LANES), 1)
    dest = jnp.zeros((tt, LANES), F32)
    for k in range(TOP_K):
        ps = jnp.sum(jnp.where(lane == te[:, k:k + 1], pad_start, 0.0), axis=1, keepdims=True)
        dest = jnp.where(lane == k, ps + pos[:, k:k + 1], dest)
    dest_ref[...] = dest.astype(I32)


def _moe_dispatch_body(dest_ref, h_ref, xs_in_ref, xs_ref, sem):
    del xs_in_ref
    tt = h_ref.shape[0]

    def copy(r, k):
        d = dest_ref[r * TOP_K + k]
        return pltpu.make_async_copy(h_ref.at[pl.ds(r, 1), :], xs_ref.at[pl.ds(d, 1), :], sem)

    def issue(r, carry):
        for k in range(TOP_K):
            copy(r, k).start()
        return carry
    lax.fori_loop(0, tt, issue, 0)

    def drain(r, carry):
        for k in range(TOP_K):
            copy(r, k).wait()
        return carry
    lax.fori_loop(0, tt, drain, 0)


def _moe_expert_body(be_ref, nu_ref, xs_ref, w1_ref, b1_ref, w2_ref, b2_ref, o_ref, w1s_ref, w2s_ref):
    i = pl.program_id(0)
    used = i < nu_ref[0]
    e = be_ref[i]
    prev = be_ref[jnp.maximum(i - 1, 0)]
    half = LANES

    @pl.when(used & ((i == 0) | (e != prev)))
    def _():
        r = lax.broadcasted_iota(I32, (2 * half, 2 * half), 0)
        c = lax.broadcasted_iota(I32, (2 * half, 2 * half), 1)
        src_col = jnp.where(c < half, 2 * c, 2 * (c - half) + 1)
        perm = jnp.where(r == src_col, 1.0, 0.0).astype(BF16)
        for j in range(2 * D_FF // (2 * half)):
            sl = slice(j * 2 * half, (j + 1) * 2 * half)
            w1s_ref[:, sl] = jnp.dot(w1_ref[0, :, sl].astype(BF16), perm, preferred_element_type=F32).astype(BF16)
        w2s_ref[...] = w2_ref[0].astype(BF16)

    @pl.when(used)
    def _():
        u = jnp.dot(xs_ref[...].astype(BF16), w1s_ref[...], preferred_element_type=F32) + b1_ref[0]
        acts = []
        for j in range(D_FF // half):
            glu = jnp.minimum(u[:, 2 * j * half:(2 * j + 1) * half], SWIGLU_LIMIT)
            lin = jnp.clip(u[:, (2 * j + 1) * half:(2 * j + 2) * half], -SWIGLU_LIMIT, SWIGLU_LIMIT)
            acts.append((glu * jax.nn.sigmoid(SWIGLU_ALPHA * glu) * (lin + 1.0)).astype(BF16))
        act = jnp.concatenate(acts, axis=1)
        o_ref[...] = jnp.dot(act, w2s_ref[...], preferred_element_type=F32) + b2_ref[0]

    @pl.when(jnp.logical_not(used))
    def _():
        o_ref[...] = jnp.zeros(o_ref.shape, F32)


def _moe_combine_body(dest_ref, gw_ref, x1_ref, gf_ref, os_ref, y_ref, buf_ref, sem):
    tt = x1_ref.shape[0]

    def copy(r, k):
        d = dest_ref[r * TOP_K + k]
        return pltpu.make_async_copy(os_ref.at[pl.ds(d, 1), :], buf_ref.at[k, pl.ds(r, 1), :], sem)

    def issue(r, carry):
        for k in range(TOP_K):
            copy(r, k).start()
        return carry
    lax.fori_loop(0, tt, issue, 0)

    def drain(r, carry):
        for k in range(TOP_K):
            copy(r, k).wait()
        return carry
    lax.fori_loop(0, tt, drain, 0)

    gw = gw_ref[...]
    y = gw[:, 0:1] * buf_ref[0]
    for k in range(1, TOP_K):
        y = y + gw[:, k:k + 1] * buf_ref[k]
    x2 = x1_ref[...] + y
    out = x2 * lax.rsqrt(jnp.mean(x2 * x2, axis=-1, keepdims=True) + EPS)
    y_ref[...] = out * gf_ref[...]


def moe_and_final_norm(x1a, x1b, h2a, h2b, te, gw, w1, b1p, w2, b2, g_final):
    n = te.shape[0]
    rows_a = x1a.shape[0]
    tt = MOE_T

    def move_tile(rows):
        return MOE_MOVE_T if rows % MOE_MOVE_T == 0 else MOE_T

    bm = MOE_BM
    nb = -(-(n * TOP_K + N_EXPERTS * (bm - 1)) // bm)
    nbp = -(-nb // SUBLANES) * SUBLANES
    row = lambda i: (i, 0)
    const = lambda i: (0, 0)

    pos, cnt = pl.pallas_call(
        _moe_pos_body,
        grid=(n // tt,),
        in_specs=[pl.BlockSpec((tt, LANES), row)],
        out_specs=[pl.BlockSpec((tt, LANES), row), pl.BlockSpec((SUBLANES, LANES), const)],
        out_shape=[jax.ShapeDtypeStruct((n, LANES), F32), jax.ShapeDtypeStruct((SUBLANES, LANES), F32)],
        scratch_shapes=[pltpu.VMEM((SUBLANES, LANES), F32)],
        compiler_params=_cparams("arbitrary"),
        name="moe_positions",
    )(te)

    dest, be, nu = pl.pallas_call(
        functools.partial(_moe_dest_body, bm=bm),
        grid=(n // tt,),
        in_specs=[pl.BlockSpec((tt, LANES), row), pl.BlockSpec((tt, LANES), row),
                  pl.BlockSpec((SUBLANES, LANES), const)],
        out_specs=[pl.BlockSpec((tt, LANES), row), pl.BlockSpec((nbp, LANES), const),
                   pl.BlockSpec((SUBLANES, LANES), const)],
        out_shape=[jax.ShapeDtypeStruct((n, LANES), I32), jax.ShapeDtypeStruct((nbp, LANES), I32),
                   jax.ShapeDtypeStruct((SUBLANES, LANES), I32)],
        scratch_shapes=[pltpu.VMEM((SUBLANES, LANES), F32)],
        compiler_params=_cparams("arbitrary"),
        name="moe_destinations",
    )(te, pos, cnt)
    dest_flat = dest[:, :TOP_K].reshape(n * TOP_K)
    block_e = be[:nb, 0]
    n_used = nu[0, 0:1]

    def dispatch(h_part, tok0, xs_buf):
        mt = move_tile(h_part.shape[0])
        tile0 = tok0 // mt
        return pl.pallas_call(
            _moe_dispatch_body,
            grid=(h_part.shape[0] // mt,),
            in_specs=[pl.BlockSpec((mt * TOP_K,), lambda i: (i + tile0,), memory_space=pltpu.SMEM),
                      pl.BlockSpec((mt, D_MODEL), row),
                      pl.BlockSpec(memory_space=pl.ANY)],
            out_specs=pl.BlockSpec(memory_space=pl.ANY),
            out_shape=jax.ShapeDtypeStruct((nb * bm, D_MODEL), F32),
            scratch_shapes=[pltpu.SemaphoreType.DMA(())],
            input_output_aliases={2: 0},
            compiler_params=_cparams("arbitrary"),
            name="moe_dispatch",
        )(dest_flat, h_part, xs_buf)

    xs = dispatch(h2b, rows_a, dispatch(h2a, 0, jnp.zeros((nb * bm, D_MODEL), F32)))

    out_sorted = pl.pallas_call(
        _moe_expert_body,
        grid_spec=pltpu.PrefetchScalarGridSpec(
            num_scalar_prefetch=2,
            grid=(nb,),
            in_specs=[pl.BlockSpec((bm, D_MODEL), lambda i, be_, nu_: (i, 0)),
                      pl.BlockSpec((1, D_MODEL, 2 * D_FF), lambda i, be_, nu_: (be_[i], 0, 0)),
                      pl.BlockSpec((1, 1, 2 * D_FF), lambda i, be_, nu_: (be_[i], 0, 0)),
                      pl.BlockSpec((1, D_FF, D_MODEL), lambda i, be_, nu_: (be_[i], 0, 0)),
                      pl.BlockSpec((1, 1, D_MODEL), lambda i, be_, nu_: (be_[i], 0, 0))],
            out_specs=pl.BlockSpec((bm, D_MODEL), lambda i, be_, nu_: (i, 0)),
            scratch_shapes=[pltpu.VMEM((D_MODEL, 2 * D_FF), BF16), pltpu.VMEM((D_FF, D_MODEL), BF16)]),
        out_shape=jax.ShapeDtypeStruct((nb * bm, D_MODEL), F32),
        compiler_params=_cparams("arbitrary"),
        name="moe_experts",
    )(block_e, n_used, xs, w1, b1p, w2, b2)

    def combine(x1_part, tok0):
        mt = move_tile(x1_part.shape[0])
        tile0 = tok0 // mt
        return pl.pallas_call(
            _moe_combine_body,
            grid=(x1_part.shape[0] // mt,),
            in_specs=[pl.BlockSpec((mt * TOP_K,), lambda i: (i + tile0,), memory_space=pltpu.SMEM),
                      pl.BlockSpec((mt, LANES), lambda i: (i + tile0, 0)),
                      pl.BlockSpec((mt, D_MODEL), row),
                      pl.BlockSpec((1, D_MODEL), const),
                      pl.BlockSpec(memory_space=pl.ANY)],
            out_specs=pl.BlockSpec((mt, D_MODEL), row),
            out_shape=jax.ShapeDtypeStruct(x1_part.shape, F32),
            scratch_shapes=[pltpu.VMEM((TOP_K, mt, D_MODEL), F32), pltpu.SemaphoreType.DMA(())],
            compiler_params=_cparams("arbitrary"),
            name="moe_combine",
        )(dest_flat, gw, x1_part, g_final.reshape(1, D_MODEL), out_sorted)

    return combine(x1a, 0), combine(x1b, rows_a)


def _split_cols(w):
    outs, off = [], 0
    for wd in IN_WIDTHS:
        outs.append(w[:, off:off + wd])
        off += wd
    return outs


def _lane_row(v, off):
    return jnp.zeros((1, LANES), F32).at[0, off:off + v.shape[0]].set(v.astype(F32))


def kernel(x_prompt, x_sample, cache_k, cache_v, cache_idx_k, state_conv, state_ssm, cache_mem_k, cache_mem_v,
           page_table, mem_prompt, g_norm1, w_in, b_gate, conv_w, conv_b, dt_bias, a_log, d_skip, g_ssd_norm,
           g_mem, w_mem_kv, w_ssd_out, w_attn_out, w_mem_out, w_out, g_norm2, w_router, b_router, w_exp1,
           b_exp1, w_exp2, b_exp2, g_final):
    assert w_in.shape[0] == 1, "single-layer trunk"
    bp, lp, _ = x_prompt.shape
    bs, ls, _ = x_sample.shape
    np_, ns = bp * lp, bs * ls

    wz, wxbc, wdt, wq, wk, wv, wqi, wki, wwi, wqm, wgate = _split_cols(w_in[0])
    w_all = jnp.concatenate(
        [wxbc, wgate, wz, wq, wqm, wqi, wk, wv, wki, wdt, wwi,
         jnp.zeros((D_MODEL, W_ALL - OFF_SM - SM_WI - IDX_HEADS), F32)], axis=1).astype(BF16)
    dtb_row = _lane_row(dt_bias[0], SM_DT)
    aneg_row = _lane_row(-jnp.exp(a_log[0].astype(F32)), SM_DT)
    dsk_row = jnp.repeat(d_skip[0].astype(F32), SSD_HEAD_DIM).reshape(1, D_INNER)
    gs_row = g_ssd_norm[0].reshape(1, D_INNER)
    cb_row = conv_b[0].reshape(1, CONV_DIM)
    wr_f32 = jnp.zeros((D_MODEL, LANES), F32).at[:, :N_EXPERTS].set(w_router[0])
    wr_hi = wr_f32.astype(BF16)
    wr_pad = jnp.stack([wr_hi, (wr_f32 - wr_hi.astype(F32)).astype(BF16)])
    br_pad = jnp.full((1, LANES), NEG, F32).at[0, :N_EXPERTS].set(b_router[0])
    b1p = b_exp1[0].reshape(N_EXPERTS, D_FF // LANES, LANES, 2).transpose(0, 1, 3, 2).reshape(N_EXPERTS, 1, 2 * D_FF)
    b2 = b_exp2[0].reshape(N_EXPERTS, 1, D_MODEL)

    xp = x_prompt.reshape(np_, D_MODEL)
    xs = x_sample.reshape(ns, D_MODEL)
    zp = norm_matmul(xp, g_norm1[0], w_all, 1024, IN_PROJ_TN)
    zs = norm_matmul(xs, g_norm1[0], w_all, ns, IN_PROJ_TN)

    kv_p = norm_matmul(mem_prompt.reshape(bp * N_MEM, D_MODEL), g_mem[0], w_mem_kv[0].astype(BF16),
                       min(1024, bp * N_MEM), MEM_WIDTH)
    om_p = mem_attn(zp, bp, lp, kv_p, 0, kv_p, 1, 512)
    om_s = mem_attn(zs, bs, ls, cache_mem_k[0].reshape(bs * N_MEM, MEM_WIDTH), 0,
                    cache_mem_v[0].reshape(bs * N_MEM, MEM_WIDTH), 0, ls)

    conv_prev_p = jnp.zeros((bp, SUBLANES, CONV_DIM), F32)
    conv_prev_s = jnp.concatenate(
        [jnp.zeros((bs, SUBLANES - (CONV_WIDTH - 1), CONV_DIM), F32), state_conv[0]], axis=1)
    ssm0_p = jnp.zeros((bp, D_INNER, D_STATE), F32)
    ssm0_s = state_ssm[0].reshape(bs, D_INNER, D_STATE)
    ys_p, ssm_p = ssd(zp, bp, lp, conv_prev_p, ssm0_p, conv_w[0], cb_row, dtb_row, aneg_row, dsk_row, gs_row)
    ys_s, ssm_s = ssd(zs, bs, ls, conv_prev_s, ssm0_s, conv_w[0], cb_row, dtb_row, aneg_row, dsk_row, gs_row)

    oa_p = dsa_prompt(zp, bp, lp)
    oa_s = dsa_sample(zs, bs, ls, cache_k[0], cache_v[0], cache_idx_k[0], page_table)

    mw = (b_gate[0].reshape(1, -1), w_ssd_out[0].astype(BF16), w_attn_out[0].astype(BF16),
          w_mem_out[0].astype(BF16), w_out[0].astype(BF16), g_norm2[0].reshape(1, D_MODEL), wr_pad, br_pad)
    x1_p, h2_p, te_p, gw_p = merge(xp, zp, ys_p, oa_p, om_p, *mw, 512)
    x1_s, h2_s, te_s, gw_s = merge(xs, zs, ys_s, oa_s, om_s, *mw, ns)

    cat = lambda a, b: jnp.concatenate([a, b], axis=0)
    y_all = moe_and_final_norm(x1_p, x1_s, h2_p, h2_s, cat(te_p, te_s), cat(gw_p, gw_s),
                               w_exp1[0], b1p, w_exp2[0], b2, g_final)
    y_prompt = y_all[0].reshape(bp, lp, D_MODEL)
    y_sample = y_all[1].reshape(bs, ls, D_MODEL)

    def kvi(z, b, l):
        k = z[:, OFF_K:OFF_K + N_KV_HEADS * HEAD_DIM].reshape(1, b, l, N_KV_HEADS, HEAD_DIM)
        v = z[:, OFF_V:OFF_V + N_KV_HEADS * HEAD_DIM].reshape(1, b, l, N_KV_HEADS, HEAD_DIM)
        ki = z[:, OFF_SM + SM_KI:OFF_SM + SM_KI + IDX_DIM].reshape(1, b, l, IDX_DIM)
        conv = z.reshape(b, l, W_ALL)[:, l - (CONV_WIDTH - 1):, OFF_XBC:OFF_XBC + CONV_DIM][None]
        return k, v, ki, conv

    k_p, v_p, ki_p, conv_p = kvi(zp, bp, lp)
    k_s, v_s, ki_s, conv_s = kvi(zs, bs, ls)
    mk_p = kv_p[:, :MEM_WIDTH].reshape(1, bp, N_MEM, MEM_HEADS, MEM_HEAD_DIM)
    mv_p = kv_p[:, MEM_WIDTH:].reshape(1, bp, N_MEM, MEM_HEADS, MEM_HEAD_DIM)
    ssm_shape = (1, -1, SSD_HEADS, SSD_HEAD_DIM, D_STATE)
    return (y_prompt, y_sample, k_p, v_p, ki_p, conv_p, ssm_p.reshape(ssm_shape), mk_p, mv_p,
            k_s, v_s, ki_s, conv_s, ssm_s.reshape(ssm_shape))
```

```python
import functools

import numpy as np
import jax
import jax.numpy as jnp
from jax import lax
from jax.experimental import pallas as pl
from jax.experimental.pallas import tpu as pltpu

F32 = jnp.float32
BF16 = jnp.bfloat16
I32 = jnp.int32
HIGHEST = lax.Precision.HIGHEST

D_MODEL = 1024
D_INNER = 2048
SSD_HEAD_DIM = 64
SSD_HEADS = 32
SSD_GROUPS = 4
D_STATE = 128
CONV_WIDTH = 4
CONV_DIM = D_INNER + 2 * SSD_GROUPS * D_STATE
SSD_CHUNK = 128
N_HEADS = 16
N_KV_HEADS = 4
HEAD_DIM = 64
IDX_HEADS = 8
IDX_DIM = 64
TOPK_MAX = 256
N_MEM = 256
MEM_HEADS = 4
MEM_HEAD_DIM = 256
MEM_WIDTH = MEM_HEADS * MEM_HEAD_DIM
N_EXPERTS = 32
TOP_K = 4
D_FF = D_MODEL
SWIGLU_LIMIT = 7.0
SWIGLU_ALPHA = 1.702
N_BRANCH = 3
EPS = 1e-6
PAGE_SIZE = 128
IN_WIDTHS = (D_INNER, CONV_DIM, SSD_HEADS, N_HEADS * HEAD_DIM, N_KV_HEADS * HEAD_DIM, N_KV_HEADS * HEAD_DIM,
             IDX_HEADS * IDX_DIM, IDX_DIM, IDX_HEADS, MEM_WIDTH, N_BRANCH * D_MODEL)

LANES = 128
SUBLANES = 8
VMEM_LIMIT = 56 * 1024 * 1024

OFF_XBC = 0
OFF_GATE = OFF_XBC + CONV_DIM
OFF_Z = OFF_GATE + N_BRANCH * D_MODEL
OFF_Q = OFF_Z + D_INNER
OFF_QM = OFF_Q + N_HEADS * HEAD_DIM
OFF_QI = OFF_QM + MEM_WIDTH
OFF_K = OFF_QI + IDX_HEADS * IDX_DIM
OFF_V = OFF_K + N_KV_HEADS * HEAD_DIM
OFF_SM = OFF_V + N_KV_HEADS * HEAD_DIM
SM_KI = 0
SM_DT = SM_KI + IDX_DIM
SM_WI = SM_DT + SSD_HEADS
IN_PROJ_TN = 1280
W_ALL = OFF_SM + 2 * LANES

NEG = -1e30
INT_MIN = np.int32(-2 ** 31)
INT_MAX = np.int32(2 ** 31 - 1)

ATT_PASSES = 1
MOE_BM = 512
MOE_T = 256
MOE_MOVE_T = 512


def _cparams(*sem):
    return pltpu.CompilerParams(dimension_semantics=sem, vmem_limit_bytes=VMEM_LIMIT)


def _nt_dot(a, b):
    return lax.dot_general(a, b, (((1,), (1,)), ((), ())), preferred_element_type=F32)


def _float_key(x):
    bits = lax.bitcast_convert_type(x, I32)
    return jnp.where(bits < 0, bits ^ INT_MAX, bits)


def _norm_matmul_body(x_ref, g_ref, w_ref, o_ref, h_ref):
    @pl.when(pl.program_id(1) == 0)
    def _():
        x = x_ref[...]
        h = x * lax.rsqrt(jnp.mean(x * x, axis=-1, keepdims=True) + EPS)
        h_ref[...] = (h * g_ref[...]).astype(BF16)

    o_ref[...] = jnp.dot(h_ref[...], w_ref[...], preferred_element_type=F32)


def norm_matmul(x, g, w, tm, tn):
    n, d = x.shape
    wn = w.shape[1]
    return pl.pallas_call(
        _norm_matmul_body,
        grid=(n // tm, wn // tn),
        in_specs=[pl.BlockSpec((tm, d), lambda i, j: (i, 0)),
                  pl.BlockSpec((1, d), lambda i, j: (0, 0)),
                  pl.BlockSpec((d, tn), lambda i, j: (0, j))],
        out_specs=pl.BlockSpec((tm, tn), lambda i, j: (i, j)),
        out_shape=jax.ShapeDtypeStruct((n, wn), F32),
        scratch_shapes=[pltpu.VMEM((tm, d), BF16)],
        compiler_params=_cparams("parallel", "arbitrary"),
        name="norm_matmul",
    )(x, g.reshape(1, d), w)


def _softplus(x):
    return jnp.maximum(x, 0.0) + jnp.log1p(jnp.exp(-jnp.abs(x)))


def _silu(x):
    return x * jax.nn.sigmoid(x)


def _expand_heads(v, sel):
    hi = v.astype(BF16)
    r1 = v - hi.astype(F32)
    mid = r1.astype(BF16)
    lo = (r1 - mid.astype(F32)).astype(BF16)
    out = jnp.dot(hi, sel, preferred_element_type=F32)
    out = out + jnp.dot(mid, sel, preferred_element_type=F32)
    return out + jnp.dot(lo, sel, preferred_element_type=F32)


def _ssd_body(xbc_ref, z_ref, sm_ref, convp_ref, init_ref, cw_ref, cb_ref, dtb_ref, aneg_ref, dsk_ref, gs_ref, sel_ref,
              y_ref, st_ref, carry_ref, state_ref, *, rows_in, q):
    c = pl.program_id(1)
    taps = CONV_WIDTH - 1

    @pl.when(c == 0)
    def _():
        prev = convp_ref[0]
        for j in range(taps):
            acc0 = cw_ref[j:j + 1, :] * prev[SUBLANES - 1:SUBLANES, :]
            for m in range(1, j + 1):
                acc0 = acc0 + cw_ref[j - m:j - m + 1, :] * prev[SUBLANES - 1 - m:SUBLANES - m, :]
            carry_ref[j:j + 1, :] = acc0
        state_ref[...] = init_ref[0]

    x = xbc_ref[...]
    if rows_in < q:
        x = jnp.concatenate([x, jnp.zeros((q - rows_in, CONV_DIM), F32)], axis=0)

    first_row = lax.broadcasted_iota(I32, (q, CONV_DIM), 0) == 0
    shifted = None
    for j in range(taps):
        stage = cw_ref[j:j + 1, :] * x
        if shifted is not None:
            stage = stage + shifted
        shifted = jnp.where(first_row, carry_ref[j:j + 1, :], pltpu.roll(stage, 1, 0))
        carry_ref[j:j + 1, :] = stage[q - 1:q, :]
    xc = _silu(cw_ref[taps:taps + 1, :] * x + shifted + cb_ref[...])

    xs = xc[:, :D_INNER]
    gn = SSD_GROUPS * D_STATE
    bm = xc[:, D_INNER:D_INNER + gn].astype(BF16)
    cm = xc[:, D_INNER + gn:].astype(BF16)

    sm = sm_ref[...]
    zz = z_ref[...]
    if rows_in < q:
        sm = jnp.concatenate([sm, jnp.zeros((q - rows_in, LANES), F32)], axis=0)
        zz = jnp.concatenate([zz, jnp.zeros((q - rows_in, D_INNER), F32)], axis=0)
    row = lax.broadcasted_iota(I32, (q, LANES), 0)
    dt = _softplus(sm + dtb_ref[...])
    if rows_in < q:
        dt = jnp.where(row < rows_in, dt, 0.0)
    a = dt * aneg_ref[...]
    tri = (lax.broadcasted_iota(I32, (q, q), 0) >= lax.broadcasted_iota(I32, (q, q), 1)).astype(F32)
    a_cs = jnp.dot(tri, a, precision=HIGHEST, preferred_element_type=F32)
    a_t = a_cs.T
    a_last = a_cs[q - 1:q, :]
    dte = jnp.exp(a_last - a_cs)
    e_in = jnp.exp(a_cs)

    sel = sel_ref[...]
    xdt = xs * _expand_heads(dt, sel)
    xdt_bf = xdt.astype(BF16)
    xw_bf = (xs * _expand_heads(dt * dte, sel)).astype(BF16)
    ein_x = _expand_heads(e_in, sel)

    causal = lax.broadcasted_iota(I32, (q, q), 0) >= lax.broadcasted_iota(I32, (q, q), 1)
    lane = lax.broadcasted_iota(I32, (q, LANES), 1)
    hpg = SSD_HEADS // SSD_GROUPS
    gw = hpg * SSD_HEAD_DIM
    y_parts = []
    for g in range(SSD_GROUPS):
        bg = bm[:, g * D_STATE:(g + 1) * D_STATE]
        cg = cm[:, g * D_STATE:(g + 1) * D_STATE]
        cb = _nt_dot(cg, bg)
        m_h = []
        for e in range(hpg):
            h = g * hpg + e
            col = a_cs[:, SM_DT + h:SM_DT + h + 1]
            rw = a_t[SM_DT + h:SM_DT + h + 1, :]
            decay = jnp.exp(jnp.where(causal, col - rw, -jnp.inf))
            m_h.append((cb * decay).astype(BF16))
        yd = []
        for t in range(hpg // 2):
            pair = g * (hpg // 2) + t
            slab = xdt_bf[:, pair * LANES:(pair + 1) * LANES]
            ya = jnp.dot(m_h[2 * t], slab, preferred_element_type=F32)
            yb = jnp.dot(m_h[2 * t + 1], slab, preferred_element_type=F32)
            yd.append(jnp.where(lane < SSD_HEAD_DIM, ya, yb))
        s_old = state_ref[g * gw:(g + 1) * gw, :]
        y_off = _nt_dot(cg, s_old.astype(BF16)) * ein_x[:, g * gw:(g + 1) * gw]
        y_parts.append(jnp.concatenate(yd, axis=1) + y_off)
        new = lax.dot_general(xw_bf[:, g * gw:(g + 1) * gw], bg, (((0,), (0,)), ((), ())),
                              preferred_element_type=F32)
        for e in range(hpg):
            h = g * hpg + e
            dec = jnp.exp(a_t[SM_DT + h:SM_DT + h + 1, q - 1:q])
            lo = e * SSD_HEAD_DIM
            state_ref[h * SSD_HEAD_DIM:(h + 1) * SSD_HEAD_DIM, :] = (
                s_old[lo:lo + SSD_HEAD_DIM, :] * dec + new[lo:lo + SSD_HEAD_DIM, :])

    y = jnp.concatenate(y_parts, axis=1) + dsk_ref[...] * xs
    y = y * _silu(zz)
    outs = []
    for g in range(SSD_GROUPS):
        yg = y[:, g * gw:(g + 1) * gw]
        outs.append(yg * lax.rsqrt(jnp.mean(yg * yg, axis=-1, keepdims=True) + EPS))
    y = jnp.concatenate(outs, axis=1) * gs_ref[...]
    y_ref[...] = y[:rows_in].astype(BF16)

    @pl.when(c == pl.num_programs(1) - 1)
    def _():
        st_ref[0] = state_ref[...]


def ssd(zall, batch, seq, conv_prev8, ssm_init, conv_w, conv_b, dtb_row, aneg_row, dsk_row, gs_row):
    q = SSD_CHUNK
    rows_in = min(seq, q)
    nch = seq // rows_in
    sel = (jnp.arange(LANES)[:, None] - SM_DT == jnp.arange(D_INNER)[None, :] // SSD_HEAD_DIM).astype(BF16)
    row_map = lambda b, c: (b * nch + c)
    const2 = lambda b, c: (0, 0)
    body = functools.partial(_ssd_body, rows_in=rows_in, q=q)
    return pl.pallas_call(
        body,
        grid=(batch, nch),
        in_specs=[pl.BlockSpec((rows_in, CONV_DIM), lambda b, c: (row_map(b, c), OFF_XBC // CONV_DIM)),
                  pl.BlockSpec((rows_in, D_INNER), lambda b, c: (row_map(b, c), OFF_Z // D_INNER)),
                  pl.BlockSpec((rows_in, LANES), lambda b, c: (row_map(b, c), OFF_SM // LANES)),
                  pl.BlockSpec((1, SUBLANES, CONV_DIM), lambda b, c: (b, 0, 0)),
                  pl.BlockSpec((1, D_INNER, D_STATE), lambda b, c: (b, 0, 0)),
                  pl.BlockSpec((CONV_WIDTH, CONV_DIM), const2),
                  pl.BlockSpec((1, CONV_DIM), const2),
                  pl.BlockSpec((1, LANES), const2),
                  pl.BlockSpec((1, LANES), const2),
                  pl.BlockSpec((1, D_INNER), const2),
                  pl.BlockSpec((1, D_INNER), const2),
                  pl.BlockSpec((LANES, D_INNER), const2)],
        out_specs=[pl.BlockSpec((rows_in, D_INNER), lambda b, c: (row_map(b, c), 0)),
                   pl.BlockSpec((1, D_INNER, D_STATE), lambda b, c: (b, 0, 0))],
        out_shape=[jax.ShapeDtypeStruct((batch * seq, D_INNER), BF16),
                   jax.ShapeDtypeStruct((batch, D_INNER, D_STATE), F32)],
        scratch_shapes=[pltpu.VMEM((SUBLANES, CONV_DIM), F32),
                        pltpu.VMEM((D_INNER, D_STATE), F32)],
        compiler_params=_cparams("parallel", "arbitrary"),
        name="ssd",
    )(zall, zall, zall, conv_prev8, ssm_init, conv_w, conv_b, dtb_row, aneg_row, dsk_row, gs_row, sel)


def _mem_attn_body(q_ref, k_ref, v_ref, o_ref):
    for h in range(MEM_HEADS):
        sl = slice(h * MEM_HEAD_DIM, (h + 1) * MEM_HEAD_DIM)
        s = _nt_dot(q_ref[:, sl].astype(BF16), k_ref[:, sl].astype(BF16)) * (MEM_HEAD_DIM ** -0.5)
        m = jnp.max(s, axis=-1, keepdims=True)
        p = jnp.exp(s - m)
        p = p / jnp.sum(p, axis=-1, keepdims=True)
        o = jnp.dot(p.astype(BF16), v_ref[:, sl].astype(BF16), preferred_element_type=F32)
        o_ref[:, sl] = o.astype(BF16)


def mem_attn(zall, batch, seq, k_arr, k_col, v_arr, v_col, tm):
    nt = seq // tm
    return pl.pallas_call(
        _mem_attn_body,
        grid=(batch, nt),
        in_specs=[pl.BlockSpec((tm, MEM_WIDTH), lambda b, i: (b * nt + i, OFF_QM // MEM_WIDTH)),
                  pl.BlockSpec((N_MEM, MEM_WIDTH), lambda b, i: (b, k_col)),
                  pl.BlockSpec((N_MEM, MEM_WIDTH), lambda b, i: (b, v_col))],
        out_specs=pl.BlockSpec((tm, MEM_WIDTH), lambda b, i: (b * nt + i, 0)),
        out_shape=jax.ShapeDtypeStruct((batch * seq, MEM_WIDTH), BF16),
        compiler_params=_cparams("parallel", "arbitrary"),
        name="mem_attn",
    )(zall, k_arr, v_arr)


def _kth_largest_key(count_ge, shape, n_sel):
    def bit_body(t, ans):
        cand = ans | jnp.left_shift(jnp.int32(1), 31 - t)
        cnt = count_ge(cand ^ INT_MIN)
        return jnp.where(cnt >= n_sel, cand, ans)

    ans = lax.fori_loop(0, 32, bit_body, jnp.zeros(shape, I32))
    return ans ^ INT_MIN


def _tie_cut(count_eq_below, need, shape, nbits):
    def bit_body(t, lo):
        cand = lo | jnp.left_shift(jnp.int32(1), nbits - 1 - t)
        cnt = count_eq_below(cand)
        return jnp.where(cnt < need, cand, lo)

    return lax.fori_loop(0, nbits, bit_body, jnp.zeros(shape, I32))


def _select_bias(key, kpos, thr, cut, visible):
    sel = (key > thr) | ((key == thr) & (kpos <= cut))
    return jnp.where(sel & visible, 0.0, NEG)


def _dsa_prompt_body(q_ref, qi_ref, smq_ref, k_ref, v_ref, smk_ref, o_ref,
                     kh_ref, vt_ref, kis_ref, qt2_ref, qit_ref, keys_ref, bias_ref, ot_ref, *, tq, kc, seq, n_sel):
    i = pl.program_id(1)

    @pl.when(i == 0)
    def _():
        def cast_rows(r, carry):
            rs = pl.ds(pl.multiple_of(r * kc, kc), kc)
            kk = k_ref[rs, :]
            for h in range(N_KV_HEADS):
                kh_ref[h, rs, :] = kk[:, h * HEAD_DIM:(h + 1) * HEAD_DIM].astype(BF16)
            vt_ref[r] = v_ref[rs, :].T.astype(BF16)
            kis_ref[rs, :] = smk_ref[rs, SM_KI:SM_KI + IDX_DIM].astype(BF16)
            return carry
        lax.fori_loop(0, seq // kc, cast_rows, 0)

    nkc = (i * tq + tq - 1) // kc + 1
    qpos = i * tq + lax.broadcasted_iota(I32, (kc, tq), 1)
    krow = lax.broadcasted_iota(I32, (kc, tq), 0)

    qt = (q_ref[...] * (HEAD_DIM ** -0.5)).T.astype(BF16)
    for h in range(N_HEADS):
        qt2_ref[:, h * tq:(h + 1) * tq] = qt[h * HEAD_DIM:(h + 1) * HEAD_DIM, :]
    qit = (qi_ref[...] * (IDX_DIM ** -0.5)).T.astype(BF16)
    for h in range(IDX_HEADS):
        qit_ref[:, h * tq:(h + 1) * tq] = qit[h * IDX_DIM:(h + 1) * IDX_DIM, :]
    wt = smq_ref[...].T[SM_WI:SM_WI + IDX_HEADS, :] * (IDX_HEADS ** -0.5)

    def score_chunk(c, carry):
        ks = pl.ds(pl.multiple_of(c * kc, kc), kc)
        d = jnp.dot(kis_ref[ks, :], qit_ref[...], preferred_element_type=F32)
        sc = jnp.zeros((kc, tq), F32)
        for h in range(IDX_HEADS):
            sc = sc + jnp.maximum(d[:, h * tq:(h + 1) * tq], 0.0) * wt[h:h + 1, :]
        key = _float_key(sc + 0.0)
        keys_ref[c] = jnp.where(c * kc + krow <= qpos, key, INT_MIN)
        return carry
    lax.fori_loop(0, nkc, score_chunk, 0)

    def count(pred):
        def body(c, acc):
            return acc + jnp.where(pred(keys_ref[c], c * kc + krow), 1.0, 0.0)
        acc = lax.fori_loop(0, nkc, body, jnp.zeros((kc, tq), F32))
        return jnp.sum(acc, axis=0, keepdims=True)

    vec = (1, tq)
    thr = _kth_largest_key(lambda cand: count(lambda key, kpos: key >= cand), vec, n_sel)
    n_gt = count(lambda key, kpos: key > thr)
    n_eq = count(lambda key, kpos: key == thr)
    need = n_sel - n_gt
    excess = jnp.max(jnp.where((n_eq > need) & (thr != INT_MIN), 1.0, 0.0))
    nbits = max(1, int(seq - 1).bit_length())
    cut = lax.cond(
        excess > 0.0,
        lambda: _tie_cut(lambda cand: count(lambda key, kpos: (key == thr) & (kpos < cand)), need, vec, nbits),
        lambda: jnp.full(vec, INT_MAX, I32))

    def bias_chunk(c, carry):
        kpos = c * kc + krow
        bias_ref[c] = _select_bias(keys_ref[c], kpos, thr, cut, kpos <= qpos)
        return carry
    lax.fori_loop(0, nkc, bias_chunk, 0)

    grp = N_HEADS // N_KV_HEADS
    kv_per_pass = N_KV_HEADS // ATT_PASSES
    hpp = kv_per_pass * grp
    for ps_i in range(ATT_PASSES):
        kv0 = ps_i * kv_per_pass

        def att_chunk(c, carry, kv0=kv0):
            ms, ls, accs = carry
            ks = pl.ds(pl.multiple_of(c * kc, kc), kc)
            bias = bias_ref[c]
            s4 = [jnp.dot(kh_ref[kv0 + j, ks, :], qt2_ref[:, (kv0 + j) * grp * tq:(kv0 + j + 1) * grp * tq],
                          preferred_element_type=F32) for j in range(kv_per_pass)]
            ms_n, ls_n, accs_n = [], [], []
            for h in range(hpp):
                s = s4[h // grp][:, (h % grp) * tq:(h % grp + 1) * tq] + bias
                m_new = jnp.maximum(ms[h], jnp.max(s, axis=0, keepdims=True))
                p = jnp.exp(s - m_new)
                alpha = jnp.exp(ms[h] - m_new)
                ms_n.append(m_new)
                ls_n.append(alpha * ls[h] + jnp.sum(p, axis=0, keepdims=True))
                kh = kv0 + h // grp
                vtc = vt_ref[c, kh * HEAD_DIM:(kh + 1) * HEAD_DIM, :]
                accs_n.append(alpha * accs[h] + jnp.dot(vtc, p.astype(BF16), preferred_element_type=F32))
            return tuple(ms_n), tuple(ls_n), tuple(accs_n)

        init = (tuple(jnp.full(vec, NEG, F32) for _ in range(hpp)),
                tuple(jnp.zeros(vec, F32) for _ in range(hpp)),
                tuple(jnp.zeros((HEAD_DIM, tq), F32) for _ in range(hpp)))
        _, ls, accs = lax.fori_loop(0, nkc, att_chunk, init)
        for h in range(hpp):
            hh = kv0 * grp + h
            ot_ref[hh * HEAD_DIM:(hh + 1) * HEAD_DIM, :] = accs[h] / ls[h]
    o_ref[...] = ot_ref[...].T.astype(BF16)


def dsa_prompt(zall, batch, seq):
    tq = 128
    kc = 256
    nq = seq // tq
    n_sel = min(TOPK_MAX, seq // 4)
    kvw = N_KV_HEADS * HEAD_DIM
    qw = N_HEADS * HEAD_DIM
    qiw = IDX_HEADS * IDX_DIM
    body = functools.partial(_dsa_prompt_body, tq=tq, kc=kc, seq=seq, n_sel=n_sel)
    return pl.pallas_call(
        body,
        grid=(batch, nq),
        in_specs=[pl.BlockSpec((tq, qw), lambda b, i: (b * nq + i, OFF_Q // qw)),
                  pl.BlockSpec((tq, qiw), lambda b, i: (b * nq + i, OFF_QI // qiw)),
                  pl.BlockSpec((tq, LANES), lambda b, i: (b * nq + i, OFF_SM // LANES)),
                  pl.BlockSpec((seq, kvw), lambda b, i: (b, OFF_K // kvw)),
                  pl.BlockSpec((seq, kvw), lambda b, i: (b, OFF_V // kvw)),
                  pl.BlockSpec((seq, LANES), lambda b, i: (b, OFF_SM // LANES))],
        out_specs=pl.BlockSpec((tq, qw), lambda b, i: (b * nq + i, 0)),
        out_shape=jax.ShapeDtypeStruct((batch * seq, qw), BF16),
        scratch_shapes=[pltpu.VMEM((N_KV_HEADS, seq, HEAD_DIM), BF16),
                        pltpu.VMEM((seq // kc, kvw, kc), BF16),
                        pltpu.VMEM((seq, IDX_DIM), BF16),
                        pltpu.VMEM((HEAD_DIM, N_HEADS * tq), BF16),
                        pltpu.VMEM((IDX_DIM, IDX_HEADS * tq), BF16),
                        pltpu.VMEM((seq // kc, kc, tq), I32),
                        pltpu.VMEM((seq // kc, kc, tq), F32),
                        pltpu.VMEM((qw, tq), F32)],
        compiler_params=_cparams("parallel", "arbitrary"),
        name="dsa_prompt",
    )(zall, zall, zall, zall, zall, zall)


def _dsa_s_score_body(pt_ref, qs_ref, w_ref, *refs, pg):
    ki_refs, o_ref = refs[:pg], refs[pg]
    qs = qs_ref[0]
    wcol = w_ref[0] * (IDX_HEADS ** -0.5)
    t = qs.shape[0] // IDX_HEADS
    for p in range(pg):
        d = jnp.dot(qs, ki_refs[p][0].astype(BF16), preferred_element_type=F32)
        r = jnp.maximum(d * (IDX_DIM ** -0.5), 0.0) * wcol
        sc = r[0:t, :]
        for h in range(1, IDX_HEADS):
            sc = sc + r[h * t:(h + 1) * t, :]
        o_ref[0, :, p * PAGE_SIZE:(p + 1) * PAGE_SIZE] = sc + 0.0


def _dsa_s_select_body(sc_ref, qs_ref, w_ref, smn_ref, o_ref, *, t, past, n_sel, nbat):
    rows = nbat * t
    sc_new = []
    for b in range(nbat):
        wcol = w_ref[b] * (IDX_HEADS ** -0.5)
        ki_new = smn_ref[b * t:(b + 1) * t, SM_KI:SM_KI + IDX_DIM].astype(BF16)
        ki_new = jnp.concatenate([ki_new, jnp.zeros((LANES - t, IDX_DIM), BF16)], axis=0)
        d = _nt_dot(qs_ref[b], ki_new)
        r = jnp.maximum(d * (IDX_DIM ** -0.5), 0.0) * wcol
        sc = r[0:t, :]
        for h in range(1, IDX_HEADS):
            sc = sc + r[h * t:(h + 1) * t, :]
        sc_new.append(sc)
    sc_new = jnp.concatenate(sc_new, axis=0)
    lane_t = lax.broadcasted_iota(I32, (t, LANES), 1)
    vis_t = lane_t <= lax.broadcasted_iota(I32, (t, LANES), 0)
    vis_n = jnp.concatenate([vis_t] * nbat, axis=0)
    lane_n = lax.broadcasted_iota(I32, (rows, LANES), 1)
    key_n = jnp.where(vis_n, _float_key(sc_new + 0.0), INT_MIN)
    key_p = _float_key(sc_ref[...].reshape(rows, past))
    pos_p = lax.broadcasted_iota(I32, (rows, past), 1)
    pos_n = past + lane_n

    def count(pred):
        return (jnp.sum(jnp.where(pred(key_p, pos_p), 1.0, 0.0), axis=1, keepdims=True)
                + jnp.sum(jnp.where(pred(key_n, pos_n), 1.0, 0.0), axis=1, keepdims=True))

    vec = (rows, 1)
    thr = _kth_largest_key(lambda cand: count(lambda key, kpos: key >= cand), vec, n_sel)
    need = n_sel - count(lambda key, kpos: key > thr)
    n_eq = count(lambda key, kpos: key == thr)
    excess = jnp.max(jnp.where((n_eq > need) & (thr != INT_MIN), 1.0, 0.0))
    nbits = max(1, int(past + t - 1).bit_length())
    cut = lax.cond(
        excess > 0.0,
        lambda: _tie_cut(lambda cand: count(lambda key, kpos: (key == thr) & (kpos < cand)), need, vec, nbits),
        lambda: jnp.full(vec, INT_MAX, I32))
    o_ref[:, :, 0:past] = _select_bias(key_p, pos_p, thr, cut, pos_p >= 0).reshape(nbat, t, past)
    o_ref[:, :, past:past + LANES] = _select_bias(key_n, pos_n, thr, cut, vis_n).reshape(nbat, t, LANES)


def _dsa_s_attn_body(pt_ref, qbd_ref, bias_ref, biasn_ref, kn_ref, vn_ref, *refs, pg, t):
    k_refs, v_refs = refs[:pg], refs[pg:2 * pg]
    o_ref, m_ref, l_ref, acc_ref = refs[2 * pg:]
    j = pl.program_id(1)
    rows = qbd_ref.shape[1]
    rep = rows // t

    @pl.when(j == 0)
    def _():
        m_ref[...] = jnp.full(m_ref.shape, NEG, F32)
        l_ref[...] = jnp.zeros(l_ref.shape, F32)
        acc_ref[...] = jnp.zeros(acc_ref.shape, F32)

    qbd = qbd_ref[0]

    def update(kt, vt, bias):
        s = jnp.dot(qbd, kt, preferred_element_type=F32) * (HEAD_DIM ** -0.5) + jnp.concatenate([bias] * rep, axis=0)
        m = m_ref[...]
        m_new = jnp.maximum(m, jnp.max(s, axis=1, keepdims=True))
        p = jnp.exp(s - m_new)
        alpha = jnp.exp(m - m_new)
        l_ref[...] = alpha * l_ref[...] + jnp.sum(p, axis=1, keepdims=True)
        acc_ref[...] = alpha * acc_ref[...] + _nt_dot(p.astype(BF16), vt)
        m_ref[...] = m_new

    kt = jnp.concatenate([r[0] for r in k_refs], axis=1).astype(BF16)
    vt = jnp.concatenate([r[0] for r in v_refs], axis=1).astype(BF16)
    update(kt, vt, bias_ref[0])

    @pl.when(j == pl.num_programs(1) - 1)
    def _():
        kvw = N_KV_HEADS * HEAD_DIM
        zpad = jnp.zeros((LANES - t, kvw), F32)
        update(jnp.concatenate([kn_ref[...], zpad], axis=0).T.astype(BF16),
               jnp.concatenate([vn_ref[...], zpad], axis=0).T.astype(BF16), biasn_ref[0])
        o_ref[0] = acc_ref[...] / l_ref[...]


def dsa_sample(zs, batch, t, cache_k, cache_v, cache_ki, page_table):
    n_pages = page_table.shape[1]
    past = n_pages * PAGE_SIZE
    n_sel = min(TOPK_MAX, (past + t) // 4)
    pg = next(p for p in (32, 16, 8) if n_pages % p == 0)
    nj = n_pages // pg
    n_pool = cache_k.shape[0]
    kvw = N_KV_HEADS * HEAD_DIM
    grp = N_HEADS // N_KV_HEADS

    qi = zs[:, OFF_QI:OFF_QI + IDX_HEADS * IDX_DIM].reshape(batch, t, IDX_HEADS, IDX_DIM)
    qs = jnp.transpose(qi, (0, 2, 1, 3)).reshape(batch, IDX_HEADS * t, IDX_DIM).astype(BF16)
    wi = zs[:, OFF_SM + SM_WI:OFF_SM + SM_WI + IDX_HEADS].reshape(batch, t, IDX_HEADS)
    wcol = jnp.transpose(wi, (0, 2, 1)).reshape(batch, IDX_HEADS * t, 1)

    def page_spec(shape, p):
        return pl.BlockSpec(shape, lambda b, j, pt: (pt[b, j * pg + p],) + (0,) * (len(shape) - 1))

    scores = pl.pallas_call(
        functools.partial(_dsa_s_score_body, pg=pg),
        grid_spec=pltpu.PrefetchScalarGridSpec(
            num_scalar_prefetch=1,
            grid=(batch, nj),
            in_specs=[pl.BlockSpec((1, IDX_HEADS * t, IDX_DIM), lambda b, j, pt: (b, 0, 0)),
                      pl.BlockSpec((1, IDX_HEADS * t, 1), lambda b, j, pt: (b, 0, 0))]
                     + [page_spec((1, IDX_DIM, PAGE_SIZE), p) for p in range(pg)],
            out_specs=pl.BlockSpec((1, t, pg * PAGE_SIZE), lambda b, j, pt: (b, 0, j))),
        out_shape=jax.ShapeDtypeStruct((batch, t, past), F32),
        compiler_params=_cparams("parallel", "arbitrary"),
        name="dsa_sample_scores",
    )(page_table, qs, wcol, *([jnp.swapaxes(cache_ki, 1, 2)] * pg))

    nbat = 4 if (batch % 4 == 0 and t % SUBLANES == 0) else 1
    bias = pl.pallas_call(
        functools.partial(_dsa_s_select_body, t=t, past=past, n_sel=n_sel, nbat=nbat),
        grid=(batch // nbat,),
        in_specs=[pl.BlockSpec((nbat, t, past), lambda b: (b, 0, 0)),
                  pl.BlockSpec((nbat, IDX_HEADS * t, IDX_DIM), lambda b: (b, 0, 0)),
                  pl.BlockSpec((nbat, IDX_HEADS * t, 1), lambda b: (b, 0, 0)),
                  pl.BlockSpec((nbat * t, LANES), lambda b: (b, OFF_SM // LANES))],
        out_specs=pl.BlockSpec((nbat, t, past + LANES), lambda b: (b, 0, 0)),
        out_shape=jax.ShapeDtypeStruct((batch, t, past + LANES), F32),
        compiler_params=_cparams("parallel"),
        name="dsa_sample_select",
    )(scores, qs, wcol, zs)

    q = zs[:, OFF_Q:OFF_Q + N_HEADS * HEAD_DIM].reshape(batch, t, N_KV_HEADS, grp, HEAD_DIM)
    q = jnp.transpose(q, (0, 2, 3, 1, 4))
    eye = jnp.eye(N_KV_HEADS, dtype=F32)
    qbd = (q[:, :, :, :, None, :] * eye[None, :, None, None, :, None]).reshape(batch, N_HEADS * t, kvw).astype(BF16)

    ck = jnp.transpose(cache_k, (0, 2, 3, 1)).reshape(n_pool, kvw, PAGE_SIZE)
    cv = jnp.transpose(cache_v, (0, 2, 3, 1)).reshape(n_pool, kvw, PAGE_SIZE)
    rows = N_HEADS * t
    out = pl.pallas_call(
        functools.partial(_dsa_s_attn_body, pg=pg, t=t),
        grid_spec=pltpu.PrefetchScalarGridSpec(
            num_scalar_prefetch=1,
            grid=(batch, nj),
            in_specs=[pl.BlockSpec((1, rows, kvw), lambda b, j, pt: (b, 0, 0)),
                      pl.BlockSpec((1, t, pg * PAGE_SIZE), lambda b, j, pt: (b, 0, j)),
                      pl.BlockSpec((1, t, LANES), lambda b, j, pt: (b, 0, past // LANES)),
                      pl.BlockSpec((t, kvw), lambda b, j, pt: (b, OFF_K // kvw)),
                      pl.BlockSpec((t, kvw), lambda b, j, pt: (b, OFF_V // kvw))]
                     + [page_spec((1, kvw, PAGE_SIZE), p) for p in range(pg)]
                     + [page_spec((1, kvw, PAGE_SIZE), p) for p in range(pg)],
            out_specs=pl.BlockSpec((1, rows, kvw), lambda b, j, pt: (b, 0, 0)),
            scratch_shapes=[pltpu.VMEM((rows, 1), F32), pltpu.VMEM((rows, 1), F32), pltpu.VMEM((rows, kvw), F32)]),
        out_shape=jax.ShapeDtypeStruct((batch, rows, kvw), F32),
        compiler_params=_cparams("parallel", "arbitrary"),
        name="dsa_sample_attn",
    )(page_table, qbd, bias, bias, zs, zs, *([ck] * pg), *([cv] * pg))

    o = out.reshape(batch, N_KV_HEADS, grp, t, N_KV_HEADS, HEAD_DIM)
    o = jnp.stack([o[:, kh, :, :, kh, :] for kh in range(N_KV_HEADS)], axis=1)
    return jnp.transpose(o, (0, 3, 1, 2, 4)).reshape(batch * t, N_HEADS * HEAD_DIM).astype(BF16)


def _merge_body(x_ref, gate_ref, ys_ref, oa_ref, om_ref, bg_ref, ws_ref, wa_ref, wm_ref, wo_ref, g2_ref, wr_ref, br_ref,
                x1_ref, h2_ref, te_ref, gw_ref):
    gates = jax.nn.sigmoid(gate_ref[...] + bg_ref[...])
    merged = (gates[:, 0:D_MODEL] * jnp.dot(ys_ref[...], ws_ref[...], preferred_element_type=F32)
              + gates[:, D_MODEL:2 * D_MODEL] * jnp.dot(oa_ref[...], wa_ref[...], preferred_element_type=F32)
              + gates[:, 2 * D_MODEL:] * jnp.dot(om_ref[...], wm_ref[...], preferred_element_type=F32))
    x1 = x_ref[...] + jnp.dot(merged.astype(BF16), wo_ref[...], preferred_element_type=F32)
    x1_ref[...] = x1
    h2 = x1 * lax.rsqrt(jnp.mean(x1 * x1, axis=-1, keepdims=True) + EPS)
    h2 = h2 * g2_ref[...]
    h2_ref[...] = h2
    h_hi = h2.astype(BF16)
    h_mid = (h2 - h_hi.astype(F32)).astype(BF16)
    wr_hi = wr_ref[0]
    logits = (jnp.dot(h_hi, wr_hi, preferred_element_type=F32) + jnp.dot(h_hi, wr_ref[1], preferred_element_type=F32)
              + jnp.dot(h_mid, wr_hi, preferred_element_type=F32)) + br_ref[...]
    lane = lax.broadcasted_iota(I32, logits.shape, 1)
    te = jnp.zeros(logits.shape, I32)
    tv = []
    for k in range(TOP_K):
        m = jnp.max(logits, axis=1, keepdims=True)
        idx = jnp.min(jnp.where(logits == m, lane, LANES), axis=1, keepdims=True)
        te = jnp.where(lane == k, idx, te)
        tv.append(m)
        logits = jnp.where(lane == idx, -jnp.inf, logits)
    ex = [jnp.exp(v - tv[0]) for v in tv]
    den = ex[0] + ex[1] + ex[2] + ex[3]
    gw = jnp.zeros(logits.shape, F32)
    for k in range(TOP_K):
        gw = jnp.where(lane == k, ex[k] / den, gw)
    te_ref[...] = te
    gw_ref[...] = gw


def merge(x, zall, ys, oa, om, bg, ws, wa, wm, wo, g2, wr, br, tm):
    n = x.shape[0]
    gw3 = N_BRANCH * D_MODEL
    row = lambda i: (i, 0)
    const = lambda i: (0, 0)
    return pl.pallas_call(
        _merge_body,
        grid=(n // tm,),
        in_specs=[pl.BlockSpec((tm, D_MODEL), row),
                  pl.BlockSpec((tm, gw3), lambda i: (i, OFF_GATE // gw3)),
                  pl.BlockSpec((tm, D_INNER), row),
                  pl.BlockSpec((tm, N_HEADS * HEAD_DIM), row),
                  pl.BlockSpec((tm, MEM_WIDTH), row),
                  pl.BlockSpec((1, gw3), const),
                  pl.BlockSpec((D_INNER, D_MODEL), const),
                  pl.BlockSpec((N_HEADS * HEAD_DIM, D_MODEL), const),
                  pl.BlockSpec((MEM_WIDTH, D_MODEL), const),
                  pl.BlockSpec((D_MODEL, D_MODEL), const),
                  pl.BlockSpec((1, D_MODEL), const),
                  pl.BlockSpec((2, D_MODEL, LANES), lambda i: (0, 0, 0)),
                  pl.BlockSpec((1, LANES), const)],
        out_specs=[pl.BlockSpec((tm, D_MODEL), row), pl.BlockSpec((tm, D_MODEL), row),
                   pl.BlockSpec((tm, LANES), row), pl.BlockSpec((tm, LANES), row)],
        out_shape=[jax.ShapeDtypeStruct((n, D_MODEL), F32), jax.ShapeDtypeStruct((n, D_MODEL), F32),
                   jax.ShapeDtypeStruct((n, LANES), I32), jax.ShapeDtypeStruct((n, LANES), F32)],
        compiler_params=_cparams("parallel"),
        name="merge",
    )(x, zall, ys, oa, om, bg, ws, wa, wm, wo, g2, wr, br)


def _moe_pos_body(te_ref, pos_ref, cnt_ref, carry_ref):
    i = pl.program_id(0)
    tt = te_ref.shape[0]

    @pl.when(i == 0)
    def _():
        carry_ref[...] = jnp.zeros(carry_ref.shape, F32)

    te = te_ref[...]
    lane = lax.broadcasted_iota(I32, (tt, LANES), 1)
    onehot = [lane == te[:, k:k + 1] for k in range(TOP_K)]
    msum = jnp.zeros((tt, LANES), F32)
    for k in range(TOP_K):
        msum = msum + jnp.where(onehot[k], 1.0, 0.0)
    strict = (lax.broadcasted_iota(I32, (tt, tt), 0) > lax.broadcasted_iota(I32, (tt, tt), 1))
    prefix = jnp.dot(jnp.where(strict, 1.0, 0.0).astype(BF16), msum.astype(BF16), preferred_element_type=F32)
    prefix = prefix + carry_ref[0:1, :]
    pos = jnp.zeros((tt, LANES), F32)
    for k in range(TOP_K):
        pk = jnp.sum(jnp.where(onehot[k], prefix, 0.0), axis=1, keepdims=True)
        pos = jnp.where(lane == k, pk, pos)
    pos_ref[...] = pos
    carry_ref[...] = carry_ref[...] + jnp.sum(msum, axis=0, keepdims=True)
    cnt_ref[...] = carry_ref[...]


def _moe_dest_body(te_ref, pos_ref, cnt_ref, dest_ref, be_ref, nu_ref, start_ref, *, bm):
    tt = te_ref.shape[0]

    @pl.when(pl.program_id(0) == 0)
    def _():
        cnt = cnt_ref[...]
        padded = jnp.floor((cnt + (bm - 1)) * (1.0 / bm)) * bm
        upper = (lax.broadcasted_iota(I32, (LANES, LANES), 0) < lax.broadcasted_iota(I32, (LANES, LANES), 1))
        pad_start = jnp.dot(padded, jnp.where(upper, 1.0, 0.0), precision=HIGHEST, preferred_element_type=F32)
        start_ref[...] = pad_start
        pad_end = pad_start + padded
        nb = be_ref.shape[0]
        bstart = (lax.broadcasted_iota(I32, (nb, LANES), 0) * bm).astype(F32)
        lane_b = lax.broadcasted_iota(I32, (nb, LANES), 1)
        done = jnp.where((pad_end[0:1, :] <= bstart) & (lane_b < N_EXPERTS), 1.0, 0.0)
        be = jnp.minimum(jnp.sum(done, axis=1, keepdims=True), N_EXPERTS - 1.0)
        be_ref[...] = jnp.broadcast_to(be, (nb, LANES)).astype(I32)
        total = jnp.sum(padded[0:1, :], axis=1, keepdims=True)
        nu_ref[...] = jnp.broadcast_to(total * (1.0 / bm), nu_ref.shape).astype(I32)

    pad_start = start_ref[0:1, :]
    te = te_ref[...]
    pos = pos_ref[...]
    lane = lax.broadcasted_iota(I32, (tt, LANES), 1)
    dest = jnp.zeros((tt, LANES), F32)
    for k in range(TOP_K):
        ps = jnp.sum(jnp.where(lane == te[:, k:k + 1], pad_start, 0.0), axis=1, keepdims=True)
        dest = jnp.where(lane == k, ps + pos[:, k:k + 1], dest)
    dest_ref[...] = dest.astype(I32)


def _moe_dispatch_body(dest_ref, h_ref, xs_in_ref, xs_ref, sem):
    del xs_in_ref
    tt = h_ref.shape[0]

    def copy(r, k):
        d = dest_ref[r * TOP_K + k]
        return pltpu.make_async_copy(h_ref.at[pl.ds(r, 1), :], xs_ref.at[pl.ds(d, 1), :], sem)

    def issue(r, carry):
        for k in range(TOP_K):
            copy(r, k).start()
        return carry
    lax.fori_loop(0, tt, issue, 0)

    def drain(r, carry):
        for k in range(TOP_K):
            copy(r, k).wait()
        return carry
    lax.fori_loop(0, tt, drain, 0)


def _moe_expert_body(be_ref, nu_ref, xs_ref, w1_ref, b1_ref, w2_ref, b2_ref, o_ref, w1s_ref, w2s_ref):
    i = pl.program_id(0)
    used = i < nu_ref[0]
    e = be_ref[i]
    prev = be_ref[jnp.maximum(i - 1, 0)]
    half = LANES

    @pl.when(used & ((i == 0) | (e != prev)))
    def _():
        r = lax.broadcasted_iota(I32, (2 * half, 2 * half), 0)
        c = lax.broadcasted_iota(I32, (2 * half, 2 * half), 1)
        src_col = jnp.where(c < half, 2 * c, 2 * (c - half) + 1)
        perm = jnp.where(r == src_col, 1.0, 0.0).astype(BF16)
        for j in range(2 * D_FF // (2 * half)):
            sl = slice(j * 2 * half, (j + 1) * 2 * half)
            w1s_ref[:, sl] = jnp.dot(w1_ref[0, :, sl].astype(BF16), perm, preferred_element_type=F32).astype(BF16)
        w2s_ref[...] = w2_ref[0].astype(BF16)

    @pl.when(used)
    def _():
        u = jnp.dot(xs_ref[...].astype(BF16), w1s_ref[...], preferred_element_type=F32) + b1_ref[0]
        acts = []
        for j in range(D_FF // half):
            glu = jnp.minimum(u[:, 2 * j * half:(2 * j + 1) * half], SWIGLU_LIMIT)
            lin = jnp.clip(u[:, (2 * j + 1) * half:(2 * j + 2) * half], -SWIGLU_LIMIT, SWIGLU_LIMIT)
            acts.append((glu * jax.nn.sigmoid(SWIGLU_ALPHA * glu) * (lin + 1.0)).astype(BF16))
        act = jnp.concatenate(acts, axis=1)
        o_ref[...] = jnp.dot(act, w2s_ref[...], preferred_element_type=F32) + b2_ref[0]

    @pl.when(jnp.logical_not(used))
    def _():
        o_ref[...] = jnp.zeros(o_ref.shape, F32)


def _moe_combine_body(dest_ref, gw_ref, x1_ref, gf_ref, os_ref, y_ref, buf_ref, sem):
    tt = x1_ref.shape[0]

    def copy(r, k):
        d = dest_ref[r * TOP_K + k]
        return pltpu.make_async_copy(os_ref.at[pl.ds(d, 1), :], buf_ref.at[k, pl.ds(r, 1), :], sem)

    def issue(r, carry):
        for k in range(TOP_K):
            copy(r, k).start()
        return carry
    lax.fori_loop(0, tt, issue, 0)

    def drain(r, carry):
        for k in range(TOP_K):
            copy(r, k).wait()
        return carry
    lax.fori_loop(0, tt, drain, 0)

    gw = gw_ref[...]
    y = gw[:, 0:1] * buf_ref[0]
    for k in range(1, TOP_K):
        y = y + gw[:, k:k + 1] * buf_ref[k]
    x2 = x1_ref[...] + y
    out = x2 * lax.rsqrt(jnp.mean(x2 * x2, axis=-1, keepdims=True) + EPS)
    y_ref[...] = out * gf_ref[...]


def moe_and_final_norm(x1a, x1b, h2a, h2b, te, gw, w1, b1p, w2, b2, g_final):
    n = te.shape[0]
    rows_a = x1a.shape[0]
    tt = MOE_T

    def move_tile(rows):
        return MOE_MOVE_T if rows % MOE_MOVE_T == 0 else MOE_T

    bm = MOE_BM
    nb = -(-(n * TOP_K + N_EXPERTS * (bm - 1)) // bm)
    nbp = -(-nb // SUBLANES) * SUBLANES
    row = lambda i: (i, 0)
    const = lambda i: (0, 0)

    pos, cnt = pl.pallas_call(
        _moe_pos_body,
        grid=(n // tt,),
        in_specs=[pl.BlockSpec((tt, LANES), row)],
        out_specs=[pl.BlockSpec((tt, LANES), row), pl.BlockSpec((SUBLANES, LANES), const)],
        out_shape=[jax.ShapeDtypeStruct((n, LANES), F32), jax.ShapeDtypeStruct((SUBLANES, LANES), F32)],
        scratch_shapes=[pltpu.VMEM((SUBLANES, LANES), F32)],
        compiler_params=_cparams("arbitrary"),
        name="moe_positions",
    )(te)

    dest, be, nu = pl.pallas_call(
        functools.partial(_moe_dest_body, bm=bm),
        grid=(n // tt,),
        in_specs=[pl.BlockSpec((tt, LANES), row), pl.BlockSpec((tt, LANES), row),
                  pl.BlockSpec((SUBLANES, LANES), const)],
        out_specs=[pl.BlockSpec((tt, LANES), row), pl.BlockSpec((nbp, LANES), const),
                   pl.BlockSpec((SUBLANES, LANES), const)],
        out_shape=[jax.ShapeDtypeStruct((n, LANES), I32), jax.ShapeDtypeStruct((nbp, LANES), I32),
                   jax.ShapeDtypeStruct((SUBLANES, LANES), I32)],
        scratch_shapes=[pltpu.VMEM((SUBLANES, LANES), F32)],
        compiler_params=_cparams("arbitrary"),
        name="moe_destinations",
    )(te, pos, cnt)
    dest_flat = dest[:, :TOP_K].reshape(n * TOP_K)
    block_e = be[:nb, 0]
    n_used = nu[0, 0:1]

    def dispatch(h_part, tok0, xs_buf):
        mt = move_tile(h_part.shape[0])
        tile0 = tok0 // mt
        return pl.pallas_call(
            _moe_dispatch_body,
            grid=(h_part.shape[0] // mt,),
            in_specs=[pl.BlockSpec((mt * TOP_K,), lambda i: (i + tile0,), memory_space=pltpu.SMEM),
                      pl.BlockSpec((mt, D_MODEL), row),
                      pl.BlockSpec(memory_space=pl.ANY)],
            out_specs=pl.BlockSpec(memory_space=pl.ANY),
            out_shape=jax.ShapeDtypeStruct((nb * bm, D_MODEL), F32),
            scratch_shapes=[pltpu.SemaphoreType.DMA(())],
            input_output_aliases={2: 0},
            compiler_params=_cparams("arbitrary"),
            name="moe_dispatch",
        )(dest_flat, h_part, xs_buf)

    xs = dispatch(h2b, rows_a, dispatch(h2a, 0, jnp.zeros((nb * bm, D_MODEL), F32)))

    out_sorted = pl.pallas_call(
        _moe_expert_body,
        grid_spec=pltpu.PrefetchScalarGridSpec(
            num_scalar_prefetch=2,
            grid=(nb,),
            in_specs=[pl.BlockSpec((bm, D_MODEL), lambda i, be_, nu_: (i, 0)),
                      pl.BlockSpec((1, D_MODEL, 2 * D_FF), lambda i, be_, nu_: (be_[i], 0, 0)),
                      pl.BlockSpec((1, 1, 2 * D_FF), lambda i, be_, nu_: (be_[i], 0, 0)),
                      pl.BlockSpec((1, D_FF, D_MODEL), lambda i, be_, nu_: (be_[i], 0, 0)),
                      pl.BlockSpec((1, 1, D_MODEL), lambda i, be_, nu_: (be_[i], 0, 0))],
            out_specs=pl.BlockSpec((bm, D_MODEL), lambda i, be_, nu_: (i, 0)),
            scratch_shapes=[pltpu.VMEM((D_MODEL, 2 * D_FF), BF16), pltpu.VMEM((D_FF, D_MODEL), BF16)]),
        out_shape=jax.ShapeDtypeStruct((nb * bm, D_MODEL), F32),
        compiler_params=_cparams("arbitrary"),
        name="moe_experts",
    )(block_e, n_used, xs, w1, b1p, w2, b2)

    def combine(x1_part, tok0):
        mt = move_tile(x1_part.shape[0])
        tile0 = tok0 // mt
        return pl.pallas_call(
            _moe_combine_body,
            grid=(x1_part.shape[0] // mt,),
            in_specs=[pl.BlockSpec((mt * TOP_K,), lambda i: (i + tile0,), memory_space=pltpu.SMEM),
                      pl.BlockSpec((mt, LANES), lambda i: (i + tile0, 0)),
                      pl.BlockSpec((mt, D_MODEL), row),
                      pl.BlockSpec((1, D_MODEL), const),
                      pl.BlockSpec(memory_space=pl.ANY)],
            out_specs=pl.BlockSpec((mt, D_MODEL), row),
            out_shape=jax.ShapeDtypeStruct(x1_part.shape, F32),
            scratch_shapes=[pltpu.VMEM((TOP_K, mt, D_MODEL), F32), pltpu.SemaphoreType.DMA(())],
            compiler_params=_cparams("arbitrary"),
            name="moe_combine",
        )(dest_flat, gw, x1_part, g_final.reshape(1, D_MODEL), out_sorted)

    return combine(x1a, 0), combine(x1b, rows_a)


def _split_cols(w):
    outs, off = [], 0
    for wd in IN_WIDTHS:
        outs.append(w[:, off:off + wd])
        off += wd
    return outs


def _lane_row(v, off):
    return jnp.zeros((1, LANES), F32).at[0, off:off + v.shape[0]].set(v.astype(F32))


def kernel(x_prompt, x_sample, cache_k, cache_v, cache_idx_k, state_conv, state_ssm, cache_mem_k, cache_mem_v,
           page_table, mem_prompt, g_norm1, w_in, b_gate, conv_w, conv_b, dt_bias, a_log, d_skip, g_ssd_norm,
           g_mem, w_mem_kv, w_ssd_out, w_attn_out, w_mem_out, w_out, g_norm2, w_router, b_router, w_exp1,
           b_exp1, w_exp2, b_exp2, g_final):
    assert w_in.shape[0] == 1, "single-layer trunk"
    bp, lp, _ = x_prompt.shape
    bs, ls, _ = x_sample.shape
    np_, ns = bp * lp, bs * ls

    wz, wxbc, wdt, wq, wk, wv, wqi, wki, wwi, wqm, wgate = _split_cols(w_in[0])
    w_all = jnp.concatenate(
        [wxbc, wgate, wz, wq, wqm, wqi, wk, wv, wki, wdt, wwi,
         jnp.zeros((D_MODEL, W_ALL - OFF_SM - SM_WI - IDX_HEADS), F32)], axis=1).astype(BF16)
    dtb_row = _lane_row(dt_bias[0], SM_DT)
    aneg_row = _lane_row(-jnp.exp(a_log[0].astype(F32)), SM_DT)
    dsk_row = jnp.repeat(d_skip[0].astype(F32), SSD_HEAD_DIM).reshape(1, D_INNER)
    gs_row = g_ssd_norm[0].reshape(1, D_INNER)
    cb_row = conv_b[0].reshape(1, CONV_DIM)
    wr_f32 = jnp.zeros((D_MODEL, LANES), F32).at[:, :N_EXPERTS].set(w_router[0])
    wr_hi = wr_f32.astype(BF16)
    wr_pad = jnp.stack([wr_hi, (wr_f32 - wr_hi.astype(F32)).astype(BF16)])
    br_pad = jnp.full((1, LANES), NEG, F32).at[0, :N_EXPERTS].set(b_router[0])
    b1p = b_exp1[0].reshape(N_EXPERTS, D_FF // LANES, LANES, 2).transpose(0, 1, 3, 2).reshape(N_EXPERTS, 1, 2 * D_FF)
    b2 = b_exp2[0].reshape(N_EXPERTS, 1, D_MODEL)

    xp = x_prompt.reshape(np_, D_MODEL)
    xs = x_sample.reshape(ns, D_MODEL)
    zp = norm_matmul(xp, g_norm1[0], w_all, 1024, IN_PROJ_TN)
    zs = norm_matmul(xs, g_norm1[0], w_all, ns, IN_PROJ_TN)

    kv_p = norm_matmul(mem_prompt.reshape(bp * N_MEM, D_MODEL), g_mem[0], w_mem_kv[0].astype(BF16),
                       min(1024, bp * N_MEM), MEM_WIDTH)
    om_p = mem_attn(zp, bp, lp, kv_p, 0, kv_p, 1, 512)
    om_s = mem_attn(zs, bs, ls, cache_mem_k[0].reshape(bs * N_MEM, MEM_WIDTH), 0,
                    cache_mem_v[0].reshape(bs * N_MEM, MEM_WIDTH), 0, ls)

    conv_prev_p = jnp.zeros((bp, SUBLANES, CONV_DIM), F32)
    conv_prev_s = jnp.concatenate(
        [jnp.zeros((bs, SUBLANES - (CONV_WIDTH - 1), CONV_DIM), F32), state_conv[0]], axis=1)
    ssm0_p = jnp.zeros((bp, D_INNER, D_STATE), F32)
    ssm0_s = state_ssm[0].reshape(bs, D_INNER, D_STATE)
    ys_p, ssm_p = ssd(zp, bp, lp, conv_prev_p, ssm0_p, conv_w[0], cb_row, dtb_row, aneg_row, dsk_row, gs_row)
    ys_s, ssm_s = ssd(zs, bs, ls, conv_prev_s, ssm0_s, conv_w[0], cb_row, dtb_row, aneg_row, dsk_row, gs_row)

    oa_p = dsa_prompt(zp, bp, lp)
    oa_s = dsa_sample(zs, bs, ls, cache_k[0], cache_v[0], cache_idx_k[0], page_table)

    mw = (b_gate[0].reshape(1, -1), w_ssd_out[0].astype(BF16), w_attn_out[0].astype(BF16),
          w_mem_out[0].astype(BF16), w_out[0].astype(BF16), g_norm2[0].reshape(1, D_MODEL), wr_pad, br_pad)
    x1_p, h2_p, te_p, gw_p = merge(xp, zp, ys_p, oa_p, om_p, *mw, 512)
    x1_s, h2_s, te_s, gw_s = merge(xs, zs, ys_s, oa_s, om_s, *mw, ns)

    cat = lambda a, b: jnp.concatenate([a, b], axis=0)
    y_all = moe_and_final_norm(x1_p, x1_s, h2_p, h2_s, cat(te_p, te_s), cat(gw_p, gw_s),
                               w_exp1[0], b1p, w_exp2[0], b2, g_final)
    y_prompt = y_all[0].reshape(bp, lp, D_MODEL)
    y_sample = y_all[1].reshape(bs, ls, D_MODEL)

    def kvi(z, b, l):
        k = z[:, OFF_K:OFF_K + N_KV_HEADS * HEAD_DIM].reshape(1, b, l, N_KV_HEADS, HEAD_DIM)
        v = z[:, OFF_V:OFF_V + N_KV_HEADS * HEAD_DIM].reshape(1, b, l, N_KV_HEADS, HEAD_DIM)
        ki = z[:, OFF_SM + SM_KI:OFF_SM + SM_KI + IDX_DIM].reshape(1, b, l, IDX_DIM)
        conv = z.reshape(b, l, W_ALL)[:, l - (CONV_WIDTH - 1):, OFF_XBC:OFF_XBC + CONV_DIM][None]
        return k, v, ki, conv

    k_p, v_p, ki_p, conv_p = kvi(zp, bp, lp)
    k_s, v_s, ki_s, conv_s = kvi(zs, bs, ls)
    mk_p = kv_p[:, :MEM_WIDTH].reshape(1, bp, N_MEM, MEM_HEADS, MEM_HEAD_DIM)
    mv_p = kv_p[:, MEM_WIDTH:].reshape(1, bp, N_MEM, MEM_HEADS, MEM_HEAD_DIM)
    ssm_shape = (1, -1, SSD_HEADS, SSD_HEAD_DIM, D_STATE)
    return (y_prompt, y_sample, k_p, v_p, ki_p, conv_p, ssm_p.reshape(ssm_shape), mk_p, mv_p,
            k_s, v_s, ki_s, conv_s, ssm_s.reshape(ssm_shape))
```
